```python
import math
import jax, jax.numpy as jnp
from jax import lax
import numpy as np

D_MODEL = 2048
BATCH = 4
SEQ = 2048
DEPTH = 2
DEC_BATCH = 8
DEC_SEQ = 16
PAST_LEN = 2048

CHUNK = 64
N_META = 16
H_ATT = 4
DQK_ATT = 128
DV_ATT = 256
H_M = 4
DQK_M = 128
DV_M = 256
D_FF = 4 * D_MODEL
N_BUCKETS = 32
MAX_DISTANCE = 128
Q_BLOCK = 128
M_BLOCK = 64
EPS = 1e-6
NEG = -1e30

ATT_QK = H_ATT * 2 * DQK_ATT
ATT_V = H_ATT * DV_ATT
M_QK = H_M * DQK_M
M_V = H_M * DV_M
N_IN = 2 * ATT_QK + ATT_V + 2 * M_QK + M_V + 2 * H_M + M_V

kernel_name = 'hymba_diffattn_mlstm_stream_step'


def rms_norm(x, g):
    xf = x.astype(jnp.float32)
    y = xf * lax.rsqrt(jnp.mean(xf * xf, axis=-1, keepdims=True) + EPS)
    return (y * g.astype(jnp.float32)).astype(x.dtype)


def rel_bucket(rel):
    nb = N_BUCKETS // 2
    max_exact = nb // 2
    ret = jnp.where(rel > 0, nb, 0)
    n = jnp.abs(rel)
    nf = jnp.maximum(n, 1).astype(jnp.float32)
    large = max_exact + (jnp.log(nf / max_exact) / math.log(MAX_DISTANCE / max_exact) * (nb - max_exact)).astype(jnp.int32)
    large = jnp.minimum(large, nb - 1)
    return ret + jnp.where(n < max_exact, n, large)


def split_in(z):
    sizes = (ATT_QK, ATT_QK, ATT_V, M_QK, M_QK, M_V, H_M, H_M, M_V)
    idx = [int(c) for c in np.cumsum(sizes)[:-1]]
    return jnp.split(z, idx, axis=-1)


def project(xn, w_in, b_i, b_f, q_norm_g, k_norm_g):
    B, T, _ = xn.shape
    f32 = jnp.float32
    aq, ak, av, mq, mk, mv, mi, mf, mo = split_in(xn @ w_in)
    aq = rms_norm(aq.reshape(B, T, H_ATT, 2, DQK_ATT), q_norm_g)
    ak = rms_norm(ak.reshape(B, T, H_ATT, 2, DQK_ATT), k_norm_g)
    av = av.reshape(B, T, H_ATT, DV_ATT)
    mq = mq.reshape(B, T, H_M, DQK_M).astype(f32)
    mk = mk.reshape(B, T, H_M, DQK_M).astype(f32) * (DQK_M ** -0.5)
    mv = mv.reshape(B, T, H_M, DV_M).astype(f32)
    ig = mi.astype(f32) + b_i.astype(f32)
    lf = jax.nn.log_sigmoid(mf.astype(f32) + b_f.astype(f32))
    og = jax.nn.sigmoid(mo).reshape(B, T, H_M, DV_M)
    return aq, ak, av, mq, mk, mv, ig, lf, og


def diff_lambda(lq1, lk1, lq2, lk2, lam_init):
    f32 = jnp.float32
    return (jnp.exp(jnp.sum(lq1.astype(f32) * lk1.astype(f32)))
            - jnp.exp(jnp.sum(lq2.astype(f32) * lk2.astype(f32))) + lam_init)


def diff_attn_block(q, k, v, qpos, qchunk, kpos, kchunk, rel_bias, lam, lam_init, subln_g):
    s = jnp.einsum('bqhcd,bkhcd->bchqk', q, k, preferred_element_type=jnp.float32) * (DQK_ATT ** -0.5)
    bias = jnp.transpose(rel_bias[rel_bucket(kpos[None, :] - qpos[:, None])], (2, 0, 1))
    s = s + bias.astype(jnp.float32)
    s = jnp.where(kchunk[None, :] <= qchunk[:, None], s, NEG)
    p = jax.nn.softmax(s, axis=-1)
    w = p[:, 0] - lam * p[:, 1]
    o = jnp.einsum('bhqk,bkhd->bqhd', w.astype(v.dtype), v)
    return rms_norm(o, subln_g) * (1.0 - lam_init)


def diff_attn_prompt(q, k, v, pos, chunk, rel_bias, lam, lam_init, subln_g):
    B, L = q.shape[:2]
    nqb = -(-L // Q_BLOCK)
    pad = nqb * Q_BLOCK - L
    qp = jnp.pad(q, ((0, 0), (0, pad), (0, 0), (0, 0), (0, 0)))
    qp = jnp.moveaxis(qp.reshape(B, nqb, Q_BLOCK, H_ATT, 2, DQK_ATT), 1, 0)
    posp = jnp.pad(pos, (0, pad), mode='edge').reshape(nqb, Q_BLOCK)
    chp = jnp.pad(chunk, (0, pad), mode='edge').reshape(nqb, Q_BLOCK)

    def one(args):
        qb, qpb, qcb = args
        return diff_attn_block(qb, k, v, qpb, qcb, pos, chunk, rel_bias, lam, lam_init, subln_g)

    out = lax.map(one, (qp, posp, chp))
    return jnp.moveaxis(out, 0, 1).reshape(B, nqb * Q_BLOCK, H_ATT, DV_ATT)[:, :L]


def mlstm_chunkwise(q, k, v, ig, lf, C0, n0, m0, blk):
    B, L = q.shape[:2]
    nb = L // blk

    def to_blocks(a):
        return jnp.moveaxis(a.reshape((B, nb, blk) + a.shape[2:]), 1, 0)

    xs = tuple(to_blocks(a) for a in (q, k, v, ig, lf))
    causal = jnp.tril(jnp.ones((blk, blk), dtype=bool))

    def step(carry, blkin):
        C, n, m = carry
        qb, kb, vb, ib, fb = blkin
        bh = jnp.moveaxis(jnp.cumsum(fb, axis=1), 1, 2)
        ih = jnp.moveaxis(ib, 1, 2)
        D = bh[:, :, :, None] - bh[:, :, None, :] + ih[:, :, None, :]
        D = jnp.where(causal, D, NEG)
        inter = bh + m[:, :, None]
        mt = jnp.maximum(inter, jnp.max(D, axis=-1))
        A = jnp.exp(D - mt[..., None]) * jnp.einsum('bthd,bshd->bhts', qb, kb)
        iw = jnp.exp(inter - mt)
        num = (jnp.einsum('bhts,bshe->bthe', A, vb)
               + jnp.moveaxis(iw, 1, 2)[..., None] * jnp.einsum('bhed,bthd->bthe', C, qb))
        den = jnp.sum(A, axis=-1) + iw * jnp.einsum('bhd,bthd->bht', n, qb)
        den = jnp.maximum(jnp.abs(den), jnp.exp(-mt))
        h = num / jnp.moveaxis(den, 1, 2)[..., None]
        bl = bh[:, :, -1]
        g = bl[:, :, None] - bh + ih
        m_new = jnp.maximum(bl + m, jnp.max(g, axis=-1))
        sw = jnp.exp(g - m_new[..., None])
        dec = jnp.exp(bl + m - m_new)
        C_new = dec[..., None, None] * C + jnp.einsum('bhs,bshe,bshd->bhed', sw, vb, kb)
        n_new = dec[..., None] * n + jnp.einsum('bhs,bshd->bhd', sw, kb)
        return (C_new, n_new, m_new), h

    (C, n, m), hs = lax.scan(step, (C0, n0, m0), xs)
    h = jnp.moveaxis(hs, 0, 1).reshape(B, L, H_M, DV_M)
    return h, C, n, m


def merge_and_ffn(x, att, hm, og, mnorm_g, w_out, norm2_g, w_up, w_down):
    B, T, _ = x.shape
    hm = rms_norm(hm, mnorm_g.reshape(H_M, DV_M)).astype(x.dtype) * og
    mix = jnp.concatenate([att.reshape(B, T, ATT_V), hm.reshape(B, T, M_V)], axis=-1)
    x = x + mix @ w_out
    u = jax.nn.relu(rms_norm(x, norm2_g) @ w_up)
    return x + jnp.square(u) @ w_down


def setup_inputs(seed: int = 0) -> dict:
    key = jax.random.key(seed)
    ks = jax.random.split(key, 32)
    f32 = jnp.float32

    def nrm(k, shape, s):
        return s * jax.random.normal(k, shape, f32)

    return {
        'x_prompt': nrm(ks[0], (BATCH, SEQ, D_MODEL), 1.0),
        'x_sample': nrm(ks[1], (DEC_BATCH, DEC_SEQ, D_MODEL), 1.0),
        'cache_k': nrm(ks[2], (DEPTH, DEC_BATCH, PAST_LEN, H_ATT, 2 * DQK_ATT), 1.0),
        'cache_v': nrm(ks[3], (DEPTH, DEC_BATCH, PAST_LEN, H_ATT, DV_ATT), 1.0),
        'state_C': nrm(ks[4], (DEPTH, DEC_BATCH, H_M, DV_M, DQK_M), 1.0),
        'state_n': nrm(ks[5], (DEPTH, DEC_BATCH, H_M, DQK_M), 1.0),
        'state_m': nrm(ks[6], (DEPTH, DEC_BATCH, H_M), 1.0),
        'meta_tokens': nrm(ks[7], (N_META, D_MODEL), 1.0),
        'rel_bias': nrm(ks[8], (N_BUCKETS, H_ATT), 0.5),
        'norm1_g': 1.0 + nrm(ks[9], (DEPTH, D_MODEL), 0.02),
        'w_in': nrm(ks[10], (DEPTH, D_MODEL, N_IN), D_MODEL ** -0.5),
        'b_i': nrm(ks[11], (DEPTH, H_M), 0.1),
        'b_f': 3.0 + nrm(ks[12], (DEPTH, H_M), 0.1),
        'q_norm_g': 1.0 + nrm(ks[13], (DEPTH, DQK_ATT), 0.02),
        'k_norm_g': 1.0 + nrm(ks[14], (DEPTH, DQK_ATT), 0.02),
        'lambda_q1': nrm(ks[15], (DEPTH, DQK_ATT), 0.1),
        'lambda_k1': nrm(ks[16], (DEPTH, DQK_ATT), 0.1),
        'lambda_q2': nrm(ks[17], (DEPTH, DQK_ATT), 0.1),
        'lambda_k2': nrm(ks[18], (DEPTH, DQK_ATT), 0.1),
        'subln_g': 1.0 + nrm(ks[19], (DEPTH, DV_ATT), 0.02),
        'mnorm_g': 1.0 + nrm(ks[20], (DEPTH, M_V), 0.02),
        'w_out': nrm(ks[21], (DEPTH, D_MODEL, D_MODEL), D_MODEL ** -0.5),
        'norm2_g': 1.0 + nrm(ks[22], (DEPTH, D_MODEL), 0.02),
        'w_up': nrm(ks[23], (DEPTH, D_MODEL, D_FF), D_MODEL ** -0.5),
        'w_down': nrm(ks[24], (DEPTH, D_FF, D_MODEL), D_FF ** -0.5),
    }


def reference(x_prompt, x_sample, cache_k, cache_v, state_C, state_n, state_m, meta_tokens, rel_bias,
              norm1_g, w_in, b_i, b_f, q_norm_g, k_norm_g, lambda_q1, lambda_k1, lambda_q2, lambda_k2,
              subln_g, mnorm_g, w_out, norm2_g, w_up, w_down):
    f32 = jnp.float32
    B, S, _ = x_prompt.shape
    Bd, Td, _ = x_sample.shape
    Pc = cache_k.shape[2]
    L = S + N_META
    Lp = -(-L // M_BLOCK) * M_BLOCK
    pad = Lp - L

    meta = jnp.broadcast_to(meta_tokens.astype(x_prompt.dtype)[None], (B, N_META, D_MODEL))
    xp = jnp.concatenate([meta, x_prompt], axis=1)
    xs = x_sample

    p_pos = jnp.arange(L, dtype=jnp.int32)
    p_chunk = jnp.where(p_pos < N_META, -1, (p_pos - N_META) // CHUNK)
    s_kpos = jnp.arange(Pc + Td, dtype=jnp.int32)
    s_qpos = s_kpos[Pc:]

    def padt(a, val):
        return jnp.pad(a, ((0, 0), (0, pad)) + ((0, 0),) * (a.ndim - 2), constant_values=val)

    kp, vp, Cp, np_, mp = [], [], [], [], []
    ksl, vsl, Csl, nsl, msl = [], [], [], [], []
    for l in range(DEPTH):
        lam_init = 0.8 - 0.6 * math.exp(-0.3 * l)
        lam = diff_lambda(lambda_q1[l], lambda_k1[l], lambda_q2[l], lambda_k2[l], lam_init)

        aq, ak, av, mq, mk, mv, ig, lf, og = project(rms_norm(xp, norm1_g[l]), w_in[l], b_i[l], b_f[l],
                                                     q_norm_g[l], k_norm_g[l])
        att = diff_attn_prompt(aq, ak, av, p_pos, p_chunk, rel_bias, lam, lam_init, subln_g[l])
        hm, C, n, m = mlstm_chunkwise(padt(mq, 0.0), padt(mk, 0.0), padt(mv, 0.0), padt(ig, NEG), padt(lf, 0.0),
                                      jnp.zeros((B, H_M, DV_M, DQK_M), f32), jnp.zeros((B, H_M, DQK_M), f32),
                                      jnp.zeros((B, H_M), f32), M_BLOCK)
        xp = merge_and_ffn(xp, att, hm[:, :L], og, mnorm_g[l], w_out[l], norm2_g[l], w_up[l], w_down[l])
        kp.append(ak.reshape(B, L, H_ATT, 2 * DQK_ATT))
        vp.append(av)
        Cp.append(C)
        np_.append(n)
        mp.append(m)

        aq, ak, av, mq, mk, mv, ig, lf, og = project(rms_norm(xs, norm1_g[l]), w_in[l], b_i[l], b_f[l],
                                                     q_norm_g[l], k_norm_g[l])
        k_new = ak.reshape(Bd, Td, H_ATT, 2 * DQK_ATT)
        k_all = jnp.concatenate([cache_k[l].astype(k_new.dtype), k_new], axis=1).reshape(Bd, Pc + Td, H_ATT, 2, DQK_ATT)
        v_all = jnp.concatenate([cache_v[l].astype(av.dtype), av], axis=1)
        att = diff_attn_block(aq, k_all, v_all, s_qpos, s_qpos // CHUNK, s_kpos, s_kpos // CHUNK,
                              rel_bias, lam, lam_init, subln_g[l])
        hm, C, n, m = mlstm_chunkwise(mq, mk, mv, ig, lf, state_C[l].astype(f32), state_n[l].astype(f32),
                                      state_m[l].astype(f32), Td)
        xs = merge_and_ffn(xs, att, hm, og, mnorm_g[l], w_out[l], norm2_g[l], w_up[l], w_down[l])
        ksl.append(k_new)
        vsl.append(av)
        Csl.append(C)
        nsl.append(n)
        msl.append(m)

    y_prompt = xp[:, N_META:]
    y_sample = xs
    return (y_prompt, y_sample,
            jnp.stack(kp), jnp.stack(vp), jnp.stack(Cp), jnp.stack(np_), jnp.stack(mp),
            jnp.stack(ksl), jnp.stack(vsl), jnp.stack(Csl), jnp.stack(nsl), jnp.stack(msl))
```

```python
import functools
import math

import jax
import jax.numpy as jnp
from jax import lax
from jax.experimental import pallas as pl
from jax.experimental.pallas import tpu as pltpu

F32 = jnp.float32
BF16 = jnp.bfloat16

CHUNK = 64
N_META = 16
H_ATT = 4
DQK_ATT = 128
DV_ATT = 256
H_M = 4
DQK_M = 128
DV_M = 256
N_BUCKETS = 32
MAX_DISTANCE = 128
EPS = 1e-6
NEG = -1e30

ATT_QK = H_ATT * 2 * DQK_ATT
ATT_V = H_ATT * DV_ATT
M_QK = H_M * DQK_M
M_V = H_M * DV_M
N_MAIN = 2 * ATT_QK + ATT_V + 2 * M_QK + M_V + M_V

LANES = 128
ATT_TILE = 256
M_TILE = 256
ROW_ALIGN = 768
VMEM_LIMIT = 56 * 1024 * 1024

_NT = (((1,), (1,)), ((), ()))


def _dot(a, b):
    return jnp.dot(a, b, preferred_element_type=F32)


def _dot_nt(a, b):
    return lax.dot_general(a, b, _NT, preferred_element_type=F32)


def _log_sigmoid(x):
    return jnp.minimum(x, 0.0) - jnp.log1p(jnp.exp(-jnp.abs(x)))


def _col(a, idx):
    lane = lax.broadcasted_iota(jnp.int32, a.shape, 1)
    return jnp.sum(jnp.where(lane == idx, a, 0.0), axis=1, keepdims=True)


def _bias_from_rel(rel, rb_ref, h):
    nb = N_BUCKETS // 2
    max_exact = nb // 2
    ret = jnp.where(rel > 0, nb, 0)
    n = jnp.abs(rel)
    nf = jnp.maximum(n, 1).astype(F32)
    large = max_exact + (jnp.log(nf / max_exact) / math.log(MAX_DISTANCE / max_exact)
                         * (nb - max_exact)).astype(jnp.int32)
    large = jnp.minimum(large, nb - 1)
    bucket = ret + jnp.where(n < max_exact, n, large)
    out = jnp.zeros(rel.shape, F32)
    for b in range(N_BUCKETS):
        out = jnp.where(bucket == b, rb_ref[b, h], out)
    return out


def _bias_kernel(rb_ref, diag_ref, prev_ref, meta_ref, far_ref, mm_ref, samp_ref, *, past_len, dec_seq):
    h = pl.program_id(0)
    shift = CHUNK.bit_length() - 1
    T = ATT_TILE

    qi = lax.broadcasted_iota(jnp.int32, (T, T), 0)
    kj = lax.broadcasted_iota(jnp.int32, (T, T), 1)
    bias = _bias_from_rel(kj - qi, rb_ref, h)
    diag_ref[...] = jnp.where((kj >> shift) <= (qi >> shift), bias, NEG)
    prev_ref[...] = _bias_from_rel(kj - T - qi, rb_ref, h)
    far_ref[...] = _bias_from_rel(jnp.full(far_ref.shape, -(2 * MAX_DISTANCE + 1), jnp.int32), rb_ref, h)

    qi = lax.broadcasted_iota(jnp.int32, (T, LANES), 0)
    kj = lax.broadcasted_iota(jnp.int32, (T, LANES), 1)
    for var in range(2):
        bias = _bias_from_rel(kj - (N_META + qi + var * T), rb_ref, h)
        meta_ref[var] = jnp.where(kj < N_META, bias, NEG)

    qi = lax.broadcasted_iota(jnp.int32, (N_META, LANES), 0)
    kj = lax.broadcasted_iota(jnp.int32, (N_META, LANES), 1)
    mm_ref[...] = jnp.where(kj < N_META, _bias_from_rel(kj - qi, rb_ref, h), NEG)

    width = samp_ref.shape[-1]
    qi = lax.broadcasted_iota(jnp.int32, (dec_seq, width), 0) + past_len
    kj = lax.broadcasted_iota(jnp.int32, (dec_seq, width), 1)
    bias = _bias_from_rel(kj - qi, rb_ref, h)
    ok = ((kj >> shift) <= (qi >> shift)) & (kj < past_len + dec_seq)
    samp_ref[...] = jnp.where(ok, bias, NEG)


def _bias_tiles(rel_bias, past_len, dec_seq):
    T = ATT_TILE
    sw = past_len + LANES
    out_shape = (
        jax.ShapeDtypeStruct((H_ATT, T, T), F32),
        jax.ShapeDtypeStruct((H_ATT, T, T), F32),
        jax.ShapeDtypeStruct((H_ATT, 2, T, LANES), F32),
        jax.ShapeDtypeStruct((H_ATT, 8, T), F32),
        jax.ShapeDtypeStruct((H_ATT, N_META, LANES), F32),
        jax.ShapeDtypeStruct((H_ATT, dec_seq, sw), F32),
    )
    out_specs = (
        pl.BlockSpec((None, T, T), lambda h: (h, 0, 0)),
        pl.BlockSpec((None, T, T), lambda h: (h, 0, 0)),
        pl.BlockSpec((None, 2, T, LANES), lambda h: (h, 0, 0, 0)),
        pl.BlockSpec((None, 8, T), lambda h: (h, 0, 0)),
        pl.BlockSpec((None, N_META, LANES), lambda h: (h, 0, 0)),
        pl.BlockSpec((None, dec_seq, sw), lambda h: (h, 0, 0)),
    )
    return pl.pallas_call(
        functools.partial(_bias_kernel, past_len=past_len, dec_seq=dec_seq),
        grid=(H_ATT,),
        in_specs=[pl.BlockSpec(memory_space=pltpu.SMEM)],
        out_specs=out_specs,
        out_shape=out_shape,
        name="rel_bias_tiles",
    )(rel_bias)


def _inproj_kernel(x_ref, g1_ref, w_ref, wg_ref, gb_ref, qg_ref, kg_ref, z_ref, g_ref, xn_s):
    j = pl.program_id(1)

    @pl.when(j == 0)
    def _():
        x = x_ref[...]
        ms = jnp.mean(x * x, axis=-1, keepdims=True)
        xb = (x * lax.rsqrt(ms + EPS) * g1_ref[...]).astype(BF16)
        xn_s[...] = xb
        gates = _dot(xb, wg_ref[...]) + gb_ref[...]
        lane = lax.broadcasted_iota(jnp.int32, gates.shape, 1)
        g_ref[...] = jnp.where(lane < H_M, gates, _log_sigmoid(gates))

    acc = _dot(xn_s[...], w_ref[...])
    bn = acc.shape[1]

    def group_norm(gain_ref, scale):
        for gi in range(bn // DQK_ATT):
            sl = slice(gi * DQK_ATT, (gi + 1) * DQK_ATT)
            blk = acc[:, sl]
            ms = jnp.mean(blk * blk, axis=-1, keepdims=True)
            y = blk * lax.rsqrt(ms + EPS) * gain_ref[...]
            z_ref[:, sl] = y if scale is None else y * scale

    @pl.when(j == 0)
    def _():
        group_norm(qg_ref, DQK_ATT ** -0.5)

    @pl.when(j == 1)
    def _():
        group_norm(kg_ref, None)

    @pl.when((j == 2) | (j == 4))
    def _():
        z_ref[...] = acc

    @pl.when(j == 3)
    def _():
        z_ref[:, :M_QK] = acc[:, :M_QK]
        z_ref[:, M_QK:] = acc[:, M_QK:] * (DQK_M ** -0.5)

    @pl.when(j == 5)
    def _():
        z_ref[...] = jax.nn.sigmoid(acc)


def _inproj(x_rows, g1, w_main, w_gate, gate_bias, qg, kg, bm):
    R, D = x_rows.shape
    bn = ATT_QK
    assert N_MAIN % bn == 0 and 2 * M_QK == bn and ATT_V == bn and M_V == bn
    return pl.pallas_call(
        _inproj_kernel,
        grid=(R // bm, N_MAIN // bn),
        in_specs=[
            pl.BlockSpec((bm, D), lambda i, j: (i, 0)),
            pl.BlockSpec((1, D), lambda i, j: (0, 0)),
            pl.BlockSpec((D, bn), lambda i, j: (0, j)),
            pl.BlockSpec((D, LANES), lambda i, j: (0, 0)),
            pl.BlockSpec((1, LANES), lambda i, j: (0, 0)),
            pl.BlockSpec((1, DQK_ATT), lambda i, j: (0, 0)),
            pl.BlockSpec((1, DQK_ATT), lambda i, j: (0, 0)),
        ],
        out_specs=(
            pl.BlockSpec((bm, bn), lambda i, j: (i, j)),
            pl.BlockSpec((bm, LANES), lambda i, j: (i, 0)),
        ),
        out_shape=(
            jax.ShapeDtypeStruct((R, N_MAIN), F32),
            jax.ShapeDtypeStruct((R, LANES), F32),
        ),
        scratch_shapes=[pltpu.VMEM((bm, D), BF16)],
        compiler_params=pltpu.CompilerParams(
            dimension_semantics=("arbitrary", "arbitrary"), vmem_limit_bytes=VMEM_LIMIT),
        name="in_proj",
    )(x_rows, g1, w_main, w_gate, gate_bias, qg, kg)


def _diff_lambda(lq1_ref, lk1_ref, lq2_ref, lk2_ref, lam_init):
    a = jnp.sum(lq1_ref[...] * lk1_ref[...], axis=-1, keepdims=True)
    b = jnp.sum(lq2_ref[...] * lk2_ref[...], axis=-1, keepdims=True)
    return jnp.exp(a) - jnp.exp(b) + lam_init


def _subln(o, sg_ref, lam_init):
    ms = jnp.mean(o * o, axis=-1, keepdims=True)
    return o * lax.rsqrt(ms + EPS) * sg_ref[...] * (1.0 - lam_init)


def _pad_rows(dst, src):
    dst[...] = jnp.zeros(dst.shape, dst.dtype)
    dst[0:src.shape[0], :] = src[...]


def _attn_prompt_kernel(q_ref, k_ref, v_ref, km_ref, vm_ref, bd_ref, bp_ref, bm_ref, bf_ref,
                        lq1_ref, lk1_ref, lq2_ref, lk2_ref, sg_ref, o_ref,
                        m_s, l_s, acc_s, kpad, vpad, *, lam_init):
    j = pl.program_id(2)
    T = ATT_TILE
    q = q_ref[...]
    qc = [q[:, c * DQK_ATT:(c + 1) * DQK_ATT].astype(BF16) for c in range(2)]

    m_s[...] = jnp.full(m_s.shape, NEG, F32)
    l_s[...] = jnp.zeros(l_s.shape, F32)
    acc_s[...] = jnp.zeros(acc_s.shape, F32)

    def update(kt, vt, bias):
        vb = vt.astype(BF16)
        for c in range(2):
            kc = kt[:, c * DQK_ATT:(c + 1) * DQK_ATT].astype(BF16)
            s = _dot_nt(qc[c], kc) + bias
            m_old = m_s[c]
            m_new = jnp.maximum(m_old, jnp.max(s, axis=-1, keepdims=True))
            alpha = jnp.exp(m_old - m_new)
            p = jnp.exp(s - m_new)
            l_s[c] = alpha * l_s[c] + jnp.sum(p, axis=-1, keepdims=True)
            acc_s[c] = alpha * acc_s[c] + _dot(p.astype(BF16), vb)
            m_s[c] = m_new

    _pad_rows(kpad, km_ref)
    _pad_rows(vpad, vm_ref)
    update(kpad[...], vpad[...], bm_ref[...])

    def far_body(i, carry):
        start = pl.multiple_of(i * T, T)
        update(k_ref[pl.ds(start, T), :], v_ref[pl.ds(start, T), :], bf_ref[0:1, :])
        return carry

    lax.fori_loop(0, jnp.maximum(j - 1, 0), far_body, 0)

    @pl.when(j >= 1)
    def _():
        start = pl.multiple_of((j - 1) * T, T)
        update(k_ref[pl.ds(start, T), :], v_ref[pl.ds(start, T), :], bp_ref[...])

    start = pl.multiple_of(j * T, T)
    update(k_ref[pl.ds(start, T), :], v_ref[pl.ds(start, T), :], bd_ref[...])

    lam = _diff_lambda(lq1_ref, lk1_ref, lq2_ref, lk2_ref, lam_init)
    o = acc_s[0] / l_s[0] - lam * (acc_s[1] / l_s[1])
    o_ref[...] = _subln(o, sg_ref, lam_init)


def _attn_prompt(z, bias, lam_vecs, sg, B, S, lam_init):
    T = ATT_TILE
    nq = S // T
    qkw = 2 * DQK_ATT
    kcol = ATT_QK // qkw
    vcol = 2 * ATT_QK // DV_ATT
    mrow = B * S // N_META
    b_diag, b_prev, b_meta, b_far, _, _ = bias
    vec = pl.BlockSpec((1, DQK_ATT), lambda b, h, j: (0, 0))
    return pl.pallas_call(
        functools.partial(_attn_prompt_kernel, lam_init=lam_init),
        grid=(B, H_ATT, nq),
        in_specs=[
            pl.BlockSpec((T, qkw), lambda b, h, j: (b * nq + j, h)),
            pl.BlockSpec((S, qkw), lambda b, h, j: (b, kcol + h)),
            pl.BlockSpec((S, DV_ATT), lambda b, h, j: (b, vcol + h)),
            pl.BlockSpec((N_META, qkw), lambda b, h, j: (mrow + b, kcol + h)),
            pl.BlockSpec((N_META, DV_ATT), lambda b, h, j: (mrow + b, vcol + h)),
            pl.BlockSpec((None, T, T), lambda b, h, j: (h, 0, 0)),
            pl.BlockSpec((None, T, T), lambda b, h, j: (h, 0, 0)),
            pl.BlockSpec((None, None, T, LANES), lambda b, h, j: (h, jnp.minimum(j, 1), 0, 0)),
            pl.BlockSpec((None, 8, T), lambda b, h, j: (h, 0, 0)),
            vec, vec, vec, vec,
            pl.BlockSpec((1, DV_ATT), lambda b, h, j: (0, 0)),
        ],
        out_specs=pl.BlockSpec((T, DV_ATT), lambda b, h, j: (b * nq + j, h)),
        out_shape=jax.ShapeDtypeStruct((B * S, ATT_V), F32),
        scratch_shapes=[
            pltpu.VMEM((2, T, 1), F32),
            pltpu.VMEM((2, T, 1), F32),
            pltpu.VMEM((2, T, DV_ATT), F32),
            pltpu.VMEM((LANES, qkw), F32),
            pltpu.VMEM((LANES, DV_ATT), F32),
        ],
        compiler_params=pltpu.CompilerParams(
            dimension_semantics=("arbitrary", "arbitrary", "arbitrary"), vmem_limit_bytes=VMEM_LIMIT),
        name="attn_prompt",
    )(z, z, z, z, z, b_diag, b_prev, b_meta, b_far, *lam_vecs, sg)


def _attn_small_kernel(*refs, lam_init, past_len):
    if past_len:
        (q_ref, kc_ref, vc_ref, kn_ref, vn_ref, b_ref,
         lq1_ref, lk1_ref, lq2_ref, lk2_ref, sg_ref, o_ref, kpad, vpad) = refs
    else:
        (q_ref, kn_ref, vn_ref, b_ref,
         lq1_ref, lk1_ref, lq2_ref, lk2_ref, sg_ref, o_ref, kpad, vpad) = refs
    q = q_ref[...]
    _pad_rows(kpad, kn_ref)
    _pad_rows(vpad, vn_ref)
    kn = kpad[...]
    vn = vpad[...].astype(BF16)
    outs = []
    for c in range(2):
        sl = slice(c * DQK_ATT, (c + 1) * DQK_ATT)
        qc = q[:, sl].astype(BF16)
        s_new = _dot_nt(qc, kn[:, sl].astype(BF16)) + b_ref[:, past_len:]
        m = jnp.max(s_new, axis=-1, keepdims=True)
        if past_len:
            s_old = _dot_nt(qc, kc_ref[:, sl].astype(BF16)) + b_ref[:, :past_len]
            m = jnp.maximum(m, jnp.max(s_old, axis=-1, keepdims=True))
        p_new = jnp.exp(s_new - m)
        l = jnp.sum(p_new, axis=-1, keepdims=True)
        o = _dot(p_new.astype(BF16), vn)
        if past_len:
            p_old = jnp.exp(s_old - m)
            l = l + jnp.sum(p_old, axis=-1, keepdims=True)
            o = o + _dot(p_old.astype(BF16), vc_ref[...].astype(BF16))
        outs.append(o / l)
    lam = _diff_lambda(lq1_ref, lk1_ref, lq2_ref, lk2_ref, lam_init)
    o_ref[...] = _subln(outs[0] - lam * outs[1], sg_ref, lam_init)


def _attn_small(z, row0, n_streams, bias, lam_vecs, sg, lam_init, cache=None):
    qkw = 2 * DQK_ATT
    kcol = ATT_QK // qkw
    vcol = 2 * ATT_QK // DV_ATT
    vec = pl.BlockSpec((1, DQK_ATT), lambda b, h: (0, 0))
    in_specs = [pl.BlockSpec((N_META, qkw), lambda b, h: (row0 + b, h))]
    args = [z]
    past_len = 0
    if cache is not None:
        ck, cv, stream0 = cache
        past_len = ck.shape[1]
        in_specs += [pl.BlockSpec((None, past_len, qkw), lambda b, h: (stream0 + b, 0, h)),
                     pl.BlockSpec((None, past_len, DV_ATT), lambda b, h: (stream0 + b, 0, h))]
        args += [ck, cv]
    in_specs += [
        pl.BlockSpec((N_META, qkw), lambda b, h: (row0 + b, kcol + h)),
        pl.BlockSpec((N_META, DV_ATT), lambda b, h: (row0 + b, vcol + h)),
        pl.BlockSpec((None, N_META, bias.shape[-1]), lambda b, h: (h, 0, 0)),
        vec, vec, vec, vec,
        pl.BlockSpec((1, DV_ATT), lambda b, h: (0, 0)),
    ]
    args += [z, z, bias, *lam_vecs, sg]
    return pl.pallas_call(
        functools.partial(_attn_small_kernel, lam_init=lam_init, past_len=past_len),
        grid=(n_streams, H_ATT),
        in_specs=in_specs,
        out_specs=pl.BlockSpec((N_META, DV_ATT), lambda b, h: (b, h)),
        out_shape=jax.ShapeDtypeStruct((n_streams * N_META, ATT_V), F32),
        scratch_shapes=[pltpu.VMEM((LANES, qkw), F32), pltpu.VMEM((LANES, DV_ATT), F32)],
        compiler_params=pltpu.CompilerParams(
            dimension_semantics=("arbitrary", "arbitrary"), vmem_limit_bytes=VMEM_LIMIT),
        name="attn_cached" if cache is not None else "attn_meta",
    )(*args)


def _mlstm_kernel(*refs, n_blocks, has_init):
    refs = list(refs)
    qh_ref, kh_ref, vh_ref, gh_ref = refs[:4]
    pos = 4
    if n_blocks:
        qf_ref, kf_ref, vf_ref, gf_ref = refs[pos:pos + 4]
        pos += 4
    if has_init:
        c0_ref, n0_ref, m0_ref = refs[pos:pos + 3]
        pos += 3
    hh_ref = refs[pos]
    pos += 1
    if n_blocks:
        hf_ref = refs[pos]
        pos += 1
    c_ref, n_ref, m_ref = refs[pos:pos + 3]
    qs, ks, vs, gs = refs[pos + 3:pos + 7]

    t = pl.program_id(1)
    T = M_TILE

    @pl.when(t == 0)
    def _():
        if has_init:
            c_ref[...] = c0_ref[...]
            n_ref[...] = n0_ref[...]
            m_ref[...] = m0_ref[...]
        else:
            c_ref[...] = jnp.zeros(c_ref.shape, F32)
            n_ref[...] = jnp.zeros(n_ref.shape, F32)
            m_ref[...] = jnp.zeros(m_ref.shape, F32)
        nh = qh_ref.shape[0]
        for dst, src in ((qs, qh_ref), (ks, kh_ref), (vs, vh_ref)):
            dst[...] = jnp.zeros(dst.shape, F32)
            dst[0:nh, :] = src[...]
        lane = lax.broadcasted_iota(jnp.int32, gs.shape, 1)
        gs[...] = jnp.where(lane < H_M, NEG, 0.0)
        gs[0:nh, :] = gh_ref[...]

    if n_blocks:
        @pl.when(t > 0)
        def _():
            qs[...] = qf_ref[...]
            ks[...] = kf_ref[...]
            vs[...] = vf_ref[...]
            gs[...] = gf_ref[...]

    g = gs[...]
    row = lax.broadcasted_iota(jnp.int32, (T, T), 0)
    colid = lax.broadcasted_iota(jnp.int32, (T, T), 1)
    causal = row >= colid
    tril = jnp.where(causal, 1.0, 0.0).astype(F32)
    cum = jnp.dot(tril, g, preferred_element_type=F32, precision=lax.Precision.HIGHEST)
    g_t = g.T
    cum_t = cum.T
    last = cum[T - 1:T, :]

    for h in range(H_M):
        q = qs[:, h * DQK_M:(h + 1) * DQK_M]
        k = ks[:, h * DQK_M:(h + 1) * DQK_M]
        v = vs[:, h * DV_M:(h + 1) * DV_M]
        qb = q.astype(BF16)
        kb = k.astype(BF16)
        m_old = m_ref[h, 0:1, 0:1]
        c_old = c_ref[h]
        n_old = n_ref[h]
        bh_c = _col(cum, H_M + h)
        ih_c = _col(g, h)
        bh_r = cum_t[H_M + h:H_M + h + 1, :]
        ih_r = g_t[h:h + 1, :]
        d = jnp.where(causal, bh_c - bh_r + ih_r, NEG)
        inter = bh_c + m_old
        mt = jnp.maximum(inter, jnp.max(d, axis=-1, keepdims=True))
        a = jnp.exp(d - mt) * _dot_nt(qb, kb)
        iw = jnp.exp(inter - mt)
        num = _dot(a.astype(BF16), v.astype(BF16)) + iw * _dot_nt(qb, c_old.astype(BF16))
        den = jnp.sum(a, axis=-1, keepdims=True) + iw * jnp.sum(q * n_old, axis=-1, keepdims=True)
        den = jnp.maximum(jnp.abs(den), jnp.exp(-mt))
        hout = num / den
        sl = slice(h * DV_M, (h + 1) * DV_M)

        @pl.when(t == 0)
        def _():
            hh_ref[:, sl] = hout[0:hh_ref.shape[0], :]

        if n_blocks:
            @pl.when(t > 0)
            def _():
                hf_ref[:, sl] = hout

        bl = _col(last, H_M + h)
        g_c = bl - bh_c + ih_c
        m_new = jnp.maximum(bl + m_old, jnp.max(g_c, axis=0, keepdims=True))
        sw = jnp.exp(g_c - m_new)
        dec = jnp.exp(bl + m_old - m_new)
        c_ref[h] = dec * c_old + _dot(v.T.astype(BF16), (sw * k).astype(BF16))
        n_ref[h] = dec * n_old + jnp.sum(sw * k, axis=0, keepdims=True)
        m_ref[h] = jnp.broadcast_to(m_new, m_ref.shape[1:])


def _mlstm(z, gates, row0, n_streams, n_blocks, frames_per_stream=0, init=None):
    T = M_TILE
    qcol = (2 * ATT_QK + ATT_V) // M_QK
    vcol = (2 * ATT_QK + ATT_V + 2 * M_QK) // M_V
    nh = N_META
    in_specs = [
        pl.BlockSpec((nh, M_QK), lambda b, t: (row0 + b, qcol)),
        pl.BlockSpec((nh, M_QK), lambda b, t: (row0 + b, qcol + 1)),
        pl.BlockSpec((nh, M_V), lambda b, t: (row0 + b, vcol)),
        pl.BlockSpec((nh, LANES), lambda b, t: (row0 + b, 0)),
    ]
    args = [z, z, z, gates]
    if n_blocks:
        fidx = lambda b, t: b * n_blocks + jnp.maximum(t - 1, 0)
        in_specs += [
            pl.BlockSpec((T, M_QK), lambda b, t: (fidx(b, t), qcol)),
            pl.BlockSpec((T, M_QK), lambda b, t: (fidx(b, t), qcol + 1)),
            pl.BlockSpec((T, M_V), lambda b, t: (fidx(b, t), vcol)),
            pl.BlockSpec((T, LANES), lambda b, t: (fidx(b, t), 0)),
        ]
        args += [z, z, z, gates]
    state_specs = (
        pl.BlockSpec((None, H_M, DV_M, DQK_M), lambda b, t: (b, 0, 0, 0)),
        pl.BlockSpec((None, H_M, 1, DQK_M), lambda b, t: (b, 0, 0, 0)),
        pl.BlockSpec((None, H_M, 8, LANES), lambda b, t: (b, 0, 0, 0)),
    )
    if init is not None:
        in_specs += list(state_specs)
        args += list(init)
    out_specs = [pl.BlockSpec((nh, M_V), lambda b, t: (b, 0))]
    out_shape = [jax.ShapeDtypeStruct((n_streams * nh, M_V), F32)]
    if n_blocks:
        out_specs.append(pl.BlockSpec((T, M_V), lambda b, t: (fidx(b, t), 0)))
        out_shape.append(jax.ShapeDtypeStruct((n_streams * frames_per_stream, M_V), F32))
    out_specs += list(state_specs)
    out_shape += [
        jax.ShapeDtypeStruct((n_streams, H_M, DV_M, DQK_M), F32),
        jax.ShapeDtypeStruct((n_streams, H_M, 1, DQK_M), F32),
        jax.ShapeDtypeStruct((n_streams, H_M, 8, LANES), F32),
    ]
    return pl.pallas_call(
        functools.partial(_mlstm_kernel, n_blocks=n_blocks, has_init=init is not None),
        grid=(n_streams, 1 + n_blocks),
        in_specs=in_specs,
        out_specs=tuple(out_specs),
        out_shape=tuple(out_shape),
        scratch_shapes=[
            pltpu.VMEM((T, M_QK), F32), pltpu.VMEM((T, M_QK), F32),
            pltpu.VMEM((T, M_V), F32), pltpu.VMEM((T, LANES), F32),
        ],
        compiler_params=pltpu.CompilerParams(
            dimension_semantics=("arbitrary", "arbitrary"), vmem_limit_bytes=VMEM_LIMIT),
        name="mlstm_prompt" if n_blocks else "mlstm_cached",
    )(*args)


def _merge_kernel(x_ref, att_ref, hm_ref, z_ref, mg_ref, wo_ref, o_ref):
    hm = hm_ref[...]
    parts = []
    for h in range(H_M):
        sl = slice(h * DV_M, (h + 1) * DV_M)
        blk = hm[:, sl]
        ms = jnp.mean(blk * blk, axis=-1, keepdims=True)
        parts.append(blk * lax.rsqrt(ms + EPS) * mg_ref[:, sl] * z_ref[:, sl])
    hmn = jnp.concatenate(parts, axis=1).astype(BF16)
    y = _dot(att_ref[...].astype(BF16), wo_ref[0:ATT_V, :]) + _dot(hmn, wo_ref[ATT_V:, :])
    o_ref[...] = x_ref[...] + y


def _merge(x_rows, att, hm, z, mg, w_out, bm):
    R, D = x_rows.shape
    ogcol = (N_MAIN - M_V) // M_V
    return pl.pallas_call(
        _merge_kernel,
        grid=(R // bm,),
        in_specs=[
            pl.BlockSpec((bm, D), lambda i: (i, 0)),
            pl.BlockSpec((bm, ATT_V), lambda i: (i, 0)),
            pl.BlockSpec((bm, M_V), lambda i: (i, 0)),
            pl.BlockSpec((bm, M_V), lambda i: (i, ogcol)),
            pl.BlockSpec((1, M_V), lambda i: (0, 0)),
            pl.BlockSpec((D, D), lambda i: (0, 0)),
        ],
        out_specs=pl.BlockSpec((bm, D), lambda i: (i, 0)),
        out_shape=jax.ShapeDtypeStruct((R, D), F32),
        compiler_params=pltpu.CompilerParams(
            dimension_semantics=("arbitrary",), vmem_limit_bytes=VMEM_LIMIT),
        name="merge_out_proj",
    )(x_rows, att, hm, z, mg, w_out)


def _ffn_kernel(x_ref, g2_ref, wu_ref, wd_ref, o_ref, xn_s):
    k = pl.program_id(1)

    @pl.when(k == 0)
    def _():
        x = x_ref[...]
        ms = jnp.mean(x * x, axis=-1, keepdims=True)
        xn_s[...] = (x * lax.rsqrt(ms + EPS) * g2_ref[...]).astype(BF16)
        o_ref[...] = x

    u = jnp.maximum(_dot(xn_s[...], wu_ref[...]), 0.0)
    o_ref[...] += _dot((u * u).astype(BF16), wd_ref[...])


def _ffn(x_rows, g2, w_up, w_down, bm, fc):
    R, D = x_rows.shape
    FF = w_up.shape[1]
    return pl.pallas_call(
        _ffn_kernel,
        grid=(R // bm, FF // fc),
        in_specs=[
            pl.BlockSpec((bm, D), lambda i, k: (i, 0)),
            pl.BlockSpec((1, D), lambda i, k: (0, 0)),
            pl.BlockSpec((D, fc), lambda i, k: (0, k)),
            pl.BlockSpec((fc, D), lambda i, k: (k, 0)),
        ],
        out_specs=pl.BlockSpec((bm, D), lambda i, k: (i, 0)),
        out_shape=jax.ShapeDtypeStruct((R, D), F32),
        scratch_shapes=[pltpu.VMEM((bm, D), BF16)],
        compiler_params=pltpu.CompilerParams(
            dimension_semantics=("arbitrary", "arbitrary"), vmem_limit_bytes=VMEM_LIMIT),
        name="ffn",
    )(x_rows, g2, w_up, w_down)


def kernel(x_prompt, x_sample, cache_k, cache_v, state_C, state_n, state_m, meta_tokens, rel_bias, norm1_g, w_in, b_i, b_f, q_norm_g, k_norm_g, lambda_q1, lambda_k1, lambda_q2, lambda_k2, subln_g, mnorm_g, w_out, norm2_g, w_up, w_down):
    B, S, D = x_prompt.shape
    Bd, Td, _ = x_sample.shape
    depth = w_in.shape[0]
    Pc = cache_k.shape[2]
    assert Td == N_META and S % ATT_TILE == 0 and S % M_TILE == 0 and Pc % LANES == 0
    assert ATT_TILE % CHUNK == 0 and CHUNK & (CHUNK - 1) == 0 and ATT_TILE >= MAX_DISTANCE

    n_frames = B * S
    n_meta = B * N_META
    n_samp = Bd * Td
    n_real = n_frames + n_meta + n_samp
    R = -(-n_real // ROW_ALIGN) * ROW_ALIGN
    mrow = n_frames // N_META
    srow = (n_frames + n_meta) // N_META

    meta = jnp.broadcast_to(meta_tokens.astype(F32)[None], (B, N_META, D)).reshape(n_meta, D)
    x_rows = jnp.concatenate([
        x_prompt.reshape(n_frames, D), meta, x_sample.reshape(n_samp, D),
        jnp.zeros((R - n_real, D), F32)], axis=0)

    bias = _bias_tiles(rel_bias.astype(F32), Pc, Td)
    ck = cache_k.astype(F32).reshape(depth * Bd, Pc, ATT_QK)
    cv = cache_v.astype(F32).reshape(depth * Bd, Pc, ATT_V)
    zeros_pad_att = jnp.zeros((R - n_real, ATT_V), F32)

    g_off = 2 * ATT_QK + ATT_V + 2 * M_QK + M_V
    outs = {name: [] for name in ("kp", "vp", "Cp", "np", "mp", "ks", "vs", "Cs", "ns", "ms")}

    for l in range(depth):
        lam_init = 0.8 - 0.6 * math.exp(-0.3 * l)
        w = w_in[l]
        w_main = jnp.concatenate([w[:, :g_off], w[:, g_off + 2 * H_M:]], axis=1).astype(BF16)
        w_gate = jnp.pad(w[:, g_off:g_off + 2 * H_M], ((0, 0), (0, LANES - 2 * H_M))).astype(BF16)
        gate_bias = jnp.pad(jnp.concatenate([b_i[l], b_f[l]]).astype(F32), (0, LANES - 2 * H_M))[None]
        lam_vecs = [v[l].astype(F32)[None] for v in (lambda_q1, lambda_k1, lambda_q2, lambda_k2)]
        sg = subln_g[l].astype(F32)[None]

        z, gates = _inproj(x_rows, norm1_g[l].astype(F32)[None], w_main, w_gate, gate_bias,
                           q_norm_g[l].astype(F32)[None], k_norm_g[l].astype(F32)[None], bm=528)

        att_f = _attn_prompt(z, bias, lam_vecs, sg, B, S, lam_init)
        att_m = _attn_small(z, mrow, B, bias[4], lam_vecs, sg, lam_init)
        att_s = _attn_small(z, srow, Bd, bias[5], lam_vecs, sg, lam_init, cache=(ck, cv, l * Bd))
        att = jnp.concatenate([att_f, att_m, att_s, zeros_pad_att], axis=0)

        hm_m, hm_f, C_p, n_p, m_p = _mlstm(z, gates, mrow, B, S // M_TILE, frames_per_stream=S)
        init = (state_C[l].astype(F32), state_n[l].astype(F32)[:, :, None, :],
                jnp.broadcast_to(state_m[l].astype(F32)[:, :, None, None], (Bd, H_M, 8, LANES)))
        hm_s, C_s, n_s, m_s = _mlstm(z, gates, srow, Bd, 0, init=init)
        hm = jnp.concatenate([hm_f, hm_m, hm_s, zeros_pad_att], axis=0)

        x1 = _merge(x_rows, att, hm, z, mnorm_g[l].astype(F32)[None], w_out[l].astype(BF16), bm=384)
        x_rows = _ffn(x1, norm2_g[l].astype(F32)[None], w_up[l].astype(BF16), w_down[l].astype(BF16),
                      bm=704, fc=512)

        zk = z[:, ATT_QK:2 * ATT_QK]
        zv = z[:, 2 * ATT_QK:2 * ATT_QK + ATT_V]

        def prompt_cache(a, width):
            fr = a[:n_frames].reshape(B, S, H_ATT, width)
            me = a[n_frames:n_frames + n_meta].reshape(B, N_META, H_ATT, width)
            return jnp.concatenate([me, fr], axis=1)

        outs["kp"].append(prompt_cache(zk, 2 * DQK_ATT))
        outs["vp"].append(prompt_cache(zv, DV_ATT))
        outs["Cp"].append(C_p)
        outs["np"].append(n_p[:, :, 0, :])
        outs["mp"].append(m_p[:, :, 0, 0])
        outs["ks"].append(zk[n_frames + n_meta:n_real].reshape(Bd, Td, H_ATT, 2 * DQK_ATT))
        outs["vs"].append(zv[n_frames + n_meta:n_real].reshape(Bd, Td, H_ATT, DV_ATT))
        outs["Cs"].append(C_s)
        outs["ns"].append(n_s[:, :, 0, :])
        outs["ms"].append(m_s[:, :, 0, 0])

    y_prompt = x_rows[:n_frames].reshape(B, S, D)
    y_sample = x_rows[n_frames + n_meta:n_real].reshape(Bd, Td, D)
    return (y_prompt, y_sample,
            jnp.stack(outs["kp"]), jnp.stack(outs["vp"]), jnp.stack(outs["Cp"]), jnp.stack(outs["np"]),
            jnp.stack(outs["mp"]),
            jnp.stack(outs["ks"]), jnp.stack(outs["vs"]), jnp.stack(outs["Cs"]), jnp.stack(outs["ns"]),
            jnp.stack(outs["ms"]))
```

```python
import functools
import math

import jax
import jax.numpy as jnp
from jax import lax
from jax.experimental import pallas as pl
from jax.experimental.pallas import tpu as pltpu

F32 = jnp.float32
BF16 = jnp.bfloat16

CHUNK = 64
N_META = 16
H_ATT = 4
DQK_ATT = 128
DV_ATT = 256
H_M = 4
DQK_M = 128
DV_M = 256
N_BUCKETS = 32
MAX_DISTANCE = 128
EPS = 1e-6
NEG = -1e30

ATT_QK = H_ATT * 2 * DQK_ATT
ATT_V = H_ATT * DV_ATT
M_QK = H_M * DQK_M
M_V = H_M * DV_M
N_MAIN = 2 * ATT_QK + ATT_V + 2 * M_QK + M_V + M_V

LANES = 128
ATT_TILE = 256
M_TILE = 256
ROW_ALIGN = 768
VMEM_LIMIT = 56 * 1024 * 1024

_NT = (((1,), (1,)), ((), ()))


def _dot(a, b):
    return jnp.dot(a, b, preferred_element_type=F32)


def _dot_nt(a, b):
    return lax.dot_general(a, b, _NT, preferred_element_type=F32)


def _log_sigmoid(x):
    return jnp.minimum(x, 0.0) - jnp.log1p(jnp.exp(-jnp.abs(x)))


def _col(a, idx):
    lane = lax.broadcasted_iota(jnp.int32, a.shape, 1)
    return jnp.sum(jnp.where(lane == idx, a, 0.0), axis=1, keepdims=True)


def _bias_from_rel(rel, rb_ref, h):
    nb = N_BUCKETS // 2
    max_exact = nb // 2
    ret = jnp.where(rel > 0, nb, 0)
    n = jnp.abs(rel)
    nf = jnp.maximum(n, 1).astype(F32)
    large = max_exact + (jnp.log(nf / max_exact) / math.log(MAX_DISTANCE / max_exact)
                         * (nb - max_exact)).astype(jnp.int32)
    large = jnp.minimum(large, nb - 1)
    bucket = ret + jnp.where(n < max_exact, n, large)
    out = jnp.zeros(rel.shape, F32)
    for b in range(N_BUCKETS):
        out = jnp.where(bucket == b, rb_ref[b, h], out)
    return out


def _bias_kernel(rb_ref, diag_ref, prev_ref, meta_ref, far_ref, mm_ref, samp_ref, *, past_len, dec_seq):
    h = pl.program_id(0)
    shift = CHUNK.bit_length() - 1
    T = ATT_TILE

    qi = lax.broadcasted_iota(jnp.int32, (T, T), 0)
    kj = lax.broadcasted_iota(jnp.int32, (T, T), 1)
    bias = _bias_from_rel(kj - qi, rb_ref, h)
    diag_ref[...] = jnp.where((kj >> shift) <= (qi >> shift), bias, NEG)
    prev_ref[...] = _bias_from_rel(kj - T - qi, rb_ref, h)
    far_ref[...] = _bias_from_rel(jnp.full(far_ref.shape, -(2 * MAX_DISTANCE + 1), jnp.int32), rb_ref, h)

    qi = lax.broadcasted_iota(jnp.int32, (T, LANES), 0)
    kj = lax.broadcasted_iota(jnp.int32, (T, LANES), 1)
    for var in range(2):
        bias = _bias_from_rel(kj - (N_META + qi + var * T), rb_ref, h)
        meta_ref[var] = jnp.where(kj < N_META, bias, NEG)

    qi = lax.broadcasted_iota(jnp.int32, (N_META, LANES), 0)
    kj = lax.broadcasted_iota(jnp.int32, (N_META, LANES), 1)
    mm_ref[...] = jnp.where(kj < N_META, _bias_from_rel(kj - qi, rb_ref, h), NEG)

    width = samp_ref.shape[-1]
    qi = lax.broadcasted_iota(jnp.int32, (dec_seq, width), 0) + past_len
    kj = lax.broadcasted_iota(jnp.int32, (dec_seq, width), 1)
    bias = _bias_from_rel(kj - qi, rb_ref, h)
    ok = ((kj >> shift) <= (qi >> shift)) & (kj < past_len + dec_seq)
    samp_ref[...] = jnp.where(ok, bias, NEG)


def _bias_tiles(rel_bias, past_len, dec_seq):
    T = ATT_TILE
    sw = past_len + LANES
    out_shape = (
        jax.ShapeDtypeStruct((H_ATT, T, T), F32),
        jax.ShapeDtypeStruct((H_ATT, T, T), F32),
        jax.ShapeDtypeStruct((H_ATT, 2, T, LANES), F32),
        jax.ShapeDtypeStruct((H_ATT, 8, T), F32),
        jax.ShapeDtypeStruct((H_ATT, N_META, LANES), F32),
        jax.ShapeDtypeStruct((H_ATT, dec_seq, sw), F32),
    )
    out_specs = (
        pl.BlockSpec((None, T, T), lambda h: (h, 0, 0)),
        pl.BlockSpec((None, T, T), lambda h: (h, 0, 0)),
        pl.BlockSpec((None, 2, T, LANES), lambda h: (h, 0, 0, 0)),
        pl.BlockSpec((None, 8, T), lambda h: (h, 0, 0)),
        pl.BlockSpec((None, N_META, LANES), lambda h: (h, 0, 0)),
        pl.BlockSpec((None, dec_seq, sw), lambda h: (h, 0, 0)),
    )
    return pl.pallas_call(
        functools.partial(_bias_kernel, past_len=past_len, dec_seq=dec_seq),
        grid=(H_ATT,),
        in_specs=[pl.BlockSpec(memory_space=pltpu.SMEM)],
        out_specs=out_specs,
        out_shape=out_shape,
        name="rel_bias_tiles",
    )(rel_bias)


def _inproj_kernel(x_ref, g1_ref, w_ref, wg_ref, gb_ref, qg_ref, kg_ref, z_ref, g_ref, xn_s):
    j = pl.program_id(1)

    @pl.when(j == 0)
    def _():
        x = x_ref[...]
        ms = jnp.mean(x * x, axis=-1, keepdims=True)
        xb = (x * lax.rsqrt(ms + EPS) * g1_ref[...]).astype(BF16)
        xn_s[...] = xb
        gates = _dot(xb, wg_ref[...]) + gb_ref[...]
        lane = lax.broadcasted_iota(jnp.int32, gates.shape, 1)
        g_ref[...] = jnp.where(lane < H_M, gates, _log_sigmoid(gates))

    acc = _dot(xn_s[...], w_ref[...])
    bn = acc.shape[1]

    def group_norm(gain_ref, scale):
        for gi in range(bn // DQK_ATT):
            sl = slice(gi * DQK_ATT, (gi + 1) * DQK_ATT)
            blk = acc[:, sl]
            ms = jnp.mean(blk * blk, axis=-1, keepdims=True)
            y = blk * lax.rsqrt(ms + EPS) * gain_ref[...]
            z_ref[:, sl] = y if scale is None else y * scale

    @pl.when(j == 0)
    def _():
        group_norm(qg_ref, DQK_ATT ** -0.5)

    @pl.when(j == 1)
    def _():
        group_norm(kg_ref, None)

    @pl.when((j == 2) | (j == 4))
    def _():
        z_ref[...] = acc

    @pl.when(j == 3)
    def _():
        z_ref[:, :M_QK] = acc[:, :M_QK]
        z_ref[:, M_QK:] = acc[:, M_QK:] * (DQK_M ** -0.5)

    @pl.when(j == 5)
    def _():
        z_ref[...] = jax.nn.sigmoid(acc)


def _inproj(x_rows, g1, w_main, w_gate, gate_bias, qg, kg, bm):
    R, D = x_rows.shape
    bn = ATT_QK
    assert N_MAIN % bn == 0 and 2 * M_QK == bn and ATT_V == bn and M_V == bn
    return pl.pallas_call(
        _inproj_kernel,
        grid=(R // bm, N_MAIN // bn),
        in_specs=[
            pl.BlockSpec((bm, D), lambda i, j: (i, 0)),
            pl.BlockSpec((1, D), lambda i, j: (0, 0)),
            pl.BlockSpec((D, bn), lambda i, j: (0, j)),
            pl.BlockSpec((D, LANES), lambda i, j: (0, 0)),
            pl.BlockSpec((1, LANES), lambda i, j: (0, 0)),
            pl.BlockSpec((1, DQK_ATT), lambda i, j: (0, 0)),
            pl.BlockSpec((1, DQK_ATT), lambda i, j: (0, 0)),
        ],
        out_specs=(
            pl.BlockSpec((bm, bn), lambda i, j: (i, j)),
            pl.BlockSpec((bm, LANES), lambda i, j: (i, 0)),
        ),
        out_shape=(
            jax.ShapeDtypeStruct((R, N_MAIN), F32),
            jax.ShapeDtypeStruct((R, LANES), F32),
        ),
        scratch_shapes=[pltpu.VMEM((bm, D), BF16)],
        compiler_params=pltpu.CompilerParams(
            dimension_semantics=("arbitrary", "arbitrary"), vmem_limit_bytes=VMEM_LIMIT),
        name="in_proj",
    )(x_rows, g1, w_main, w_gate, gate_bias, qg, kg)


def _diff_lambda(lq1_ref, lk1_ref, lq2_ref, lk2_ref, lam_init):
    a = jnp.sum(lq1_ref[...] * lk1_ref[...], axis=-1, keepdims=True)
    b = jnp.sum(lq2_ref[...] * lk2_ref[...], axis=-1, keepdims=True)
    return jnp.exp(a) - jnp.exp(b) + lam_init


def _subln(o, sg_ref, lam_init):
    ms = jnp.mean(o * o, axis=-1, keepdims=True)
    return o * lax.rsqrt(ms + EPS) * sg_ref[...] * (1.0 - lam_init)


def _pad_rows(dst, src):
    dst[...] = jnp.zeros(dst.shape, dst.dtype)
    dst[0:src.shape[0], :] = src[...]


def _attn_prompt_kernel(q_ref, k_ref, v_ref, km_ref, vm_ref, bd_ref, bp_ref, bm_ref, bf_ref,
                        lq1_ref, lk1_ref, lq2_ref, lk2_ref, sg_ref, o_ref,
                        kb_s, vb_s, kmb_s, vmb_s, *, lam_init):
    T = ATT_TILE
    n_tiles = q_ref.shape[0] // T
    kb_s[...] = k_ref[...].astype(BF16)
    vb_s[...] = v_ref[...].astype(BF16)
    kmb_s[...] = jnp.zeros(kmb_s.shape, BF16)
    vmb_s[...] = jnp.zeros(vmb_s.shape, BF16)
    kmb_s[0:N_META, :] = km_ref[...].astype(BF16)
    vmb_s[0:N_META, :] = vm_ref[...].astype(BF16)
    lam = _diff_lambda(lq1_ref, lk1_ref, lq2_ref, lk2_ref, lam_init)
    far = bf_ref[0:1, 0:1]

    for j in range(n_tiles):
        rows = slice(j * T, (j + 1) * T)
        regions = [(kmb_s, vmb_s, slice(0, LANES), bm_ref[min(j, 1)])]
        if j >= 2:
            regions.append((kb_s, vb_s, slice(0, (j - 1) * T), far))
        if j >= 1:
            regions.append((kb_s, vb_s, slice((j - 1) * T, j * T), bp_ref[...]))
        regions.append((kb_s, vb_s, rows, bd_ref[...]))

        probs, denom = [], []
        for c in range(2):
            cols = slice(c * DQK_ATT, (c + 1) * DQK_ATT)
            qc = q_ref[rows, cols].astype(BF16)
            ss = [_dot_nt(qc, kr[rs, cols]) + br for kr, _, rs, br in regions]
            m = functools.reduce(jnp.maximum, [jnp.max(s, axis=-1, keepdims=True) for s in ss])
            pc = [jnp.exp(s - m) for s in ss]
            denom.append(functools.reduce(jnp.add, [jnp.sum(p, axis=-1, keepdims=True) for p in pc]))
            probs.append(pc)
        a0 = 1.0 / denom[0]
        a1 = lam / denom[1]
        o = None
        for r, (_, vr, rs, _) in enumerate(regions):
            w = (probs[0][r] * a0 - probs[1][r] * a1).astype(BF16)
            t = _dot(w, vr[rs, :])
            o = t if o is None else o + t
        o_ref[rows, :] = _subln(o, sg_ref, lam_init)


def _attn_prompt(z, bias, lam_vecs, sg, B, S, lam_init):
    T = ATT_TILE
    qkw = 2 * DQK_ATT
    kcol = ATT_QK // qkw
    vcol = 2 * ATT_QK // DV_ATT
    mrow = B * S // N_META
    b_diag, b_prev, b_meta, b_far, _, _ = bias
    vec = pl.BlockSpec((1, DQK_ATT), lambda b, h: (0, 0))
    return pl.pallas_call(
        functools.partial(_attn_prompt_kernel, lam_init=lam_init),
        grid=(B, H_ATT),
        in_specs=[
            pl.BlockSpec((S, qkw), lambda b, h: (b, h)),
            pl.BlockSpec((S, qkw), lambda b, h: (b, kcol + h)),
            pl.BlockSpec((S, DV_ATT), lambda b, h: (b, vcol + h)),
            pl.BlockSpec((N_META, qkw), lambda b, h: (mrow + b, kcol + h)),
            pl.BlockSpec((N_META, DV_ATT), lambda b, h: (mrow + b, vcol + h)),
            pl.BlockSpec((None, T, T), lambda b, h: (h, 0, 0)),
            pl.BlockSpec((None, T, T), lambda b, h: (h, 0, 0)),
            pl.BlockSpec((None, 2, T, LANES), lambda b, h: (h, 0, 0, 0)),
            pl.BlockSpec((None, 8, T), lambda b, h: (h, 0, 0)),
            vec, vec, vec, vec,
            pl.BlockSpec((1, DV_ATT), lambda b, h: (0, 0)),
        ],
        out_specs=pl.BlockSpec((S, DV_ATT), lambda b, h: (b, h)),
        out_shape=jax.ShapeDtypeStruct((B * S, ATT_V), F32),
        scratch_shapes=[
            pltpu.VMEM((S, qkw), BF16),
            pltpu.VMEM((S, DV_ATT), BF16),
            pltpu.VMEM((LANES, qkw), BF16),
            pltpu.VMEM((LANES, DV_ATT), BF16),
        ],
        compiler_params=pltpu.CompilerParams(
            dimension_semantics=("arbitrary", "arbitrary"), vmem_limit_bytes=VMEM_LIMIT),
        name="attn_prompt",
    )(z, z, z, z, z, b_diag, b_prev, b_meta, b_far, *lam_vecs, sg)


def _attn_small_kernel(*refs, lam_init, has_cache):
    if has_cache:
        (q_ref, kc_ref, vc_ref, kn_ref, vn_ref, b_ref,
         lq1_ref, lk1_ref, lq2_ref, lk2_ref, sg_ref, o_ref, kpad, vpad) = refs
        past_len = kc_ref.shape[0]
    else:
        (q_ref, kn_ref, vn_ref, b_ref,
         lq1_ref, lk1_ref, lq2_ref, lk2_ref, sg_ref, o_ref, kpad, vpad) = refs
        past_len = 0
    lam = _diff_lambda(lq1_ref, lk1_ref, lq2_ref, lk2_ref, lam_init)
    _pad_rows(kpad, kn_ref)
    _pad_rows(vpad, vn_ref)
    for h in range(H_ATT):
        vn = vpad[:, h * DV_ATT:(h + 1) * DV_ATT].astype(BF16)
        outs = []
        for c in range(2):
            sl = slice(h * 2 * DQK_ATT + c * DQK_ATT, h * 2 * DQK_ATT + (c + 1) * DQK_ATT)
            qc = q_ref[:, sl].astype(BF16)
            s_new = _dot_nt(qc, kpad[:, sl].astype(BF16)) + b_ref[h, :, past_len:]
            m = jnp.max(s_new, axis=-1, keepdims=True)
            if has_cache:
                kc = kc_ref[:, h, c * DQK_ATT:(c + 1) * DQK_ATT].astype(BF16)
                s_old = _dot_nt(qc, kc) + b_ref[h, :, :past_len]
                m = jnp.maximum(m, jnp.max(s_old, axis=-1, keepdims=True))
            p_new = jnp.exp(s_new - m)
            l = jnp.sum(p_new, axis=-1, keepdims=True)
            o = _dot(p_new.astype(BF16), vn)
            if has_cache:
                p_old = jnp.exp(s_old - m)
                l = l + jnp.sum(p_old, axis=-1, keepdims=True)
                o = o + _dot(p_old.astype(BF16), vc_ref[:, h, :].astype(BF16))
            outs.append(o / l)
        o_ref[:, h * DV_ATT:(h + 1) * DV_ATT] = _subln(outs[0] - lam * outs[1], sg_ref, lam_init)


def _attn_small(z, row0, n_streams, bias, lam_vecs, sg, lam_init, cache=None):
    vec = pl.BlockSpec((1, DQK_ATT), lambda b: (0, 0))
    in_specs = [pl.BlockSpec((N_META, ATT_QK), lambda b: (row0 + b, 0))]
    args = [z]
    if cache is not None:
        ck, cv, layer = cache
        past_len = ck.shape[2]
        in_specs += [
            pl.BlockSpec((None, None, past_len, H_ATT, 2 * DQK_ATT), lambda b: (layer, b, 0, 0, 0)),
            pl.BlockSpec((None, None, past_len, H_ATT, DV_ATT), lambda b: (layer, b, 0, 0, 0))]
        args += [ck, cv]
    in_specs += [
        pl.BlockSpec((N_META, ATT_QK), lambda b: (row0 + b, 1)),
        pl.BlockSpec((N_META, ATT_V), lambda b: (row0 + b, 2 * ATT_QK // ATT_V)),
        pl.BlockSpec(bias.shape, lambda b: (0, 0, 0)),
        vec, vec, vec, vec,
        pl.BlockSpec((1, DV_ATT), lambda b: (0, 0)),
    ]
    args += [z, z, bias, *lam_vecs, sg]
    return pl.pallas_call(
        functools.partial(_attn_small_kernel, lam_init=lam_init, has_cache=cache is not None),
        grid=(n_streams,),
        in_specs=in_specs,
        out_specs=pl.BlockSpec((N_META, ATT_V), lambda b: (b, 0)),
        out_shape=jax.ShapeDtypeStruct((n_streams * N_META, ATT_V), F32),
        scratch_shapes=[pltpu.VMEM((LANES, ATT_QK), F32), pltpu.VMEM((LANES, ATT_V), F32)],
        compiler_params=pltpu.CompilerParams(
            dimension_semantics=("arbitrary",), vmem_limit_bytes=VMEM_LIMIT),
        name="attn_cached" if cache is not None else "attn_meta",
    )(*args)


def _mlstm_kernel(*refs, n_blocks, has_init):
    refs = list(refs)
    qh_ref, kh_ref, vh_ref, gh_ref = refs[:4]
    pos = 4
    if n_blocks:
        qf_ref, kf_ref, vf_ref, gf_ref = refs[pos:pos + 4]
        pos += 4
    if has_init:
        c0_ref, n0_ref, m0_ref = refs[pos:pos + 3]
        pos += 3
    hh_ref = refs[pos]
    pos += 1
    if n_blocks:
        hf_ref = refs[pos]
        pos += 1
    c_ref, n_ref, m_ref = refs[pos:pos + 3]
    qs, ks, vs, gs = refs[pos + 3:pos + 7]

    t = pl.program_id(1)
    T = M_TILE

    @pl.when(t == 0)
    def _():
        if has_init:
            c_ref[...] = c0_ref[...]
            n_ref[...] = n0_ref[...]
            m_ref[...] = m0_ref[...]
        else:
            c_ref[...] = jnp.zeros(c_ref.shape, F32)
            n_ref[...] = jnp.zeros(n_ref.shape, F32)
            m_ref[...] = jnp.zeros(m_ref.shape, F32)
        nh = qh_ref.shape[0]
        for dst, src in ((qs, qh_ref), (ks, kh_ref), (vs, vh_ref)):
            dst[...] = jnp.zeros(dst.shape, F32)
            dst[0:nh, :] = src[...]
        lane = lax.broadcasted_iota(jnp.int32, gs.shape, 1)
        gs[...] = jnp.where(lane < H_M, NEG, 0.0)
        gs[0:nh, :] = gh_ref[...]

    if n_blocks:
        @pl.when(t > 0)
        def _():
            qs[...] = qf_ref[...]
            ks[...] = kf_ref[...]
            vs[...] = vf_ref[...]
            gs[...] = gf_ref[...]

    g = gs[...]
    row = lax.broadcasted_iota(jnp.int32, (T, T), 0)
    colid = lax.broadcasted_iota(jnp.int32, (T, T), 1)
    causal = row >= colid
    tril = jnp.where(causal, 1.0, 0.0).astype(F32)
    cum = jnp.dot(tril, g, preferred_element_type=F32, precision=lax.Precision.HIGHEST)
    g_t = g.T
    cum_t = cum.T
    last = cum[T - 1:T, :]

    for h in range(H_M):
        q = qs[:, h * DQK_M:(h + 1) * DQK_M]
        k = ks[:, h * DQK_M:(h + 1) * DQK_M]
        v = vs[:, h * DV_M:(h + 1) * DV_M]
        qb = q.astype(BF16)
        kb = k.astype(BF16)
        m_old = m_ref[h, 0:1, 0:1]
        c_old = c_ref[h]
        n_old = n_ref[h]
        bh_c = _col(cum, H_M + h)
        ih_c = _col(g, h)
        bh_r = cum_t[H_M + h:H_M + h + 1, :]
        ih_r = g_t[h:h + 1, :]
        d = jnp.where(causal, bh_c - bh_r + ih_r, NEG)
        inter = bh_c + m_old
        mt = jnp.maximum(inter, jnp.max(d, axis=-1, keepdims=True))
        a = jnp.exp(d - mt) * _dot_nt(qb, kb)
        iw = jnp.exp(inter - mt)
        num = _dot(a.astype(BF16), v.astype(BF16)) + iw * _dot_nt(qb, c_old.astype(BF16))
        den = jnp.sum(a, axis=-1, keepdims=True) + iw * jnp.sum(q * n_old, axis=-1, keepdims=True)
        den = jnp.maximum(jnp.abs(den), jnp.exp(-mt))
        hout = num / den
        sl = slice(h * DV_M, (h + 1) * DV_M)

        @pl.when(t == 0)
        def _():
            hh_ref[:, sl] = hout[0:hh_ref.shape[0], :]

        if n_blocks:
            @pl.when(t > 0)
            def _():
                hf_ref[:, sl] = hout

        bl = _col(last, H_M + h)
        g_c = bl - bh_c + ih_c
        m_new = jnp.maximum(bl + m_old, jnp.max(g_c, axis=0, keepdims=True))
        sw = jnp.exp(g_c - m_new)
        dec = jnp.exp(bl + m_old - m_new)
        c_ref[h] = dec * c_old + _dot(v.T.astype(BF16), (sw * k).astype(BF16))
        n_ref[h] = dec * n_old + jnp.sum(sw * k, axis=0, keepdims=True)
        m_ref[h] = jnp.broadcast_to(m_new, m_ref.shape[1:])


def _mlstm(z, gates, row0, n_streams, n_blocks, frames_per_stream=0, init=None):
    T = M_TILE
    qcol = (2 * ATT_QK + ATT_V) // M_QK
    vcol = (2 * ATT_QK + ATT_V + 2 * M_QK) // M_V
    nh = N_META
    in_specs = [
        pl.BlockSpec((nh, M_QK), lambda b, t: (row0 + b, qcol)),
        pl.BlockSpec((nh, M_QK), lambda b, t: (row0 + b, qcol + 1)),
        pl.BlockSpec((nh, M_V), lambda b, t: (row0 + b, vcol)),
        pl.BlockSpec((nh, LANES), lambda b, t: (row0 + b, 0)),
    ]
    args = [z, z, z, gates]
    if n_blocks:
        fidx = lambda b, t: b * n_blocks + jnp.maximum(t - 1, 0)
        in_specs += [
            pl.BlockSpec((T, M_QK), lambda b, t: (fidx(b, t), qcol)),
            pl.BlockSpec((T, M_QK), lambda b, t: (fidx(b, t), qcol + 1)),
            pl.BlockSpec((T, M_V), lambda b, t: (fidx(b, t), vcol)),
            pl.BlockSpec((T, LANES), lambda b, t: (fidx(b, t), 0)),
        ]
        args += [z, z, z, gates]
    state_specs = (
        pl.BlockSpec((None, H_M, DV_M, DQK_M), lambda b, t: (b, 0, 0, 0)),
        pl.BlockSpec((None, H_M, 1, DQK_M), lambda b, t: (b, 0, 0, 0)),
        pl.BlockSpec((None, H_M, 8, LANES), lambda b, t: (b, 0, 0, 0)),
    )
    if init is not None:
        in_specs += list(state_specs)
        args += list(init)
    out_specs = [pl.BlockSpec((nh, M_V), lambda b, t: (b, 0))]
    out_shape = [jax.ShapeDtypeStruct((n_streams * nh, M_V), F32)]
    if n_blocks:
        out_specs.append(pl.BlockSpec((T, M_V), lambda b, t: (fidx(b, t), 0)))
        out_shape.append(jax.ShapeDtypeStruct((n_streams * frames_per_stream, M_V), F32))
    out_specs += list(state_specs)
    out_shape += [
        jax.ShapeDtypeStruct((n_streams, H_M, DV_M, DQK_M), F32),
        jax.ShapeDtypeStruct((n_streams, H_M, 1, DQK_M), F32),
        jax.ShapeDtypeStruct((n_streams, H_M, 8, LANES), F32),
    ]
    return pl.pallas_call(
        functools.partial(_mlstm_kernel, n_blocks=n_blocks, has_init=init is not None),
        grid=(n_streams, 1 + n_blocks),
        in_specs=in_specs,
        out_specs=tuple(out_specs),
        out_shape=tuple(out_shape),
        scratch_shapes=[
            pltpu.VMEM((T, M_QK), F32), pltpu.VMEM((T, M_QK), F32),
            pltpu.VMEM((T, M_V), F32), pltpu.VMEM((T, LANES), F32),
        ],
        compiler_params=pltpu.CompilerParams(
            dimension_semantics=("arbitrary", "arbitrary"), vmem_limit_bytes=VMEM_LIMIT),
        name="mlstm_prompt" if n_blocks else "mlstm_cached",
    )(*args)


def _merge_kernel(x_ref, attf_ref, attt_ref, hmf_ref, hmt_ref, z_ref, mg_ref, wo_ref, o_ref, *, n_frame_tiles):
    is_tail = pl.program_id(0) >= n_frame_tiles
    att = jnp.where(is_tail, attt_ref[...], attf_ref[...])
    hm = jnp.where(is_tail, hmt_ref[...], hmf_ref[...])
    parts = []
    for h in range(H_M):
        sl = slice(h * DV_M, (h + 1) * DV_M)
        blk = hm[:, sl]
        ms = jnp.mean(blk * blk, axis=-1, keepdims=True)
        parts.append(blk * lax.rsqrt(ms + EPS) * mg_ref[:, sl] * z_ref[:, sl])
    hmn = jnp.concatenate(parts, axis=1).astype(BF16)
    y = _dot(att.astype(BF16), wo_ref[0:ATT_V, :]) + _dot(hmn, wo_ref[ATT_V:, :])
    o_ref[...] = x_ref[...] + y


def _merge(x_rows, att_f, att_t, hm_f, hm_t, z, mg, w_out):
    R, D = x_rows.shape
    bm = att_t.shape[0]
    nft = att_f.shape[0] // bm
    assert att_f.shape[0] % bm == 0 and att_f.shape[0] + bm == R
    ogcol = (N_MAIN - M_V) // M_V
    frames = lambda i: (jnp.minimum(i, nft - 1), 0)
    return pl.pallas_call(
        functools.partial(_merge_kernel, n_frame_tiles=nft),
        grid=(R // bm,),
        in_specs=[
            pl.BlockSpec((bm, D), lambda i: (i, 0)),
            pl.BlockSpec((bm, ATT_V), frames),
            pl.BlockSpec((bm, ATT_V), lambda i: (0, 0)),
            pl.BlockSpec((bm, M_V), frames),
            pl.BlockSpec((bm, M_V), lambda i: (0, 0)),
            pl.BlockSpec((bm, M_V), lambda i: (i, ogcol)),
            pl.BlockSpec((1, M_V), lambda i: (0, 0)),
            pl.BlockSpec((D, D), lambda i: (0, 0)),
        ],
        out_specs=pl.BlockSpec((bm, D), lambda i: (i, 0)),
        out_shape=jax.ShapeDtypeStruct((R, D), F32),
        compiler_params=pltpu.CompilerParams(
            dimension_semantics=("arbitrary",), vmem_limit_bytes=VMEM_LIMIT),
        name="merge_out_proj",
    )(x_rows, att_f, att_t, hm_f, hm_t, z, mg, w_out)


def _ffn_kernel(x_ref, g2_ref, wu_ref, wd_ref, o_ref, xn_s):
    k = pl.program_id(1)

    @pl.when(k == 0)
    def _():
        x = x_ref[...]
        ms = jnp.mean(x * x, axis=-1, keepdims=True)
        xn_s[...] = (x * lax.rsqrt(ms + EPS) * g2_ref[...]).astype(BF16)
        o_ref[...] = x

    u = jnp.maximum(_dot(xn_s[...], wu_ref[...]), 0.0)
    o_ref[...] += _dot((u * u).astype(BF16), wd_ref[...])


def _ffn(x_rows, g2, w_up, w_down, bm, fc):
    R, D = x_rows.shape
    FF = w_up.shape[1]
    return pl.pallas_call(
        _ffn_kernel,
        grid=(R // bm, FF // fc),
        in_specs=[
            pl.BlockSpec((bm, D), lambda i, k: (i, 0)),
            pl.BlockSpec((1, D), lambda i, k: (0, 0)),
            pl.BlockSpec((D, fc), lambda i, k: (0, k)),
            pl.BlockSpec((fc, D), lambda i, k: (k, 0)),
        ],
        out_specs=pl.BlockSpec((bm, D), lambda i, k: (i, 0)),
        out_shape=jax.ShapeDtypeStruct((R, D), F32),
        scratch_shapes=[pltpu.VMEM((bm, D), BF16)],
        compiler_params=pltpu.CompilerParams(
            dimension_semantics=("arbitrary", "arbitrary"), vmem_limit_bytes=VMEM_LIMIT),
        name="ffn",
    )(x_rows, g2, w_up, w_down)


def kernel(x_prompt, x_sample, cache_k, cache_v, state_C, state_n, state_m, meta_tokens, rel_bias, norm1_g, w_in, b_i, b_f, q_norm_g, k_norm_g, lambda_q1, lambda_k1, lambda_q2, lambda_k2, subln_g, mnorm_g, w_out, norm2_g, w_up, w_down):
    B, S, D = x_prompt.shape
    Bd, Td, _ = x_sample.shape
    depth = w_in.shape[0]
    Pc = cache_k.shape[2]
    assert Td == N_META and S % ATT_TILE == 0 and S % M_TILE == 0 and Pc % LANES == 0
    assert ATT_TILE % CHUNK == 0 and CHUNK & (CHUNK - 1) == 0 and ATT_TILE >= MAX_DISTANCE

    n_frames = B * S
    n_meta = B * N_META
    n_samp = Bd * Td
    n_real = n_frames + n_meta + n_samp
    R = -(-n_real // ROW_ALIGN) * ROW_ALIGN
    mrow = n_frames // N_META
    srow = (n_frames + n_meta) // N_META

    meta = jnp.broadcast_to(meta_tokens.astype(F32)[None], (B, N_META, D)).reshape(n_meta, D)
    x_rows = jnp.concatenate([
        x_prompt.reshape(n_frames, D), meta, x_sample.reshape(n_samp, D),
        jnp.zeros((R - n_real, D), F32)], axis=0)

    bias = _bias_tiles(rel_bias.astype(F32), Pc, Td)
    ck = cache_k.astype(F32)
    cv = cache_v.astype(F32)
    zeros_pad_att = jnp.zeros((R - n_real, ATT_V), F32)

    g_off = 2 * ATT_QK + ATT_V + 2 * M_QK + M_V
    outs = {name: [] for name in ("kp", "vp", "Cp", "np", "mp", "ks", "vs", "Cs", "ns", "ms")}

    for l in range(depth):
        lam_init = 0.8 - 0.6 * math.exp(-0.3 * l)
        w = w_in[l]
        w_main = jnp.concatenate([w[:, :g_off], w[:, g_off + 2 * H_M:]], axis=1).astype(BF16)
        w_gate = jnp.pad(w[:, g_off:g_off + 2 * H_M], ((0, 0), (0, LANES - 2 * H_M))).astype(BF16)
        gate_bias = jnp.pad(jnp.concatenate([b_i[l], b_f[l]]).astype(F32), (0, LANES - 2 * H_M))[None]
        lam_vecs = [v[l].astype(F32)[None] for v in (lambda_q1, lambda_k1, lambda_q2, lambda_k2)]
        sg = subln_g[l].astype(F32)[None]

        z, gates = _inproj(x_rows, norm1_g[l].astype(F32)[None], w_main, w_gate, gate_bias,
                           q_norm_g[l].astype(F32)[None], k_norm_g[l].astype(F32)[None], bm=528)

        att_f = _attn_prompt(z, bias, lam_vecs, sg, B, S, lam_init)
        att_m = _attn_small(z, mrow, B, bias[4], lam_vecs, sg, lam_init)
        att_s = _attn_small(z, srow, Bd, bias[5], lam_vecs, sg, lam_init, cache=(ck, cv, l))
        att_t = jnp.concatenate([att_m, att_s, zeros_pad_att], axis=0)

        hm_m, hm_f, C_p, n_p, m_p = _mlstm(z, gates, mrow, B, S // M_TILE, frames_per_stream=S)
        init = (state_C[l].astype(F32), state_n[l].astype(F32)[:, :, None, :],
                jnp.broadcast_to(state_m[l].astype(F32)[:, :, None, None], (Bd, H_M, 8, LANES)))
        hm_s, C_s, n_s, m_s = _mlstm(z, gates, srow, Bd, 0, init=init)
        hm_t = jnp.concatenate([hm_m, hm_s, zeros_pad_att], axis=0)

        x1 = _merge(x_rows, att_f, att_t, hm_f, hm_t, z, mnorm_g[l].astype(F32)[None], w_out[l].astype(BF16))
        x_rows = _ffn(x1, norm2_g[l].astype(F32)[None], w_up[l].astype(BF16), w_down[l].astype(BF16),
                      bm=704, fc=512)

        zk = z[:, ATT_QK:2 * ATT_QK]
        zv = z[:, 2 * ATT_QK:2 * ATT_QK + ATT_V]

        def prompt_cache(a, width):
            fr = a[:n_frames].reshape(B, S, H_ATT, width)
            me = a[n_frames:n_frames + n_meta].reshape(B, N_META, H_ATT, width)
            return jnp.concatenate([me, fr], axis=1)

        outs["kp"].append(prompt_cache(zk, 2 * DQK_ATT))
        outs["vp"].append(prompt_cache(zv, DV_ATT))
        outs["Cp"].append(C_p)
        outs["np"].append(n_p[:, :, 0, :])
        outs["mp"].append(m_p[:, :, 0, 0])
        outs["ks"].append(zk[n_frames + n_meta:n_real].reshape(Bd, Td, H_ATT, 2 * DQK_ATT))
        outs["vs"].append(zv[n_frames + n_meta:n_real].reshape(Bd, Td, H_ATT, DV_ATT))
        outs["Cs"].append(C_s)
        outs["ns"].append(n_s[:, :, 0, :])
        outs["ms"].append(m_s[:, :, 0, 0])

    y_prompt = x_rows[:n_frames].reshape(B, S, D)
    y_sample = x_rows[n_frames + n_meta:n_real].reshape(Bd, Td, D)
    return (y_prompt, y_sample,
            jnp.stack(outs["kp"]), jnp.stack(outs["vp"]), jnp.stack(outs["Cp"]), jnp.stack(outs["np"]),
            jnp.stack(outs["mp"]),
            jnp.stack(outs["ks"]), jnp.stack(outs["vs"]), jnp.stack(outs["Cs"]), jnp.stack(outs["ns"]),
            jnp.stack(outs["ms"]))
```

```python
import functools
import math

import jax
import jax.numpy as jnp
from jax import lax
from jax.experimental import pallas as pl
from jax.experimental.pallas import tpu as pltpu

F32 = jnp.float32
BF16 = jnp.bfloat16

CHUNK = 64
N_META = 16
H_ATT = 4
DQK_ATT = 128
DV_ATT = 256
H_M = 4
DQK_M = 128
DV_M = 256
N_BUCKETS = 32
MAX_DISTANCE = 128
EPS = 1e-6
NEG = -1e30

ATT_QK = H_ATT * 2 * DQK_ATT
ATT_V = H_ATT * DV_ATT
M_QK = H_M * DQK_M
M_V = H_M * DV_M
N_MAIN = 2 * ATT_QK + ATT_V + 2 * M_QK + M_V + M_V

LANES = 128
ATT_TILE = 256
M_TILE = 256
TAIL_ROWS = 256
PROJ_ROWS = 1024
PROJ_COLS = 512
MERGE_ROWS = 256
CACHE_CHUNK = 1024
FF_ROWS = 512
FF_COLS = 512
VMEM_LIMIT = 56 * 1024 * 1024

_NT = (((1,), (1,)), ((), ()))


def _dot(a, b):
    return jnp.dot(a, b, preferred_element_type=F32)


def _dot_nt(a, b):
    return lax.dot_general(a, b, _NT, preferred_element_type=F32)


def _log_sigmoid(x):
    return jnp.minimum(x, 0.0) - jnp.log1p(jnp.exp(-jnp.abs(x)))


def _col(a, idx):
    lane = lax.broadcasted_iota(jnp.int32, a.shape, 1)
    return jnp.sum(jnp.where(lane == idx, a, 0.0), axis=1, keepdims=True)


def _rms(x, gain):
    ms = jnp.mean(x * x, axis=-1, keepdims=True)
    return x * lax.rsqrt(ms + EPS) * gain


def _bias_from_rel(rel, rb_ref, h):
    nb = N_BUCKETS // 2
    max_exact = nb // 2
    ret = jnp.where(rel > 0, nb, 0)
    n = jnp.abs(rel)
    nf = jnp.maximum(n, 1).astype(F32)
    large = max_exact + (jnp.log(nf / max_exact) / math.log(MAX_DISTANCE / max_exact)
                         * (nb - max_exact)).astype(jnp.int32)
    large = jnp.minimum(large, nb - 1)
    bucket = ret + jnp.where(n < max_exact, n, large)
    out = jnp.zeros(rel.shape, F32)
    for b in range(N_BUCKETS):
        out = jnp.where(bucket == b, rb_ref[b, h], out)
    return out


def _bias_kernel(rb_ref, diag_ref, prev_ref, meta_ref, far_ref, mm_ref, samp_ref, *, past_len, dec_seq):
    h = pl.program_id(0)
    shift = CHUNK.bit_length() - 1
    T = ATT_TILE

    qi = lax.broadcasted_iota(jnp.int32, (T, T), 0)
    kj = lax.broadcasted_iota(jnp.int32, (T, T), 1)
    bias = _bias_from_rel(kj - qi, rb_ref, h)
    diag_ref[...] = jnp.where((kj >> shift) <= (qi >> shift), bias, NEG)
    prev_ref[...] = _bias_from_rel(kj - T - qi, rb_ref, h)
    far_ref[...] = _bias_from_rel(jnp.full(far_ref.shape, -(2 * MAX_DISTANCE + 1), jnp.int32), rb_ref, h)

    qi = lax.broadcasted_iota(jnp.int32, (T, LANES), 0)
    kj = lax.broadcasted_iota(jnp.int32, (T, LANES), 1)
    for var in range(2):
        bias = _bias_from_rel(kj - (N_META + qi + var * T), rb_ref, h)
        meta_ref[var] = jnp.where(kj < N_META, bias, NEG)

    qi = lax.broadcasted_iota(jnp.int32, (N_META, LANES), 0)
    kj = lax.broadcasted_iota(jnp.int32, (N_META, LANES), 1)
    mm_ref[...] = jnp.where(kj < N_META, _bias_from_rel(kj - qi, rb_ref, h), NEG)

    width = samp_ref.shape[-1]
    qi = lax.broadcasted_iota(jnp.int32, (dec_seq, width), 0) + past_len
    kj = lax.broadcasted_iota(jnp.int32, (dec_seq, width), 1)
    bias = _bias_from_rel(kj - qi, rb_ref, h)
    ok = ((kj >> shift) <= (qi >> shift)) & (kj < past_len + dec_seq)
    samp_ref[...] = jnp.where(ok, bias, NEG)


def _bias_tiles(rel_bias, past_len, dec_seq):
    T = ATT_TILE
    sw = past_len + LANES
    out_shape = (
        jax.ShapeDtypeStruct((H_ATT, T, T), F32),
        jax.ShapeDtypeStruct((H_ATT, T, T), F32),
        jax.ShapeDtypeStruct((H_ATT, 2, T, LANES), F32),
        jax.ShapeDtypeStruct((H_ATT, 8, T), F32),
        jax.ShapeDtypeStruct((H_ATT, N_META, LANES), F32),
        jax.ShapeDtypeStruct((H_ATT, dec_seq, sw), F32),
    )
    out_specs = (
        pl.BlockSpec((None, T, T), lambda h: (h, 0, 0)),
        pl.BlockSpec((None, T, T), lambda h: (h, 0, 0)),
        pl.BlockSpec((None, 2, T, LANES), lambda h: (h, 0, 0, 0)),
        pl.BlockSpec((None, 8, T), lambda h: (h, 0, 0)),
        pl.BlockSpec((None, N_META, LANES), lambda h: (h, 0, 0)),
        pl.BlockSpec((None, dec_seq, sw), lambda h: (h, 0, 0)),
    )
    return pl.pallas_call(
        functools.partial(_bias_kernel, past_len=past_len, dec_seq=dec_seq),
        grid=(H_ATT,),
        in_specs=[pl.BlockSpec(memory_space=pltpu.SMEM)],
        out_specs=out_specs,
        out_shape=out_shape,
        name="rel_bias_tiles",
    )(rel_bias)


def _kv_copies(stage, sem, dst_prompt, dst_sample, i, *, tail, n_batch, frames_per_batch, n_sample_rows):
    if not tail:
        rows = stage.shape[0]
        tiles_per_batch = frames_per_batch // rows
        b = i // tiles_per_batch
        t = i - b * tiles_per_batch
        return [pltpu.make_async_copy(stage, dst_prompt.at[b, pl.ds(N_META + t * rows, rows)], sem.at[0])]
    copies = [pltpu.make_async_copy(stage.at[pl.ds(b * N_META, N_META)], dst_prompt.at[b, pl.ds(0, N_META)],
                                    sem.at[b]) for b in range(n_batch)]
    copies.append(pltpu.make_async_copy(stage.at[pl.ds(n_batch * N_META, n_sample_rows)], dst_sample,
                                        sem.at[n_batch]))
    return copies


def _inproj_kernel(xf_ref, xt_ref, g1_ref, w_ref, wg_ref, gb_ref, qg_ref, kg_ref,
                   zf_ref, zt_ref, gf_ref, gt_ref, kp_ref, vp_ref, ks_ref, vs_ref,
                   xn_s, stage, sem, *, n_frame_tiles, n_batch, frames_per_batch, n_sample_rows):
    i = pl.program_id(0)
    j = pl.program_id(1)
    bn = w_ref.shape[1]
    assert ATT_QK % bn == 0 and ATT_V % bn == 0 and M_QK % bn == 0 and M_V % bn == 0 and bn % DV_ATT == 0
    j_k = ATT_QK // bn
    j_v = j_k + ATT_QK // bn
    j_mq = j_v + ATT_V // bn
    j_mk = j_mq + M_QK // bn
    j_mv = j_mk + M_QK // bn
    j_og = j_mv + M_V // bn
    heads_per_tile = bn // DV_ATT

    def tile(x_ref, z_ref, g_ref, tail):
        nrows = x_ref.shape[0]
        st = stage.at[pl.ds(0, nrows)]
        copies = functools.partial(_kv_copies, st, sem, i=i, tail=tail, n_batch=n_batch,
                                   frames_per_batch=frames_per_batch, n_sample_rows=n_sample_rows)

        @pl.when(j == 0)
        def _():
            xb = _rms(x_ref[...], g1_ref[...]).astype(BF16)
            xn_s[0:nrows, :] = xb
            gates = _dot(xb, wg_ref[...]) + gb_ref[...]
            lane = lax.broadcasted_iota(jnp.int32, gates.shape, 1)
            g_ref[...] = jnp.where(lane < H_M, gates, _log_sigmoid(gates))

        acc = _dot(xn_s[0:nrows, :], w_ref[...])

        def group_norm(gain_ref, scale):
            parts = []
            for gi in range(bn // DQK_ATT):
                y = _rms(acc[:, gi * DQK_ATT:(gi + 1) * DQK_ATT], gain_ref[...])
                parts.append(y if scale is None else y * scale)
            return jnp.concatenate(parts, axis=1)

        def stage_heads(y, jj, j0, dst_prompt, dst_sample):
            if jj == j0 and j0 != j_k:
                for c in copies(kp_ref, ks_ref):
                    c.wait()
            h0 = (jj - j0) * heads_per_tile
            st[:, h0:h0 + heads_per_tile, :] = y.reshape(nrows, heads_per_tile, DV_ATT)
            if jj == j0 + ATT_QK // bn - 1:
                for c in copies(dst_prompt, dst_sample):
                    c.start()

        @pl.when(j < j_k)
        def _():
            z_ref[...] = group_norm(qg_ref, DQK_ATT ** -0.5)

        for jj in range(j_k, j_v):
            @pl.when(j == jj)
            def _(jj=jj):
                y = group_norm(kg_ref, None)
                z_ref[...] = y
                stage_heads(y, jj, j_k, kp_ref, ks_ref)

        for jj in range(j_v, j_mq):
            @pl.when(j == jj)
            def _(jj=jj):
                z_ref[...] = acc
                stage_heads(acc, jj, j_v, vp_ref, vs_ref)

        @pl.when(j == j_mq)
        def _():
            for c in copies(vp_ref, vs_ref):
                c.wait()

        @pl.when(((j >= j_mq) & (j < j_mk)) | ((j >= j_mv) & (j < j_og)))
        def _():
            z_ref[...] = acc

        @pl.when((j >= j_mk) & (j < j_mv))
        def _():
            z_ref[...] = acc * (DQK_M ** -0.5)

        @pl.when(j >= j_og)
        def _():
            z_ref[...] = jax.nn.sigmoid(acc)

    @pl.when(i < n_frame_tiles)
    def _():
        tile(xf_ref, zf_ref, gf_ref, False)

    @pl.when(i == n_frame_tiles)
    def _():
        tile(xt_ref, zt_ref, gt_ref, True)


def _inproj(x_f, x_t, g1, w_main, w_gate, gate_bias, qg, kg, n_batch, n_sample_streams):
    NF, D = x_f.shape
    NT = x_t.shape[0]
    bm, bn = PROJ_ROWS, PROJ_COLS
    nft = NF // bm
    nj = N_MAIN // bn
    S = NF // n_batch
    L = N_META + S
    n_sample_rows = n_sample_streams * N_META
    assert NF % bm == 0 and S % bm == 0 and N_MAIN % bn == 0 and NT <= bm
    fr = lambda i: jnp.minimum(i, nft - 1)
    any_spec = pl.BlockSpec(memory_space=pl.ANY)
    return pl.pallas_call(
        functools.partial(_inproj_kernel, n_frame_tiles=nft, n_batch=n_batch, frames_per_batch=S,
                          n_sample_rows=n_sample_rows),
        grid=(nft + 1, nj),
        in_specs=[
            pl.BlockSpec((bm, D), lambda i, j: (fr(i), 0)),
            pl.BlockSpec((NT, D), lambda i, j: (0, 0)),
            pl.BlockSpec((1, D), lambda i, j: (0, 0)),
            pl.BlockSpec((D, bn), lambda i, j: (0, j)),
            pl.BlockSpec((D, LANES), lambda i, j: (0, 0)),
            pl.BlockSpec((1, LANES), lambda i, j: (0, 0)),
            pl.BlockSpec((1, DQK_ATT), lambda i, j: (0, 0)),
            pl.BlockSpec((1, DQK_ATT), lambda i, j: (0, 0)),
        ],
        out_specs=(
            pl.BlockSpec((bm, bn), lambda i, j: (fr(i), jnp.where(i < nft, j, nj - 1))),
            pl.BlockSpec((NT, bn), lambda i, j: (0, jnp.where(i < nft, 0, j))),
            pl.BlockSpec((bm, LANES), lambda i, j: (fr(i), 0)),
            pl.BlockSpec((NT, LANES), lambda i, j: (0, 0)),
            any_spec, any_spec, any_spec, any_spec,
        ),
        out_shape=(
            jax.ShapeDtypeStruct((NF, N_MAIN), F32),
            jax.ShapeDtypeStruct((NT, N_MAIN), F32),
            jax.ShapeDtypeStruct((NF, LANES), F32),
            jax.ShapeDtypeStruct((NT, LANES), F32),
            jax.ShapeDtypeStruct((n_batch, L, H_ATT, 2 * DQK_ATT), F32),
            jax.ShapeDtypeStruct((n_batch, L, H_ATT, DV_ATT), F32),
            jax.ShapeDtypeStruct((n_sample_rows, H_ATT, 2 * DQK_ATT), F32),
            jax.ShapeDtypeStruct((n_sample_rows, H_ATT, DV_ATT), F32),
        ),
        scratch_shapes=[
            pltpu.VMEM((bm, D), BF16),
            pltpu.VMEM((bm, H_ATT, DV_ATT), F32),
            pltpu.SemaphoreType.DMA((n_batch + 1,)),
        ],
        compiler_params=pltpu.CompilerParams(
            dimension_semantics=("arbitrary", "arbitrary"), vmem_limit_bytes=VMEM_LIMIT),
        name="in_proj",
    )(x_f, x_t, g1, w_main, w_gate, gate_bias, qg, kg)


def _diff_lambda(lq1_ref, lk1_ref, lq2_ref, lk2_ref, lam_init):
    a = jnp.sum(lq1_ref[...] * lk1_ref[...], axis=-1, keepdims=True)
    b = jnp.sum(lq2_ref[...] * lk2_ref[...], axis=-1, keepdims=True)
    return jnp.exp(a) - jnp.exp(b) + lam_init


def _subln(o, sg_ref, lam_init):
    return _rms(o, sg_ref[...]) * (1.0 - lam_init)


def _pad_rows(dst, src):
    dst[...] = jnp.zeros(dst.shape, dst.dtype)
    dst[0:src.shape[0], :] = src[...].astype(dst.dtype)


def _attn_prompt_kernel(q_ref, k_ref, v_ref, km_ref, vm_ref, bd_ref, bp_ref, bm_ref, bf_ref,
                        lq1_ref, lk1_ref, lq2_ref, lk2_ref, sg_ref, o_ref,
                        kb_s, vb_s, kmb_s, vmb_s, *, lam_init):
    T = ATT_TILE
    n_tiles = q_ref.shape[0] // T
    kb_s[...] = k_ref[...].astype(BF16)
    vb_s[...] = v_ref[...].astype(BF16)
    _pad_rows(kmb_s, km_ref)
    _pad_rows(vmb_s, vm_ref)
    lam = _diff_lambda(lq1_ref, lk1_ref, lq2_ref, lk2_ref, lam_init)
    far = bf_ref[0:1, 0:1]

    for j in range(n_tiles):
        rows = slice(j * T, (j + 1) * T)
        regions = [(kmb_s, vmb_s, slice(0, LANES), bm_ref[min(j, 1)])]
        if j >= 2:
            regions.append((kb_s, vb_s, slice(0, (j - 1) * T), far))
        if j >= 1:
            regions.append((kb_s, vb_s, slice((j - 1) * T, j * T), bp_ref[...]))
        regions.append((kb_s, vb_s, rows, bd_ref[...]))

        probs, denom = [], []
        for c in range(2):
            cols = slice(c * DQK_ATT, (c + 1) * DQK_ATT)
            qc = q_ref[rows, cols].astype(BF16)
            ss = [_dot_nt(qc, kr[rs, cols]) + br for kr, _, rs, br in regions]
            m = functools.reduce(jnp.maximum, [jnp.max(s, axis=-1, keepdims=True) for s in ss])
            pc = [jnp.exp(s - m) for s in ss]
            denom.append(functools.reduce(jnp.add, [jnp.sum(p, axis=-1, keepdims=True) for p in pc]))
            probs.append(pc)
        a0 = 1.0 / denom[0]
        a1 = lam / denom[1]
        o = None
        for r, (_, vr, rs, _) in enumerate(regions):
            w = (probs[0][r] * a0 - probs[1][r] * a1).astype(BF16)
            t = _dot(w, vr[rs, :])
            o = t if o is None else o + t
        o_ref[rows, :] = _subln(o, sg_ref, lam_init)


def _attn_prompt(z_f, z_t, bias, lam_vecs, sg, B, S, lam_init):
    T = ATT_TILE
    qkw = 2 * DQK_ATT
    kcol = ATT_QK // qkw
    vcol = 2 * ATT_QK // DV_ATT
    b_diag, b_prev, b_meta, b_far, _, _ = bias
    vec = pl.BlockSpec((1, DQK_ATT), lambda b, h: (0, 0))
    return pl.pallas_call(
        functools.partial(_attn_prompt_kernel, lam_init=lam_init),
        grid=(B, H_ATT),
        in_specs=[
            pl.BlockSpec((S, qkw), lambda b, h: (b, h)),
            pl.BlockSpec((S, qkw), lambda b, h: (b, kcol + h)),
            pl.BlockSpec((S, DV_ATT), lambda b, h: (b, vcol + h)),
            pl.BlockSpec((N_META, qkw), lambda b, h: (b, kcol + h)),
            pl.BlockSpec((N_META, DV_ATT), lambda b, h: (b, vcol + h)),
            pl.BlockSpec((None, T, T), lambda b, h: (h, 0, 0)),
            pl.BlockSpec((None, T, T), lambda b, h: (h, 0, 0)),
            pl.BlockSpec((None, 2, T, LANES), lambda b, h: (h, 0, 0, 0)),
            pl.BlockSpec((None, 8, T), lambda b, h: (h, 0, 0)),
            vec, vec, vec, vec,
            pl.BlockSpec((1, DV_ATT), lambda b, h: (0, 0)),
        ],
        out_specs=pl.BlockSpec((S, DV_ATT), lambda b, h: (b, h)),
        out_shape=jax.ShapeDtypeStruct((B * S, ATT_V), F32),
        scratch_shapes=[
            pltpu.VMEM((S, qkw), BF16),
            pltpu.VMEM((S, DV_ATT), BF16),
            pltpu.VMEM((LANES, qkw), BF16),
            pltpu.VMEM((LANES, DV_ATT), BF16),
        ],
        compiler_params=pltpu.CompilerParams(
            dimension_semantics=("arbitrary", "arbitrary"), vmem_limit_bytes=VMEM_LIMIT),
        name="attn_prompt",
    )(z_f, z_f, z_f, z_t, z_t, b_diag, b_prev, b_meta, b_far, *lam_vecs, sg)


def _attn_small_kernel(*refs, lam_init, has_cache):
    if has_cache:
        (q_ref, kc_ref, vc_ref, kn_ref, vn_ref, b_ref,
         lq1_ref, lk1_ref, lq2_ref, lk2_ref, sg_ref, o_ref, kpad, vpad, kb_s, vb_s) = refs
        past_len = kb_s.shape[0]
        chunk = kc_ref.shape[0]
        c = pl.program_id(1)
        r0 = pl.multiple_of(c * chunk, chunk)
        kb_s[pl.ds(r0, chunk), :] = kc_ref[...].reshape(chunk, ATT_QK).astype(BF16)
        vb_s[pl.ds(r0, chunk), :] = vc_ref[...].reshape(chunk, ATT_V).astype(BF16)

        @pl.when(c == past_len // chunk - 1)
        def _():
            _attn_small_body(q_ref, kn_ref, vn_ref, b_ref, lq1_ref, lk1_ref, lq2_ref, lk2_ref, sg_ref, o_ref,
                             kpad, vpad, kb_s, vb_s, lam_init=lam_init, past_len=past_len)
    else:
        (q_ref, kn_ref, vn_ref, b_ref,
         lq1_ref, lk1_ref, lq2_ref, lk2_ref, sg_ref, o_ref, kpad, vpad) = refs
        _attn_small_body(q_ref, kn_ref, vn_ref, b_ref, lq1_ref, lk1_ref, lq2_ref, lk2_ref, sg_ref, o_ref,
                         kpad, vpad, None, None, lam_init=lam_init, past_len=0)


def _attn_small_body(q_ref, kn_ref, vn_ref, b_ref, lq1_ref, lk1_ref, lq2_ref, lk2_ref, sg_ref, o_ref,
                     kpad, vpad, kb_s, vb_s, *, lam_init, past_len):
    has_cache = past_len > 0
    lam = _diff_lambda(lq1_ref, lk1_ref, lq2_ref, lk2_ref, lam_init)
    _pad_rows(kpad, kn_ref)
    _pad_rows(vpad, vn_ref)
    for h in range(H_ATT):
        vsl = slice(h * DV_ATT, (h + 1) * DV_ATT)
        outs = []
        for c in range(2):
            sl = slice(h * 2 * DQK_ATT + c * DQK_ATT, h * 2 * DQK_ATT + (c + 1) * DQK_ATT)
            qc = q_ref[:, sl].astype(BF16)
            s_new = _dot_nt(qc, kpad[:, sl]) + b_ref[h, :, past_len:]
            m = jnp.max(s_new, axis=-1, keepdims=True)
            if has_cache:
                s_old = _dot_nt(qc, kb_s[:, sl]) + b_ref[h, :, :past_len]
                m = jnp.maximum(m, jnp.max(s_old, axis=-1, keepdims=True))
            p_new = jnp.exp(s_new - m)
            l = jnp.sum(p_new, axis=-1, keepdims=True)
            o = _dot(p_new.astype(BF16), vpad[:, vsl])
            if has_cache:
                p_old = jnp.exp(s_old - m)
                l = l + jnp.sum(p_old, axis=-1, keepdims=True)
                o = o + _dot(p_old.astype(BF16), vb_s[:, vsl])
            outs.append(o / l)
        o_ref[:, vsl] = _subln(outs[0] - lam * outs[1], sg_ref, lam_init)


def _attn_small(z_t, row0, n_streams, bias, lam_vecs, sg, lam_init, cache=None):
    vec = pl.BlockSpec((1, DQK_ATT), lambda b, c: (0, 0))
    in_specs = [pl.BlockSpec((N_META, ATT_QK), lambda b, c: (row0 + b, 0))]
    args = [z_t]
    scratch = [pltpu.VMEM((LANES, ATT_QK), BF16), pltpu.VMEM((LANES, ATT_V), BF16)]
    n_chunks = 1
    if cache is not None:
        ck, cv, layer = cache
        past_len = ck.shape[2]
        chunk = min(past_len, CACHE_CHUNK)
        assert past_len % chunk == 0
        n_chunks = past_len // chunk
        in_specs += [
            pl.BlockSpec((None, None, chunk, H_ATT, 2 * DQK_ATT), lambda b, c: (layer, b, c, 0, 0)),
            pl.BlockSpec((None, None, chunk, H_ATT, DV_ATT), lambda b, c: (layer, b, c, 0, 0))]
        args += [ck, cv]
        scratch += [pltpu.VMEM((past_len, ATT_QK), BF16), pltpu.VMEM((past_len, ATT_V), BF16)]
    in_specs += [
        pl.BlockSpec((N_META, ATT_QK), lambda b, c: (row0 + b, 1)),
        pl.BlockSpec((N_META, ATT_V), lambda b, c: (row0 + b, 2 * ATT_QK // ATT_V)),
        pl.BlockSpec(bias.shape, lambda b, c: (0, 0, 0)),
        vec, vec, vec, vec,
        pl.BlockSpec((1, DV_ATT), lambda b, c: (0, 0)),
    ]
    args += [z_t, z_t, bias, *lam_vecs, sg]
    return pl.pallas_call(
        functools.partial(_attn_small_kernel, lam_init=lam_init, has_cache=cache is not None),
        grid=(n_streams, n_chunks),
        in_specs=in_specs,
        out_specs=pl.BlockSpec((N_META, ATT_V), lambda b, c: (b, 0)),
        out_shape=jax.ShapeDtypeStruct((n_streams * N_META, ATT_V), F32),
        scratch_shapes=scratch,
        compiler_params=pltpu.CompilerParams(
            dimension_semantics=("arbitrary", "arbitrary"), vmem_limit_bytes=VMEM_LIMIT),
        name="attn_cached" if cache is not None else "attn_meta",
    )(*args)


def _mlstm_kernel(*refs, n_blocks, has_init):
    refs = list(refs)
    qh_ref, kh_ref, vh_ref, gh_ref = refs[:4]
    pos = 4
    if n_blocks:
        qf_ref, kf_ref, vf_ref, gf_ref = refs[pos:pos + 4]
        pos += 4
    if has_init:
        c0_ref, n0_ref, m0_ref = refs[pos:pos + 3]
        pos += 3
    hh_ref = refs[pos]
    pos += 1
    if n_blocks:
        hf_ref = refs[pos]
        pos += 1
    c_ref, n_ref, m_ref = refs[pos:pos + 3]
    qs, ks, vs, gs = refs[pos + 3:pos + 7]

    t = pl.program_id(1)
    T = M_TILE

    @pl.when(t == 0)
    def _():
        if has_init:
            c_ref[...] = c0_ref[...]
            n_ref[...] = n0_ref[...]
            m_ref[...] = m0_ref[...]
        else:
            c_ref[...] = jnp.zeros(c_ref.shape, F32)
            n_ref[...] = jnp.zeros(n_ref.shape, F32)
            m_ref[...] = jnp.zeros(m_ref.shape, F32)
        nh = qh_ref.shape[0]
        for dst, src in ((qs, qh_ref), (ks, kh_ref), (vs, vh_ref)):
            dst[...] = jnp.zeros(dst.shape, F32)
            dst[0:nh, :] = src[...]
        lane = lax.broadcasted_iota(jnp.int32, gs.shape, 1)
        gs[...] = jnp.where(lane < H_M, NEG, 0.0)
        gs[0:nh, :] = gh_ref[...]

    if n_blocks:
        @pl.when(t > 0)
        def _():
            qs[...] = qf_ref[...]
            ks[...] = kf_ref[...]
            vs[...] = vf_ref[...]
            gs[...] = gf_ref[...]

    g = gs[...]
    row = lax.broadcasted_iota(jnp.int32, (T, T), 0)
    colid = lax.broadcasted_iota(jnp.int32, (T, T), 1)
    causal = row >= colid
    tril = jnp.where(causal, 1.0, 0.0).astype(F32)
    cum = jnp.dot(tril, g, preferred_element_type=F32, precision=lax.Precision.HIGHEST)
    g_t = g.T
    cum_t = cum.T
    last = cum[T - 1:T, :]

    for h in range(H_M):
        q = qs[:, h * DQK_M:(h + 1) * DQK_M]
        k = ks[:, h * DQK_M:(h + 1) * DQK_M]
        v = vs[:, h * DV_M:(h + 1) * DV_M]
        qb = q.astype(BF16)
        kb = k.astype(BF16)
        m_old = m_ref[h, 0:1, 0:1]
        c_old = c_ref[h]
        n_old = n_ref[h]
        bh_c = _col(cum, H_M + h)
        ih_c = _col(g, h)
        bh_r = cum_t[H_M + h:H_M + h + 1, :]
        ih_r = g_t[h:h + 1, :]
        d = jnp.where(causal, bh_c - bh_r + ih_r, NEG)
        inter = bh_c + m_old
        mt = jnp.maximum(inter, jnp.max(d, axis=-1, keepdims=True))
        a = jnp.exp(d - mt) * _dot_nt(qb, kb)
        iw = jnp.exp(inter - mt)
        num = _dot(a.astype(BF16), v.astype(BF16)) + iw * _dot_nt(qb, c_old.astype(BF16))
        den = jnp.sum(a, axis=-1, keepdims=True) + iw * jnp.sum(q * n_old, axis=-1, keepdims=True)
        den = jnp.maximum(jnp.abs(den), jnp.exp(-mt))
        hout = num / den
        sl = slice(h * DV_M, (h + 1) * DV_M)

        @pl.when(t == 0)
        def _():
            hh_ref[:, sl] = hout[0:hh_ref.shape[0], :]

        if n_blocks:
            @pl.when(t > 0)
            def _():
                hf_ref[:, sl] = hout

        bl = _col(last, H_M + h)
        g_c = bl - bh_c + ih_c
        m_new = jnp.maximum(bl + m_old, jnp.max(g_c, axis=0, keepdims=True))
        sw = jnp.exp(g_c - m_new)
        dec = jnp.exp(bl + m_old - m_new)
        c_ref[h] = dec * c_old + _dot(v.T.astype(BF16), (sw * k).astype(BF16))
        n_ref[h] = dec * n_old + jnp.sum(sw * k, axis=0, keepdims=True)
        m_ref[h] = jnp.broadcast_to(m_new, m_ref.shape[1:])


def _mlstm(z_t, g_t, row0, n_streams, frames=None, init=None):
    T = M_TILE
    qcol = (2 * ATT_QK + ATT_V) // M_QK
    vcol = (2 * ATT_QK + ATT_V + 2 * M_QK) // M_V
    nh = N_META
    in_specs = [
        pl.BlockSpec((nh, M_QK), lambda b, t: (row0 + b, qcol)),
        pl.BlockSpec((nh, M_QK), lambda b, t: (row0 + b, qcol + 1)),
        pl.BlockSpec((nh, M_V), lambda b, t: (row0 + b, vcol)),
        pl.BlockSpec((nh, LANES), lambda b, t: (row0 + b, 0)),
    ]
    args = [z_t, z_t, z_t, g_t]
    n_blocks = 0
    if frames is not None:
        z_f, g_f, fps = frames
        n_blocks = fps // T
        fidx = lambda b, t: b * n_blocks + jnp.maximum(t - 1, 0)
        in_specs += [
            pl.BlockSpec((T, M_QK), lambda b, t: (fidx(b, t), qcol)),
            pl.BlockSpec((T, M_QK), lambda b, t: (fidx(b, t), qcol + 1)),
            pl.BlockSpec((T, M_V), lambda b, t: (fidx(b, t), vcol)),
            pl.BlockSpec((T, LANES), lambda b, t: (fidx(b, t), 0)),
        ]
        args += [z_f, z_f, z_f, g_f]
    state_specs = (
        pl.BlockSpec((None, H_M, DV_M, DQK_M), lambda b, t: (b, 0, 0, 0)),
        pl.BlockSpec((None, H_M, 1, DQK_M), lambda b, t: (b, 0, 0, 0)),
        pl.BlockSpec((None, H_M, 8, LANES), lambda b, t: (b, 0, 0, 0)),
    )
    if init is not None:
        in_specs += list(state_specs)
        args += list(init)
    out_specs = [pl.BlockSpec((nh, M_V), lambda b, t: (b, 0))]
    out_shape = [jax.ShapeDtypeStruct((n_streams * nh, M_V), F32)]
    if n_blocks:
        out_specs.append(pl.BlockSpec((T, M_V), lambda b, t: (fidx(b, t), 0)))
        out_shape.append(jax.ShapeDtypeStruct((n_streams * fps, M_V), F32))
    out_specs += list(state_specs)
    out_shape += [
        jax.ShapeDtypeStruct((n_streams, H_M, DV_M, DQK_M), F32),
        jax.ShapeDtypeStruct((n_streams, H_M, 1, DQK_M), F32),
        jax.ShapeDtypeStruct((n_streams, H_M, 8, LANES), F32),
    ]
    return pl.pallas_call(
        functools.partial(_mlstm_kernel, n_blocks=n_blocks, has_init=init is not None),
        grid=(n_streams, 1 + n_blocks),
        in_specs=in_specs,
        out_specs=tuple(out_specs),
        out_shape=tuple(out_shape),
        scratch_shapes=[
            pltpu.VMEM((T, M_QK), F32), pltpu.VMEM((T, M_QK), F32),
            pltpu.VMEM((T, M_V), F32), pltpu.VMEM((T, LANES), F32),
        ],
        compiler_params=pltpu.CompilerParams(
            dimension_semantics=("arbitrary", "arbitrary"), vmem_limit_bytes=VMEM_LIMIT),
        name="mlstm_prompt" if n_blocks else "mlstm_cached",
    )(*args)


def _merge_kernel(xf_ref, xt_ref, attf_ref, attt_ref, hmf_ref, hmt_ref, zf_ref, zt_ref, mg_ref, wo_ref,
                  of_ref, ot_ref, *, n_frame_tiles):
    is_tail = pl.program_id(0) >= n_frame_tiles
    x = jnp.where(is_tail, xt_ref[...], xf_ref[...])
    att = jnp.where(is_tail, attt_ref[...], attf_ref[...])
    hm = jnp.where(is_tail, hmt_ref[...], hmf_ref[...])
    og = jnp.where(is_tail, zt_ref[...], zf_ref[...])
    parts = []
    for h in range(H_M):
        sl = slice(h * DV_M, (h + 1) * DV_M)
        parts.append(_rms(hm[:, sl], mg_ref[:, sl]) * og[:, sl])
    hmn = jnp.concatenate(parts, axis=1).astype(BF16)
    y = x + _dot(att.astype(BF16), wo_ref[0:ATT_V, :]) + _dot(hmn, wo_ref[ATT_V:, :])

    @pl.when(jnp.logical_not(is_tail))
    def _():
        of_ref[...] = y

    @pl.when(is_tail)
    def _():
        ot_ref[...] = y


def _merge(x_f, x_t, att_f, att_t, hm_f, hm_t, z_f, z_t, mg, w_out):
    NF, D = x_f.shape
    bm = x_t.shape[0]
    nft = NF // bm
    assert NF % bm == 0 and bm == MERGE_ROWS
    ogcol = (N_MAIN - M_V) // M_V
    fr = lambda i: (jnp.minimum(i, nft - 1), 0)
    tl = lambda i: (0, 0)
    return pl.pallas_call(
        functools.partial(_merge_kernel, n_frame_tiles=nft),
        grid=(nft + 1,),
        in_specs=[
            pl.BlockSpec((bm, D), fr), pl.BlockSpec((bm, D), tl),
            pl.BlockSpec((bm, ATT_V), fr), pl.BlockSpec((bm, ATT_V), tl),
            pl.BlockSpec((bm, M_V), fr), pl.BlockSpec((bm, M_V), tl),
            pl.BlockSpec((bm, M_V), lambda i: (jnp.minimum(i, nft - 1), ogcol)),
            pl.BlockSpec((bm, M_V), lambda i: (0, ogcol)),
            pl.BlockSpec((1, M_V), tl),
            pl.BlockSpec((D, D), tl),
        ],
        out_specs=(pl.BlockSpec((bm, D), fr), pl.BlockSpec((bm, D), tl)),
        out_shape=(jax.ShapeDtypeStruct((NF, D), F32), jax.ShapeDtypeStruct((bm, D), F32)),
        compiler_params=pltpu.CompilerParams(
            dimension_semantics=("arbitrary",), vmem_limit_bytes=VMEM_LIMIT),
        name="merge_out_proj",
    )(x_f, x_t, att_f, att_t, hm_f, hm_t, z_f, z_t, mg, w_out)


def _ffn_kernel(xf_ref, xt_ref, g2_ref, wu_ref, wd_ref, of_ref, ot_ref, xn_s, *, n_frame_tiles):
    i = pl.program_id(0)
    k = pl.program_id(1)

    def tile(x_ref, o_ref):
        nrows = x_ref.shape[0]

        @pl.when(k == 0)
        def _():
            x = x_ref[...]
            xn_s[0:nrows, :] = _rms(x, g2_ref[...]).astype(BF16)
            o_ref[...] = x

        u = jnp.maximum(_dot(xn_s[0:nrows, :], wu_ref[...]), 0.0)
        o_ref[...] += _dot((u * u).astype(BF16), wd_ref[...])

    @pl.when(i < n_frame_tiles)
    def _():
        tile(xf_ref, of_ref)

    @pl.when(i == n_frame_tiles)
    def _():
        tile(xt_ref, ot_ref)


def _ffn(x_f, x_t, g2, w_up, w_down):
    NF, D = x_f.shape
    NT = x_t.shape[0]
    FF = w_up.shape[1]
    bm, fc = FF_ROWS, FF_COLS
    nft = NF // bm
    assert NF % bm == 0 and FF % fc == 0 and NT <= bm
    fr = lambda i, k: (jnp.minimum(i, nft - 1), 0)
    tl = lambda i, k: (0, 0)
    return pl.pallas_call(
        functools.partial(_ffn_kernel, n_frame_tiles=nft),
        grid=(nft + 1, FF // fc),
        in_specs=[
            pl.BlockSpec((bm, D), fr),
            pl.BlockSpec((NT, D), tl),
            pl.BlockSpec((1, D), tl),
            pl.BlockSpec((D, fc), lambda i, k: (0, k)),
            pl.BlockSpec((fc, D), lambda i, k: (k, 0)),
        ],
        out_specs=(pl.BlockSpec((bm, D), fr), pl.BlockSpec((NT, D), tl)),
        out_shape=(jax.ShapeDtypeStruct((NF, D), F32), jax.ShapeDtypeStruct((NT, D), F32)),
        scratch_shapes=[pltpu.VMEM((bm, D), BF16)],
        compiler_params=pltpu.CompilerParams(
            dimension_semantics=("arbitrary", "arbitrary"), vmem_limit_bytes=VMEM_LIMIT),
        name="ffn",
    )(x_f, x_t, g2, w_up, w_down)


def kernel(x_prompt, x_sample, cache_k, cache_v, state_C, state_n, state_m, meta_tokens, rel_bias, norm1_g, w_in, b_i, b_f, q_norm_g, k_norm_g, lambda_q1, lambda_k1, lambda_q2, lambda_k2, subln_g, mnorm_g, w_out, norm2_g, w_up, w_down):
    B, S, D = x_prompt.shape
    Bd, Td, _ = x_sample.shape
    depth = w_in.shape[0]
    Pc = cache_k.shape[2]
    assert Td == N_META and S % ATT_TILE == 0 and S % M_TILE == 0 and Pc % LANES == 0
    assert ATT_TILE % CHUNK == 0 and CHUNK & (CHUNK - 1) == 0 and ATT_TILE >= MAX_DISTANCE

    n_frames = B * S
    n_meta = B * N_META
    n_samp = Bd * Td
    assert n_meta + n_samp <= TAIL_ROWS
    srow = n_meta // N_META

    meta = jnp.broadcast_to(meta_tokens.astype(F32)[None], (B, N_META, D)).reshape(n_meta, D)
    x_f = x_prompt.astype(F32).reshape(n_frames, D)
    x_t = jnp.concatenate([meta, x_sample.astype(F32).reshape(n_samp, D),
                           jnp.zeros((TAIL_ROWS - n_meta - n_samp, D), F32)], axis=0)

    bias = _bias_tiles(rel_bias.astype(F32), Pc, Td)
    ck = cache_k.astype(F32)
    cv = cache_v.astype(F32)
    zeros_pad_att = jnp.zeros((TAIL_ROWS - n_meta - n_samp, ATT_V), F32)

    g_off = 2 * ATT_QK + ATT_V + 2 * M_QK + M_V
    outs = {name: [] for name in ("kp", "vp", "Cp", "np", "mp", "ks", "vs", "Cs", "ns", "ms")}

    for l in range(depth):
        lam_init = 0.8 - 0.6 * math.exp(-0.3 * l)
        w = w_in[l]
        w_main = jnp.concatenate([w[:, :g_off], w[:, g_off + 2 * H_M:]], axis=1).astype(BF16)
        w_gate = jnp.pad(w[:, g_off:g_off + 2 * H_M], ((0, 0), (0, LANES - 2 * H_M))).astype(BF16)
        gate_bias = jnp.pad(jnp.concatenate([b_i[l], b_f[l]]).astype(F32), (0, LANES - 2 * H_M))[None]
        lam_vecs = [v[l].astype(F32)[None] for v in (lambda_q1, lambda_k1, lambda_q2, lambda_k2)]
        sg = subln_g[l].astype(F32)[None]

        z_f, z_t, g_f, g_t, k_p, v_p, k_s, v_s = _inproj(
            x_f, x_t, norm1_g[l].astype(F32)[None], w_main, w_gate, gate_bias,
            q_norm_g[l].astype(F32)[None], k_norm_g[l].astype(F32)[None], B, Bd)

        att_f = _attn_prompt(z_f, z_t, bias, lam_vecs, sg, B, S, lam_init)
        att_m = _attn_small(z_t, 0, B, bias[4], lam_vecs, sg, lam_init)
        att_s = _attn_small(z_t, srow, Bd, bias[5], lam_vecs, sg, lam_init, cache=(ck, cv, l))
        att_t = jnp.concatenate([att_m, att_s, zeros_pad_att], axis=0)

        hm_m, hm_f, C_p, n_p, m_p = _mlstm(z_t, g_t, 0, B, frames=(z_f, g_f, S))
        init = (state_C[l].astype(F32), state_n[l].astype(F32)[:, :, None, :],
                jnp.broadcast_to(state_m[l].astype(F32)[:, :, None, None], (Bd, H_M, 8, LANES)))
        hm_s, C_s, n_s, m_s = _mlstm(z_t, g_t, srow, Bd, init=init)
        hm_t = jnp.concatenate([hm_m, hm_s, zeros_pad_att], axis=0)

        x_f, x_t = _merge(x_f, x_t, att_f, att_t, hm_f, hm_t, z_f, z_t,
                          mnorm_g[l].astype(F32)[None], w_out[l].astype(BF16))
        x_f, x_t = _ffn(x_f, x_t, norm2_g[l].astype(F32)[None], w_up[l].astype(BF16), w_down[l].astype(BF16))

        outs["kp"].append(k_p)
        outs["vp"].append(v_p)
        outs["Cp"].append(C_p)
        outs["np"].append(n_p[:, :, 0, :])
        outs["mp"].append(m_p[:, :, 0, 0])
        outs["ks"].append(k_s.reshape(Bd, Td, H_ATT, 2 * DQK_ATT))
        outs["vs"].append(v_s.reshape(Bd, Td, H_ATT, DV_ATT))
        outs["Cs"].append(C_s)
        outs["ns"].append(n_s[:, :, 0, :])
        outs["ms"].append(m_s[:, :, 0, 0])

    y_prompt = x_f.reshape(B, S, D)
    y_sample = x_t[n_meta:n_meta + n_samp].reshape(Bd, Td, D)
    return (y_prompt, y_sample,
            jnp.stack(outs["kp"]), jnp.stack(outs["vp"]), jnp.stack(outs["Cp"]), jnp.stack(outs["np"]),
            jnp.stack(outs["mp"]),
            jnp.stack(outs["ks"]), jnp.stack(outs["vs"]), jnp.stack(outs["Cs"]), jnp.stack(outs["ns"]),
            jnp.stack(outs["ms"]))
```

```python
import functools
import math

import jax
import jax.numpy as jnp
from jax import lax
from jax.experimental import pallas as pl
from jax.experimental.pallas import tpu as pltpu

F32 = jnp.float32
BF16 = jnp.bfloat16

CHUNK = 64
N_META = 16
H_ATT = 4
DQK_ATT = 128
DV_ATT = 256
H_M = 4
DQK_M = 128
DV_M = 256
N_BUCKETS = 32
MAX_DISTANCE = 128
EPS = 1e-6
NEG = -1e30

ATT_QK = H_ATT * 2 * DQK_ATT
ATT_V = H_ATT * DV_ATT
M_QK = H_M * DQK_M
M_V = H_M * DV_M
N_MAIN = 2 * ATT_QK + ATT_V + 2 * M_QK + M_V + M_V

LANES = 128
ATT_TILE = 256
M_TILE = 256
TAIL_ROWS = 256
PROJ_ROWS = 1024
PROJ_COLS = 512
MERGE_ROWS = 256
CACHE_CHUNK = 1024
FF_ROWS = 512
FF_COLS = 1024
VMEM_LIMIT = 56 * 1024 * 1024

_NT = (((1,), (1,)), ((), ()))


def _dot(a, b):
    return jnp.dot(a, b, preferred_element_type=F32)


def _dot_nt(a, b):
    return lax.dot_general(a, b, _NT, preferred_element_type=F32)


def _log_sigmoid(x):
    return jnp.minimum(x, 0.0) - jnp.log1p(jnp.exp(-jnp.abs(x)))


def _col(a, idx):
    lane = lax.broadcasted_iota(jnp.int32, a.shape, 1)
    return jnp.sum(jnp.where(lane == idx, a, 0.0), axis=1, keepdims=True)


def _rms(x, gain):
    ms = jnp.mean(x * x, axis=-1, keepdims=True)
    return x * lax.rsqrt(ms + EPS) * gain


def _bias_from_rel(rel, rb_ref, h):
    nb = N_BUCKETS // 2
    max_exact = nb // 2
    ret = jnp.where(rel > 0, nb, 0)
    n = jnp.abs(rel)
    nf = jnp.maximum(n, 1).astype(F32)
    large = max_exact + (jnp.log(nf / max_exact) / math.log(MAX_DISTANCE / max_exact)
                         * (nb - max_exact)).astype(jnp.int32)
    large = jnp.minimum(large, nb - 1)
    bucket = ret + jnp.where(n < max_exact, n, large)
    out = jnp.zeros(rel.shape, F32)
    for b in range(N_BUCKETS):
        out = jnp.where(bucket == b, rb_ref[b, h], out)
    return out


def _bias_kernel(rb_ref, diag_ref, prev_ref, meta_ref, far_ref, mm_ref, samp_ref, *, past_len, dec_seq):
    h = pl.program_id(0)
    shift = CHUNK.bit_length() - 1
    T = ATT_TILE

    qi = lax.broadcasted_iota(jnp.int32, (T, T), 0)
    kj = lax.broadcasted_iota(jnp.int32, (T, T), 1)
    bias = _bias_from_rel(kj - qi, rb_ref, h)
    diag_ref[...] = jnp.where((kj >> shift) <= (qi >> shift), bias, NEG)
    prev_ref[...] = _bias_from_rel(kj - T - qi, rb_ref, h)
    far_ref[...] = _bias_from_rel(jnp.full(far_ref.shape, -(2 * MAX_DISTANCE + 1), jnp.int32), rb_ref, h)

    qi = lax.broadcasted_iota(jnp.int32, (T, LANES), 0)
    kj = lax.broadcasted_iota(jnp.int32, (T, LANES), 1)
    for var in range(2):
        bias = _bias_from_rel(kj - (N_META + qi + var * T), rb_ref, h)
        meta_ref[var] = jnp.where(kj < N_META, bias, NEG)

    qi = lax.broadcasted_iota(jnp.int32, (N_META, LANES), 0)
    kj = lax.broadcasted_iota(jnp.int32, (N_META, LANES), 1)
    mm_ref[...] = jnp.where(kj < N_META, _bias_from_rel(kj - qi, rb_ref, h), NEG)

    width = samp_ref.shape[-1]
    qi = lax.broadcasted_iota(jnp.int32, (dec_seq, width), 0) + past_len
    kj = lax.broadcasted_iota(jnp.int32, (dec_seq, width), 1)
    bias = _bias_from_rel(kj - qi, rb_ref, h)
    ok = ((kj >> shift) <= (qi >> shift)) & (kj < past_len + dec_seq)
    samp_ref[...] = jnp.where(ok, bias, NEG)


def _bias_tiles(rel_bias, past_len, dec_seq):
    T = ATT_TILE
    sw = past_len + LANES
    out_shape = (
        jax.ShapeDtypeStruct((H_ATT, T, T), F32),
        jax.ShapeDtypeStruct((H_ATT, T, T), F32),
        jax.ShapeDtypeStruct((H_ATT, 2, T, LANES), F32),
        jax.ShapeDtypeStruct((H_ATT, 8, T), F32),
        jax.ShapeDtypeStruct((H_ATT, N_META, LANES), F32),
        jax.ShapeDtypeStruct((H_ATT, dec_seq, sw), F32),
    )
    out_specs = (
        pl.BlockSpec((None, T, T), lambda h: (h, 0, 0)),
        pl.BlockSpec((None, T, T), lambda h: (h, 0, 0)),
        pl.BlockSpec((None, 2, T, LANES), lambda h: (h, 0, 0, 0)),
        pl.BlockSpec((None, 8, T), lambda h: (h, 0, 0)),
        pl.BlockSpec((None, N_META, LANES), lambda h: (h, 0, 0)),
        pl.BlockSpec((None, dec_seq, sw), lambda h: (h, 0, 0)),
    )
    return pl.pallas_call(
        functools.partial(_bias_kernel, past_len=past_len, dec_seq=dec_seq),
        grid=(H_ATT,),
        in_specs=[pl.BlockSpec(memory_space=pltpu.SMEM)],
        out_specs=out_specs,
        out_shape=out_shape,
        name="rel_bias_tiles",
    )(rel_bias)


def _kv_copies(stage, sem, dst_prompt, dst_sample, i, *, tail, n_batch, frames_per_batch, n_sample_rows):
    if not tail:
        rows = stage.shape[0]
        tiles_per_batch = frames_per_batch // rows
        b = i // tiles_per_batch
        t = i - b * tiles_per_batch
        return [pltpu.make_async_copy(stage, dst_prompt.at[b, pl.ds(N_META + t * rows, rows)], sem.at[0])]
    copies = [pltpu.make_async_copy(stage.at[pl.ds(b * N_META, N_META)], dst_prompt.at[b, pl.ds(0, N_META)],
                                    sem.at[b]) for b in range(n_batch)]
    copies.append(pltpu.make_async_copy(stage.at[pl.ds(n_batch * N_META, n_sample_rows)], dst_sample,
                                        sem.at[n_batch]))
    return copies


def _inproj_kernel(xf_ref, xt_ref, g1_ref, wa_ref, wb_ref, wg_ref, gb_ref, qg_ref, kg_ref,
                   zf_ref, zt_ref, gf_ref, gt_ref, kp_ref, vp_ref, ks_ref, vs_ref,
                   xn_s, stage, sem, *, n_frame_tiles, n_batch, frames_per_batch, n_sample_rows):
    i = pl.program_id(0)
    j = pl.program_id(1)
    bn = wa_ref.shape[1]
    assert ATT_QK % bn == 0 and ATT_V % bn == 0 and M_QK % bn == 0 and M_V % bn == 0 and bn % DV_ATT == 0
    j_k = ATT_QK // bn
    j_v = j_k + ATT_QK // bn
    j_mq = j_v + ATT_V // bn
    j_mk = j_mq + M_QK // bn
    j_mv = j_mk + M_QK // bn
    j_og = j_mv + M_V // bn
    heads_per_tile = bn // DV_ATT

    def tile(x_ref, z_ref, g_ref, tail):
        nrows = x_ref.shape[0]
        st = stage.at[pl.ds(0, nrows)]
        copies = functools.partial(_kv_copies, st, sem, i=i, tail=tail, n_batch=n_batch,
                                   frames_per_batch=frames_per_batch, n_sample_rows=n_sample_rows)

        @pl.when(j == 0)
        def _():
            xb = _rms(x_ref[...], g1_ref[...]).astype(BF16)
            xn_s[0:nrows, :] = xb
            gates = _dot(xb, wg_ref[...]) + gb_ref[...]
            lane = lax.broadcasted_iota(jnp.int32, gates.shape, 1)
            g_ref[...] = jnp.where(lane < H_M, gates, _log_sigmoid(gates))

        acc = _dot(xn_s[0:nrows, :], jnp.where(j < j_og, wa_ref[...], wb_ref[...]))

        def group_norm(gain_ref, scale):
            parts = []
            for gi in range(bn // DQK_ATT):
                y = _rms(acc[:, gi * DQK_ATT:(gi + 1) * DQK_ATT], gain_ref[...])
                parts.append(y if scale is None else y * scale)
            return jnp.concatenate(parts, axis=1)

        def stage_heads(y, jj, j0, dst_prompt, dst_sample):
            if jj == j0 and j0 != j_k:
                for c in copies(kp_ref, ks_ref):
                    c.wait()
            h0 = (jj - j0) * heads_per_tile
            st[:, h0:h0 + heads_per_tile, :] = y.reshape(nrows, heads_per_tile, DV_ATT)
            if jj == j0 + ATT_QK // bn - 1:
                for c in copies(dst_prompt, dst_sample):
                    c.start()

        @pl.when(j < j_k)
        def _():
            z_ref[...] = group_norm(qg_ref, DQK_ATT ** -0.5)

        for jj in range(j_k, j_v):
            @pl.when(j == jj)
            def _(jj=jj):
                y = group_norm(kg_ref, None)
                z_ref[...] = y
                stage_heads(y, jj, j_k, kp_ref, ks_ref)

        for jj in range(j_v, j_mq):
            @pl.when(j == jj)
            def _(jj=jj):
                z_ref[...] = acc
                stage_heads(acc, jj, j_v, vp_ref, vs_ref)

        @pl.when(j == j_mq)
        def _():
            for c in copies(vp_ref, vs_ref):
                c.wait()

        @pl.when(((j >= j_mq) & (j < j_mk)) | ((j >= j_mv) & (j < j_og)))
        def _():
            z_ref[...] = acc

        @pl.when((j >= j_mk) & (j < j_mv))
        def _():
            z_ref[...] = acc * (DQK_M ** -0.5)

        @pl.when(j >= j_og)
        def _():
            z_ref[...] = jax.nn.sigmoid(acc)

    @pl.when(i < n_frame_tiles)
    def _():
        tile(xf_ref, zf_ref, gf_ref, False)

    @pl.when(i == n_frame_tiles)
    def _():
        tile(xt_ref, zt_ref, gt_ref, True)


def _inproj(x_f, x_t, g1, w_all, layer, w_og, w_gate, gate_bias, qg, kg, n_batch, n_sample_streams):
    NF, D = x_f.shape
    NT = x_t.shape[0]
    bm, bn = PROJ_ROWS, PROJ_COLS
    nft = NF // bm
    nj = N_MAIN // bn
    nja = (N_MAIN - M_V) // bn
    S = NF // n_batch
    L = N_META + S
    n_sample_rows = n_sample_streams * N_META
    assert NF % bm == 0 and S % bm == 0 and N_MAIN % bn == 0 and NT <= bm
    fr = lambda i: jnp.minimum(i, nft - 1)
    any_spec = pl.BlockSpec(memory_space=pl.ANY)
    return pl.pallas_call(
        functools.partial(_inproj_kernel, n_frame_tiles=nft, n_batch=n_batch, frames_per_batch=S,
                          n_sample_rows=n_sample_rows),
        grid=(nft + 1, nj),
        in_specs=[
            pl.BlockSpec((bm, D), lambda i, j: (fr(i), 0)),
            pl.BlockSpec((NT, D), lambda i, j: (0, 0)),
            pl.BlockSpec((1, D), lambda i, j: (0, 0)),
            pl.BlockSpec((None, D, bn), lambda i, j: (layer, 0, jnp.minimum(j, nja - 1))),
            pl.BlockSpec((D, bn), lambda i, j: (0, jnp.maximum(j - nja, 0))),
            pl.BlockSpec((D, LANES), lambda i, j: (0, 0)),
            pl.BlockSpec((1, LANES), lambda i, j: (0, 0)),
            pl.BlockSpec((1, DQK_ATT), lambda i, j: (0, 0)),
            pl.BlockSpec((1, DQK_ATT), lambda i, j: (0, 0)),
        ],
        out_specs=(
            pl.BlockSpec((bm, bn), lambda i, j: (fr(i), jnp.where(i < nft, j, nj - 1))),
            pl.BlockSpec((NT, bn), lambda i, j: (0, jnp.where(i < nft, 0, j))),
            pl.BlockSpec((bm, LANES), lambda i, j: (fr(i), 0)),
            pl.BlockSpec((NT, LANES), lambda i, j: (0, 0)),
            any_spec, any_spec, any_spec, any_spec,
        ),
        out_shape=(
            jax.ShapeDtypeStruct((NF, N_MAIN), F32),
            jax.ShapeDtypeStruct((NT, N_MAIN), F32),
            jax.ShapeDtypeStruct((NF, LANES), F32),
            jax.ShapeDtypeStruct((NT, LANES), F32),
            jax.ShapeDtypeStruct((n_batch, L, H_ATT, 2 * DQK_ATT), F32),
            jax.ShapeDtypeStruct((n_batch, L, H_ATT, DV_ATT), F32),
            jax.ShapeDtypeStruct((n_sample_rows, H_ATT, 2 * DQK_ATT), F32),
            jax.ShapeDtypeStruct((n_sample_rows, H_ATT, DV_ATT), F32),
        ),
        scratch_shapes=[
            pltpu.VMEM((bm, D), BF16),
            pltpu.VMEM((bm, H_ATT, DV_ATT), F32),
            pltpu.SemaphoreType.DMA((n_batch + 1,)),
        ],
        compiler_params=pltpu.CompilerParams(
            dimension_semantics=("arbitrary", "arbitrary"), vmem_limit_bytes=VMEM_LIMIT),
        name="in_proj",
    )(x_f, x_t, g1, w_all, w_og, w_gate, gate_bias, qg, kg)


def _diff_lambda(lq1_ref, lk1_ref, lq2_ref, lk2_ref, lam_init):
    a = jnp.sum(lq1_ref[...] * lk1_ref[...], axis=-1, keepdims=True)
    b = jnp.sum(lq2_ref[...] * lk2_ref[...], axis=-1, keepdims=True)
    return jnp.exp(a) - jnp.exp(b) + lam_init


def _subln(o, sg_ref, lam_init):
    return _rms(o, sg_ref[...]) * (1.0 - lam_init)


def _pad_rows(dst, src):
    dst[...] = jnp.zeros(dst.shape, dst.dtype)
    dst[0:src.shape[0], :] = src[...].astype(dst.dtype)


def _attn_prompt_kernel(q_ref, k_ref, v_ref, km_ref, vm_ref, bd_ref, bp_ref, bm_ref, bf_ref,
                        lq1_ref, lk1_ref, lq2_ref, lk2_ref, sg_ref, o_ref,
                        kb_s, vb_s, kmb_s, vmb_s, *, lam_init):
    T = ATT_TILE
    n_tiles = q_ref.shape[0] // T
    kb_s[...] = k_ref[...].astype(BF16)
    vb_s[...] = v_ref[...].astype(BF16)
    _pad_rows(kmb_s, km_ref)
    _pad_rows(vmb_s, vm_ref)
    lam = _diff_lambda(lq1_ref, lk1_ref, lq2_ref, lk2_ref, lam_init)
    far = bf_ref[0:1, 0:1]

    for j in range(n_tiles):
        rows = slice(j * T, (j + 1) * T)
        regions = [(kmb_s, vmb_s, slice(0, LANES), bm_ref[min(j, 1)])]
        if j >= 2:
            regions.append((kb_s, vb_s, slice(0, (j - 1) * T), far))
        if j >= 1:
            regions.append((kb_s, vb_s, slice((j - 1) * T, j * T), bp_ref[...]))
        regions.append((kb_s, vb_s, rows, bd_ref[...]))

        probs, denom = [], []
        for c in range(2):
            cols = slice(c * DQK_ATT, (c + 1) * DQK_ATT)
            qc = q_ref[rows, cols].astype(BF16)
            ss = [_dot_nt(qc, kr[rs, cols]) + br for kr, _, rs, br in regions]
            m = functools.reduce(jnp.maximum, [jnp.max(s, axis=-1, keepdims=True) for s in ss])
            pc = [jnp.exp(s - m) for s in ss]
            denom.append(functools.reduce(jnp.add, [jnp.sum(p, axis=-1, keepdims=True) for p in pc]))
            probs.append(pc)
        a0 = 1.0 / denom[0]
        a1 = lam / denom[1]
        o = None
        for r, (_, vr, rs, _) in enumerate(regions):
            w = (probs[0][r] * a0 - probs[1][r] * a1).astype(BF16)
            t = _dot(w, vr[rs, :])
            o = t if o is None else o + t
        o_ref[rows, :] = _subln(o, sg_ref, lam_init)


def _attn_prompt(z_f, z_t, bias, lam_vecs, sg, B, S, lam_init):
    T = ATT_TILE
    qkw = 2 * DQK_ATT
    kcol = ATT_QK // qkw
    vcol = 2 * ATT_QK // DV_ATT
    b_diag, b_prev, b_meta, b_far, _, _ = bias
    vec = pl.BlockSpec((1, DQK_ATT), lambda b, h: (0, 0))
    return pl.pallas_call(
        functools.partial(_attn_prompt_kernel, lam_init=lam_init),
        grid=(B, H_ATT),
        in_specs=[
            pl.BlockSpec((S, qkw), lambda b, h: (b, h)),
            pl.BlockSpec((S, qkw), lambda b, h: (b, kcol + h)),
            pl.BlockSpec((S, DV_ATT), lambda b, h: (b, vcol + h)),
            pl.BlockSpec((N_META, qkw), lambda b, h: (b, kcol + h)),
            pl.BlockSpec((N_META, DV_ATT), lambda b, h: (b, vcol + h)),
            pl.BlockSpec((None, T, T), lambda b, h: (h, 0, 0)),
            pl.BlockSpec((None, T, T), lambda b, h: (h, 0, 0)),
            pl.BlockSpec((None, 2, T, LANES), lambda b, h: (h, 0, 0, 0)),
            pl.BlockSpec((None, 8, T), lambda b, h: (h, 0, 0)),
            vec, vec, vec, vec,
            pl.BlockSpec((1, DV_ATT), lambda b, h: (0, 0)),
        ],
        out_specs=pl.BlockSpec((S, DV_ATT), lambda b, h: (b, h)),
        out_shape=jax.ShapeDtypeStruct((B * S, ATT_V), F32),
        scratch_shapes=[
            pltpu.VMEM((S, qkw), BF16),
            pltpu.VMEM((S, DV_ATT), BF16),
            pltpu.VMEM((LANES, qkw), BF16),
            pltpu.VMEM((LANES, DV_ATT), BF16),
        ],
        compiler_params=pltpu.CompilerParams(
            dimension_semantics=("arbitrary", "arbitrary"), vmem_limit_bytes=VMEM_LIMIT),
        name="attn_prompt",
    )(z_f, z_f, z_f, z_t, z_t, b_diag, b_prev, b_meta, b_far, *lam_vecs, sg)


def _attn_small_kernel(*refs, lam_init, has_cache):
    if has_cache:
        (q_ref, kc_ref, vc_ref, kn_ref, vn_ref, b_ref,
         lq1_ref, lk1_ref, lq2_ref, lk2_ref, sg_ref, o_ref, kpad, vpad, kb_s, vb_s) = refs
        past_len = kb_s.shape[0]
        chunk = kc_ref.shape[0]
        c = pl.program_id(1)
        r0 = pl.multiple_of(c * chunk, chunk)
        kb_s[pl.ds(r0, chunk), :] = kc_ref[...].reshape(chunk, ATT_QK).astype(BF16)
        vb_s[pl.ds(r0, chunk), :] = vc_ref[...].reshape(chunk, ATT_V).astype(BF16)

        @pl.when(c == past_len // chunk - 1)
        def _():
            _attn_small_body(q_ref, kn_ref, vn_ref, b_ref, lq1_ref, lk1_ref, lq2_ref, lk2_ref, sg_ref, o_ref,
                             kpad, vpad, kb_s, vb_s, lam_init=lam_init, past_len=past_len)
    else:
        (q_ref, kn_ref, vn_ref, b_ref,
         lq1_ref, lk1_ref, lq2_ref, lk2_ref, sg_ref, o_ref, kpad, vpad) = refs
        _attn_small_body(q_ref, kn_ref, vn_ref, b_ref, lq1_ref, lk1_ref, lq2_ref, lk2_ref, sg_ref, o_ref,
                         kpad, vpad, None, None, lam_init=lam_init, past_len=0)


def _attn_small_body(q_ref, kn_ref, vn_ref, b_ref, lq1_ref, lk1_ref, lq2_ref, lk2_ref, sg_ref, o_ref,
                     kpad, vpad, kb_s, vb_s, *, lam_init, past_len):
    has_cache = past_len > 0
    lam = _diff_lambda(lq1_ref, lk1_ref, lq2_ref, lk2_ref, lam_init)
    _pad_rows(kpad, kn_ref)
    _pad_rows(vpad, vn_ref)
    for h in range(H_ATT):
        vsl = slice(h * DV_ATT, (h + 1) * DV_ATT)
        outs = []
        for c in range(2):
            sl = slice(h * 2 * DQK_ATT + c * DQK_ATT, h * 2 * DQK_ATT + (c + 1) * DQK_ATT)
            qc = q_ref[:, sl].astype(BF16)
            s_new = _dot_nt(qc, kpad[:, sl]) + b_ref[h, :, past_len:]
            m = jnp.max(s_new, axis=-1, keepdims=True)
            if has_cache:
                s_old = _dot_nt(qc, kb_s[:, sl]) + b_ref[h, :, :past_len]
                m = jnp.maximum(m, jnp.max(s_old, axis=-1, keepdims=True))
            p_new = jnp.exp(s_new - m)
            l = jnp.sum(p_new, axis=-1, keepdims=True)
            o = _dot(p_new.astype(BF16), vpad[:, vsl])
            if has_cache:
                p_old = jnp.exp(s_old - m)
                l = l + jnp.sum(p_old, axis=-1, keepdims=True)
                o = o + _dot(p_old.astype(BF16), vb_s[:, vsl])
            outs.append(o / l)
        o_ref[:, vsl] = _subln(outs[0] - lam * outs[1], sg_ref, lam_init)


def _attn_small(z_t, row0, n_streams, bias, lam_vecs, sg, lam_init, cache=None):
    vec = pl.BlockSpec((1, DQK_ATT), lambda b, c: (0, 0))
    in_specs = [pl.BlockSpec((N_META, ATT_QK), lambda b, c: (row0 + b, 0))]
    args = [z_t]
    scratch = [pltpu.VMEM((LANES, ATT_QK), BF16), pltpu.VMEM((LANES, ATT_V), BF16)]
    n_chunks = 1
    if cache is not None:
        ck, cv, layer = cache
        past_len = ck.shape[2]
        chunk = min(past_len, CACHE_CHUNK)
        assert past_len % chunk == 0
        n_chunks = past_len // chunk
        in_specs += [
            pl.BlockSpec((None, None, chunk, H_ATT, 2 * DQK_ATT), lambda b, c: (layer, b, c, 0, 0)),
            pl.BlockSpec((None, None, chunk, H_ATT, DV_ATT), lambda b, c: (layer, b, c, 0, 0))]
        args += [ck, cv]
        scratch += [pltpu.VMEM((past_len, ATT_QK), BF16), pltpu.VMEM((past_len, ATT_V), BF16)]
    in_specs += [
        pl.BlockSpec((N_META, ATT_QK), lambda b, c: (row0 + b, 1)),
        pl.BlockSpec((N_META, ATT_V), lambda b, c: (row0 + b, 2 * ATT_QK // ATT_V)),
        pl.BlockSpec(bias.shape, lambda b, c: (0, 0, 0)),
        vec, vec, vec, vec,
        pl.BlockSpec((1, DV_ATT), lambda b, c: (0, 0)),
    ]
    args += [z_t, z_t, bias, *lam_vecs, sg]
    return pl.pallas_call(
        functools.partial(_attn_small_kernel, lam_init=lam_init, has_cache=cache is not None),
        grid=(n_streams, n_chunks),
        in_specs=in_specs,
        out_specs=pl.BlockSpec((N_META, ATT_V), lambda b, c: (b, 0)),
        out_shape=jax.ShapeDtypeStruct((n_streams * N_META, ATT_V), F32),
        scratch_shapes=scratch,
        compiler_params=pltpu.CompilerParams(
            dimension_semantics=("arbitrary", "arbitrary"), vmem_limit_bytes=VMEM_LIMIT),
        name="attn_cached" if cache is not None else "attn_meta",
    )(*args)


def _mlstm_kernel(*refs, n_blocks, has_init):
    refs = list(refs)
    qh_ref, kh_ref, vh_ref, gh_ref = refs[:4]
    pos = 4
    if n_blocks:
        qf_ref, kf_ref, vf_ref, gf_ref = refs[pos:pos + 4]
        pos += 4
    if has_init:
        c0_ref, n0_ref, m0_ref = refs[pos:pos + 3]
        pos += 3
    hh_ref = refs[pos]
    pos += 1
    if n_blocks:
        hf_ref = refs[pos]
        pos += 1
    c_ref, n_ref, m_ref = refs[pos:pos + 3]
    qs, ks, vs, gs = refs[pos + 3:pos + 7]

    t = pl.program_id(1)
    T = M_TILE

    @pl.when(t == 0)
    def _():
        if has_init:
            c_ref[...] = c0_ref[...]
            n_ref[...] = n0_ref[...]
            m_ref[...] = m0_ref[...]
        else:
            c_ref[...] = jnp.zeros(c_ref.shape, F32)
            n_ref[...] = jnp.zeros(n_ref.shape, F32)
            m_ref[...] = jnp.zeros(m_ref.shape, F32)
        nh = qh_ref.shape[0]
        for dst, src in ((qs, qh_ref), (ks, kh_ref), (vs, vh_ref)):
            dst[...] = jnp.zeros(dst.shape, F32)
            dst[0:nh, :] = src[...]
        lane = lax.broadcasted_iota(jnp.int32, gs.shape, 1)
        gs[...] = jnp.where(lane < H_M, NEG, 0.0)
        gs[0:nh, :] = gh_ref[...]

    if n_blocks:
        @pl.when(t > 0)
        def _():
            qs[...] = qf_ref[...]
            ks[...] = kf_ref[...]
            vs[...] = vf_ref[...]
            gs[...] = gf_ref[...]

    g = gs[...]
    row = lax.broadcasted_iota(jnp.int32, (T, T), 0)
    colid = lax.broadcasted_iota(jnp.int32, (T, T), 1)
    causal = row >= colid
    tril = jnp.where(causal, 1.0, 0.0).astype(F32)
    cum = jnp.dot(tril, g, preferred_element_type=F32, precision=lax.Precision.HIGHEST)
    g_t = g.T
    cum_t = cum.T
    last = cum[T - 1:T, :]

    for h in range(H_M):
        q = qs[:, h * DQK_M:(h + 1) * DQK_M]
        k = ks[:, h * DQK_M:(h + 1) * DQK_M]
        v = vs[:, h * DV_M:(h + 1) * DV_M]
        qb = q.astype(BF16)
        kb = k.astype(BF16)
        m_old = m_ref[h, 0:1, 0:1]
        c_old = c_ref[h]
        n_old = n_ref[h]
        bh_c = _col(cum, H_M + h)
        ih_c = _col(g, h)
        bh_r = cum_t[H_M + h:H_M + h + 1, :]
        ih_r = g_t[h:h + 1, :]
        d = jnp.where(causal, bh_c - bh_r + ih_r, NEG)
        inter = bh_c + m_old
        mt = jnp.maximum(inter, jnp.max(d, axis=-1, keepdims=True))
        a = jnp.exp(d - mt) * _dot_nt(qb, kb)
        iw = jnp.exp(inter - mt)
        num = _dot(a.astype(BF16), v.astype(BF16)) + iw * _dot_nt(qb, c_old.astype(BF16))
        den = jnp.sum(a, axis=-1, keepdims=True) + iw * jnp.sum(q * n_old, axis=-1, keepdims=True)
        den = jnp.maximum(jnp.abs(den), jnp.exp(-mt))
        hout = num / den
        sl = slice(h * DV_M, (h + 1) * DV_M)

        @pl.when(t == 0)
        def _():
            hh_ref[:, sl] = hout[0:hh_ref.shape[0], :]

        if n_blocks:
            @pl.when(t > 0)
            def _():
                hf_ref[:, sl] = hout

        bl = _col(last, H_M + h)
        g_c = bl - bh_c + ih_c
        m_new = jnp.maximum(bl + m_old, jnp.max(g_c, axis=0, keepdims=True))
        sw = jnp.exp(g_c - m_new)
        dec = jnp.exp(bl + m_old - m_new)
        c_ref[h] = dec * c_old + _dot(v.T.astype(BF16), (sw * k).astype(BF16))
        n_ref[h] = dec * n_old + jnp.sum(sw * k, axis=0, keepdims=True)
        m_ref[h] = jnp.broadcast_to(m_new, m_ref.shape[1:])


def _mlstm(z_t, g_t, row0, n_streams, frames=None, init=None):
    T = M_TILE
    qcol = (2 * ATT_QK + ATT_V) // M_QK
    vcol = (2 * ATT_QK + ATT_V + 2 * M_QK) // M_V
    nh = N_META
    in_specs = [
        pl.BlockSpec((nh, M_QK), lambda b, t: (row0 + b, qcol)),
        pl.BlockSpec((nh, M_QK), lambda b, t: (row0 + b, qcol + 1)),
        pl.BlockSpec((nh, M_V), lambda b, t: (row0 + b, vcol)),
        pl.BlockSpec((nh, LANES), lambda b, t: (row0 + b, 0)),
    ]
    args = [z_t, z_t, z_t, g_t]
    n_blocks = 0
    if frames is not None:
        z_f, g_f, fps = frames
        n_blocks = fps // T
        fidx = lambda b, t: b * n_blocks + jnp.maximum(t - 1, 0)
        in_specs += [
            pl.BlockSpec((T, M_QK), lambda b, t: (fidx(b, t), qcol)),
            pl.BlockSpec((T, M_QK), lambda b, t: (fidx(b, t), qcol + 1)),
            pl.BlockSpec((T, M_V), lambda b, t: (fidx(b, t), vcol)),
            pl.BlockSpec((T, LANES), lambda b, t: (fidx(b, t), 0)),
        ]
        args += [z_f, z_f, z_f, g_f]
    state_specs = (
        pl.BlockSpec((None, H_M, DV_M, DQK_M), lambda b, t: (b, 0, 0, 0)),
        pl.BlockSpec((None, H_M, 1, DQK_M), lambda b, t: (b, 0, 0, 0)),
        pl.BlockSpec((None, H_M, 8, LANES), lambda b, t: (b, 0, 0, 0)),
    )
    if init is not None:
        in_specs += list(state_specs)
        args += list(init)
    out_specs = [pl.BlockSpec((nh, M_V), lambda b, t: (b, 0))]
    out_shape = [jax.ShapeDtypeStruct((n_streams * nh, M_V), F32)]
    if n_blocks:
        out_specs.append(pl.BlockSpec((T, M_V), lambda b, t: (fidx(b, t), 0)))
        out_shape.append(jax.ShapeDtypeStruct((n_streams * fps, M_V), F32))
    out_specs += list(state_specs)
    out_shape += [
        jax.ShapeDtypeStruct((n_streams, H_M, DV_M, DQK_M), F32),
        jax.ShapeDtypeStruct((n_streams, H_M, 1, DQK_M), F32),
        jax.ShapeDtypeStruct((n_streams, H_M, 8, LANES), F32),
    ]
    return pl.pallas_call(
        functools.partial(_mlstm_kernel, n_blocks=n_blocks, has_init=init is not None),
        grid=(n_streams, 1 + n_blocks),
        in_specs=in_specs,
        out_specs=tuple(out_specs),
        out_shape=tuple(out_shape),
        scratch_shapes=[
            pltpu.VMEM((T, M_QK), F32), pltpu.VMEM((T, M_QK), F32),
            pltpu.VMEM((T, M_V), F32), pltpu.VMEM((T, LANES), F32),
        ],
        compiler_params=pltpu.CompilerParams(
            dimension_semantics=("arbitrary", "arbitrary"), vmem_limit_bytes=VMEM_LIMIT),
        name="mlstm_prompt" if n_blocks else "mlstm_cached",
    )(*args)


def _merge_kernel(xf_ref, xt_ref, attf_ref, attt_ref, hmf_ref, hmt_ref, zf_ref, zt_ref, mg_ref, wo_ref,
                  of_ref, ot_ref, *, n_frame_tiles):
    is_tail = pl.program_id(0) >= n_frame_tiles
    x = jnp.where(is_tail, xt_ref[...], xf_ref[...])
    att = jnp.where(is_tail, attt_ref[...], attf_ref[...])
    hm = jnp.where(is_tail, hmt_ref[...], hmf_ref[...])
    og = jnp.where(is_tail, zt_ref[...], zf_ref[...])
    parts = []
    for h in range(H_M):
        sl = slice(h * DV_M, (h + 1) * DV_M)
        parts.append(_rms(hm[:, sl], mg_ref[:, sl]) * og[:, sl])
    hmn = jnp.concatenate(parts, axis=1).astype(BF16)
    y = x + _dot(att.astype(BF16), wo_ref[0:ATT_V, :]) + _dot(hmn, wo_ref[ATT_V:, :])

    @pl.when(jnp.logical_not(is_tail))
    def _():
        of_ref[...] = y

    @pl.when(is_tail)
    def _():
        ot_ref[...] = y


def _merge(x_f, x_t, att_f, att_t, hm_f, hm_t, z_f, z_t, mg, w_out, layer):
    NF, D = x_f.shape
    bm = x_t.shape[0]
    nft = NF // bm
    assert NF % bm == 0 and bm == MERGE_ROWS
    ogcol = (N_MAIN - M_V) // M_V
    fr = lambda i: (jnp.minimum(i, nft - 1), 0)
    tl = lambda i: (0, 0)
    return pl.pallas_call(
        functools.partial(_merge_kernel, n_frame_tiles=nft),
        grid=(nft + 1,),
        in_specs=[
            pl.BlockSpec((bm, D), fr), pl.BlockSpec((bm, D), tl),
            pl.BlockSpec((bm, ATT_V), fr), pl.BlockSpec((bm, ATT_V), tl),
            pl.BlockSpec((bm, M_V), fr), pl.BlockSpec((bm, M_V), tl),
            pl.BlockSpec((bm, M_V), lambda i: (jnp.minimum(i, nft - 1), ogcol)),
            pl.BlockSpec((bm, M_V), lambda i: (0, ogcol)),
            pl.BlockSpec((1, M_V), tl),
            pl.BlockSpec((None, D, D), lambda i: (layer, 0, 0)),
        ],
        out_specs=(pl.BlockSpec((bm, D), fr), pl.BlockSpec((bm, D), tl)),
        out_shape=(jax.ShapeDtypeStruct((NF, D), F32), jax.ShapeDtypeStruct((bm, D), F32)),
        compiler_params=pltpu.CompilerParams(
            dimension_semantics=("arbitrary",), vmem_limit_bytes=VMEM_LIMIT),
        name="merge_out_proj",
    )(x_f, x_t, att_f, att_t, hm_f, hm_t, z_f, z_t, mg, w_out)


def _ffn_kernel(xf_ref, xt_ref, g2_ref, wu_ref, wd_ref, of_ref, ot_ref, xn_s, *, n_frame_tiles):
    i = pl.program_id(0)
    k = pl.program_id(1)

    def tile(x_ref, o_ref):
        nrows = x_ref.shape[0]

        @pl.when(k == 0)
        def _():
            x = x_ref[...]
            xn_s[0:nrows, :] = _rms(x, g2_ref[...]).astype(BF16)
            o_ref[...] = x

        u = jnp.maximum(_dot(xn_s[0:nrows, :], wu_ref[...]), 0.0)
        o_ref[...] += _dot((u * u).astype(BF16), wd_ref[...])

    @pl.when(i < n_frame_tiles)
    def _():
        tile(xf_ref, of_ref)

    @pl.when(i == n_frame_tiles)
    def _():
        tile(xt_ref, ot_ref)


def _ffn(x_f, x_t, g2, w_up, w_down, layer):
    NF, D = x_f.shape
    NT = x_t.shape[0]
    FF = w_up.shape[2]
    bm, fc = FF_ROWS, FF_COLS
    nft = NF // bm
    assert NF % bm == 0 and FF % fc == 0 and NT <= bm
    fr = lambda i, k: (jnp.minimum(i, nft - 1), 0)
    tl = lambda i, k: (0, 0)
    return pl.pallas_call(
        functools.partial(_ffn_kernel, n_frame_tiles=nft),
        grid=(nft + 1, FF // fc),
        in_specs=[
            pl.BlockSpec((bm, D), fr),
            pl.BlockSpec((NT, D), tl),
            pl.BlockSpec((1, D), tl),
            pl.BlockSpec((None, D, fc), lambda i, k: (layer, 0, k)),
            pl.BlockSpec((None, fc, D), lambda i, k: (layer, k, 0)),
        ],
        out_specs=(pl.BlockSpec((bm, D), fr), pl.BlockSpec((NT, D), tl)),
        out_shape=(jax.ShapeDtypeStruct((NF, D), F32), jax.ShapeDtypeStruct((NT, D), F32)),
        scratch_shapes=[pltpu.VMEM((bm, D), BF16)],
        compiler_params=pltpu.CompilerParams(
            dimension_semantics=("arbitrary", "arbitrary"), vmem_limit_bytes=VMEM_LIMIT),
        name="ffn",
    )(x_f, x_t, g2, w_up, w_down)


def kernel(x_prompt, x_sample, cache_k, cache_v, state_C, state_n, state_m, meta_tokens, rel_bias, norm1_g, w_in, b_i, b_f, q_norm_g, k_norm_g, lambda_q1, lambda_k1, lambda_q2, lambda_k2, subln_g, mnorm_g, w_out, norm2_g, w_up, w_down):
    B, S, D = x_prompt.shape
    Bd, Td, _ = x_sample.shape
    depth = w_in.shape[0]
    Pc = cache_k.shape[2]
    assert Td == N_META and S % ATT_TILE == 0 and S % M_TILE == 0 and Pc % LANES == 0
    assert ATT_TILE % CHUNK == 0 and CHUNK & (CHUNK - 1) == 0 and ATT_TILE >= MAX_DISTANCE

    n_frames = B * S
    n_meta = B * N_META
    n_samp = Bd * Td
    assert n_meta + n_samp <= TAIL_ROWS
    srow = n_meta // N_META

    meta = jnp.broadcast_to(meta_tokens.astype(F32)[None], (B, N_META, D)).reshape(n_meta, D)
    x_f = x_prompt.astype(F32).reshape(n_frames, D)
    x_t = jnp.concatenate([meta, x_sample.astype(F32).reshape(n_samp, D),
                           jnp.zeros((TAIL_ROWS - n_meta - n_samp, D), F32)], axis=0)

    bias = _bias_tiles(rel_bias.astype(F32), Pc, Td)
    ck = cache_k.astype(F32)
    cv = cache_v.astype(F32)
    zeros_pad_att = jnp.zeros((TAIL_ROWS - n_meta - n_samp, ATT_V), F32)

    g_off = 2 * ATT_QK + ATT_V + 2 * M_QK + M_V
    w_in_b, w_out_b, w_up_b, w_down_b = (w.astype(BF16) for w in (w_in, w_out, w_up, w_down))
    outs = {name: [] for name in ("kp", "vp", "Cp", "np", "mp", "ks", "vs", "Cs", "ns", "ms")}

    for l in range(depth):
        lam_init = 0.8 - 0.6 * math.exp(-0.3 * l)
        w_og = w_in[l, :, g_off + 2 * H_M:].astype(BF16)
        w_gate = jnp.pad(w_in[l, :, g_off:g_off + 2 * H_M], ((0, 0), (0, LANES - 2 * H_M))).astype(BF16)
        gate_bias = jnp.pad(jnp.concatenate([b_i[l], b_f[l]]).astype(F32), (0, LANES - 2 * H_M))[None]
        lam_vecs = [v[l].astype(F32)[None] for v in (lambda_q1, lambda_k1, lambda_q2, lambda_k2)]
        sg = subln_g[l].astype(F32)[None]

        z_f, z_t, g_f, g_t, k_p, v_p, k_s, v_s = _inproj(
            x_f, x_t, norm1_g[l].astype(F32)[None], w_in_b, l, w_og, w_gate, gate_bias,
            q_norm_g[l].astype(F32)[None], k_norm_g[l].astype(F32)[None], B, Bd)

        att_f = _attn_prompt(z_f, z_t, bias, lam_vecs, sg, B, S, lam_init)
        att_m = _attn_small(z_t, 0, B, bias[4], lam_vecs, sg, lam_init)
        att_s = _attn_small(z_t, srow, Bd, bias[5], lam_vecs, sg, lam_init, cache=(ck, cv, l))
        att_t = jnp.concatenate([att_m, att_s, zeros_pad_att], axis=0)

        hm_m, hm_f, C_p, n_p, m_p = _mlstm(z_t, g_t, 0, B, frames=(z_f, g_f, S))
        init = (state_C[l].astype(F32), state_n[l].astype(F32)[:, :, None, :],
                jnp.broadcast_to(state_m[l].astype(F32)[:, :, None, None], (Bd, H_M, 8, LANES)))
        hm_s, C_s, n_s, m_s = _mlstm(z_t, g_t, srow, Bd, init=init)
        hm_t = jnp.concatenate([hm_m, hm_s, zeros_pad_att], axis=0)

        x_f, x_t = _merge(x_f, x_t, att_f, att_t, hm_f, hm_t, z_f, z_t,
                          mnorm_g[l].astype(F32)[None], w_out_b, l)
        x_f, x_t = _ffn(x_f, x_t, norm2_g[l].astype(F32)[None], w_up_b, w_down_b, l)

        outs["kp"].append(k_p)
        outs["vp"].append(v_p)
        outs["Cp"].append(C_p)
        outs["np"].append(n_p[:, :, 0, :])
        outs["mp"].append(m_p[:, :, 0, 0])
        outs["ks"].append(k_s.reshape(Bd, Td, H_ATT, 2 * DQK_ATT))
        outs["vs"].append(v_s.reshape(Bd, Td, H_ATT, DV_ATT))
        outs["Cs"].append(C_s)
        outs["ns"].append(n_s[:, :, 0, :])
        outs["ms"].append(m_s[:, :, 0, 0])

    y_prompt = x_f.reshape(B, S, D)
    y_sample = x_t[n_meta:n_meta + n_samp].reshape(Bd, Td, D)
    return (y_prompt, y_sample,
            jnp.stack(outs["kp"]), jnp.stack(outs["vp"]), jnp.stack(outs["Cp"]), jnp.stack(outs["np"]),
            jnp.stack(outs["mp"]),
            jnp.stack(outs["ks"]), jnp.stack(outs["vs"]), jnp.stack(outs["Cs"]), jnp.stack(outs["ns"]),
            jnp.stack(outs["ms"]))
```

```python
import functools
import math

import jax
import jax.numpy as jnp
from jax import lax
from jax.experimental import pallas as pl
from jax.experimental.pallas import tpu as pltpu

F32 = jnp.float32
BF16 = jnp.bfloat16

CHUNK = 64
N_META = 16
H_ATT = 4
DQK_ATT = 128
DV_ATT = 256
H_M = 4
DQK_M = 128
DV_M = 256
N_BUCKETS = 32
MAX_DISTANCE = 128
EPS = 1e-6
NEG = -1e30

ATT_QK = H_ATT * 2 * DQK_ATT
ATT_V = H_ATT * DV_ATT
M_QK = H_M * DQK_M
M_V = H_M * DV_M
N_MAIN = 2 * ATT_QK + ATT_V + 2 * M_QK + M_V + M_V

LANES = 128
ATT_TILE = 256
M_TILE = 256
TAIL_ROWS = 256
PROJ_ROWS = 1024
PROJ_COLS = 512
MERGE_ROWS = 256
CACHE_CHUNK = 1024
FF_ROWS = 1024
FF_COLS = 512
VMEM_LIMIT = 56 * 1024 * 1024

_NT = (((1,), (1,)), ((), ()))


def _dot(a, b):
    return jnp.dot(a, b, preferred_element_type=F32)


def _dot_nt(a, b):
    return lax.dot_general(a, b, _NT, preferred_element_type=F32)


def _log_sigmoid(x):
    return jnp.minimum(x, 0.0) - jnp.log1p(jnp.exp(-jnp.abs(x)))


def _col(a, idx):
    lane = lax.broadcasted_iota(jnp.int32, a.shape, 1)
    return jnp.sum(jnp.where(lane == idx, a, 0.0), axis=1, keepdims=True)


def _rms(x, gain):
    ms = jnp.mean(x * x, axis=-1, keepdims=True)
    return x * lax.rsqrt(ms + EPS) * gain


def _bias_from_rel(rel, rb_ref, h):
    nb = N_BUCKETS // 2
    max_exact = nb // 2
    ret = jnp.where(rel > 0, nb, 0)
    n = jnp.abs(rel)
    nf = jnp.maximum(n, 1).astype(F32)
    large = max_exact + (jnp.log(nf / max_exact) / math.log(MAX_DISTANCE / max_exact)
                         * (nb - max_exact)).astype(jnp.int32)
    large = jnp.minimum(large, nb - 1)
    bucket = ret + jnp.where(n < max_exact, n, large)
    out = jnp.zeros(rel.shape, F32)
    for b in range(N_BUCKETS):
        out = jnp.where(bucket == b, rb_ref[b, h], out)
    return out


def _bias_kernel(rb_ref, diag_ref, prev_ref, meta_ref, far_ref, mm_ref, samp_ref, *, past_len, dec_seq):
    h = pl.program_id(0)
    shift = CHUNK.bit_length() - 1
    T = ATT_TILE

    qi = lax.broadcasted_iota(jnp.int32, (T, T), 0)
    kj = lax.broadcasted_iota(jnp.int32, (T, T), 1)
    bias = _bias_from_rel(kj - qi, rb_ref, h)
    diag_ref[...] = jnp.where((kj >> shift) <= (qi >> shift), bias, NEG)
    prev_ref[...] = _bias_from_rel(kj - T - qi, rb_ref, h)
    far_ref[...] = _bias_from_rel(jnp.full(far_ref.shape, -(2 * MAX_DISTANCE + 1), jnp.int32), rb_ref, h)

    qi = lax.broadcasted_iota(jnp.int32, (T, LANES), 0)
    kj = lax.broadcasted_iota(jnp.int32, (T, LANES), 1)
    for var in range(2):
        bias = _bias_from_rel(kj - (N_META + qi + var * T), rb_ref, h)
        meta_ref[var] = jnp.where(kj < N_META, bias, NEG)

    qi = lax.broadcasted_iota(jnp.int32, (N_META, LANES), 0)
    kj = lax.broadcasted_iota(jnp.int32, (N_META, LANES), 1)
    mm_ref[...] = jnp.where(kj < N_META, _bias_from_rel(kj - qi, rb_ref, h), NEG)

    width = samp_ref.shape[-1]
    qi = lax.broadcasted_iota(jnp.int32, (dec_seq, width), 0) + past_len
    kj = lax.broadcasted_iota(jnp.int32, (dec_seq, width), 1)
    bias = _bias_from_rel(kj - qi, rb_ref, h)
    ok = ((kj >> shift) <= (qi >> shift)) & (kj < past_len + dec_seq)
    samp_ref[...] = jnp.where(ok, bias, NEG)


def _bias_tiles(rel_bias, past_len, dec_seq):
    T = ATT_TILE
    sw = past_len + LANES
    out_shape = (
        jax.ShapeDtypeStruct((H_ATT, T, T), F32),
        jax.ShapeDtypeStruct((H_ATT, T, T), F32),
        jax.ShapeDtypeStruct((H_ATT, 2, T, LANES), F32),
        jax.ShapeDtypeStruct((H_ATT, 8, T), F32),
        jax.ShapeDtypeStruct((H_ATT, N_META, LANES), F32),
        jax.ShapeDtypeStruct((H_ATT, dec_seq, sw), F32),
    )
    out_specs = (
        pl.BlockSpec((None, T, T), lambda h: (h, 0, 0)),
        pl.BlockSpec((None, T, T), lambda h: (h, 0, 0)),
        pl.BlockSpec((None, 2, T, LANES), lambda h: (h, 0, 0, 0)),
        pl.BlockSpec((None, 8, T), lambda h: (h, 0, 0)),
        pl.BlockSpec((None, N_META, LANES), lambda h: (h, 0, 0)),
        pl.BlockSpec((None, dec_seq, sw), lambda h: (h, 0, 0)),
    )
    return pl.pallas_call(
        functools.partial(_bias_kernel, past_len=past_len, dec_seq=dec_seq),
        grid=(H_ATT,),
        in_specs=[pl.BlockSpec(memory_space=pltpu.SMEM)],
        out_specs=out_specs,
        out_shape=out_shape,
        name="rel_bias_tiles",
    )(rel_bias)


def _kv_copies(stage, sem, dst_prompt, dst_sample, i, *, tail, n_batch, frames_per_batch, n_sample_rows):
    if not tail:
        rows = stage.shape[0]
        tiles_per_batch = frames_per_batch // rows
        b = i // tiles_per_batch
        t = i - b * tiles_per_batch
        return [pltpu.make_async_copy(stage, dst_prompt.at[b, pl.ds(N_META + t * rows, rows)], sem.at[0])]
    copies = [pltpu.make_async_copy(stage.at[pl.ds(b * N_META, N_META)], dst_prompt.at[b, pl.ds(0, N_META)],
                                    sem.at[b]) for b in range(n_batch)]
    copies.append(pltpu.make_async_copy(stage.at[pl.ds(n_batch * N_META, n_sample_rows)], dst_sample,
                                        sem.at[n_batch]))
    return copies


def _inproj_kernel(xf_ref, xt_ref, g1_ref, wa_ref, wb_ref, wg_ref, gb_ref, qg_ref, kg_ref,
                   zf_ref, zt_ref, gf_ref, gt_ref, kp_ref, vp_ref, ks_ref, vs_ref,
                   xn_s, stage, sem, *, n_frame_tiles, n_batch, frames_per_batch, n_sample_rows):
    i = pl.program_id(0)
    j = pl.program_id(1)
    bn = wa_ref.shape[1]
    assert ATT_QK % bn == 0 and ATT_V % bn == 0 and M_QK % bn == 0 and M_V % bn == 0 and bn % DV_ATT == 0
    j_k = ATT_QK // bn
    j_v = j_k + ATT_QK // bn
    j_mq = j_v + ATT_V // bn
    j_mk = j_mq + M_QK // bn
    j_mv = j_mk + M_QK // bn
    j_og = j_mv + M_V // bn
    heads_per_tile = bn // DV_ATT

    def tile(x_ref, z_ref, g_ref, tail):
        nrows = x_ref.shape[0]
        st = stage.at[pl.ds(0, nrows)]
        copies = functools.partial(_kv_copies, st, sem, i=i, tail=tail, n_batch=n_batch,
                                   frames_per_batch=frames_per_batch, n_sample_rows=n_sample_rows)

        @pl.when(j == 0)
        def _():
            xb = _rms(x_ref[...], g1_ref[...]).astype(BF16)
            xn_s[0:nrows, :] = xb
            gates = _dot(xb, wg_ref[...]) + gb_ref[...]
            lane = lax.broadcasted_iota(jnp.int32, gates.shape, 1)
            g_ref[...] = jnp.where(lane < H_M, gates, _log_sigmoid(gates))

        acc = _dot(xn_s[0:nrows, :], jnp.where(j < j_og, wa_ref[...], wb_ref[...]))

        is_norm = j < j_v
        is_gate = j >= j_og
        gain = jnp.where(j < j_k, qg_ref[...] * (DQK_ATT ** -0.5), kg_ref[...])
        lin_scale = jnp.where((j >= j_mk) & (j < j_mv), DQK_M ** -0.5, 1.0)
        parts = []
        for gi in range(bn // DQK_ATT):
            blk = acc[:, gi * DQK_ATT:(gi + 1) * DQK_ATT]
            other = jnp.where(is_gate, jax.nn.sigmoid(blk), blk * lin_scale)
            parts.append(jnp.where(is_norm, _rms(blk, gain), other))
        z_ref[...] = jnp.concatenate(parts, axis=1)

        def stage_heads(y, jj, j0, dst_prompt, dst_sample):
            if jj == j0 and j0 != j_k:
                for c in copies(kp_ref, ks_ref):
                    c.wait()
            h0 = (jj - j0) * heads_per_tile
            st[:, h0:h0 + heads_per_tile, :] = y.reshape(nrows, heads_per_tile, DV_ATT)
            if jj == j0 + ATT_QK // bn - 1:
                for c in copies(dst_prompt, dst_sample):
                    c.start()

        for jj in range(j_k, j_v):
            @pl.when(j == jj)
            def _(jj=jj):
                stage_heads(z_ref[...], jj, j_k, kp_ref, ks_ref)

        for jj in range(j_v, j_mq):
            @pl.when(j == jj)
            def _(jj=jj):
                stage_heads(z_ref[...], jj, j_v, vp_ref, vs_ref)

        @pl.when(j == j_mq)
        def _():
            for c in copies(vp_ref, vs_ref):
                c.wait()

    @pl.when(i < n_frame_tiles)
    def _():
        tile(xf_ref, zf_ref, gf_ref, False)

    @pl.when(i == n_frame_tiles)
    def _():
        tile(xt_ref, zt_ref, gt_ref, True)


def _inproj(x_f, x_t, g1, w_all, layer, w_og, w_gate, gate_bias, qg, kg, n_batch, n_sample_streams):
    NF, D = x_f.shape
    NT = x_t.shape[0]
    bm, bn = PROJ_ROWS, PROJ_COLS
    nft = NF // bm
    nj = N_MAIN // bn
    nja = (N_MAIN - M_V) // bn
    S = NF // n_batch
    L = N_META + S
    n_sample_rows = n_sample_streams * N_META
    assert NF % bm == 0 and S % bm == 0 and N_MAIN % bn == 0 and NT <= bm
    fr = lambda i: jnp.minimum(i, nft - 1)
    any_spec = pl.BlockSpec(memory_space=pl.ANY)
    return pl.pallas_call(
        functools.partial(_inproj_kernel, n_frame_tiles=nft, n_batch=n_batch, frames_per_batch=S,
                          n_sample_rows=n_sample_rows),
        grid=(nft + 1, nj),
        in_specs=[
            pl.BlockSpec((bm, D), lambda i, j: (fr(i), 0)),
            pl.BlockSpec((NT, D), lambda i, j: (0, 0)),
            pl.BlockSpec((1, D), lambda i, j: (0, 0)),
            pl.BlockSpec((None, D, bn), lambda i, j: (layer, 0, jnp.minimum(j, nja - 1))),
            pl.BlockSpec((D, bn), lambda i, j: (0, jnp.maximum(j - nja, 0))),
            pl.BlockSpec((D, LANES), lambda i, j: (0, 0)),
            pl.BlockSpec((1, LANES), lambda i, j: (0, 0)),
            pl.BlockSpec((1, DQK_ATT), lambda i, j: (0, 0)),
            pl.BlockSpec((1, DQK_ATT), lambda i, j: (0, 0)),
        ],
        out_specs=(
            pl.BlockSpec((bm, bn), lambda i, j: (fr(i), jnp.where(i < nft, j, nj - 1))),
            pl.BlockSpec((NT, bn), lambda i, j: (0, jnp.where(i < nft, 0, j))),
            pl.BlockSpec((bm, LANES), lambda i, j: (fr(i), 0)),
            pl.BlockSpec((NT, LANES), lambda i, j: (0, 0)),
            any_spec, any_spec, any_spec, any_spec,
        ),
        out_shape=(
            jax.ShapeDtypeStruct((NF, N_MAIN), F32),
            jax.ShapeDtypeStruct((NT, N_MAIN), F32),
            jax.ShapeDtypeStruct((NF, LANES), F32),
            jax.ShapeDtypeStruct((NT, LANES), F32),
            jax.ShapeDtypeStruct((n_batch, L, H_ATT, 2 * DQK_ATT), F32),
            jax.ShapeDtypeStruct((n_batch, L, H_ATT, DV_ATT), F32),
            jax.ShapeDtypeStruct((n_sample_rows, H_ATT, 2 * DQK_ATT), F32),
            jax.ShapeDtypeStruct((n_sample_rows, H_ATT, DV_ATT), F32),
        ),
        scratch_shapes=[
            pltpu.VMEM((bm, D), BF16),
            pltpu.VMEM((bm, H_ATT, DV_ATT), F32),
            pltpu.SemaphoreType.DMA((n_batch + 1,)),
        ],
        compiler_params=pltpu.CompilerParams(
            dimension_semantics=("arbitrary", "arbitrary"), vmem_limit_bytes=VMEM_LIMIT),
        name="in_proj",
    )(x_f, x_t, g1, w_all, w_og, w_gate, gate_bias, qg, kg)


def _diff_lambda(lq1_ref, lk1_ref, lq2_ref, lk2_ref, lam_init):
    a = jnp.sum(lq1_ref[...] * lk1_ref[...], axis=-1, keepdims=True)
    b = jnp.sum(lq2_ref[...] * lk2_ref[...], axis=-1, keepdims=True)
    return jnp.exp(a) - jnp.exp(b) + lam_init


def _subln(o, sg_ref, lam_init):
    return _rms(o, sg_ref[...]) * (1.0 - lam_init)


def _pad_rows(dst, src):
    dst[...] = jnp.zeros(dst.shape, dst.dtype)
    dst[0:src.shape[0], :] = src[...].astype(dst.dtype)


def _attn_prompt_kernel(q_ref, k_ref, v_ref, km_ref, vm_ref, bd_ref, bp_ref, bm_ref, bf_ref,
                        lq1_ref, lk1_ref, lq2_ref, lk2_ref, sg_ref, o_ref,
                        kb_s, vb_s, kmb_s, vmb_s, *, lam_init):
    T = ATT_TILE
    n_tiles = q_ref.shape[0] // T
    kb_s[...] = k_ref[...].astype(BF16)
    vb_s[...] = v_ref[...].astype(BF16)
    _pad_rows(kmb_s, km_ref)
    _pad_rows(vmb_s, vm_ref)
    lam = _diff_lambda(lq1_ref, lk1_ref, lq2_ref, lk2_ref, lam_init)
    far = bf_ref[0:1, 0:1]

    for j in range(n_tiles):
        rows = slice(j * T, (j + 1) * T)
        regions = [(kmb_s, vmb_s, slice(0, LANES), bm_ref[min(j, 1)])]
        if j >= 2:
            regions.append((kb_s, vb_s, slice(0, (j - 1) * T), far))
        if j >= 1:
            regions.append((kb_s, vb_s, slice((j - 1) * T, j * T), bp_ref[...]))
        regions.append((kb_s, vb_s, rows, bd_ref[...]))

        probs, denom = [], []
        for c in range(2):
            cols = slice(c * DQK_ATT, (c + 1) * DQK_ATT)
            qc = q_ref[rows, cols].astype(BF16)
            ss = [_dot_nt(qc, kr[rs, cols]) + br for kr, _, rs, br in regions]
            m = functools.reduce(jnp.maximum, [jnp.max(s, axis=-1, keepdims=True) for s in ss])
            pc = [jnp.exp(s - m) for s in ss]
            denom.append(functools.reduce(jnp.add, [jnp.sum(p, axis=-1, keepdims=True) for p in pc]))
            probs.append(pc)
        a0 = 1.0 / denom[0]
        a1 = lam / denom[1]
        o = None
        for r, (_, vr, rs, _) in enumerate(regions):
            w = (probs[0][r] * a0 - probs[1][r] * a1).astype(BF16)
            t = _dot(w, vr[rs, :])
            o = t if o is None else o + t
        o_ref[rows, :] = _subln(o, sg_ref, lam_init)


def _attn_prompt(z_f, z_t, bias, lam_vecs, sg, B, S, lam_init):
    T = ATT_TILE
    qkw = 2 * DQK_ATT
    kcol = ATT_QK // qkw
    vcol = 2 * ATT_QK // DV_ATT
    b_diag, b_prev, b_meta, b_far, _, _ = bias
    vec = pl.BlockSpec((1, DQK_ATT), lambda b, h: (0, 0))
    return pl.pallas_call(
        functools.partial(_attn_prompt_kernel, lam_init=lam_init),
        grid=(B, H_ATT),
        in_specs=[
            pl.BlockSpec((S, qkw), lambda b, h: (b, h)),
            pl.BlockSpec((S, qkw), lambda b, h: (b, kcol + h)),
            pl.BlockSpec((S, DV_ATT), lambda b, h: (b, vcol + h)),
            pl.BlockSpec((N_META, qkw), lambda b, h: (b, kcol + h)),
            pl.BlockSpec((N_META, DV_ATT), lambda b, h: (b, vcol + h)),
            pl.BlockSpec((None, T, T), lambda b, h: (h, 0, 0)),
            pl.BlockSpec((None, T, T), lambda b, h: (h, 0, 0)),
            pl.BlockSpec((None, 2, T, LANES), lambda b, h: (h, 0, 0, 0)),
            pl.BlockSpec((None, 8, T), lambda b, h: (h, 0, 0)),
            vec, vec, vec, vec,
            pl.BlockSpec((1, DV_ATT), lambda b, h: (0, 0)),
        ],
        out_specs=pl.BlockSpec((S, DV_ATT), lambda b, h: (b, h)),
        out_shape=jax.ShapeDtypeStruct((B * S, ATT_V), F32),
        scratch_shapes=[
            pltpu.VMEM((S, qkw), BF16),
            pltpu.VMEM((S, DV_ATT), BF16),
            pltpu.VMEM((LANES, qkw), BF16),
            pltpu.VMEM((LANES, DV_ATT), BF16),
        ],
        compiler_params=pltpu.CompilerParams(
            dimension_semantics=("arbitrary", "arbitrary"), vmem_limit_bytes=VMEM_LIMIT),
        name="attn_prompt",
    )(z_f, z_f, z_f, z_t, z_t, b_diag, b_prev, b_meta, b_far, *lam_vecs, sg)


def _attn_small_kernel(*refs, lam_init, has_cache):
    if has_cache:
        (q_ref, kc_ref, vc_ref, kn_ref, vn_ref, b_ref,
         lq1_ref, lk1_ref, lq2_ref, lk2_ref, sg_ref, o_ref, kpad, vpad, kb_s, vb_s) = refs
        past_len = kb_s.shape[0]
        chunk = kc_ref.shape[0]
        c = pl.program_id(1)
        r0 = pl.multiple_of(c * chunk, chunk)
        kb_s[pl.ds(r0, chunk), :] = kc_ref[...].reshape(chunk, ATT_QK).astype(BF16)
        vb_s[pl.ds(r0, chunk), :] = vc_ref[...].reshape(chunk, ATT_V).astype(BF16)

        @pl.when(c == past_len // chunk - 1)
        def _():
            _attn_small_body(q_ref, kn_ref, vn_ref, b_ref, lq1_ref, lk1_ref, lq2_ref, lk2_ref, sg_ref, o_ref,
                             kpad, vpad, kb_s, vb_s, lam_init=lam_init, past_len=past_len)
    else:
        (q_ref, kn_ref, vn_ref, b_ref,
         lq1_ref, lk1_ref, lq2_ref, lk2_ref, sg_ref, o_ref, kpad, vpad) = refs
        _attn_small_body(q_ref, kn_ref, vn_ref, b_ref, lq1_ref, lk1_ref, lq2_ref, lk2_ref, sg_ref, o_ref,
                         kpad, vpad, None, None, lam_init=lam_init, past_len=0)


def _attn_small_body(q_ref, kn_ref, vn_ref, b_ref, lq1_ref, lk1_ref, lq2_ref, lk2_ref, sg_ref, o_ref,
                     kpad, vpad, kb_s, vb_s, *, lam_init, past_len):
    has_cache = past_len > 0
    lam = _diff_lambda(lq1_ref, lk1_ref, lq2_ref, lk2_ref, lam_init)
    _pad_rows(kpad, kn_ref)
    _pad_rows(vpad, vn_ref)
    for h in range(H_ATT):
        vsl = slice(h * DV_ATT, (h + 1) * DV_ATT)
        outs = []
        for c in range(2):
            sl = slice(h * 2 * DQK_ATT + c * DQK_ATT, h * 2 * DQK_ATT + (c + 1) * DQK_ATT)
            qc = q_ref[:, sl].astype(BF16)
            s_new = _dot_nt(qc, kpad[:, sl]) + b_ref[h, :, past_len:]
            m = jnp.max(s_new, axis=-1, keepdims=True)
            if has_cache:
                s_old = _dot_nt(qc, kb_s[:, sl]) + b_ref[h, :, :past_len]
                m = jnp.maximum(m, jnp.max(s_old, axis=-1, keepdims=True))
            p_new = jnp.exp(s_new - m)
            l = jnp.sum(p_new, axis=-1, keepdims=True)
            o = _dot(p_new.astype(BF16), vpad[:, vsl])
            if has_cache:
                p_old = jnp.exp(s_old - m)
                l = l + jnp.sum(p_old, axis=-1, keepdims=True)
                o = o + _dot(p_old.astype(BF16), vb_s[:, vsl])
            outs.append(o / l)
        o_ref[:, vsl] = _subln(outs[0] - lam * outs[1], sg_ref, lam_init)


def _attn_small(z_t, row0, n_streams, bias, lam_vecs, sg, lam_init, cache=None):
    vec = pl.BlockSpec((1, DQK_ATT), lambda b, c: (0, 0))
    in_specs = [pl.BlockSpec((N_META, ATT_QK), lambda b, c: (row0 + b, 0))]
    args = [z_t]
    scratch = [pltpu.VMEM((LANES, ATT_QK), BF16), pltpu.VMEM((LANES, ATT_V), BF16)]
    n_chunks = 1
    if cache is not None:
        ck, cv, layer = cache
        past_len = ck.shape[2]
        chunk = min(past_len, CACHE_CHUNK)
        assert past_len % chunk == 0
        n_chunks = past_len // chunk
        in_specs += [
            pl.BlockSpec((None, None, chunk, H_ATT, 2 * DQK_ATT), lambda b, c: (layer, b, c, 0, 0)),
            pl.BlockSpec((None, None, chunk, H_ATT, DV_ATT), lambda b, c: (layer, b, c, 0, 0))]
        args += [ck, cv]
        scratch += [pltpu.VMEM((past_len, ATT_QK), BF16), pltpu.VMEM((past_len, ATT_V), BF16)]
    in_specs += [
        pl.BlockSpec((N_META, ATT_QK), lambda b, c: (row0 + b, 1)),
        pl.BlockSpec((N_META, ATT_V), lambda b, c: (row0 + b, 2 * ATT_QK // ATT_V)),
        pl.BlockSpec(bias.shape, lambda b, c: (0, 0, 0)),
        vec, vec, vec, vec,
        pl.BlockSpec((1, DV_ATT), lambda b, c: (0, 0)),
    ]
    args += [z_t, z_t, bias, *lam_vecs, sg]
    return pl.pallas_call(
        functools.partial(_attn_small_kernel, lam_init=lam_init, has_cache=cache is not None),
        grid=(n_streams, n_chunks),
        in_specs=in_specs,
        out_specs=pl.BlockSpec((N_META, ATT_V), lambda b, c: (b, 0)),
        out_shape=jax.ShapeDtypeStruct((n_streams * N_META, ATT_V), F32),
        scratch_shapes=scratch,
        compiler_params=pltpu.CompilerParams(
            dimension_semantics=("arbitrary", "arbitrary"), vmem_limit_bytes=VMEM_LIMIT),
        name="attn_cached" if cache is not None else "attn_meta",
    )(*args)


def _mlstm_kernel(*refs, n_blocks, has_init):
    refs = list(refs)
    qh_ref, kh_ref, vh_ref, gh_ref = refs[:4]
    pos = 4
    if n_blocks:
        qf_ref, kf_ref, vf_ref, gf_ref = refs[pos:pos + 4]
        pos += 4
    if has_init:
        c0_ref, n0_ref, m0_ref = refs[pos:pos + 3]
        pos += 3
    hh_ref = refs[pos]
    pos += 1
    if n_blocks:
        hf_ref = refs[pos]
        pos += 1
    c_ref, n_ref, m_ref = refs[pos:pos + 3]
    qs, ks, vs, gs = refs[pos + 3:pos + 7]

    t = pl.program_id(1)
    T = M_TILE

    @pl.when(t == 0)
    def _():
        if has_init:
            c_ref[...] = c0_ref[...]
            n_ref[...] = n0_ref[...]
            m_ref[...] = m0_ref[...]
        else:
            c_ref[...] = jnp.zeros(c_ref.shape, F32)
            n_ref[...] = jnp.zeros(n_ref.shape, F32)
            m_ref[...] = jnp.zeros(m_ref.shape, F32)
        nh = qh_ref.shape[0]
        for dst, src in ((qs, qh_ref), (ks, kh_ref), (vs, vh_ref)):
            dst[...] = jnp.zeros(dst.shape, F32)
            dst[0:nh, :] = src[...]
        lane = lax.broadcasted_iota(jnp.int32, gs.shape, 1)
        gs[...] = jnp.where(lane < H_M, NEG, 0.0)
        gs[0:nh, :] = gh_ref[...]

    if n_blocks:
        @pl.when(t > 0)
        def _():
            qs[...] = qf_ref[...]
            ks[...] = kf_ref[...]
            vs[...] = vf_ref[...]
            gs[...] = gf_ref[...]

    g = gs[...]
    row = lax.broadcasted_iota(jnp.int32, (T, T), 0)
    colid = lax.broadcasted_iota(jnp.int32, (T, T), 1)
    causal = row >= colid
    tril = jnp.where(causal, 1.0, 0.0).astype(F32)
    cum = jnp.dot(tril, g, preferred_element_type=F32, precision=lax.Precision.HIGHEST)
    g_t = g.T
    cum_t = cum.T
    last = cum[T - 1:T, :]

    for h in range(H_M):
        q = qs[:, h * DQK_M:(h + 1) * DQK_M]
        k = ks[:, h * DQK_M:(h + 1) * DQK_M]
        v = vs[:, h * DV_M:(h + 1) * DV_M]
        qb = q.astype(BF16)
        kb = k.astype(BF16)
        m_old = m_ref[h, 0:1, 0:1]
        c_old = c_ref[h]
        n_old = n_ref[h]
        bh_c = _col(cum, H_M + h)
        ih_c = _col(g, h)
        bh_r = cum_t[H_M + h:H_M + h + 1, :]
        ih_r = g_t[h:h + 1, :]
        d = jnp.where(causal, bh_c - bh_r + ih_r, NEG)
        inter = bh_c + m_old
        mt = jnp.maximum(inter, jnp.max(d, axis=-1, keepdims=True))
        a = jnp.exp(d - mt) * _dot_nt(qb, kb)
        iw = jnp.exp(inter - mt)
        num = _dot(a.astype(BF16), v.astype(BF16)) + iw * _dot_nt(qb, c_old.astype(BF16))
        den = jnp.sum(a, axis=-1, keepdims=True) + iw * jnp.sum(q * n_old, axis=-1, keepdims=True)
        den = jnp.maximum(jnp.abs(den), jnp.exp(-mt))
        hout = num / den
        sl = slice(h * DV_M, (h + 1) * DV_M)

        @pl.when(t == 0)
        def _():
            hh_ref[:, sl] = hout[0:hh_ref.shape[0], :]

        if n_blocks:
            @pl.when(t > 0)
            def _():
                hf_ref[:, sl] = hout

        bl = _col(last, H_M + h)
        g_c = bl - bh_c + ih_c
        m_new = jnp.maximum(bl + m_old, jnp.max(g_c, axis=0, keepdims=True))
        sw = jnp.exp(g_c - m_new)
        dec = jnp.exp(bl + m_old - m_new)
        c_ref[h] = dec * c_old + _dot(v.T.astype(BF16), (sw * k).astype(BF16))
        n_ref[h] = dec * n_old + jnp.sum(sw * k, axis=0, keepdims=True)
        m_ref[h] = jnp.broadcast_to(m_new, m_ref.shape[1:])


def _mlstm(z_t, g_t, row0, n_streams, frames=None, init=None):
    T = M_TILE
    qcol = (2 * ATT_QK + ATT_V) // M_QK
    vcol = (2 * ATT_QK + ATT_V + 2 * M_QK) // M_V
    nh = N_META
    in_specs = [
        pl.BlockSpec((nh, M_QK), lambda b, t: (row0 + b, qcol)),
        pl.BlockSpec((nh, M_QK), lambda b, t: (row0 + b, qcol + 1)),
        pl.BlockSpec((nh, M_V), lambda b, t: (row0 + b, vcol)),
        pl.BlockSpec((nh, LANES), lambda b, t: (row0 + b, 0)),
    ]
    args = [z_t, z_t, z_t, g_t]
    n_blocks = 0
    if frames is not None:
        z_f, g_f, fps = frames
        n_blocks = fps // T
        fidx = lambda b, t: b * n_blocks + jnp.maximum(t - 1, 0)
        in_specs += [
            pl.BlockSpec((T, M_QK), lambda b, t: (fidx(b, t), qcol)),
            pl.BlockSpec((T, M_QK), lambda b, t: (fidx(b, t), qcol + 1)),
            pl.BlockSpec((T, M_V), lambda b, t: (fidx(b, t), vcol)),
            pl.BlockSpec((T, LANES), lambda b, t: (fidx(b, t), 0)),
        ]
        args += [z_f, z_f, z_f, g_f]
    state_specs = (
        pl.BlockSpec((None, H_M, DV_M, DQK_M), lambda b, t: (b, 0, 0, 0)),
        pl.BlockSpec((None, H_M, 1, DQK_M), lambda b, t: (b, 0, 0, 0)),
        pl.BlockSpec((None, H_M, 8, LANES), lambda b, t: (b, 0, 0, 0)),
    )
    if init is not None:
        in_specs += list(state_specs)
        args += list(init)
    out_specs = [pl.BlockSpec((nh, M_V), lambda b, t: (b, 0))]
    out_shape = [jax.ShapeDtypeStruct((n_streams * nh, M_V), F32)]
    if n_blocks:
        out_specs.append(pl.BlockSpec((T, M_V), lambda b, t: (fidx(b, t), 0)))
        out_shape.append(jax.ShapeDtypeStruct((n_streams * fps, M_V), F32))
    out_specs += list(state_specs)
    out_shape += [
        jax.ShapeDtypeStruct((n_streams, H_M, DV_M, DQK_M), F32),
        jax.ShapeDtypeStruct((n_streams, H_M, 1, DQK_M), F32),
        jax.ShapeDtypeStruct((n_streams, H_M, 8, LANES), F32),
    ]
    return pl.pallas_call(
        functools.partial(_mlstm_kernel, n_blocks=n_blocks, has_init=init is not None),
        grid=(n_streams, 1 + n_blocks),
        in_specs=in_specs,
        out_specs=tuple(out_specs),
        out_shape=tuple(out_shape),
        scratch_shapes=[
            pltpu.VMEM((T, M_QK), F32), pltpu.VMEM((T, M_QK), F32),
            pltpu.VMEM((T, M_V), F32), pltpu.VMEM((T, LANES), F32),
        ],
        compiler_params=pltpu.CompilerParams(
            dimension_semantics=("arbitrary", "arbitrary"), vmem_limit_bytes=VMEM_LIMIT),
        name="mlstm_prompt" if n_blocks else "mlstm_cached",
    )(*args)


def _merge_kernel(xf_ref, xt_ref, attf_ref, attt_ref, hmf_ref, hmt_ref, zf_ref, zt_ref, mg_ref, wo_ref,
                  of_ref, ot_ref, *, n_frame_tiles):
    is_tail = pl.program_id(0) >= n_frame_tiles
    x = jnp.where(is_tail, xt_ref[...], xf_ref[...])
    att = jnp.where(is_tail, attt_ref[...], attf_ref[...])
    hm = jnp.where(is_tail, hmt_ref[...], hmf_ref[...])
    og = jnp.where(is_tail, zt_ref[...], zf_ref[...])
    parts = []
    for h in range(H_M):
        sl = slice(h * DV_M, (h + 1) * DV_M)
        parts.append(_rms(hm[:, sl], mg_ref[:, sl]) * og[:, sl])
    hmn = jnp.concatenate(parts, axis=1).astype(BF16)
    y = x + _dot(att.astype(BF16), wo_ref[0:ATT_V, :]) + _dot(hmn, wo_ref[ATT_V:, :])

    @pl.when(jnp.logical_not(is_tail))
    def _():
        of_ref[...] = y

    @pl.when(is_tail)
    def _():
        ot_ref[...] = y


def _merge(x_f, x_t, att_f, att_t, hm_f, hm_t, z_f, z_t, mg, w_out, layer):
    NF, D = x_f.shape
    bm = x_t.shape[0]
    nft = NF // bm
    assert NF % bm == 0 and bm == MERGE_ROWS
    ogcol = (N_MAIN - M_V) // M_V
    fr = lambda i: (jnp.minimum(i, nft - 1), 0)
    tl = lambda i: (0, 0)
    return pl.pallas_call(
        functools.partial(_merge_kernel, n_frame_tiles=nft),
        grid=(nft + 1,),
        in_specs=[
            pl.BlockSpec((bm, D), fr), pl.BlockSpec((bm, D), tl),
            pl.BlockSpec((bm, ATT_V), fr), pl.BlockSpec((bm, ATT_V), tl),
            pl.BlockSpec((bm, M_V), fr), pl.BlockSpec((bm, M_V), tl),
            pl.BlockSpec((bm, M_V), lambda i: (jnp.minimum(i, nft - 1), ogcol)),
            pl.BlockSpec((bm, M_V), lambda i: (0, ogcol)),
            pl.BlockSpec((1, M_V), tl),
            pl.BlockSpec((None, D, D), lambda i: (layer, 0, 0)),
        ],
        out_specs=(pl.BlockSpec((bm, D), fr), pl.BlockSpec((bm, D), tl)),
        out_shape=(jax.ShapeDtypeStruct((NF, D), F32), jax.ShapeDtypeStruct((bm, D), F32)),
        compiler_params=pltpu.CompilerParams(
            dimension_semantics=("arbitrary",), vmem_limit_bytes=VMEM_LIMIT),
        name="merge_out_proj",
    )(x_f, x_t, att_f, att_t, hm_f, hm_t, z_f, z_t, mg, w_out)


def _ffn_kernel(xf_ref, xt_ref, g2_ref, wu_ref, wd_ref, of_ref, ot_ref, xn_s, *, n_frame_tiles):
    i = pl.program_id(0)
    k = pl.program_id(1)

    def tile(x_ref, o_ref):
        nrows = x_ref.shape[0]

        @pl.when(k == 0)
        def _():
            x = x_ref[...]
            xn_s[0:nrows, :] = _rms(x, g2_ref[...]).astype(BF16)
            o_ref[...] = x

        u = jnp.maximum(_dot(xn_s[0:nrows, :], wu_ref[...]), 0.0)
        o_ref[...] += _dot((u * u).astype(BF16), wd_ref[...])

    @pl.when(i < n_frame_tiles)
    def _():
        tile(xf_ref, of_ref)

    @pl.when(i == n_frame_tiles)
    def _():
        tile(xt_ref, ot_ref)


def _ffn(x_f, x_t, g2, w_up, w_down, layer):
    NF, D = x_f.shape
    NT = x_t.shape[0]
    FF = w_up.shape[2]
    bm, fc = FF_ROWS, FF_COLS
    nft = NF // bm
    assert NF % bm == 0 and FF % fc == 0 and NT <= bm
    fr = lambda i, k: (jnp.minimum(i, nft - 1), 0)
    tl = lambda i, k: (0, 0)
    return pl.pallas_call(
        functools.partial(_ffn_kernel, n_frame_tiles=nft),
        grid=(nft + 1, FF // fc),
        in_specs=[
            pl.BlockSpec((bm, D), fr, pipeline_mode=pl.Buffered(1)),
            pl.BlockSpec((NT, D), tl, pipeline_mode=pl.Buffered(1)),
            pl.BlockSpec((1, D), tl),
            pl.BlockSpec((None, D, fc), lambda i, k: (layer, 0, k)),
            pl.BlockSpec((None, fc, D), lambda i, k: (layer, k, 0)),
        ],
        out_specs=(pl.BlockSpec((bm, D), fr), pl.BlockSpec((NT, D), tl)),
        out_shape=(jax.ShapeDtypeStruct((NF, D), F32), jax.ShapeDtypeStruct((NT, D), F32)),
        scratch_shapes=[pltpu.VMEM((bm, D), BF16)],
        compiler_params=pltpu.CompilerParams(
            dimension_semantics=("arbitrary", "arbitrary"), vmem_limit_bytes=VMEM_LIMIT),
        name="ffn",
    )(x_f, x_t, g2, w_up, w_down)


def kernel(x_prompt, x_sample, cache_k, cache_v, state_C, state_n, state_m, meta_tokens, rel_bias, norm1_g, w_in, b_i, b_f, q_norm_g, k_norm_g, lambda_q1, lambda_k1, lambda_q2, lambda_k2, subln_g, mnorm_g, w_out, norm2_g, w_up, w_down):
    B, S, D = x_prompt.shape
    Bd, Td, _ = x_sample.shape
    depth = w_in.shape[0]
    Pc = cache_k.shape[2]
    assert Td == N_META and S % ATT_TILE == 0 and S % M_TILE == 0 and Pc % LANES == 0
    assert ATT_TILE % CHUNK == 0 and CHUNK & (CHUNK - 1) == 0 and ATT_TILE >= MAX_DISTANCE

    n_frames = B * S
    n_meta = B * N_META
    n_samp = Bd * Td
    assert n_meta + n_samp <= TAIL_ROWS
    srow = n_meta // N_META

    meta = jnp.broadcast_to(meta_tokens.astype(F32)[None], (B, N_META, D)).reshape(n_meta, D)
    x_f = x_prompt.astype(F32).reshape(n_frames, D)
    x_t = jnp.concatenate([meta, x_sample.astype(F32).reshape(n_samp, D),
                           jnp.zeros((TAIL_ROWS - n_meta - n_samp, D), F32)], axis=0)

    bias = _bias_tiles(rel_bias.astype(F32), Pc, Td)
    ck = cache_k.astype(F32)
    cv = cache_v.astype(F32)
    zeros_pad_att = jnp.zeros((TAIL_ROWS - n_meta - n_samp, ATT_V), F32)

    g_off = 2 * ATT_QK + ATT_V + 2 * M_QK + M_V
    w_in_b, w_out_b, w_up_b, w_down_b = (w.astype(BF16) for w in (w_in, w_out, w_up, w_down))
    outs = {name: [] for name in ("kp", "vp", "Cp", "np", "mp", "ks", "vs", "Cs", "ns", "ms")}

    for l in range(depth):
        lam_init = 0.8 - 0.6 * math.exp(-0.3 * l)
        w_og = w_in[l, :, g_off + 2 * H_M:].astype(BF16)
        w_gate = jnp.pad(w_in[l, :, g_off:g_off + 2 * H_M], ((0, 0), (0, LANES - 2 * H_M))).astype(BF16)
        gate_bias = jnp.pad(jnp.concatenate([b_i[l], b_f[l]]).astype(F32), (0, LANES - 2 * H_M))[None]
        lam_vecs = [v[l].astype(F32)[None] for v in (lambda_q1, lambda_k1, lambda_q2, lambda_k2)]
        sg = subln_g[l].astype(F32)[None]

        z_f, z_t, g_f, g_t, k_p, v_p, k_s, v_s = _inproj(
            x_f, x_t, norm1_g[l].astype(F32)[None], w_in_b, l, w_og, w_gate, gate_bias,
            q_norm_g[l].astype(F32)[None], k_norm_g[l].astype(F32)[None], B, Bd)

        att_f = _attn_prompt(z_f, z_t, bias, lam_vecs, sg, B, S, lam_init)
        att_m = _attn_small(z_t, 0, B, bias[4], lam_vecs, sg, lam_init)
        att_s = _attn_small(z_t, srow, Bd, bias[5], lam_vecs, sg, lam_init, cache=(ck, cv, l))
        att_t = jnp.concatenate([att_m, att_s, zeros_pad_att], axis=0)

        hm_m, hm_f, C_p, n_p, m_p = _mlstm(z_t, g_t, 0, B, frames=(z_f, g_f, S))
        init = (state_C[l].astype(F32), state_n[l].astype(F32)[:, :, None, :],
                jnp.broadcast_to(state_m[l].astype(F32)[:, :, None, None], (Bd, H_M, 8, LANES)))
        hm_s, C_s, n_s, m_s = _mlstm(z_t, g_t, srow, Bd, init=init)
        hm_t = jnp.concatenate([hm_m, hm_s, zeros_pad_att], axis=0)

        x_f, x_t = _merge(x_f, x_t, att_f, att_t, hm_f, hm_t, z_f, z_t,
                          mnorm_g[l].astype(F32)[None], w_out_b, l)
        x_f, x_t = _ffn(x_f, x_t, norm2_g[l].astype(F32)[None], w_up_b, w_down_b, l)

        outs["kp"].append(k_p)
        outs["vp"].append(v_p)
        outs["Cp"].append(C_p)
        outs["np"].append(n_p[:, :, 0, :])
        outs["mp"].append(m_p[:, :, 0, 0])
        outs["ks"].append(k_s.reshape(Bd, Td, H_ATT, 2 * DQK_ATT))
        outs["vs"].append(v_s.reshape(Bd, Td, H_ATT, DV_ATT))
        outs["Cs"].append(C_s)
        outs["ns"].append(n_s[:, :, 0, :])
        outs["ms"].append(m_s[:, :, 0, 0])

    y_prompt = x_f.reshape(B, S, D)
    y_sample = x_t[n_meta:n_meta + n_samp].reshape(Bd, Td, D)
    return (y_prompt, y_sample,
            jnp.stack(outs["kp"]), jnp.stack(outs["vp"]), jnp.stack(outs["Cp"]), jnp.stack(outs["np"]),
            jnp.stack(outs["mp"]),
            jnp.stack(outs["ks"]), jnp.stack(outs["vs"]), jnp.stack(outs["Cs"]), jnp.stack(outs["ns"]),
            jnp.stack(outs["ms"]))
```

```python
import functools
import math

import jax
import jax.numpy as jnp
from jax import lax
from jax.experimental import pallas as pl
from jax.experimental.pallas import tpu as pltpu

F32 = jnp.float32
BF16 = jnp.bfloat16

CHUNK = 64
N_META = 16
H_ATT = 4
DQK_ATT = 128
DV_ATT = 256
H_M = 4
DQK_M = 128
DV_M = 256
N_BUCKETS = 32
MAX_DISTANCE = 128
EPS = 1e-6
NEG = -1e30

ATT_QK = H_ATT * 2 * DQK_ATT
ATT_V = H_ATT * DV_ATT
M_QK = H_M * DQK_M
M_V = H_M * DV_M
N_MAIN = 2 * ATT_QK + ATT_V + 2 * M_QK + M_V + M_V

LANES = 128
ATT_TILE = 256
M_TILE = 256
TAIL_ROWS = 256
PROJ_ROWS = 1024
PROJ_COLS = 512
MERGE_ROWS = 256
CACHE_CHUNK = 1024
FF_ROWS = 512
FF_COLS = 1024
VMEM_LIMIT = 56 * 1024 * 1024

_NT = (((1,), (1,)), ((), ()))


def _dot(a, b):
    return jnp.dot(a, b, preferred_element_type=F32)


def _dot_nt(a, b):
    return lax.dot_general(a, b, _NT, preferred_element_type=F32)


def _log_sigmoid(x):
    return jnp.minimum(x, 0.0) - jnp.log1p(jnp.exp(-jnp.abs(x)))


def _col(a, idx):
    lane = lax.broadcasted_iota(jnp.int32, a.shape, 1)
    return jnp.sum(jnp.where(lane == idx, a, 0.0), axis=1, keepdims=True)


def _rms(x, gain):
    ms = jnp.mean(x * x, axis=-1, keepdims=True)
    return x * lax.rsqrt(ms + EPS) * gain


def _bias_from_rel(rel, rb_ref, h):
    nb = N_BUCKETS // 2
    max_exact = nb // 2
    ret = jnp.where(rel > 0, nb, 0)
    n = jnp.abs(rel)
    nf = jnp.maximum(n, 1).astype(F32)
    large = max_exact + (jnp.log(nf / max_exact) / math.log(MAX_DISTANCE / max_exact)
                         * (nb - max_exact)).astype(jnp.int32)
    large = jnp.minimum(large, nb - 1)
    bucket = ret + jnp.where(n < max_exact, n, large)
    out = jnp.zeros(rel.shape, F32)
    for b in range(N_BUCKETS):
        out = jnp.where(bucket == b, rb_ref[b, h], out)
    return out


def _bias_kernel(rb_ref, diag_ref, prev_ref, meta_ref, far_ref, mm_ref, samp_ref, *, past_len, dec_seq):
    h = pl.program_id(0)
    shift = CHUNK.bit_length() - 1
    T = ATT_TILE

    qi = lax.broadcasted_iota(jnp.int32, (T, T), 0)
    kj = lax.broadcasted_iota(jnp.int32, (T, T), 1)
    bias = _bias_from_rel(kj - qi, rb_ref, h)
    diag_ref[...] = jnp.where((kj >> shift) <= (qi >> shift), bias, NEG)
    prev_ref[...] = _bias_from_rel(kj - T - qi, rb_ref, h)
    far_ref[...] = _bias_from_rel(jnp.full(far_ref.shape, -(2 * MAX_DISTANCE + 1), jnp.int32), rb_ref, h)

    qi = lax.broadcasted_iota(jnp.int32, (T, LANES), 0)
    kj = lax.broadcasted_iota(jnp.int32, (T, LANES), 1)
    for var in range(2):
        bias = _bias_from_rel(kj - (N_META + qi + var * T), rb_ref, h)
        meta_ref[var] = jnp.where(kj < N_META, bias, NEG)

    qi = lax.broadcasted_iota(jnp.int32, (N_META, LANES), 0)
    kj = lax.broadcasted_iota(jnp.int32, (N_META, LANES), 1)
    mm_ref[...] = jnp.where(kj < N_META, _bias_from_rel(kj - qi, rb_ref, h), NEG)

    width = samp_ref.shape[-1]
    qi = lax.broadcasted_iota(jnp.int32, (dec_seq, width), 0) + past_len
    kj = lax.broadcasted_iota(jnp.int32, (dec_seq, width), 1)
    bias = _bias_from_rel(kj - qi, rb_ref, h)
    ok = ((kj >> shift) <= (qi >> shift)) & (kj < past_len + dec_seq)
    samp_ref[...] = jnp.where(ok, bias, NEG)


def _bias_tiles(rel_bias, past_len, dec_seq):
    T = ATT_TILE
    sw = past_len + LANES
    out_shape = (
        jax.ShapeDtypeStruct((H_ATT, T, T), F32),
        jax.ShapeDtypeStruct((H_ATT, T, T), F32),
        jax.ShapeDtypeStruct((H_ATT, 2, T, LANES), F32),
        jax.ShapeDtypeStruct((H_ATT, 8, T), F32),
        jax.ShapeDtypeStruct((H_ATT, N_META, LANES), F32),
        jax.ShapeDtypeStruct((H_ATT, dec_seq, sw), F32),
    )
    out_specs = (
        pl.BlockSpec((None, T, T), lambda h: (h, 0, 0)),
        pl.BlockSpec((None, T, T), lambda h: (h, 0, 0)),
        pl.BlockSpec((None, 2, T, LANES), lambda h: (h, 0, 0, 0)),
        pl.BlockSpec((None, 8, T), lambda h: (h, 0, 0)),
        pl.BlockSpec((None, N_META, LANES), lambda h: (h, 0, 0)),
        pl.BlockSpec((None, dec_seq, sw), lambda h: (h, 0, 0)),
    )
    return pl.pallas_call(
        functools.partial(_bias_kernel, past_len=past_len, dec_seq=dec_seq),
        grid=(H_ATT,),
        in_specs=[pl.BlockSpec(memory_space=pltpu.SMEM)],
        out_specs=out_specs,
        out_shape=out_shape,
        name="rel_bias_tiles",
    )(rel_bias)


def _kv_copies(stage, sem, dst_prompt, dst_sample, i, *, layer, tail, n_batch, frames_per_batch, n_sample_rows):
    dst_prompt = dst_prompt.at[layer]
    dst_sample = dst_sample.at[layer]
    if not tail:
        rows = stage.shape[0]
        tiles_per_batch = frames_per_batch // rows
        b = i // tiles_per_batch
        t = i - b * tiles_per_batch
        return [pltpu.make_async_copy(stage, dst_prompt.at[b, pl.ds(N_META + t * rows, rows)], sem.at[0])]
    copies = [pltpu.make_async_copy(stage.at[pl.ds(b * N_META, N_META)], dst_prompt.at[b, pl.ds(0, N_META)],
                                    sem.at[b]) for b in range(n_batch)]
    copies.append(pltpu.make_async_copy(stage.at[pl.ds(n_batch * N_META, n_sample_rows)], dst_sample,
                                        sem.at[n_batch]))
    return copies


def _inproj_kernel(*refs, layer, n_frame_tiles, n_batch, frames_per_batch, n_sample_rows):
    xf_ref, xt_ref, g1_ref, wa_ref, wb_ref, wg_ref, gb_ref, qg_ref, kg_ref = refs[:9]
    prev_refs = refs[9:-12]
    (zf_ref, zt_ref, gf_ref, gt_ref, kp_ref, vp_ref, ks_ref, vs_ref, xn_s, stage, sem, prev_sem) = refs[-12:]
    i = pl.program_id(0)
    j = pl.program_id(1)
    prev_copies = [pltpu.make_async_copy(src, dst.at[pl.ds(0, layer)], prev_sem.at[c])
                   for c, (src, dst) in enumerate(zip(prev_refs, (kp_ref, vp_ref, ks_ref, vs_ref)))]

    @pl.when((i == 0) & (j == 0))
    def _():
        for c in prev_copies:
            c.start()

    @pl.when((i == pl.num_programs(0) - 1) & (j == pl.num_programs(1) - 1))
    def _():
        for c in prev_copies:
            c.wait()

    bn = wa_ref.shape[1]
    assert ATT_QK % bn == 0 and ATT_V % bn == 0 and M_QK % bn == 0 and M_V % bn == 0 and bn % DV_ATT == 0
    j_k = ATT_QK // bn
    j_v = j_k + ATT_QK // bn
    j_mq = j_v + ATT_V // bn
    j_mk = j_mq + M_QK // bn
    j_mv = j_mk + M_QK // bn
    j_og = j_mv + M_V // bn
    heads_per_tile = bn // DV_ATT

    def tile(x_ref, z_ref, g_ref, tail):
        nrows = x_ref.shape[0]
        st = stage.at[pl.ds(0, nrows)]
        copies = functools.partial(_kv_copies, st, sem, i=i, layer=layer, tail=tail, n_batch=n_batch,
                                   frames_per_batch=frames_per_batch, n_sample_rows=n_sample_rows)

        @pl.when(j == 0)
        def _():
            xb = _rms(x_ref[...], g1_ref[...]).astype(BF16)
            xn_s[0:nrows, :] = xb
            gates = _dot(xb, wg_ref[...]) + gb_ref[...]
            lane = lax.broadcasted_iota(jnp.int32, gates.shape, 1)
            g_ref[...] = jnp.where(lane < H_M, gates, _log_sigmoid(gates))

        acc = _dot(xn_s[0:nrows, :], jnp.where(j < j_og, wa_ref[...].astype(BF16), wb_ref[...]))

        def group_norm(gain_ref, scale):
            parts = []
            for gi in range(bn // DQK_ATT):
                y = _rms(acc[:, gi * DQK_ATT:(gi + 1) * DQK_ATT], gain_ref[...])
                parts.append(y if scale is None else y * scale)
            return jnp.concatenate(parts, axis=1)

        def stage_heads(y, jj, j0, dst_prompt, dst_sample):
            if jj == j0 and j0 != j_k:
                for c in copies(kp_ref, ks_ref):
                    c.wait()
            h0 = (jj - j0) * heads_per_tile
            st[:, h0:h0 + heads_per_tile, :] = y.reshape(nrows, heads_per_tile, DV_ATT)
            if jj == j0 + ATT_QK // bn - 1:
                for c in copies(dst_prompt, dst_sample):
                    c.start()

        @pl.when(j < j_k)
        def _():
            z_ref[...] = group_norm(qg_ref, DQK_ATT ** -0.5)

        for jj in range(j_k, j_v):
            @pl.when(j == jj)
            def _(jj=jj):
                y = group_norm(kg_ref, None)
                z_ref[...] = y
                stage_heads(y, jj, j_k, kp_ref, ks_ref)

        for jj in range(j_v, j_mq):
            @pl.when(j == jj)
            def _(jj=jj):
                z_ref[...] = acc
                stage_heads(acc, jj, j_v, vp_ref, vs_ref)

        @pl.when(j == j_mq)
        def _():
            for c in copies(vp_ref, vs_ref):
                c.wait()

        @pl.when(((j >= j_mq) & (j < j_mk)) | ((j >= j_mv) & (j < j_og)))
        def _():
            z_ref[...] = acc

        @pl.when((j >= j_mk) & (j < j_mv))
        def _():
            z_ref[...] = acc * (DQK_M ** -0.5)

        @pl.when(j >= j_og)
        def _():
            z_ref[...] = jax.nn.sigmoid(acc)

    @pl.when(i < n_frame_tiles)
    def _():
        tile(xf_ref, zf_ref, gf_ref, False)

    @pl.when(i == n_frame_tiles)
    def _():
        tile(xt_ref, zt_ref, gt_ref, True)


def _inproj(x_f, x_t, g1, w_all, layer, w_og, w_gate, gate_bias, qg, kg, n_batch, n_sample_streams, caches):
    depth = layer + 1
    NF, D = x_f.shape
    NT = x_t.shape[0]
    bm, bn = PROJ_ROWS, PROJ_COLS
    nft = NF // bm
    nj = N_MAIN // bn
    nja = (N_MAIN - M_V) // bn
    S = NF // n_batch
    L = N_META + S
    n_sample_rows = n_sample_streams * N_META
    assert NF % bm == 0 and S % bm == 0 and N_MAIN % bn == 0 and NT <= bm
    fr = lambda i: jnp.minimum(i, nft - 1)
    any_spec = pl.BlockSpec(memory_space=pl.ANY)
    n_in = 9
    extra_args = () if caches is None else tuple(caches)
    return pl.pallas_call(
        functools.partial(_inproj_kernel, layer=layer, n_frame_tiles=nft, n_batch=n_batch, frames_per_batch=S,
                          n_sample_rows=n_sample_rows),
        grid=(nft + 1, nj),
        in_specs=[
            pl.BlockSpec((bm, D), lambda i, j: (fr(i), 0)),
            pl.BlockSpec((NT, D), lambda i, j: (0, 0)),
            pl.BlockSpec((1, D), lambda i, j: (0, 0)),
            pl.BlockSpec((None, D, bn), lambda i, j: (layer, 0, jnp.minimum(j, nja - 1))),
            pl.BlockSpec((D, bn), lambda i, j: (0, jnp.maximum(j - nja, 0))),
            pl.BlockSpec((D, LANES), lambda i, j: (0, 0)),
            pl.BlockSpec((1, LANES), lambda i, j: (0, 0)),
            pl.BlockSpec((1, DQK_ATT), lambda i, j: (0, 0)),
            pl.BlockSpec((1, DQK_ATT), lambda i, j: (0, 0)),
        ] + [any_spec] * len(extra_args),
        out_specs=(
            pl.BlockSpec((bm, bn), lambda i, j: (fr(i), jnp.where(i < nft, j, nj - 1))),
            pl.BlockSpec((NT, bn), lambda i, j: (0, jnp.where(i < nft, 0, j))),
            pl.BlockSpec((bm, LANES), lambda i, j: (fr(i), 0)),
            pl.BlockSpec((NT, LANES), lambda i, j: (0, 0)),
            any_spec, any_spec, any_spec, any_spec,
        ),
        out_shape=(
            jax.ShapeDtypeStruct((NF, N_MAIN), F32),
            jax.ShapeDtypeStruct((NT, N_MAIN), F32),
            jax.ShapeDtypeStruct((NF, LANES), F32),
            jax.ShapeDtypeStruct((NT, LANES), F32),
            jax.ShapeDtypeStruct((depth, n_batch, L, H_ATT, 2 * DQK_ATT), F32),
            jax.ShapeDtypeStruct((depth, n_batch, L, H_ATT, DV_ATT), F32),
            jax.ShapeDtypeStruct((depth, n_sample_rows, H_ATT, 2 * DQK_ATT), F32),
            jax.ShapeDtypeStruct((depth, n_sample_rows, H_ATT, DV_ATT), F32),
        ),
        scratch_shapes=[
            pltpu.VMEM((bm, D), BF16),
            pltpu.VMEM((bm, H_ATT, DV_ATT), F32),
            pltpu.SemaphoreType.DMA((n_batch + 1,)),
            pltpu.SemaphoreType.DMA((4,)),
        ],
        compiler_params=pltpu.CompilerParams(
            dimension_semantics=("arbitrary", "arbitrary"), vmem_limit_bytes=VMEM_LIMIT),
        name="in_proj",
    )(x_f, x_t, g1, w_all, w_og, w_gate, gate_bias, qg, kg, *extra_args)


def _diff_lambda(lq1_ref, lk1_ref, lq2_ref, lk2_ref, lam_init):
    a = jnp.sum(lq1_ref[...] * lk1_ref[...], axis=-1, keepdims=True)
    b = jnp.sum(lq2_ref[...] * lk2_ref[...], axis=-1, keepdims=True)
    return jnp.exp(a) - jnp.exp(b) + lam_init


def _subln(o, sg_ref, lam_init):
    return _rms(o, sg_ref[...]) * (1.0 - lam_init)


def _pad_rows(dst, src):
    dst[...] = jnp.zeros(dst.shape, dst.dtype)
    dst[0:src.shape[0], :] = src[...].astype(dst.dtype)


def _attn_prompt_kernel(q_ref, k_ref, v_ref, km_ref, vm_ref, bd_ref, bp_ref, bm_ref, bf_ref,
                        lq1_ref, lk1_ref, lq2_ref, lk2_ref, sg_ref, o_ref,
                        kb_s, vb_s, kmb_s, vmb_s, *, lam_init):
    T = ATT_TILE
    n_tiles = q_ref.shape[0] // T
    kb_s[...] = k_ref[...].astype(BF16)
    vb_s[...] = v_ref[...].astype(BF16)
    _pad_rows(kmb_s, km_ref)
    _pad_rows(vmb_s, vm_ref)
    lam = _diff_lambda(lq1_ref, lk1_ref, lq2_ref, lk2_ref, lam_init)
    far = bf_ref[0:1, 0:1]

    for j in range(n_tiles):
        rows = slice(j * T, (j + 1) * T)
        regions = [(kmb_s, vmb_s, slice(0, LANES), bm_ref[min(j, 1)])]
        if j >= 2:
            regions.append((kb_s, vb_s, slice(0, (j - 1) * T), far))
        if j >= 1:
            regions.append((kb_s, vb_s, slice((j - 1) * T, j * T), bp_ref[...]))
        regions.append((kb_s, vb_s, rows, bd_ref[...]))

        probs, denom = [], []
        for c in range(2):
            cols = slice(c * DQK_ATT, (c + 1) * DQK_ATT)
            qc = q_ref[rows, cols].astype(BF16)
            ss = [_dot_nt(qc, kr[rs, cols]) + br for kr, _, rs, br in regions]
            m = functools.reduce(jnp.maximum, [jnp.max(s, axis=-1, keepdims=True) for s in ss])
            pc = [jnp.exp(s - m) for s in ss]
            denom.append(functools.reduce(jnp.add, [jnp.sum(p, axis=-1, keepdims=True) for p in pc]))
            probs.append(pc)
        a0 = 1.0 / denom[0]
        a1 = lam / denom[1]
        o = None
        for r, (_, vr, rs, _) in enumerate(regions):
            w = (probs[0][r] * a0 - probs[1][r] * a1).astype(BF16)
            t = _dot(w, vr[rs, :])
            o = t if o is None else o + t
        o_ref[rows, :] = _subln(o, sg_ref, lam_init)


def _attn_prompt(z_f, z_t, bias, lam_vecs, sg, B, S, lam_init):
    T = ATT_TILE
    qkw = 2 * DQK_ATT
    kcol = ATT_QK // qkw
    vcol = 2 * ATT_QK // DV_ATT
    b_diag, b_prev, b_meta, b_far, _, _ = bias
    vec = pl.BlockSpec((1, DQK_ATT), lambda b, h: (0, 0))
    return pl.pallas_call(
        functools.partial(_attn_prompt_kernel, lam_init=lam_init),
        grid=(B, H_ATT),
        in_specs=[
            pl.BlockSpec((S, qkw), lambda b, h: (b, h)),
            pl.BlockSpec((S, qkw), lambda b, h: (b, kcol + h)),
            pl.BlockSpec((S, DV_ATT), lambda b, h: (b, vcol + h)),
            pl.BlockSpec((N_META, qkw), lambda b, h: (b, kcol + h)),
            pl.BlockSpec((N_META, DV_ATT), lambda b, h: (b, vcol + h)),
            pl.BlockSpec((None, T, T), lambda b, h: (h, 0, 0)),
            pl.BlockSpec((None, T, T), lambda b, h: (h, 0, 0)),
            pl.BlockSpec((None, 2, T, LANES), lambda b, h: (h, 0, 0, 0)),
            pl.BlockSpec((None, 8, T), lambda b, h: (h, 0, 0)),
            vec, vec, vec, vec,
            pl.BlockSpec((1, DV_ATT), lambda b, h: (0, 0)),
        ],
        out_specs=pl.BlockSpec((S, DV_ATT), lambda b, h: (b, h)),
        out_shape=jax.ShapeDtypeStruct((B * S, ATT_V), F32),
        scratch_shapes=[
            pltpu.VMEM((S, qkw), BF16),
            pltpu.VMEM((S, DV_ATT), BF16),
            pltpu.VMEM((LANES, qkw), BF16),
            pltpu.VMEM((LANES, DV_ATT), BF16),
        ],
        compiler_params=pltpu.CompilerParams(
            dimension_semantics=("arbitrary", "arbitrary"), vmem_limit_bytes=VMEM_LIMIT),
        name="attn_prompt",
    )(z_f, z_f, z_f, z_t, z_t, b_diag, b_prev, b_meta, b_far, *lam_vecs, sg)


def _attn_small_kernel(*refs, lam_init, has_cache):
    if has_cache:
        (q_ref, kc_ref, vc_ref, kn_ref, vn_ref, b_ref,
         lq1_ref, lk1_ref, lq2_ref, lk2_ref, sg_ref, o_ref, kpad, vpad, kb_s, vb_s) = refs
        past_len = kb_s.shape[0]
        chunk = kc_ref.shape[0]
        c = pl.program_id(1)
        r0 = pl.multiple_of(c * chunk, chunk)
        kb_s[pl.ds(r0, chunk), :] = kc_ref[...].reshape(chunk, ATT_QK).astype(BF16)
        vb_s[pl.ds(r0, chunk), :] = vc_ref[...].reshape(chunk, ATT_V).astype(BF16)

        @pl.when(c == past_len // chunk - 1)
        def _():
            _attn_small_body(q_ref, kn_ref, vn_ref, b_ref, lq1_ref, lk1_ref, lq2_ref, lk2_ref, sg_ref, o_ref,
                             kpad, vpad, kb_s, vb_s, lam_init=lam_init, past_len=past_len)
    else:
        (q_ref, kn_ref, vn_ref, b_ref,
         lq1_ref, lk1_ref, lq2_ref, lk2_ref, sg_ref, o_ref, kpad, vpad) = refs
        _attn_small_body(q_ref, kn_ref, vn_ref, b_ref, lq1_ref, lk1_ref, lq2_ref, lk2_ref, sg_ref, o_ref,
                         kpad, vpad, None, None, lam_init=lam_init, past_len=0)


def _attn_small_body(q_ref, kn_ref, vn_ref, b_ref, lq1_ref, lk1_ref, lq2_ref, lk2_ref, sg_ref, o_ref,
                     kpad, vpad, kb_s, vb_s, *, lam_init, past_len):
    has_cache = past_len > 0
    lam = _diff_lambda(lq1_ref, lk1_ref, lq2_ref, lk2_ref, lam_init)
    _pad_rows(kpad, kn_ref)
    _pad_rows(vpad, vn_ref)
    for h in range(H_ATT):
        vsl = slice(h * DV_ATT, (h + 1) * DV_ATT)
        outs = []
        for c in range(2):
            sl = slice(h * 2 * DQK_ATT + c * DQK_ATT, h * 2 * DQK_ATT + (c + 1) * DQK_ATT)
            qc = q_ref[:, sl].astype(BF16)
            s_new = _dot_nt(qc, kpad[:, sl]) + b_ref[h, :, past_len:]
            m = jnp.max(s_new, axis=-1, keepdims=True)
            if has_cache:
                s_old = _dot_nt(qc, kb_s[:, sl]) + b_ref[h, :, :past_len]
                m = jnp.maximum(m, jnp.max(s_old, axis=-1, keepdims=True))
            p_new = jnp.exp(s_new - m)
            l = jnp.sum(p_new, axis=-1, keepdims=True)
            o = _dot(p_new.astype(BF16), vpad[:, vsl])
            if has_cache:
                p_old = jnp.exp(s_old - m)
                l = l + jnp.sum(p_old, axis=-1, keepdims=True)
                o = o + _dot(p_old.astype(BF16), vb_s[:, vsl])
            outs.append(o / l)
        o_ref[:, vsl] = _subln(outs[0] - lam * outs[1], sg_ref, lam_init)


def _attn_small(z_t, row0, n_streams, bias, lam_vecs, sg, lam_init, cache=None):
    vec = pl.BlockSpec((1, DQK_ATT), lambda b, c: (0, 0))
    in_specs = [pl.BlockSpec((N_META, ATT_QK), lambda b, c: (row0 + b, 0))]
    args = [z_t]
    scratch = [pltpu.VMEM((LANES, ATT_QK), BF16), pltpu.VMEM((LANES, ATT_V), BF16)]
    n_chunks = 1
    if cache is not None:
        ck, cv, layer = cache
        past_len = ck.shape[2]
        chunk = min(past_len, CACHE_CHUNK)
        assert past_len % chunk == 0
        n_chunks = past_len // chunk
        in_specs += [
            pl.BlockSpec((None, None, chunk, H_ATT, 2 * DQK_ATT), lambda b, c: (layer, b, c, 0, 0)),
            pl.BlockSpec((None, None, chunk, H_ATT, DV_ATT), lambda b, c: (layer, b, c, 0, 0))]
        args += [ck, cv]
        scratch += [pltpu.VMEM((past_len, ATT_QK), BF16), pltpu.VMEM((past_len, ATT_V), BF16)]
    in_specs += [
        pl.BlockSpec((N_META, ATT_QK), lambda b, c: (row0 + b, 1)),
        pl.BlockSpec((N_META, ATT_V), lambda b, c: (row0 + b, 2 * ATT_QK // ATT_V)),
        pl.BlockSpec(bias.shape, lambda b, c: (0, 0, 0)),
        vec, vec, vec, vec,
        pl.BlockSpec((1, DV_ATT), lambda b, c: (0, 0)),
    ]
    args += [z_t, z_t, bias, *lam_vecs, sg]
    return pl.pallas_call(
        functools.partial(_attn_small_kernel, lam_init=lam_init, has_cache=cache is not None),
        grid=(n_streams, n_chunks),
        in_specs=in_specs,
        out_specs=pl.BlockSpec((N_META, ATT_V), lambda b, c: (b, 0)),
        out_shape=jax.ShapeDtypeStruct((n_streams * N_META, ATT_V), F32),
        scratch_shapes=scratch,
        compiler_params=pltpu.CompilerParams(
            dimension_semantics=("arbitrary", "arbitrary"), vmem_limit_bytes=VMEM_LIMIT),
        name="attn_cached" if cache is not None else "attn_meta",
    )(*args)


def _mlstm_kernel(*refs, n_blocks, has_init):
    refs = list(refs)
    qh_ref, kh_ref, vh_ref, gh_ref = refs[:4]
    pos = 4
    if n_blocks:
        qf_ref, kf_ref, vf_ref, gf_ref = refs[pos:pos + 4]
        pos += 4
    if has_init:
        c0_ref, n0_ref, m0_ref = refs[pos:pos + 3]
        pos += 3
    hh_ref = refs[pos]
    pos += 1
    if n_blocks:
        hf_ref = refs[pos]
        pos += 1
    c_ref, n_ref, m_ref = refs[pos:pos + 3]
    qs, ks, vs, gs = refs[pos + 3:pos + 7]

    t = pl.program_id(1)
    T = M_TILE

    @pl.when(t == 0)
    def _():
        if has_init:
            c_ref[...] = c0_ref[...]
            n_ref[...] = n0_ref[...]
            m_ref[...] = m0_ref[...]
        else:
            c_ref[...] = jnp.zeros(c_ref.shape, F32)
            n_ref[...] = jnp.zeros(n_ref.shape, F32)
            m_ref[...] = jnp.zeros(m_ref.shape, F32)
        nh = qh_ref.shape[0]
        for dst, src in ((qs, qh_ref), (ks, kh_ref), (vs, vh_ref)):
            dst[...] = jnp.zeros(dst.shape, F32)
            dst[0:nh, :] = src[...]
        lane = lax.broadcasted_iota(jnp.int32, gs.shape, 1)
        gs[...] = jnp.where(lane < H_M, NEG, 0.0)
        gs[0:nh, :] = gh_ref[...]

    if n_blocks:
        @pl.when(t > 0)
        def _():
            qs[...] = qf_ref[...]
            ks[...] = kf_ref[...]
            vs[...] = vf_ref[...]
            gs[...] = gf_ref[...]

    g = gs[...]
    row = lax.broadcasted_iota(jnp.int32, (T, T), 0)
    colid = lax.broadcasted_iota(jnp.int32, (T, T), 1)
    causal = row >= colid
    tril = jnp.where(causal, 1.0, 0.0).astype(F32)
    cum = jnp.dot(tril, g, preferred_element_type=F32, precision=lax.Precision.HIGHEST)
    g_t = g.T
    cum_t = cum.T
    last = cum[T - 1:T, :]

    for h in range(H_M):
        q = qs[:, h * DQK_M:(h + 1) * DQK_M]
        k = ks[:, h * DQK_M:(h + 1) * DQK_M]
        v = vs[:, h * DV_M:(h + 1) * DV_M]
        qb = q.astype(BF16)
        kb = k.astype(BF16)
        m_old = m_ref[h, 0:1, 0:1]
        c_old = c_ref[h]
        n_old = n_ref[h]
        bh_c = _col(cum, H_M + h)
        ih_c = _col(g, h)
        bh_r = cum_t[H_M + h:H_M + h + 1, :]
        ih_r = g_t[h:h + 1, :]
        d = jnp.where(causal, bh_c - bh_r + ih_r, NEG)
        inter = bh_c + m_old
        mt = jnp.maximum(inter, jnp.max(d, axis=-1, keepdims=True))
        a = jnp.exp(d - mt) * _dot_nt(qb, kb)
        iw = jnp.exp(inter - mt)
        num = _dot(a.astype(BF16), v.astype(BF16)) + iw * _dot_nt(qb, c_old.astype(BF16))
        den = jnp.sum(a, axis=-1, keepdims=True) + iw * jnp.sum(q * n_old, axis=-1, keepdims=True)
        den = jnp.maximum(jnp.abs(den), jnp.exp(-mt))
        hout = num / den
        sl = slice(h * DV_M, (h + 1) * DV_M)

        @pl.when(t == 0)
        def _():
            hh_ref[:, sl] = hout[0:hh_ref.shape[0], :]

        if n_blocks:
            @pl.when(t > 0)
            def _():
                hf_ref[:, sl] = hout

        bl = _col(last, H_M + h)
        g_c = bl - bh_c + ih_c
        m_new = jnp.maximum(bl + m_old, jnp.max(g_c, axis=0, keepdims=True))
        sw = jnp.exp(g_c - m_new)
        dec = jnp.exp(bl + m_old - m_new)
        c_ref[h] = dec * c_old + _dot(v.T.astype(BF16), (sw * k).astype(BF16))
        n_ref[h] = dec * n_old + jnp.sum(sw * k, axis=0, keepdims=True)
        m_ref[h] = jnp.broadcast_to(m_new, m_ref.shape[1:])


def _mlstm(z_t, g_t, row0, n_streams, frames=None, init=None):
    T = M_TILE
    qcol = (2 * ATT_QK + ATT_V) // M_QK
    vcol = (2 * ATT_QK + ATT_V + 2 * M_QK) // M_V
    nh = N_META
    in_specs = [
        pl.BlockSpec((nh, M_QK), lambda b, t: (row0 + b, qcol)),
        pl.BlockSpec((nh, M_QK), lambda b, t: (row0 + b, qcol + 1)),
        pl.BlockSpec((nh, M_V), lambda b, t: (row0 + b, vcol)),
        pl.BlockSpec((nh, LANES), lambda b, t: (row0 + b, 0)),
    ]
    args = [z_t, z_t, z_t, g_t]
    n_blocks = 0
    if frames is not None:
        z_f, g_f, fps = frames
        n_blocks = fps // T
        fidx = lambda b, t: b * n_blocks + jnp.maximum(t - 1, 0)
        in_specs += [
            pl.BlockSpec((T, M_QK), lambda b, t: (fidx(b, t), qcol)),
            pl.BlockSpec((T, M_QK), lambda b, t: (fidx(b, t), qcol + 1)),
            pl.BlockSpec((T, M_V), lambda b, t: (fidx(b, t), vcol)),
            pl.BlockSpec((T, LANES), lambda b, t: (fidx(b, t), 0)),
        ]
        args += [z_f, z_f, z_f, g_f]
    state_specs = (
        pl.BlockSpec((None, H_M, DV_M, DQK_M), lambda b, t: (b, 0, 0, 0)),
        pl.BlockSpec((None, H_M, 1, DQK_M), lambda b, t: (b, 0, 0, 0)),
        pl.BlockSpec((None, H_M, 8, LANES), lambda b, t: (b, 0, 0, 0)),
    )
    if init is not None:
        in_specs += list(state_specs)
        args += list(init)
    out_specs = [pl.BlockSpec((nh, M_V), lambda b, t: (b, 0))]
    out_shape = [jax.ShapeDtypeStruct((n_streams * nh, M_V), F32)]
    if n_blocks:
        out_specs.append(pl.BlockSpec((T, M_V), lambda b, t: (fidx(b, t), 0)))
        out_shape.append(jax.ShapeDtypeStruct((n_streams * fps, M_V), F32))
    out_specs += list(state_specs)
    out_shape += [
        jax.ShapeDtypeStruct((n_streams, H_M, DV_M, DQK_M), F32),
        jax.ShapeDtypeStruct((n_streams, H_M, 1, DQK_M), F32),
        jax.ShapeDtypeStruct((n_streams, H_M, 8, LANES), F32),
    ]
    return pl.pallas_call(
        functools.partial(_mlstm_kernel, n_blocks=n_blocks, has_init=init is not None),
        grid=(n_streams, 1 + n_blocks),
        in_specs=in_specs,
        out_specs=tuple(out_specs),
        out_shape=tuple(out_shape),
        scratch_shapes=[
            pltpu.VMEM((T, M_QK), F32), pltpu.VMEM((T, M_QK), F32),
            pltpu.VMEM((T, M_V), F32), pltpu.VMEM((T, LANES), F32),
        ],
        compiler_params=pltpu.CompilerParams(
            dimension_semantics=("arbitrary", "arbitrary"), vmem_limit_bytes=VMEM_LIMIT),
        name="mlstm_prompt" if n_blocks else "mlstm_cached",
    )(*args)


def _merge_kernel(xf_ref, xt_ref, attf_ref, attt_ref, hmf_ref, hmt_ref, zf_ref, zt_ref, mg_ref, wo_ref,
                  of_ref, ot_ref, *, n_frame_tiles):
    is_tail = pl.program_id(0) >= n_frame_tiles
    x = jnp.where(is_tail, xt_ref[...], xf_ref[...])
    att = jnp.where(is_tail, attt_ref[...], attf_ref[...])
    hm = jnp.where(is_tail, hmt_ref[...], hmf_ref[...])
    og = jnp.where(is_tail, zt_ref[...], zf_ref[...])
    parts = []
    for h in range(H_M):
        sl = slice(h * DV_M, (h + 1) * DV_M)
        parts.append(_rms(hm[:, sl], mg_ref[:, sl]) * og[:, sl])
    hmn = jnp.concatenate(parts, axis=1).astype(BF16)
    y = x + _dot(att.astype(BF16), wo_ref[0:ATT_V, :]) + _dot(hmn, wo_ref[ATT_V:, :])

    @pl.when(jnp.logical_not(is_tail))
    def _():
        of_ref[...] = y

    @pl.when(is_tail)
    def _():
        ot_ref[...] = y


def _merge(x_f, x_t, att_f, att_t, hm_f, hm_t, z_f, z_t, mg, w_out, layer):
    NF, D = x_f.shape
    bm = x_t.shape[0]
    nft = NF // bm
    assert NF % bm == 0 and bm == MERGE_ROWS
    ogcol = (N_MAIN - M_V) // M_V
    fr = lambda i: (jnp.minimum(i, nft - 1), 0)
    tl = lambda i: (0, 0)
    return pl.pallas_call(
        functools.partial(_merge_kernel, n_frame_tiles=nft),
        grid=(nft + 1,),
        in_specs=[
            pl.BlockSpec((bm, D), fr), pl.BlockSpec((bm, D), tl),
            pl.BlockSpec((bm, ATT_V), fr), pl.BlockSpec((bm, ATT_V), tl),
            pl.BlockSpec((bm, M_V), fr), pl.BlockSpec((bm, M_V), tl),
            pl.BlockSpec((bm, M_V), lambda i: (jnp.minimum(i, nft - 1), ogcol)),
            pl.BlockSpec((bm, M_V), lambda i: (0, ogcol)),
            pl.BlockSpec((1, M_V), tl),
            pl.BlockSpec((None, D, D), lambda i: (layer, 0, 0)),
        ],
        out_specs=(pl.BlockSpec((bm, D), fr), pl.BlockSpec((bm, D), tl)),
        out_shape=(jax.ShapeDtypeStruct((NF, D), F32), jax.ShapeDtypeStruct((bm, D), F32)),
        compiler_params=pltpu.CompilerParams(
            dimension_semantics=("arbitrary",), vmem_limit_bytes=VMEM_LIMIT),
        name="merge_out_proj",
    )(x_f, x_t, att_f, att_t, hm_f, hm_t, z_f, z_t, mg, w_out)


def _ffn_kernel(xf_ref, xt_ref, g2_ref, wu_ref, wd_ref, of_ref, ot_ref, xn_s, *, n_frame_tiles):
    i = pl.program_id(0)
    k = pl.program_id(1)

    def tile(x_ref, o_ref):
        nrows = x_ref.shape[0]

        @pl.when(k == 0)
        def _():
            x = x_ref[...]
            xn_s[0:nrows, :] = _rms(x, g2_ref[...]).astype(BF16)
            o_ref[...] = x

        u = jnp.maximum(_dot(xn_s[0:nrows, :], wu_ref[...]), 0.0)
        o_ref[...] += _dot((u * u).astype(BF16), wd_ref[...])

    @pl.when(i < n_frame_tiles)
    def _():
        tile(xf_ref, of_ref)

    @pl.when(i == n_frame_tiles)
    def _():
        tile(xt_ref, ot_ref)


def _ffn(x_f, x_t, g2, w_up, w_down, layer):
    NF, D = x_f.shape
    NT = x_t.shape[0]
    FF = w_up.shape[2]
    bm, fc = FF_ROWS, FF_COLS
    nft = NF // bm
    assert NF % bm == 0 and FF % fc == 0 and NT <= bm
    fr = lambda i, k: (jnp.minimum(i, nft - 1), 0)
    tl = lambda i, k: (0, 0)
    return pl.pallas_call(
        functools.partial(_ffn_kernel, n_frame_tiles=nft),
        grid=(nft + 1, FF // fc),
        in_specs=[
            pl.BlockSpec((bm, D), fr),
            pl.BlockSpec((NT, D), tl),
            pl.BlockSpec((1, D), tl),
            pl.BlockSpec((None, D, fc), lambda i, k: (layer, 0, k)),
            pl.BlockSpec((None, fc, D), lambda i, k: (layer, k, 0)),
        ],
        out_specs=(pl.BlockSpec((bm, D), fr), pl.BlockSpec((NT, D), tl)),
        out_shape=(jax.ShapeDtypeStruct((NF, D), F32), jax.ShapeDtypeStruct((NT, D), F32)),
        scratch_shapes=[pltpu.VMEM((bm, D), BF16)],
        compiler_params=pltpu.CompilerParams(
            dimension_semantics=("arbitrary", "arbitrary"), vmem_limit_bytes=VMEM_LIMIT),
        name="ffn",
    )(x_f, x_t, g2, w_up, w_down)


def kernel(x_prompt, x_sample, cache_k, cache_v, state_C, state_n, state_m, meta_tokens, rel_bias, norm1_g, w_in, b_i, b_f, q_norm_g, k_norm_g, lambda_q1, lambda_k1, lambda_q2, lambda_k2, subln_g, mnorm_g, w_out, norm2_g, w_up, w_down):
    B, S, D = x_prompt.shape
    Bd, Td, _ = x_sample.shape
    depth = w_in.shape[0]
    Pc = cache_k.shape[2]
    assert Td == N_META and S % ATT_TILE == 0 and S % M_TILE == 0 and Pc % LANES == 0
    assert ATT_TILE % CHUNK == 0 and CHUNK & (CHUNK - 1) == 0 and ATT_TILE >= MAX_DISTANCE

    n_frames = B * S
    n_meta = B * N_META
    n_samp = Bd * Td
    assert n_meta + n_samp <= TAIL_ROWS
    srow = n_meta // N_META

    meta = jnp.broadcast_to(meta_tokens.astype(F32)[None], (B, N_META, D)).reshape(n_meta, D)
    x_f = x_prompt.astype(F32).reshape(n_frames, D)
    x_t = jnp.concatenate([meta, x_sample.astype(F32).reshape(n_samp, D),
                           jnp.zeros((TAIL_ROWS - n_meta - n_samp, D), F32)], axis=0)

    bias = _bias_tiles(rel_bias.astype(F32), Pc, Td)
    ck = cache_k.astype(F32)
    cv = cache_v.astype(F32)
    zeros_pad_att = jnp.zeros((TAIL_ROWS - n_meta - n_samp, ATT_V), F32)

    g_off = 2 * ATT_QK + ATT_V + 2 * M_QK + M_V
    w_out_b, w_up_b, w_down_b = (w.astype(BF16) for w in (w_out, w_up, w_down))
    outs = {name: [] for name in ("Cp", "np", "mp", "Cs", "ns", "ms")}
    caches = None

    for l in range(depth):
        lam_init = 0.8 - 0.6 * math.exp(-0.3 * l)
        w_og = w_in[l, :, g_off + 2 * H_M:].astype(BF16)
        w_gate = jnp.pad(w_in[l, :, g_off:g_off + 2 * H_M], ((0, 0), (0, LANES - 2 * H_M))).astype(BF16)
        gate_bias = jnp.pad(jnp.concatenate([b_i[l], b_f[l]]).astype(F32), (0, LANES - 2 * H_M))[None]
        lam_vecs = [v[l].astype(F32)[None] for v in (lambda_q1, lambda_k1, lambda_q2, lambda_k2)]
        sg = subln_g[l].astype(F32)[None]

        z_f, z_t, g_f, g_t, *caches = _inproj(
            x_f, x_t, norm1_g[l].astype(F32)[None], w_in.astype(F32), l, w_og, w_gate, gate_bias,
            q_norm_g[l].astype(F32)[None], k_norm_g[l].astype(F32)[None], B, Bd, caches)

        att_f = _attn_prompt(z_f, z_t, bias, lam_vecs, sg, B, S, lam_init)
        att_m = _attn_small(z_t, 0, B, bias[4], lam_vecs, sg, lam_init)
        att_s = _attn_small(z_t, srow, Bd, bias[5], lam_vecs, sg, lam_init, cache=(ck, cv, l))
        att_t = jnp.concatenate([att_m, att_s, zeros_pad_att], axis=0)

        hm_m, hm_f, C_p, n_p, m_p = _mlstm(z_t, g_t, 0, B, frames=(z_f, g_f, S))
        init = (state_C[l].astype(F32), state_n[l].astype(F32)[:, :, None, :],
                jnp.broadcast_to(state_m[l].astype(F32)[:, :, None, None], (Bd, H_M, 8, LANES)))
        hm_s, C_s, n_s, m_s = _mlstm(z_t, g_t, srow, Bd, init=init)
        hm_t = jnp.concatenate([hm_m, hm_s, zeros_pad_att], axis=0)

        x_f, x_t = _merge(x_f, x_t, att_f, att_t, hm_f, hm_t, z_f, z_t,
                          mnorm_g[l].astype(F32)[None], w_out_b, l)
        x_f, x_t = _ffn(x_f, x_t, norm2_g[l].astype(F32)[None], w_up_b, w_down_b, l)

        outs["Cp"].append(C_p)
        outs["np"].append(n_p[:, :, 0, :])
        outs["mp"].append(m_p[:, :, 0, 0])
        outs["Cs"].append(C_s)
        outs["ns"].append(n_s[:, :, 0, :])
        outs["ms"].append(m_s[:, :, 0, 0])

    y_prompt = x_f.reshape(B, S, D)
    y_sample = x_t[n_meta:n_meta + n_samp].reshape(Bd, Td, D)
    k_p, v_p, k_s, v_s = caches
    return (y_prompt, y_sample,
            k_p, v_p, jnp.stack(outs["Cp"]), jnp.stack(outs["np"]), jnp.stack(outs["mp"]),
            k_s.reshape(depth, Bd, Td, H_ATT, 2 * DQK_ATT), v_s.reshape(depth, Bd, Td, H_ATT, DV_ATT),
            jnp.stack(outs["Cs"]), jnp.stack(outs["ns"]), jnp.stack(outs["ms"]))
```

```python
import functools
import math

import jax
import jax.numpy as jnp
from jax import lax
from jax.experimental import pallas as pl
from jax.experimental.pallas import tpu as pltpu

F32 = jnp.float32
BF16 = jnp.bfloat16

CHUNK = 64
N_META = 16
H_ATT = 4
DQK_ATT = 128
DV_ATT = 256
H_M = 4
DQK_M = 128
DV_M = 256
N_BUCKETS = 32
MAX_DISTANCE = 128
EPS = 1e-6
NEG = -1e30
LOG2E = math.log2(math.e)

ATT_QK = H_ATT * 2 * DQK_ATT
ATT_V = H_ATT * DV_ATT
M_QK = H_M * DQK_M
M_V = H_M * DV_M
N_MAIN = 2 * ATT_QK + ATT_V + 2 * M_QK + M_V + M_V

LANES = 128
ATT_TILE = 256
M_TILE = 256
TAIL_ROWS = 256
PROJ_ROWS = 1024
PROJ_COLS = 512
MERGE_ROWS = 512
CACHE_CHUNK = 1024
FF_ROWS = 512
FF_COLS = 1024
VMEM_LIMIT = 56 * 1024 * 1024

_NT = (((1,), (1,)), ((), ()))


def _dot(a, b):
    return jnp.dot(a, b, preferred_element_type=F32)


def _dot_nt(a, b):
    return lax.dot_general(a, b, _NT, preferred_element_type=F32)


def _log_sigmoid(x):
    return jnp.minimum(x, 0.0) - jnp.log1p(jnp.exp(-jnp.abs(x)))


def _col(a, idx):
    lane = lax.broadcasted_iota(jnp.int32, a.shape, 1)
    return jnp.sum(jnp.where(lane == idx, a, 0.0), axis=1, keepdims=True)


def _rms(x, gain):
    ms = jnp.mean(x * x, axis=-1, keepdims=True)
    return x * lax.rsqrt(ms + EPS) * gain


def _bias_from_rel(rel, rb_ref, h):
    nb = N_BUCKETS // 2
    max_exact = nb // 2
    ret = jnp.where(rel > 0, nb, 0)
    n = jnp.abs(rel)
    nf = jnp.maximum(n, 1).astype(F32)
    large = max_exact + (jnp.log(nf / max_exact) / math.log(MAX_DISTANCE / max_exact)
                         * (nb - max_exact)).astype(jnp.int32)
    large = jnp.minimum(large, nb - 1)
    bucket = ret + jnp.where(n < max_exact, n, large)
    out = jnp.zeros(rel.shape, F32)
    for b in range(N_BUCKETS):
        out = jnp.where(bucket == b, rb_ref[b, h], out)
    return out


def _bias_kernel(rb_ref, diag_ref, prev_ref, meta_ref, mm_ref, samp_ref, *, past_len, dec_seq):
    h = pl.program_id(0)
    shift = CHUNK.bit_length() - 1
    T = ATT_TILE
    far = _bias_from_rel(jnp.full((8, LANES), -(2 * MAX_DISTANCE + 1), jnp.int32), rb_ref, h)[0:1, 0:1]

    qi = lax.broadcasted_iota(jnp.int32, (T, T), 0)
    kj = lax.broadcasted_iota(jnp.int32, (T, T), 1)
    bias = (_bias_from_rel(kj - qi, rb_ref, h) - far) * LOG2E
    diag_ref[...] = jnp.where((kj >> shift) <= (qi >> shift), bias, NEG)
    prev_ref[...] = (_bias_from_rel(kj - T - qi, rb_ref, h) - far) * LOG2E

    qi = lax.broadcasted_iota(jnp.int32, (T, LANES), 0)
    kj = lax.broadcasted_iota(jnp.int32, (T, LANES), 1)
    for var in range(2):
        bias = (_bias_from_rel(kj - (N_META + qi + var * T), rb_ref, h) - far) * LOG2E
        meta_ref[var] = jnp.where(kj < N_META, bias, NEG)

    qi = lax.broadcasted_iota(jnp.int32, (N_META, LANES), 0)
    kj = lax.broadcasted_iota(jnp.int32, (N_META, LANES), 1)
    mm_ref[...] = jnp.where(kj < N_META, _bias_from_rel(kj - qi, rb_ref, h) * LOG2E, NEG)

    width = samp_ref.shape[-1]
    qi = lax.broadcasted_iota(jnp.int32, (dec_seq, width), 0) + past_len
    kj = lax.broadcasted_iota(jnp.int32, (dec_seq, width), 1)
    bias = _bias_from_rel(kj - qi, rb_ref, h) * LOG2E
    ok = ((kj >> shift) <= (qi >> shift)) & (kj < past_len + dec_seq)
    samp_ref[...] = jnp.where(ok, bias, NEG)


def _bias_tiles(rel_bias, past_len, dec_seq):
    T = ATT_TILE
    sw = past_len + LANES
    out_shape = (
        jax.ShapeDtypeStruct((H_ATT, T, T), F32),
        jax.ShapeDtypeStruct((H_ATT, T, T), F32),
        jax.ShapeDtypeStruct((H_ATT, 2, T, LANES), F32),
        jax.ShapeDtypeStruct((H_ATT, N_META, LANES), F32),
        jax.ShapeDtypeStruct((H_ATT, dec_seq, sw), F32),
    )
    out_specs = (
        pl.BlockSpec((None, T, T), lambda h: (h, 0, 0)),
        pl.BlockSpec((None, T, T), lambda h: (h, 0, 0)),
        pl.BlockSpec((None, 2, T, LANES), lambda h: (h, 0, 0, 0)),
        pl.BlockSpec((None, N_META, LANES), lambda h: (h, 0, 0)),
        pl.BlockSpec((None, dec_seq, sw), lambda h: (h, 0, 0)),
    )
    return pl.pallas_call(
        functools.partial(_bias_kernel, past_len=past_len, dec_seq=dec_seq),
        grid=(H_ATT,),
        in_specs=[pl.BlockSpec(memory_space=pltpu.SMEM)],
        out_specs=out_specs,
        out_shape=out_shape,
        name="rel_bias_tiles",
    )(rel_bias)


def _kv_copies(stage, sem, dst_prompt, dst_sample, i, *, tail, n_batch, frames_per_batch, n_sample_rows):
    if not tail:
        rows = stage.shape[0]
        tiles_per_batch = frames_per_batch // rows
        b = i // tiles_per_batch
        t = i - b * tiles_per_batch
        return [pltpu.make_async_copy(stage, dst_prompt.at[b, pl.ds(N_META + t * rows, rows)], sem.at[0])]
    copies = [pltpu.make_async_copy(stage.at[pl.ds(b * N_META, N_META)], dst_prompt.at[b, pl.ds(0, N_META)],
                                    sem.at[b]) for b in range(n_batch)]
    copies.append(pltpu.make_async_copy(stage.at[pl.ds(n_batch * N_META, n_sample_rows)], dst_sample,
                                        sem.at[n_batch]))
    return copies


def _inproj_kernel(xf_ref, xt_ref, g1_ref, wa_ref, wb_ref, wg_ref, gb_ref, qg_ref, kg_ref,
                   zf_ref, zt_ref, gf_ref, gt_ref, kp_ref, vp_ref, ks_ref, vs_ref,
                   xn_s, y2d, stage, sem, *, n_frame_tiles, n_batch, frames_per_batch, n_sample_rows):
    i = pl.program_id(0)
    j = pl.program_id(1)
    bn = wa_ref.shape[1]
    assert ATT_QK % bn == 0 and ATT_V % bn == 0 and M_QK % bn == 0 and M_V % bn == 0 and bn % DV_ATT == 0
    j_k = ATT_QK // bn
    j_v = j_k + ATT_QK // bn
    j_mq = j_v + ATT_V // bn
    j_mk = j_mq + M_QK // bn
    j_mv = j_mk + M_QK // bn
    j_og = j_mv + M_V // bn

    def tile(x_ref, z_ref, g_ref, tail):
        nrows = x_ref.shape[0]
        st = stage.at[pl.ds(0, nrows)]
        copies = functools.partial(_kv_copies, st, sem, i=i, tail=tail, n_batch=n_batch,
                                   frames_per_batch=frames_per_batch, n_sample_rows=n_sample_rows)

        @pl.when(j == 0)
        def _():
            xb = _rms(x_ref[...], g1_ref[...]).astype(BF16)
            xn_s[0:nrows, :] = xb
            gates = _dot(xb, wg_ref[...]) + gb_ref[...]
            lane = lax.broadcasted_iota(jnp.int32, gates.shape, 1)
            g_ref[...] = jnp.where(lane < H_M, gates, _log_sigmoid(gates))

        acc = _dot(xn_s[0:nrows, :], jnp.where(j < j_og, wa_ref[...], wb_ref[...]))

        def group_norm(gain_ref, scale):
            parts = []
            for gi in range(bn // DQK_ATT):
                y = _rms(acc[:, gi * DQK_ATT:(gi + 1) * DQK_ATT], gain_ref[...])
                parts.append(y if scale is None else y * scale)
            return jnp.concatenate(parts, axis=1)

        def stage_heads(y, jj, j0, dst_prompt, dst_sample):
            c0 = (jj - j0) * bn
            y2d[0:nrows, c0:c0 + bn] = y
            if jj == j0 + ATT_QK // bn - 1:
                if j0 != j_k:
                    for c in copies(kp_ref, ks_ref):
                        c.wait()
                st[...] = y2d[0:nrows, :].reshape(nrows, H_ATT, DV_ATT)
                for c in copies(dst_prompt, dst_sample):
                    c.start()

        @pl.when(j < j_k)
        def _():
            z_ref[...] = group_norm(qg_ref, DQK_ATT ** -0.5 * LOG2E)

        for jj in range(j_k, j_v):
            @pl.when(j == jj)
            def _(jj=jj):
                y = group_norm(kg_ref, None)
                z_ref[...] = y
                stage_heads(y, jj, j_k, kp_ref, ks_ref)

        for jj in range(j_v, j_mq):
            @pl.when(j == jj)
            def _(jj=jj):
                z_ref[...] = acc
                stage_heads(acc, jj, j_v, vp_ref, vs_ref)

        @pl.when(j == j_mq)
        def _():
            for c in copies(vp_ref, vs_ref):
                c.wait()

        @pl.when(((j >= j_mq) & (j < j_mk)) | ((j >= j_mv) & (j < j_og)))
        def _():
            z_ref[...] = acc

        @pl.when((j >= j_mk) & (j < j_mv))
        def _():
            z_ref[...] = acc * (DQK_M ** -0.5)

        @pl.when(j >= j_og)
        def _():
            z_ref[...] = jax.nn.sigmoid(acc)

    @pl.when(i < n_frame_tiles)
    def _():
        tile(xf_ref, zf_ref, gf_ref, False)

    @pl.when(i == n_frame_tiles)
    def _():
        tile(xt_ref, zt_ref, gt_ref, True)


def _inproj(x_f, x_t, g1, w_all, layer, w_og, w_gate, gate_bias, qg, kg, n_batch, n_sample_streams):
    NF, D = x_f.shape
    NT = x_t.shape[0]
    bm, bn = PROJ_ROWS, PROJ_COLS
    nft = NF // bm
    nj = N_MAIN // bn
    nja = (N_MAIN - M_V) // bn
    S = NF // n_batch
    L = N_META + S
    n_sample_rows = n_sample_streams * N_META
    assert NF % bm == 0 and S % bm == 0 and N_MAIN % bn == 0 and NT <= bm
    fr = lambda i: jnp.minimum(i, nft - 1)
    any_spec = pl.BlockSpec(memory_space=pl.ANY)
    return pl.pallas_call(
        functools.partial(_inproj_kernel, n_frame_tiles=nft, n_batch=n_batch, frames_per_batch=S,
                          n_sample_rows=n_sample_rows),
        grid=(nft + 1, nj),
        in_specs=[
            pl.BlockSpec((bm, D), lambda i, j: (fr(i), 0)),
            pl.BlockSpec((NT, D), lambda i, j: (0, 0)),
            pl.BlockSpec((1, D), lambda i, j: (0, 0)),
            pl.BlockSpec((None, D, bn), lambda i, j: (layer, 0, jnp.minimum(j, nja - 1))),
            pl.BlockSpec((D, bn), lambda i, j: (0, jnp.maximum(j - nja, 0))),
            pl.BlockSpec((D, LANES), lambda i, j: (0, 0)),
            pl.BlockSpec((1, LANES), lambda i, j: (0, 0)),
            pl.BlockSpec((1, DQK_ATT), lambda i, j: (0, 0)),
            pl.BlockSpec((1, DQK_ATT), lambda i, j: (0, 0)),
        ],
        out_specs=(
            pl.BlockSpec((bm, bn), lambda i, j: (fr(i), jnp.where(i < nft, j, nj - 1))),
            pl.BlockSpec((NT, bn), lambda i, j: (0, jnp.where(i < nft, 0, j))),
            pl.BlockSpec((bm, LANES), lambda i, j: (fr(i), 0)),
            pl.BlockSpec((NT, LANES), lambda i, j: (0, 0)),
            any_spec, any_spec, any_spec, any_spec,
        ),
        out_shape=(
            jax.ShapeDtypeStruct((NF, N_MAIN), F32),
            jax.ShapeDtypeStruct((NT, N_MAIN), F32),
            jax.ShapeDtypeStruct((NF, LANES), F32),
            jax.ShapeDtypeStruct((NT, LANES), F32),
            jax.ShapeDtypeStruct((n_batch, L, H_ATT, 2 * DQK_ATT), F32),
            jax.ShapeDtypeStruct((n_batch, L, H_ATT, DV_ATT), F32),
            jax.ShapeDtypeStruct((n_sample_rows, H_ATT, 2 * DQK_ATT), F32),
            jax.ShapeDtypeStruct((n_sample_rows, H_ATT, DV_ATT), F32),
        ),
        scratch_shapes=[
            pltpu.VMEM((bm, D), BF16),
            pltpu.VMEM((bm, ATT_QK), F32),
            pltpu.VMEM((bm, H_ATT, DV_ATT), F32),
            pltpu.SemaphoreType.DMA((n_batch + 1,)),
        ],
        compiler_params=pltpu.CompilerParams(
            dimension_semantics=("arbitrary", "arbitrary"), vmem_limit_bytes=VMEM_LIMIT),
        name="in_proj",
    )(x_f, x_t, g1, w_all, w_og, w_gate, gate_bias, qg, kg)


def _diff_lambda(lq1_ref, lk1_ref, lq2_ref, lk2_ref, lam_init):
    a = jnp.sum(lq1_ref[...] * lk1_ref[...], axis=-1, keepdims=True)
    b = jnp.sum(lq2_ref[...] * lk2_ref[...], axis=-1, keepdims=True)
    return jnp.exp(a) - jnp.exp(b) + lam_init


def _subln(o, sg_ref, lam_init):
    return _rms(o, sg_ref[...]) * (1.0 - lam_init)


def _pad_rows(dst, src):
    dst[...] = jnp.zeros(dst.shape, dst.dtype)
    dst[0:src.shape[0], :] = src[...].astype(dst.dtype)


def _attn_prompt_kernel(q_ref, k_ref, v_ref, km_ref, vm_ref, bd_ref, bp_ref, bm_ref,
                        lq1_ref, lk1_ref, lq2_ref, lk2_ref, sg_ref, o_ref,
                        kb_s, vb_s, kmb_s, vmb_s, *, lam_init):
    T = ATT_TILE
    n_tiles = q_ref.shape[0] // T
    kb_s[...] = k_ref[...].astype(BF16)
    vb_s[...] = v_ref[...].astype(BF16)
    _pad_rows(kmb_s, km_ref)
    _pad_rows(vmb_s, vm_ref)
    lam = _diff_lambda(lq1_ref, lk1_ref, lq2_ref, lk2_ref, lam_init)

    for j in range(n_tiles):
        rows = slice(j * T, (j + 1) * T)
        regions = [(kmb_s, vmb_s, slice(0, LANES), bm_ref[min(j, 1)])]
        if j >= 2:
            regions.append((kb_s, vb_s, slice(0, (j - 1) * T), None))
        if j >= 1:
            regions.append((kb_s, vb_s, slice((j - 1) * T, j * T), bp_ref[...]))
        regions.append((kb_s, vb_s, rows, bd_ref[...]))

        probs, denom = [], []
        for c in range(2):
            cols = slice(c * DQK_ATT, (c + 1) * DQK_ATT)
            qc = q_ref[rows, cols].astype(BF16)
            ss = [_dot_nt(qc, kr[rs, cols]) for kr, _, rs, _ in regions]
            ss = [s if br is None else s + br for s, (_, _, _, br) in zip(ss, regions)]
            m = functools.reduce(jnp.maximum, [jnp.max(s, axis=-1, keepdims=True) for s in ss])
            pc = [jnp.exp2(s - m) for s in ss]
            denom.append(functools.reduce(jnp.add, [jnp.sum(p, axis=-1, keepdims=True) for p in pc]))
            probs.append(pc)
        a0 = 1.0 / denom[0]
        a1 = lam / denom[1]
        o = None
        for r, (_, vr, rs, _) in enumerate(regions):
            w = (probs[0][r] * a0 - probs[1][r] * a1).astype(BF16)
            t = _dot(w, vr[rs, :])
            o = t if o is None else o + t
        o_ref[rows, :] = _subln(o, sg_ref, lam_init)


def _attn_prompt(z_f, z_t, bias, lam_vecs, sg, B, S, lam_init):
    T = ATT_TILE
    qkw = 2 * DQK_ATT
    kcol = ATT_QK // qkw
    vcol = 2 * ATT_QK // DV_ATT
    b_diag, b_prev, b_meta, _, _ = bias
    vec = pl.BlockSpec((1, DQK_ATT), lambda b, h: (0, 0))
    return pl.pallas_call(
        functools.partial(_attn_prompt_kernel, lam_init=lam_init),
        grid=(B, H_ATT),
        in_specs=[
            pl.BlockSpec((S, qkw), lambda b, h: (b, h)),
            pl.BlockSpec((S, qkw), lambda b, h: (b, kcol + h)),
            pl.BlockSpec((S, DV_ATT), lambda b, h: (b, vcol + h)),
            pl.BlockSpec((N_META, qkw), lambda b, h: (b, kcol + h)),
            pl.BlockSpec((N_META, DV_ATT), lambda b, h: (b, vcol + h)),
            pl.BlockSpec((None, T, T), lambda b, h: (h, 0, 0)),
            pl.BlockSpec((None, T, T), lambda b, h: (h, 0, 0)),
            pl.BlockSpec((None, 2, T, LANES), lambda b, h: (h, 0, 0, 0)),
            vec, vec, vec, vec,
            pl.BlockSpec((1, DV_ATT), lambda b, h: (0, 0)),
        ],
        out_specs=pl.BlockSpec((S, DV_ATT), lambda b, h: (b, h)),
        out_shape=jax.ShapeDtypeStruct((B * S, ATT_V), F32),
        scratch_shapes=[
            pltpu.VMEM((S, qkw), BF16),
            pltpu.VMEM((S, DV_ATT), BF16),
            pltpu.VMEM((LANES, qkw), BF16),
            pltpu.VMEM((LANES, DV_ATT), BF16),
        ],
        compiler_params=pltpu.CompilerParams(
            dimension_semantics=("arbitrary", "arbitrary"), vmem_limit_bytes=VMEM_LIMIT),
        name="attn_prompt",
    )(z_f, z_f, z_f, z_t, z_t, b_diag, b_prev, b_meta, *lam_vecs, sg)


def _attn_small_kernel(*refs, lam_init, has_cache):
    if has_cache:
        (q_ref, kc_ref, vc_ref, kn_ref, vn_ref, b_ref,
         lq1_ref, lk1_ref, lq2_ref, lk2_ref, sg_ref, o_ref, kpad, vpad, kb_s, vb_s) = refs
        past_len = kb_s.shape[0]
        chunk = kc_ref.shape[0]
        c = pl.program_id(1)
        r0 = pl.multiple_of(c * chunk, chunk)
        kb_s[pl.ds(r0, chunk), :] = kc_ref[...].reshape(chunk, ATT_QK).astype(BF16)
        vb_s[pl.ds(r0, chunk), :] = vc_ref[...].reshape(chunk, ATT_V).astype(BF16)

        @pl.when(c == past_len // chunk - 1)
        def _():
            _attn_small_body(q_ref, kn_ref, vn_ref, b_ref, lq1_ref, lk1_ref, lq2_ref, lk2_ref, sg_ref, o_ref,
                             kpad, vpad, kb_s, vb_s, lam_init=lam_init, past_len=past_len)
    else:
        (q_ref, kn_ref, vn_ref, b_ref,
         lq1_ref, lk1_ref, lq2_ref, lk2_ref, sg_ref, o_ref, kpad, vpad) = refs
        _attn_small_body(q_ref, kn_ref, vn_ref, b_ref, lq1_ref, lk1_ref, lq2_ref, lk2_ref, sg_ref, o_ref,
                         kpad, vpad, None, None, lam_init=lam_init, past_len=0)


def _attn_small_body(q_ref, kn_ref, vn_ref, b_ref, lq1_ref, lk1_ref, lq2_ref, lk2_ref, sg_ref, o_ref,
                     kpad, vpad, kb_s, vb_s, *, lam_init, past_len):
    has_cache = past_len > 0
    lam = _diff_lambda(lq1_ref, lk1_ref, lq2_ref, lk2_ref, lam_init)
    _pad_rows(kpad, kn_ref)
    _pad_rows(vpad, vn_ref)
    for h in range(H_ATT):
        vsl = slice(h * DV_ATT, (h + 1) * DV_ATT)
        outs = []
        for c in range(2):
            sl = slice(h * 2 * DQK_ATT + c * DQK_ATT, h * 2 * DQK_ATT + (c + 1) * DQK_ATT)
            qc = q_ref[:, sl].astype(BF16)
            s_new = _dot_nt(qc, kpad[:, sl]) + b_ref[h, :, past_len:]
            m = jnp.max(s_new, axis=-1, keepdims=True)
            if has_cache:
                s_old = _dot_nt(qc, kb_s[:, sl]) + b_ref[h, :, :past_len]
                m = jnp.maximum(m, jnp.max(s_old, axis=-1, keepdims=True))
            p_new = jnp.exp2(s_new - m)
            l = jnp.sum(p_new, axis=-1, keepdims=True)
            o = _dot(p_new.astype(BF16), vpad[:, vsl])
            if has_cache:
                p_old = jnp.exp2(s_old - m)
                l = l + jnp.sum(p_old, axis=-1, keepdims=True)
                o = o + _dot(p_old.astype(BF16), vb_s[:, vsl])
            outs.append(o / l)
        o_ref[:, vsl] = _subln(outs[0] - lam * outs[1], sg_ref, lam_init)


def _attn_small(z_t, row0, n_streams, bias, lam_vecs, sg, lam_init, cache=None):
    vec = pl.BlockSpec((1, DQK_ATT), lambda b, c: (0, 0))
    in_specs = [pl.BlockSpec((N_META, ATT_QK), lambda b, c: (row0 + b, 0))]
    args = [z_t]
    scratch = [pltpu.VMEM((LANES, ATT_QK), BF16), pltpu.VMEM((LANES, ATT_V), BF16)]
    n_chunks = 1
    if cache is not None:
        ck, cv, layer = cache
        past_len = ck.shape[2]
        chunk = min(past_len, CACHE_CHUNK)
        assert past_len % chunk == 0
        n_chunks = past_len // chunk
        in_specs += [
            pl.BlockSpec((None, None, chunk, H_ATT, 2 * DQK_ATT), lambda b, c: (layer, b, c, 0, 0)),
            pl.BlockSpec((None, None, chunk, H_ATT, DV_ATT), lambda b, c: (layer, b, c, 0, 0))]
        args += [ck, cv]
        scratch += [pltpu.VMEM((past_len, ATT_QK), BF16), pltpu.VMEM((past_len, ATT_V), BF16)]
    in_specs += [
        pl.BlockSpec((N_META, ATT_QK), lambda b, c: (row0 + b, 1)),
        pl.BlockSpec((N_META, ATT_V), lambda b, c: (row0 + b, 2 * ATT_QK // ATT_V)),
        pl.BlockSpec(bias.shape, lambda b, c: (0, 0, 0)),
        vec, vec, vec, vec,
        pl.BlockSpec((1, DV_ATT), lambda b, c: (0, 0)),
    ]
    args += [z_t, z_t, bias, *lam_vecs, sg]
    return pl.pallas_call(
        functools.partial(_attn_small_kernel, lam_init=lam_init, has_cache=cache is not None),
        grid=(n_streams, n_chunks),
        in_specs=in_specs,
        out_specs=pl.BlockSpec((N_META, ATT_V), lambda b, c: (b, 0)),
        out_shape=jax.ShapeDtypeStruct((n_streams * N_META, ATT_V), F32),
        scratch_shapes=scratch,
        compiler_params=pltpu.CompilerParams(
            dimension_semantics=("arbitrary", "arbitrary"), vmem_limit_bytes=VMEM_LIMIT),
        name="attn_cached" if cache is not None else "attn_meta",
    )(*args)


def _mlstm_kernel(*refs, n_blocks, has_init):
    refs = list(refs)
    qh_ref, kh_ref, vh_ref, gh_ref = refs[:4]
    pos = 4
    if n_blocks:
        qf_ref, kf_ref, vf_ref, gf_ref = refs[pos:pos + 4]
        pos += 4
    if has_init:
        c0_ref, n0_ref, m0_ref = refs[pos:pos + 3]
        pos += 3
    hh_ref = refs[pos]
    pos += 1
    if n_blocks:
        hf_ref = refs[pos]
        pos += 1
    c_ref, n_ref, m_ref = refs[pos:pos + 3]
    qs, ks, vs, gs = refs[pos + 3:pos + 7]

    t = pl.program_id(1)
    T = M_TILE
    block = functools.partial(_mlstm_block, c_ref=c_ref, n_ref=n_ref, m_ref=m_ref)

    @pl.when(t == 0)
    def _():
        if has_init:
            c_ref[...] = c0_ref[...]
            n_ref[...] = n0_ref[...]
            m_ref[...] = m0_ref[...]
        else:
            c_ref[...] = jnp.zeros(c_ref.shape, F32)
            n_ref[...] = jnp.zeros(n_ref.shape, F32)
            m_ref[...] = jnp.zeros(m_ref.shape, F32)
        nh = qh_ref.shape[0]
        for dst, src in ((qs, qh_ref), (ks, kh_ref), (vs, vh_ref)):
            dst[...] = jnp.zeros(dst.shape, F32)
            dst[0:nh, :] = src[...]
        lane = lax.broadcasted_iota(jnp.int32, gs.shape, 1)
        gs[...] = jnp.where(lane < H_M, NEG, 0.0)
        gs[0:nh, :] = gh_ref[...]

        block(qs, ks, vs, gs, hh_ref, nh)

    if n_blocks:
        @pl.when(t > 0)
        def _():
            block(qf_ref, kf_ref, vf_ref, gf_ref, hf_ref, T)


def _mlstm_block(q_src, k_src, v_src, g_src, h_dst, n_out, c_ref, n_ref, m_ref):
    T = M_TILE
    g = g_src[...]
    row = lax.broadcasted_iota(jnp.int32, (T, T), 0)
    colid = lax.broadcasted_iota(jnp.int32, (T, T), 1)
    causal = row >= colid
    tril = jnp.where(causal, 1.0, 0.0).astype(F32)
    cum = jnp.dot(tril, g, preferred_element_type=F32, precision=lax.Precision.HIGHEST)
    g_t = g.T
    cum_t = cum.T
    last = cum[T - 1:T, :]

    for h in range(H_M):
        q = q_src[:, h * DQK_M:(h + 1) * DQK_M]
        k = k_src[:, h * DQK_M:(h + 1) * DQK_M]
        v = v_src[:, h * DV_M:(h + 1) * DV_M]
        qb = q.astype(BF16)
        kb = k.astype(BF16)
        m_old = m_ref[h, 0:1, 0:1]
        c_old = c_ref[h]
        n_old = n_ref[h]
        bh_c = _col(cum, H_M + h)
        ih_c = _col(g, h)
        bh_r = cum_t[H_M + h:H_M + h + 1, :]
        ih_r = g_t[h:h + 1, :]
        d = jnp.where(causal, bh_c - bh_r + ih_r, NEG)
        inter = bh_c + m_old
        mt = jnp.maximum(inter, jnp.max(d, axis=-1, keepdims=True))
        a = jnp.exp(d - mt) * _dot_nt(qb, kb)
        iw = jnp.exp(inter - mt)
        num = _dot(a.astype(BF16), v.astype(BF16)) + iw * _dot_nt(qb, c_old.astype(BF16))
        den = jnp.sum(a, axis=-1, keepdims=True) + iw * jnp.sum(q * n_old, axis=-1, keepdims=True)
        den = jnp.maximum(jnp.abs(den), jnp.exp(-mt))
        hout = num / den
        h_dst[:, h * DV_M:(h + 1) * DV_M] = hout[0:n_out, :]

        bl = _col(last, H_M + h)
        g_c = bl - bh_c + ih_c
        m_new = jnp.maximum(bl + m_old, jnp.max(g_c, axis=0, keepdims=True))
        sw = jnp.exp(g_c - m_new)
        dec = jnp.exp(bl + m_old - m_new)
        c_ref[h] = dec * c_old + _dot(v.T.astype(BF16), (sw * k).astype(BF16))
        n_ref[h] = dec * n_old + jnp.sum(sw * k, axis=0, keepdims=True)
        m_ref[h] = jnp.broadcast_to(m_new, m_ref.shape[1:])


def _mlstm(z_t, g_t, row0, n_streams, frames=None, init=None):
    T = M_TILE
    qcol = (2 * ATT_QK + ATT_V) // M_QK
    vcol = (2 * ATT_QK + ATT_V + 2 * M_QK) // M_V
    nh = N_META
    in_specs = [
        pl.BlockSpec((nh, M_QK), lambda b, t: (row0 + b, qcol)),
        pl.BlockSpec((nh, M_QK), lambda b, t: (row0 + b, qcol + 1)),
        pl.BlockSpec((nh, M_V), lambda b, t: (row0 + b, vcol)),
        pl.BlockSpec((nh, LANES), lambda b, t: (row0 + b, 0)),
    ]
    args = [z_t, z_t, z_t, g_t]
    n_blocks = 0
    if frames is not None:
        z_f, g_f, fps = frames
        n_blocks = fps // T
        fidx = lambda b, t: b * n_blocks + jnp.maximum(t - 1, 0)
        in_specs += [
            pl.BlockSpec((T, M_QK), lambda b, t: (fidx(b, t), qcol)),
            pl.BlockSpec((T, M_QK), lambda b, t: (fidx(b, t), qcol + 1)),
            pl.BlockSpec((T, M_V), lambda b, t: (fidx(b, t), vcol)),
            pl.BlockSpec((T, LANES), lambda b, t: (fidx(b, t), 0)),
        ]
        args += [z_f, z_f, z_f, g_f]
    state_specs = (
        pl.BlockSpec((None, H_M, DV_M, DQK_M), lambda b, t: (b, 0, 0, 0)),
        pl.BlockSpec((None, H_M, 1, DQK_M), lambda b, t: (b, 0, 0, 0)),
        pl.BlockSpec((None, H_M, 8, LANES), lambda b, t: (b, 0, 0, 0)),
    )
    if init is not None:
        in_specs += list(state_specs)
        args += list(init)
    out_specs = [pl.BlockSpec((nh, M_V), lambda b, t: (b, 0))]
    out_shape = [jax.ShapeDtypeStruct((n_streams * nh, M_V), F32)]
    if n_blocks:
        out_specs.append(pl.BlockSpec((T, M_V), lambda b, t: (fidx(b, t), 0)))
        out_shape.append(jax.ShapeDtypeStruct((n_streams * fps, M_V), F32))
    out_specs += list(state_specs)
    out_shape += [
        jax.ShapeDtypeStruct((n_streams, H_M, DV_M, DQK_M), F32),
        jax.ShapeDtypeStruct((n_streams, H_M, 1, DQK_M), F32),
        jax.ShapeDtypeStruct((n_streams, H_M, 8, LANES), F32),
    ]
    return pl.pallas_call(
        functools.partial(_mlstm_kernel, n_blocks=n_blocks, has_init=init is not None),
        grid=(n_streams, 1 + n_blocks),
        in_specs=in_specs,
        out_specs=tuple(out_specs),
        out_shape=tuple(out_shape),
        scratch_shapes=[
            pltpu.VMEM((T, M_QK), F32), pltpu.VMEM((T, M_QK), F32),
            pltpu.VMEM((T, M_V), F32), pltpu.VMEM((T, LANES), F32),
        ],
        compiler_params=pltpu.CompilerParams(
            dimension_semantics=("arbitrary", "arbitrary"), vmem_limit_bytes=VMEM_LIMIT),
        name="mlstm_prompt" if n_blocks else "mlstm_cached",
    )(*args)


def _merge_kernel(xf_ref, xt_ref, attf_ref, attt_ref, hmf_ref, hmt_ref, zf_ref, zt_ref, mg_ref, wo_ref,
                  of_ref, ot_ref, *, n_frame_tiles):
    i = pl.program_id(0)

    def tile(x_ref, att_ref, hm_ref, og_ref, o_ref):
        parts = []
        for h in range(H_M):
            sl = slice(h * DV_M, (h + 1) * DV_M)
            parts.append(_rms(hm_ref[:, sl], mg_ref[:, sl]) * og_ref[:, sl])
        hmn = jnp.concatenate(parts, axis=1).astype(BF16)
        o_ref[...] = (x_ref[...] + _dot(att_ref[...].astype(BF16), wo_ref[0:ATT_V, :])
                      + _dot(hmn, wo_ref[ATT_V:, :]))

    @pl.when(i < n_frame_tiles)
    def _():
        tile(xf_ref, attf_ref, hmf_ref, zf_ref, of_ref)

    @pl.when(i == n_frame_tiles)
    def _():
        tile(xt_ref, attt_ref, hmt_ref, zt_ref, ot_ref)


def _merge(x_f, x_t, att_f, att_t, hm_f, hm_t, z_f, z_t, mg, w_out, layer):
    NF, D = x_f.shape
    NT = x_t.shape[0]
    bm = MERGE_ROWS
    nft = NF // bm
    assert NF % bm == 0
    ogcol = (N_MAIN - M_V) // M_V
    fr = lambda i: (jnp.minimum(i, nft - 1), 0)
    tl = lambda i: (0, 0)
    once = dict(pipeline_mode=pl.Buffered(1))
    return pl.pallas_call(
        functools.partial(_merge_kernel, n_frame_tiles=nft),
        grid=(nft + 1,),
        in_specs=[
            pl.BlockSpec((bm, D), fr), pl.BlockSpec((NT, D), tl, **once),
            pl.BlockSpec((bm, ATT_V), fr), pl.BlockSpec((NT, ATT_V), tl, **once),
            pl.BlockSpec((bm, M_V), fr), pl.BlockSpec((NT, M_V), tl, **once),
            pl.BlockSpec((bm, M_V), lambda i: (jnp.minimum(i, nft - 1), ogcol)),
            pl.BlockSpec((NT, M_V), lambda i: (0, ogcol), **once),
            pl.BlockSpec((1, M_V), tl),
            pl.BlockSpec((None, D, D), lambda i: (layer, 0, 0), **once),
        ],
        out_specs=(pl.BlockSpec((bm, D), fr), pl.BlockSpec((NT, D), tl)),
        out_shape=(jax.ShapeDtypeStruct((NF, D), F32), jax.ShapeDtypeStruct((NT, D), F32)),
        compiler_params=pltpu.CompilerParams(
            dimension_semantics=("arbitrary",), vmem_limit_bytes=VMEM_LIMIT),
        name="merge_out_proj",
    )(x_f, x_t, att_f, att_t, hm_f, hm_t, z_f, z_t, mg, w_out)


def _ffn_kernel(xf_ref, xt_ref, g2_ref, wu_ref, wd_ref, of_ref, ot_ref, xn_s, *, n_frame_tiles):
    i = pl.program_id(0)
    k = pl.program_id(1)

    def tile(x_ref, o_ref):
        nrows = x_ref.shape[0]

        @pl.when(k == 0)
        def _():
            x = x_ref[...]
            xn_s[0:nrows, :] = _rms(x, g2_ref[...]).astype(BF16)
            o_ref[...] = x

        u = jnp.maximum(_dot(xn_s[0:nrows, :], wu_ref[...]), 0.0)
        o_ref[...] += _dot((u * u).astype(BF16), wd_ref[...])

    @pl.when(i < n_frame_tiles)
    def _():
        tile(xf_ref, of_ref)

    @pl.when(i == n_frame_tiles)
    def _():
        tile(xt_ref, ot_ref)


def _ffn(x_f, x_t, g2, w_up, w_down, layer):
    NF, D = x_f.shape
    NT = x_t.shape[0]
    FF = w_up.shape[2]
    bm, fc = FF_ROWS, FF_COLS
    nft = NF // bm
    assert NF % bm == 0 and FF % fc == 0 and NT <= bm
    fr = lambda i, k: (jnp.minimum(i, nft - 1), 0)
    tl = lambda i, k: (0, 0)
    return pl.pallas_call(
        functools.partial(_ffn_kernel, n_frame_tiles=nft),
        grid=(nft + 1, FF // fc),
        in_specs=[
            pl.BlockSpec((bm, D), fr),
            pl.BlockSpec((NT, D), tl),
            pl.BlockSpec((1, D), tl),
            pl.BlockSpec((None, D, fc), lambda i, k: (layer, 0, k)),
            pl.BlockSpec((None, fc, D), lambda i, k: (layer, k, 0)),
        ],
        out_specs=(pl.BlockSpec((bm, D), fr), pl.BlockSpec((NT, D), tl)),
        out_shape=(jax.ShapeDtypeStruct((NF, D), F32), jax.ShapeDtypeStruct((NT, D), F32)),
        scratch_shapes=[pltpu.VMEM((bm, D), BF16)],
        compiler_params=pltpu.CompilerParams(
            dimension_semantics=("arbitrary", "arbitrary"), vmem_limit_bytes=VMEM_LIMIT),
        name="ffn",
    )(x_f, x_t, g2, w_up, w_down)


def kernel(x_prompt, x_sample, cache_k, cache_v, state_C, state_n, state_m, meta_tokens, rel_bias, norm1_g, w_in, b_i, b_f, q_norm_g, k_norm_g, lambda_q1, lambda_k1, lambda_q2, lambda_k2, subln_g, mnorm_g, w_out, norm2_g, w_up, w_down):
    B, S, D = x_prompt.shape
    Bd, Td, _ = x_sample.shape
    depth = w_in.shape[0]
    Pc = cache_k.shape[2]
    assert Td == N_META and S % ATT_TILE == 0 and S % M_TILE == 0 and Pc % LANES == 0
    assert ATT_TILE % CHUNK == 0 and CHUNK & (CHUNK - 1) == 0 and ATT_TILE >= MAX_DISTANCE

    n_frames = B * S
    n_meta = B * N_META
    n_samp = Bd * Td
    assert n_meta + n_samp <= TAIL_ROWS
    srow = n_meta // N_META

    meta = jnp.broadcast_to(meta_tokens.astype(F32)[None], (B, N_META, D)).reshape(n_meta, D)
    x_f = x_prompt.astype(F32).reshape(n_frames, D)
    x_t = jnp.concatenate([meta, x_sample.astype(F32).reshape(n_samp, D),
                           jnp.zeros((TAIL_ROWS - n_meta - n_samp, D), F32)], axis=0)

    bias = _bias_tiles(rel_bias.astype(F32), Pc, Td)
    ck = cache_k.astype(F32)
    cv = cache_v.astype(F32)
    zeros_pad_att = jnp.zeros((TAIL_ROWS - n_meta - n_samp, ATT_V), F32)

    g_off = 2 * ATT_QK + ATT_V + 2 * M_QK + M_V
    w_in_b, w_out_b, w_up_b, w_down_b = (w.astype(BF16) for w in (w_in[:, :, :g_off], w_out, w_up, w_down))
    outs = {name: [] for name in ("kp", "vp", "Cp", "np", "mp", "ks", "vs", "Cs", "ns", "ms")}

    for l in range(depth):
        lam_init = 0.8 - 0.6 * math.exp(-0.3 * l)
        w_og = w_in[l, :, g_off + 2 * H_M:].astype(BF16)
        w_gate = jnp.pad(w_in[l, :, g_off:g_off + 2 * H_M], ((0, 0), (0, LANES - 2 * H_M))).astype(BF16)
        gate_bias = jnp.pad(jnp.concatenate([b_i[l], b_f[l]]).astype(F32), (0, LANES - 2 * H_M))[None]
        lam_vecs = [v[l].astype(F32)[None] for v in (lambda_q1, lambda_k1, lambda_q2, lambda_k2)]
        sg = subln_g[l].astype(F32)[None]

        z_f, z_t, g_f, g_t, k_p, v_p, k_s, v_s = _inproj(
            x_f, x_t, norm1_g[l].astype(F32)[None], w_in_b, l, w_og, w_gate, gate_bias,
            q_norm_g[l].astype(F32)[None], k_norm_g[l].astype(F32)[None], B, Bd)

        att_f = _attn_prompt(z_f, z_t, bias, lam_vecs, sg, B, S, lam_init)
        att_m = _attn_small(z_t, 0, B, bias[3], lam_vecs, sg, lam_init)
        att_s = _attn_small(z_t, srow, Bd, bias[4], lam_vecs, sg, lam_init, cache=(ck, cv, l))
        att_t = jnp.concatenate([att_m, att_s, zeros_pad_att], axis=0)

        hm_m, hm_f, C_p, n_p, m_p = _mlstm(z_t, g_t, 0, B, frames=(z_f, g_f, S))
        init = (state_C[l].astype(F32), state_n[l].astype(F32)[:, :, None, :],
                jnp.broadcast_to(state_m[l].astype(F32)[:, :, None, None], (Bd, H_M, 8, LANES)))
        hm_s, C_s, n_s, m_s = _mlstm(z_t, g_t, srow, Bd, init=init)
        hm_t = jnp.concatenate([hm_m, hm_s, zeros_pad_att], axis=0)

        x_f, x_t = _merge(x_f, x_t, att_f, att_t, hm_f, hm_t, z_f, z_t,
                          mnorm_g[l].astype(F32)[None], w_out_b, l)
        x_f, x_t = _ffn(x_f, x_t, norm2_g[l].astype(F32)[None], w_up_b, w_down_b, l)

        outs["kp"].append(k_p)
        outs["vp"].append(v_p)
        outs["Cp"].append(C_p)
        outs["np"].append(n_p[:, :, 0, :])
        outs["mp"].append(m_p[:, :, 0, 0])
        outs["ks"].append(k_s.reshape(Bd, Td, H_ATT, 2 * DQK_ATT))
        outs["vs"].append(v_s.reshape(Bd, Td, H_ATT, DV_ATT))
        outs["Cs"].append(C_s)
        outs["ns"].append(n_s[:, :, 0, :])
        outs["ms"].append(m_s[:, :, 0, 0])

    y_prompt = x_f.reshape(B, S, D)
    y_sample = x_t[n_meta:n_meta + n_samp].reshape(Bd, Td, D)
    return (y_prompt, y_sample,
            jnp.stack(outs["kp"]), jnp.stack(outs["vp"]), jnp.stack(outs["Cp"]), jnp.stack(outs["np"]),
            jnp.stack(outs["mp"]),
            jnp.stack(outs["ks"]), jnp.stack(outs["vs"]), jnp.stack(outs["Cs"]), jnp.stack(outs["ns"]),
            jnp.stack(outs["ms"]))
```

```python
import functools
import math

import jax
import jax.numpy as jnp
from jax import lax
from jax.experimental import pallas as pl
from jax.experimental.pallas import tpu as pltpu

F32 = jnp.float32
BF16 = jnp.bfloat16

CHUNK = 64
N_META = 16
H_ATT = 4
DQK_ATT = 128
DV_ATT = 256
H_M = 4
DQK_M = 128
DV_M = 256
N_BUCKETS = 32
MAX_DISTANCE = 128
EPS = 1e-6
NEG = -1e30
LOG2E = math.log2(math.e)

ATT_QK = H_ATT * 2 * DQK_ATT
ATT_V = H_ATT * DV_ATT
M_QK = H_M * DQK_M
M_V = H_M * DV_M
N_MAIN = 2 * ATT_QK + ATT_V + 2 * M_QK + M_V + M_V

LANES = 128
ATT_TILE = 256
M_TILE = 256
TAIL_ROWS = 256
PROJ_ROWS = 1024
PROJ_COLS = 512
MERGE_ROWS = 512
CACHE_CHUNK = 1024
FF_ROWS = 512
FF_COLS = 1024
VMEM_LIMIT = 56 * 1024 * 1024

_NT = (((1,), (1,)), ((), ()))


def _dot(a, b):
    return jnp.dot(a, b, preferred_element_type=F32)


def _dot_nt(a, b):
    return lax.dot_general(a, b, _NT, preferred_element_type=F32)


def _log_sigmoid(x):
    return jnp.minimum(x, 0.0) - jnp.log1p(jnp.exp(-jnp.abs(x)))


def _col(a, idx):
    lane = lax.broadcasted_iota(jnp.int32, a.shape, 1)
    return jnp.sum(jnp.where(lane == idx, a, 0.0), axis=1, keepdims=True)


def _rms(x, gain):
    ms = jnp.mean(x * x, axis=-1, keepdims=True)
    return x * lax.rsqrt(ms + EPS) * gain


def _bias_from_rel(rel, rb_ref, h):
    nb = N_BUCKETS // 2
    max_exact = nb // 2
    ret = jnp.where(rel > 0, nb, 0)
    n = jnp.abs(rel)
    nf = jnp.maximum(n, 1).astype(F32)
    large = max_exact + (jnp.log(nf / max_exact) / math.log(MAX_DISTANCE / max_exact)
                         * (nb - max_exact)).astype(jnp.int32)
    large = jnp.minimum(large, nb - 1)
    bucket = ret + jnp.where(n < max_exact, n, large)
    out = jnp.zeros(rel.shape, F32)
    for b in range(N_BUCKETS):
        out = jnp.where(bucket == b, rb_ref[b, h], out)
    return out


def _bias_kernel(rb_ref, diag_ref, prev_ref, meta_ref, mm_ref, samp_ref, *, past_len, dec_seq):
    h = pl.program_id(0)
    shift = CHUNK.bit_length() - 1
    T = ATT_TILE
    far = _bias_from_rel(jnp.full((8, LANES), -(2 * MAX_DISTANCE + 1), jnp.int32), rb_ref, h)[0:1, 0:1]

    qi = lax.broadcasted_iota(jnp.int32, (T, T), 0)
    kj = lax.broadcasted_iota(jnp.int32, (T, T), 1)
    bias = (_bias_from_rel(kj - qi, rb_ref, h) - far) * LOG2E
    diag_ref[...] = jnp.where((kj >> shift) <= (qi >> shift), bias, NEG)
    prev_ref[...] = (_bias_from_rel(kj - T - qi, rb_ref, h) - far) * LOG2E

    qi = lax.broadcasted_iota(jnp.int32, (T, LANES), 0)
    kj = lax.broadcasted_iota(jnp.int32, (T, LANES), 1)
    for var in range(2):
        bias = (_bias_from_rel(kj - (N_META + qi + var * T), rb_ref, h) - far) * LOG2E
        meta_ref[var] = jnp.where(kj < N_META, bias, NEG)

    qi = lax.broadcasted_iota(jnp.int32, (N_META, LANES), 0)
    kj = lax.broadcasted_iota(jnp.int32, (N_META, LANES), 1)
    mm_ref[...] = jnp.where(kj < N_META, _bias_from_rel(kj - qi, rb_ref, h) * LOG2E, NEG)

    width = samp_ref.shape[-1]
    qi = lax.broadcasted_iota(jnp.int32, (dec_seq, width), 0) + past_len
    kj = lax.broadcasted_iota(jnp.int32, (dec_seq, width), 1)
    bias = _bias_from_rel(kj - qi, rb_ref, h) * LOG2E
    ok = ((kj >> shift) <= (qi >> shift)) & (kj < past_len + dec_seq)
    samp_ref[...] = jnp.where(ok, bias, NEG)


def _bias_tiles(rel_bias, past_len, dec_seq):
    T = ATT_TILE
    sw = past_len + LANES
    out_shape = (
        jax.ShapeDtypeStruct((H_ATT, T, T), F32),
        jax.ShapeDtypeStruct((H_ATT, T, T), F32),
        jax.ShapeDtypeStruct((H_ATT, 2, T, LANES), F32),
        jax.ShapeDtypeStruct((H_ATT, N_META, LANES), F32),
        jax.ShapeDtypeStruct((H_ATT, dec_seq, sw), F32),
    )
    out_specs = (
        pl.BlockSpec((None, T, T), lambda h: (h, 0, 0)),
        pl.BlockSpec((None, T, T), lambda h: (h, 0, 0)),
        pl.BlockSpec((None, 2, T, LANES), lambda h: (h, 0, 0, 0)),
        pl.BlockSpec((None, N_META, LANES), lambda h: (h, 0, 0)),
        pl.BlockSpec((None, dec_seq, sw), lambda h: (h, 0, 0)),
    )
    return pl.pallas_call(
        functools.partial(_bias_kernel, past_len=past_len, dec_seq=dec_seq),
        grid=(H_ATT,),
        in_specs=[pl.BlockSpec(memory_space=pltpu.SMEM)],
        out_specs=out_specs,
        out_shape=out_shape,
        name="rel_bias_tiles",
    )(rel_bias)


def _kv_copies(stage, sem, dst_prompt, dst_sample, i, *, tail, n_batch, frames_per_batch, n_sample_rows):
    if not tail:
        rows = stage.shape[0]
        tiles_per_batch = frames_per_batch // rows
        b = i // tiles_per_batch
        t = i - b * tiles_per_batch
        return [pltpu.make_async_copy(stage, dst_prompt.at[b, pl.ds(N_META + t * rows, rows)], sem.at[0])]
    copies = [pltpu.make_async_copy(stage.at[pl.ds(b * N_META, N_META)], dst_prompt.at[b, pl.ds(0, N_META)],
                                    sem.at[b]) for b in range(n_batch)]
    copies.append(pltpu.make_async_copy(stage.at[pl.ds(n_batch * N_META, n_sample_rows)], dst_sample,
                                        sem.at[n_batch]))
    return copies


def _inproj_kernel(xf_ref, xt_ref, g1_ref, wa_ref, wb_ref, wg_ref, gb_ref, qg_ref, kg_ref,
                   zf_ref, zt_ref, gf_ref, gt_ref, kp_ref, vp_ref, ks_ref, vs_ref,
                   xn_s, y2d, stage, sem, *, n_frame_tiles, n_batch, frames_per_batch, n_sample_rows):
    i = pl.program_id(0)
    j = pl.program_id(1)
    bn = wa_ref.shape[0]
    assert ATT_QK % bn == 0 and ATT_V % bn == 0 and M_QK % bn == 0 and M_V % bn == 0 and bn % DV_ATT == 0
    j_k = ATT_QK // bn
    j_v = j_k + ATT_QK // bn
    j_mq = j_v + ATT_V // bn
    j_mk = j_mq + M_QK // bn
    j_mv = j_mk + M_QK // bn
    j_og = j_mv + M_V // bn

    def tile(x_ref, z_ref, g_ref, tail):
        nrows = x_ref.shape[0]
        st = stage.at[pl.ds(0, nrows)]
        copies = functools.partial(_kv_copies, st, sem, i=i, tail=tail, n_batch=n_batch,
                                   frames_per_batch=frames_per_batch, n_sample_rows=n_sample_rows)

        @pl.when(j == 0)
        def _():
            xb = _rms(x_ref[...], g1_ref[...]).astype(BF16)
            xn_s[0:nrows, :] = xb
            gates = _dot(xb, wg_ref[...]) + gb_ref[...]
            lane = lax.broadcasted_iota(jnp.int32, gates.shape, 1)
            g_ref[...] = jnp.where(lane < H_M, gates, _log_sigmoid(gates))

        acc = _dot_nt(xn_s[0:nrows, :], jnp.where(j < j_og, wa_ref[...], wb_ref[...]))

        def group_norm(gain_ref, scale):
            parts = []
            for gi in range(bn // DQK_ATT):
                y = _rms(acc[:, gi * DQK_ATT:(gi + 1) * DQK_ATT], gain_ref[...])
                parts.append(y if scale is None else y * scale)
            return jnp.concatenate(parts, axis=1)

        def stage_heads(y, jj, j0, dst_prompt, dst_sample):
            c0 = (jj - j0) * bn
            y2d[0:nrows, c0:c0 + bn] = y
            if jj == j0 + ATT_QK // bn - 1:
                if j0 != j_k:
                    for c in copies(kp_ref, ks_ref):
                        c.wait()
                st[...] = y2d[0:nrows, :].reshape(nrows, H_ATT, DV_ATT)
                for c in copies(dst_prompt, dst_sample):
                    c.start()

        @pl.when(j < j_k)
        def _():
            z_ref[...] = group_norm(qg_ref, DQK_ATT ** -0.5 * LOG2E)

        for jj in range(j_k, j_v):
            @pl.when(j == jj)
            def _(jj=jj):
                y = group_norm(kg_ref, None)
                z_ref[...] = y
                stage_heads(y, jj, j_k, kp_ref, ks_ref)

        for jj in range(j_v, j_mq):
            @pl.when(j == jj)
            def _(jj=jj):
                z_ref[...] = acc
                stage_heads(acc, jj, j_v, vp_ref, vs_ref)

        @pl.when(j == j_mq)
        def _():
            for c in copies(vp_ref, vs_ref):
                c.wait()

        @pl.when(((j >= j_mq) & (j < j_mk)) | ((j >= j_mv) & (j < j_og)))
        def _():
            z_ref[...] = acc

        @pl.when((j >= j_mk) & (j < j_mv))
        def _():
            z_ref[...] = acc * (DQK_M ** -0.5)

        @pl.when(j >= j_og)
        def _():
            z_ref[...] = jax.nn.sigmoid(acc)

    @pl.when(i < n_frame_tiles)
    def _():
        tile(xf_ref, zf_ref, gf_ref, False)

    @pl.when(i == n_frame_tiles)
    def _():
        tile(xt_ref, zt_ref, gt_ref, True)


def _inproj(x_f, x_t, g1, w_all, layer, w_og, w_gate, gate_bias, qg, kg, n_batch, n_sample_streams):
    NF, D = x_f.shape
    NT = x_t.shape[0]
    bm, bn = PROJ_ROWS, PROJ_COLS
    nft = NF // bm
    nj = N_MAIN // bn
    nja = (N_MAIN - M_V) // bn
    S = NF // n_batch
    L = N_META + S
    n_sample_rows = n_sample_streams * N_META
    assert NF % bm == 0 and S % bm == 0 and N_MAIN % bn == 0 and NT <= bm
    fr = lambda i: jnp.minimum(i, nft - 1)
    any_spec = pl.BlockSpec(memory_space=pl.ANY)
    return pl.pallas_call(
        functools.partial(_inproj_kernel, n_frame_tiles=nft, n_batch=n_batch, frames_per_batch=S,
                          n_sample_rows=n_sample_rows),
        grid=(nft + 1, nj),
        in_specs=[
            pl.BlockSpec((bm, D), lambda i, j: (fr(i), 0)),
            pl.BlockSpec((NT, D), lambda i, j: (0, 0)),
            pl.BlockSpec((1, D), lambda i, j: (0, 0)),
            pl.BlockSpec((None, bn, D), lambda i, j: (layer, jnp.minimum(j, nja - 1), 0)),
            pl.BlockSpec((bn, D), lambda i, j: (jnp.maximum(j - nja, 0), 0)),
            pl.BlockSpec((D, LANES), lambda i, j: (0, 0)),
            pl.BlockSpec((1, LANES), lambda i, j: (0, 0)),
            pl.BlockSpec((1, DQK_ATT), lambda i, j: (0, 0)),
            pl.BlockSpec((1, DQK_ATT), lambda i, j: (0, 0)),
        ],
        out_specs=(
            pl.BlockSpec((bm, bn), lambda i, j: (fr(i), jnp.where(i < nft, j, nj - 1))),
            pl.BlockSpec((NT, bn), lambda i, j: (0, jnp.where(i < nft, 0, j))),
            pl.BlockSpec((bm, LANES), lambda i, j: (fr(i), 0)),
            pl.BlockSpec((NT, LANES), lambda i, j: (0, 0)),
            any_spec, any_spec, any_spec, any_spec,
        ),
        out_shape=(
            jax.ShapeDtypeStruct((NF, N_MAIN), F32),
            jax.ShapeDtypeStruct((NT, N_MAIN), F32),
            jax.ShapeDtypeStruct((NF, LANES), F32),
            jax.ShapeDtypeStruct((NT, LANES), F32),
            jax.ShapeDtypeStruct((n_batch, L, H_ATT, 2 * DQK_ATT), F32),
            jax.ShapeDtypeStruct((n_batch, L, H_ATT, DV_ATT), F32),
            jax.ShapeDtypeStruct((n_sample_rows, H_ATT, 2 * DQK_ATT), F32),
            jax.ShapeDtypeStruct((n_sample_rows, H_ATT, DV_ATT), F32),
        ),
        scratch_shapes=[
            pltpu.VMEM((bm, D), BF16),
            pltpu.VMEM((bm, ATT_QK), F32),
            pltpu.VMEM((bm, H_ATT, DV_ATT), F32),
            pltpu.SemaphoreType.DMA((n_batch + 1,)),
        ],
        compiler_params=pltpu.CompilerParams(
            dimension_semantics=("arbitrary", "arbitrary"), vmem_limit_bytes=VMEM_LIMIT),
        name="in_proj",
    )(x_f, x_t, g1, w_all, w_og, w_gate, gate_bias, qg, kg)


def _diff_lambda(lq1_ref, lk1_ref, lq2_ref, lk2_ref, lam_init):
    a = jnp.sum(lq1_ref[...] * lk1_ref[...], axis=-1, keepdims=True)
    b = jnp.sum(lq2_ref[...] * lk2_ref[...], axis=-1, keepdims=True)
    return jnp.exp(a) - jnp.exp(b) + lam_init


def _subln(o, sg_ref, lam_init):
    return _rms(o, sg_ref[...]) * (1.0 - lam_init)


def _pad_rows(dst, src):
    dst[...] = jnp.zeros(dst.shape, dst.dtype)
    dst[0:src.shape[0], :] = src[...].astype(dst.dtype)


def _attn_prompt_kernel(q_ref, k_ref, v_ref, km_ref, vm_ref, bd_ref, bp_ref, bm_ref,
                        lq1_ref, lk1_ref, lq2_ref, lk2_ref, sg_ref, o_ref,
                        kb_s, vb_s, kmb_s, vmb_s, *, lam_init):
    T = ATT_TILE
    n_tiles = q_ref.shape[0] // T
    kb_s[...] = k_ref[...].astype(BF16)
    vb_s[...] = v_ref[...].astype(BF16)
    _pad_rows(kmb_s, km_ref)
    _pad_rows(vmb_s, vm_ref)
    lam = _diff_lambda(lq1_ref, lk1_ref, lq2_ref, lk2_ref, lam_init)

    for j in range(n_tiles):
        rows = slice(j * T, (j + 1) * T)
        regions = [(kmb_s, vmb_s, slice(0, LANES), bm_ref[min(j, 1)])]
        if j >= 2:
            regions.append((kb_s, vb_s, slice(0, (j - 1) * T), None))
        if j >= 1:
            regions.append((kb_s, vb_s, slice((j - 1) * T, j * T), bp_ref[...]))
        regions.append((kb_s, vb_s, rows, bd_ref[...]))

        probs, denom = [], []
        for c in range(2):
            cols = slice(c * DQK_ATT, (c + 1) * DQK_ATT)
            qc = q_ref[rows, cols].astype(BF16)
            ss = [_dot_nt(qc, kr[rs, cols]) for kr, _, rs, _ in regions]
            ss = [s if br is None else s + br for s, (_, _, _, br) in zip(ss, regions)]
            m = functools.reduce(jnp.maximum, [jnp.max(s, axis=-1, keepdims=True) for s in ss])
            pc = [jnp.exp2(s - m) for s in ss]
            denom.append(functools.reduce(jnp.add, [jnp.sum(p, axis=-1, keepdims=True) for p in pc]))
            probs.append(pc)
        a0 = 1.0 / denom[0]
        a1 = lam / denom[1]
        o = None
        for r, (_, vr, rs, _) in enumerate(regions):
            w = (probs[0][r] * a0 - probs[1][r] * a1).astype(BF16)
            t = _dot(w, vr[rs, :])
            o = t if o is None else o + t
        o_ref[rows, :] = _subln(o, sg_ref, lam_init)


def _attn_prompt(z_f, z_t, bias, lam_vecs, sg, B, S, lam_init):
    T = ATT_TILE
    qkw = 2 * DQK_ATT
    kcol = ATT_QK // qkw
    vcol = 2 * ATT_QK // DV_ATT
    b_diag, b_prev, b_meta, _, _ = bias
    vec = pl.BlockSpec((1, DQK_ATT), lambda b, h: (0, 0))
    return pl.pallas_call(
        functools.partial(_attn_prompt_kernel, lam_init=lam_init),
        grid=(B, H_ATT),
        in_specs=[
            pl.BlockSpec((S, qkw), lambda b, h: (b, h)),
            pl.BlockSpec((S, qkw), lambda b, h: (b, kcol + h)),
            pl.BlockSpec((S, DV_ATT), lambda b, h: (b, vcol + h)),
            pl.BlockSpec((N_META, qkw), lambda b, h: (b, kcol + h)),
            pl.BlockSpec((N_META, DV_ATT), lambda b, h: (b, vcol + h)),
            pl.BlockSpec((None, T, T), lambda b, h: (h, 0, 0)),
            pl.BlockSpec((None, T, T), lambda b, h: (h, 0, 0)),
            pl.BlockSpec((None, 2, T, LANES), lambda b, h: (h, 0, 0, 0)),
            vec, vec, vec, vec,
            pl.BlockSpec((1, DV_ATT), lambda b, h: (0, 0)),
        ],
        out_specs=pl.BlockSpec((S, DV_ATT), lambda b, h: (b, h)),
        out_shape=jax.ShapeDtypeStruct((B * S, ATT_V), F32),
        scratch_shapes=[
            pltpu.VMEM((S, qkw), BF16),
            pltpu.VMEM((S, DV_ATT), BF16),
            pltpu.VMEM((LANES, qkw), BF16),
            pltpu.VMEM((LANES, DV_ATT), BF16),
        ],
        compiler_params=pltpu.CompilerParams(
            dimension_semantics=("arbitrary", "arbitrary"), vmem_limit_bytes=VMEM_LIMIT),
        name="attn_prompt",
    )(z_f, z_f, z_f, z_t, z_t, b_diag, b_prev, b_meta, *lam_vecs, sg)


def _attn_small_kernel(*refs, lam_init, has_cache):
    if has_cache:
        (q_ref, kc_ref, vc_ref, kn_ref, vn_ref, b_ref,
         lq1_ref, lk1_ref, lq2_ref, lk2_ref, sg_ref, o_ref, kpad, vpad, kb_s, vb_s) = refs
        past_len = kb_s.shape[0]
        chunk = kc_ref.shape[0]
        c = pl.program_id(1)
        r0 = pl.multiple_of(c * chunk, chunk)
        kb_s[pl.ds(r0, chunk), :] = kc_ref[...].reshape(chunk, ATT_QK).astype(BF16)
        vb_s[pl.ds(r0, chunk), :] = vc_ref[...].reshape(chunk, ATT_V).astype(BF16)

        @pl.when(c == past_len // chunk - 1)
        def _():
            _attn_small_body(q_ref, kn_ref, vn_ref, b_ref, lq1_ref, lk1_ref, lq2_ref, lk2_ref, sg_ref, o_ref,
                             kpad, vpad, kb_s, vb_s, lam_init=lam_init, past_len=past_len)
    else:
        (q_ref, kn_ref, vn_ref, b_ref,
         lq1_ref, lk1_ref, lq2_ref, lk2_ref, sg_ref, o_ref, kpad, vpad) = refs
        _attn_small_body(q_ref, kn_ref, vn_ref, b_ref, lq1_ref, lk1_ref, lq2_ref, lk2_ref, sg_ref, o_ref,
                         kpad, vpad, None, None, lam_init=lam_init, past_len=0)


def _attn_small_body(q_ref, kn_ref, vn_ref, b_ref, lq1_ref, lk1_ref, lq2_ref, lk2_ref, sg_ref, o_ref,
                     kpad, vpad, kb_s, vb_s, *, lam_init, past_len):
    has_cache = past_len > 0
    lam = _diff_lambda(lq1_ref, lk1_ref, lq2_ref, lk2_ref, lam_init)
    _pad_rows(kpad, kn_ref)
    _pad_rows(vpad, vn_ref)
    for h in range(H_ATT):
        vsl = slice(h * DV_ATT, (h + 1) * DV_ATT)
        outs = []
        for c in range(2):
            sl = slice(h * 2 * DQK_ATT + c * DQK_ATT, h * 2 * DQK_ATT + (c + 1) * DQK_ATT)
            qc = q_ref[:, sl].astype(BF16)
            s_new = _dot_nt(qc, kpad[:, sl]) + b_ref[h, :, past_len:]
            m = jnp.max(s_new, axis=-1, keepdims=True)
            if has_cache:
                s_old = _dot_nt(qc, kb_s[:, sl]) + b_ref[h, :, :past_len]
                m = jnp.maximum(m, jnp.max(s_old, axis=-1, keepdims=True))
            p_new = jnp.exp2(s_new - m)
            l = jnp.sum(p_new, axis=-1, keepdims=True)
            o = _dot(p_new.astype(BF16), vpad[:, vsl])
            if has_cache:
                p_old = jnp.exp2(s_old - m)
                l = l + jnp.sum(p_old, axis=-1, keepdims=True)
                o = o + _dot(p_old.astype(BF16), vb_s[:, vsl])
            outs.append(o / l)
        o_ref[:, vsl] = _subln(outs[0] - lam * outs[1], sg_ref, lam_init)


def _attn_small(z_t, row0, n_streams, bias, lam_vecs, sg, lam_init, cache=None):
    vec = pl.BlockSpec((1, DQK_ATT), lambda b, c: (0, 0))
    in_specs = [pl.BlockSpec((N_META, ATT_QK), lambda b, c: (row0 + b, 0))]
    args = [z_t]
    scratch = [pltpu.VMEM((LANES, ATT_QK), BF16), pltpu.VMEM((LANES, ATT_V), BF16)]
    n_chunks = 1
    if cache is not None:
        ck, cv, layer = cache
        past_len = ck.shape[2]
        chunk = min(past_len, CACHE_CHUNK)
        assert past_len % chunk == 0
        n_chunks = past_len // chunk
        in_specs += [
            pl.BlockSpec((None, None, chunk, H_ATT, 2 * DQK_ATT), lambda b, c: (layer, b, c, 0, 0)),
            pl.BlockSpec((None, None, chunk, H_ATT, DV_ATT), lambda b, c: (layer, b, c, 0, 0))]
        args += [ck, cv]
        scratch += [pltpu.VMEM((past_len, ATT_QK), BF16), pltpu.VMEM((past_len, ATT_V), BF16)]
    in_specs += [
        pl.BlockSpec((N_META, ATT_QK), lambda b, c: (row0 + b, 1)),
        pl.BlockSpec((N_META, ATT_V), lambda b, c: (row0 + b, 2 * ATT_QK // ATT_V)),
        pl.BlockSpec(bias.shape, lambda b, c: (0, 0, 0)),
        vec, vec, vec, vec,
        pl.BlockSpec((1, DV_ATT), lambda b, c: (0, 0)),
    ]
    args += [z_t, z_t, bias, *lam_vecs, sg]
    return pl.pallas_call(
        functools.partial(_attn_small_kernel, lam_init=lam_init, has_cache=cache is not None),
        grid=(n_streams, n_chunks),
        in_specs=in_specs,
        out_specs=pl.BlockSpec((N_META, ATT_V), lambda b, c: (b, 0)),
        out_shape=jax.ShapeDtypeStruct((n_streams * N_META, ATT_V), F32),
        scratch_shapes=scratch,
        compiler_params=pltpu.CompilerParams(
            dimension_semantics=("arbitrary", "arbitrary"), vmem_limit_bytes=VMEM_LIMIT),
        name="attn_cached" if cache is not None else "attn_meta",
    )(*args)


def _mlstm_kernel(*refs, n_blocks, has_init):
    refs = list(refs)
    qh_ref, kh_ref, vh_ref, gh_ref = refs[:4]
    pos = 4
    if n_blocks:
        qf_ref, kf_ref, vf_ref, gf_ref = refs[pos:pos + 4]
        pos += 4
    if has_init:
        c0_ref, n0_ref, m0_ref = refs[pos:pos + 3]
        pos += 3
    hh_ref = refs[pos]
    pos += 1
    if n_blocks:
        hf_ref = refs[pos]
        pos += 1
    c_ref, n_ref, m_ref = refs[pos:pos + 3]
    qs, ks, vs, gs = refs[pos + 3:pos + 7]

    t = pl.program_id(1)
    T = M_TILE
    block = functools.partial(_mlstm_block, c_ref=c_ref, n_ref=n_ref, m_ref=m_ref)

    @pl.when(t == 0)
    def _():
        if has_init:
            c_ref[...] = c0_ref[...]
            n_ref[...] = n0_ref[...]
            m_ref[...] = m0_ref[...]
        else:
            c_ref[...] = jnp.zeros(c_ref.shape, F32)
            n_ref[...] = jnp.zeros(n_ref.shape, F32)
            m_ref[...] = jnp.zeros(m_ref.shape, F32)
        nh = qh_ref.shape[0]
        for dst, src in ((qs, qh_ref), (ks, kh_ref), (vs, vh_ref)):
            dst[...] = jnp.zeros(dst.shape, F32)
            dst[0:nh, :] = src[...]
        lane = lax.broadcasted_iota(jnp.int32, gs.shape, 1)
        gs[...] = jnp.where(lane < H_M, NEG, 0.0)
        gs[0:nh, :] = gh_ref[...]

        block(qs, ks, vs, gs, hh_ref, nh)

    if n_blocks:
        @pl.when(t > 0)
        def _():
            block(qf_ref, kf_ref, vf_ref, gf_ref, hf_ref, T)


def _mlstm_block(q_src, k_src, v_src, g_src, h_dst, n_out, c_ref, n_ref, m_ref):
    T = M_TILE
    g = g_src[...]
    row = lax.broadcasted_iota(jnp.int32, (T, T), 0)
    colid = lax.broadcasted_iota(jnp.int32, (T, T), 1)
    causal = row >= colid
    tril = jnp.where(causal, 1.0, 0.0).astype(F32)
    cum = jnp.dot(tril, g, preferred_element_type=F32, precision=lax.Precision.HIGHEST)
    g_t = g.T
    cum_t = cum.T
    last = cum[T - 1:T, :]

    for h in range(H_M):
        q = q_src[:, h * DQK_M:(h + 1) * DQK_M]
        k = k_src[:, h * DQK_M:(h + 1) * DQK_M]
        v = v_src[:, h * DV_M:(h + 1) * DV_M]
        qb = q.astype(BF16)
        kb = k.astype(BF16)
        m_old = m_ref[h, 0:1, 0:1]
        c_old = c_ref[h]
        n_old = n_ref[h]
        bh_c = _col(cum, H_M + h)
        ih_c = _col(g, h)
        bh_r = cum_t[H_M + h:H_M + h + 1, :]
        ih_r = g_t[h:h + 1, :]
        d = jnp.where(causal, bh_c - bh_r + ih_r, NEG)
        inter = bh_c + m_old
        mt = jnp.maximum(inter, jnp.max(d, axis=-1, keepdims=True))
        a = jnp.exp(d - mt) * _dot_nt(qb, kb)
        iw = jnp.exp(inter - mt)
        num = _dot(a.astype(BF16), v.astype(BF16)) + iw * _dot_nt(qb, c_old.astype(BF16))
        den = jnp.sum(a, axis=-1, keepdims=True) + iw * jnp.sum(q * n_old, axis=-1, keepdims=True)
        den = jnp.maximum(jnp.abs(den), jnp.exp(-mt))
        hout = num / den
        h_dst[:, h * DV_M:(h + 1) * DV_M] = hout[0:n_out, :]

        bl = _col(last, H_M + h)
        g_c = bl - bh_c + ih_c
        m_new = jnp.maximum(bl + m_old, jnp.max(g_c, axis=0, keepdims=True))
        sw = jnp.exp(g_c - m_new)
        dec = jnp.exp(bl + m_old - m_new)
        c_ref[h] = dec * c_old + _dot(v.T.astype(BF16), (sw * k).astype(BF16))
        n_ref[h] = dec * n_old + jnp.sum(sw * k, axis=0, keepdims=True)
        m_ref[h] = jnp.broadcast_to(m_new, m_ref.shape[1:])


def _mlstm(z_t, g_t, row0, n_streams, frames=None, init=None):
    T = M_TILE
    qcol = (2 * ATT_QK + ATT_V) // M_QK
    vcol = (2 * ATT_QK + ATT_V + 2 * M_QK) // M_V
    nh = N_META
    in_specs = [
        pl.BlockSpec((nh, M_QK), lambda b, t: (row0 + b, qcol)),
        pl.BlockSpec((nh, M_QK), lambda b, t: (row0 + b, qcol + 1)),
        pl.BlockSpec((nh, M_V), lambda b, t: (row0 + b, vcol)),
        pl.BlockSpec((nh, LANES), lambda b, t: (row0 + b, 0)),
    ]
    args = [z_t, z_t, z_t, g_t]
    n_blocks = 0
    if frames is not None:
        z_f, g_f, fps = frames
        n_blocks = fps // T
        fidx = lambda b, t: b * n_blocks + jnp.maximum(t - 1, 0)
        in_specs += [
            pl.BlockSpec((T, M_QK), lambda b, t: (fidx(b, t), qcol)),
            pl.BlockSpec((T, M_QK), lambda b, t: (fidx(b, t), qcol + 1)),
            pl.BlockSpec((T, M_V), lambda b, t: (fidx(b, t), vcol)),
            pl.BlockSpec((T, LANES), lambda b, t: (fidx(b, t), 0)),
        ]
        args += [z_f, z_f, z_f, g_f]
    state_specs = (
        pl.BlockSpec((None, H_M, DV_M, DQK_M), lambda b, t: (b, 0, 0, 0)),
        pl.BlockSpec((None, H_M, 1, DQK_M), lambda b, t: (b, 0, 0, 0)),
        pl.BlockSpec((None, H_M, 8, LANES), lambda b, t: (b, 0, 0, 0)),
    )
    if init is not None:
        in_specs += list(state_specs)
        args += list(init)
    out_specs = [pl.BlockSpec((nh, M_V), lambda b, t: (b, 0))]
    out_shape = [jax.ShapeDtypeStruct((n_streams * nh, M_V), F32)]
    if n_blocks:
        out_specs.append(pl.BlockSpec((T, M_V), lambda b, t: (fidx(b, t), 0)))
        out_shape.append(jax.ShapeDtypeStruct((n_streams * fps, M_V), F32))
    out_specs += list(state_specs)
    out_shape += [
        jax.ShapeDtypeStruct((n_streams, H_M, DV_M, DQK_M), F32),
        jax.ShapeDtypeStruct((n_streams, H_M, 1, DQK_M), F32),
        jax.ShapeDtypeStruct((n_streams, H_M, 8, LANES), F32),
    ]
    return pl.pallas_call(
        functools.partial(_mlstm_kernel, n_blocks=n_blocks, has_init=init is not None),
        grid=(n_streams, 1 + n_blocks),
        in_specs=in_specs,
        out_specs=tuple(out_specs),
        out_shape=tuple(out_shape),
        scratch_shapes=[
            pltpu.VMEM((T, M_QK), F32), pltpu.VMEM((T, M_QK), F32),
            pltpu.VMEM((T, M_V), F32), pltpu.VMEM((T, LANES), F32),
        ],
        compiler_params=pltpu.CompilerParams(
            dimension_semantics=("arbitrary", "arbitrary"), vmem_limit_bytes=VMEM_LIMIT),
        name="mlstm_prompt" if n_blocks else "mlstm_cached",
    )(*args)


def _merge_kernel(xf_ref, xt_ref, attf_ref, attt_ref, hmf_ref, hmt_ref, zf_ref, zt_ref, mg_ref, wo_ref,
                  of_ref, ot_ref, *, n_frame_tiles):
    i = pl.program_id(0)

    def tile(x_ref, att_ref, hm_ref, og_ref, o_ref):
        parts = []
        for h in range(H_M):
            sl = slice(h * DV_M, (h + 1) * DV_M)
            parts.append(_rms(hm_ref[:, sl], mg_ref[:, sl]) * og_ref[:, sl])
        hmn = jnp.concatenate(parts, axis=1).astype(BF16)
        o_ref[...] = (x_ref[...] + _dot(att_ref[...].astype(BF16), wo_ref[0:ATT_V, :])
                      + _dot(hmn, wo_ref[ATT_V:, :]))

    @pl.when(i < n_frame_tiles)
    def _():
        tile(xf_ref, attf_ref, hmf_ref, zf_ref, of_ref)

    @pl.when(i == n_frame_tiles)
    def _():
        tile(xt_ref, attt_ref, hmt_ref, zt_ref, ot_ref)


def _merge(x_f, x_t, att_f, att_t, hm_f, hm_t, z_f, z_t, mg, w_out, layer):
    NF, D = x_f.shape
    NT = x_t.shape[0]
    bm = MERGE_ROWS
    nft = NF // bm
    assert NF % bm == 0
    ogcol = (N_MAIN - M_V) // M_V
    fr = lambda i: (jnp.minimum(i, nft - 1), 0)
    tl = lambda i: (0, 0)
    once = dict(pipeline_mode=pl.Buffered(1))
    return pl.pallas_call(
        functools.partial(_merge_kernel, n_frame_tiles=nft),
        grid=(nft + 1,),
        in_specs=[
            pl.BlockSpec((bm, D), fr), pl.BlockSpec((NT, D), tl, **once),
            pl.BlockSpec((bm, ATT_V), fr), pl.BlockSpec((NT, ATT_V), tl, **once),
            pl.BlockSpec((bm, M_V), fr), pl.BlockSpec((NT, M_V), tl, **once),
            pl.BlockSpec((bm, M_V), lambda i: (jnp.minimum(i, nft - 1), ogcol)),
            pl.BlockSpec((NT, M_V), lambda i: (0, ogcol), **once),
            pl.BlockSpec((1, M_V), tl),
            pl.BlockSpec((None, D, D), lambda i: (layer, 0, 0), **once),
        ],
        out_specs=(pl.BlockSpec((bm, D), fr), pl.BlockSpec((NT, D), tl)),
        out_shape=(jax.ShapeDtypeStruct((NF, D), F32), jax.ShapeDtypeStruct((NT, D), F32)),
        compiler_params=pltpu.CompilerParams(
            dimension_semantics=("arbitrary",), vmem_limit_bytes=VMEM_LIMIT),
        name="merge_out_proj",
    )(x_f, x_t, att_f, att_t, hm_f, hm_t, z_f, z_t, mg, w_out)


def _ffn_kernel(xf_ref, xt_ref, g2_ref, wu_ref, wd_ref, of_ref, ot_ref, xn_s, *, n_frame_tiles):
    i = pl.program_id(0)
    k = pl.program_id(1)

    def tile(x_ref, o_ref):
        nrows = x_ref.shape[0]

        @pl.when(k == 0)
        def _():
            x = x_ref[...]
            xn_s[0:nrows, :] = _rms(x, g2_ref[...]).astype(BF16)
            o_ref[...] = x

        u = jnp.maximum(_dot(xn_s[0:nrows, :], wu_ref[...]), 0.0)
        o_ref[...] += _dot((u * u).astype(BF16), wd_ref[...])

    @pl.when(i < n_frame_tiles)
    def _():
        tile(xf_ref, of_ref)

    @pl.when(i == n_frame_tiles)
    def _():
        tile(xt_ref, ot_ref)


def _ffn(x_f, x_t, g2, w_up, w_down, layer):
    NF, D = x_f.shape
    NT = x_t.shape[0]
    FF = w_up.shape[2]
    bm, fc = FF_ROWS, FF_COLS
    nft = NF // bm
    assert NF % bm == 0 and FF % fc == 0 and NT <= bm
    fr = lambda i, k: (jnp.minimum(i, nft - 1), 0)
    tl = lambda i, k: (0, 0)
    return pl.pallas_call(
        functools.partial(_ffn_kernel, n_frame_tiles=nft),
        grid=(nft + 1, FF // fc),
        in_specs=[
            pl.BlockSpec((bm, D), fr),
            pl.BlockSpec((NT, D), tl),
            pl.BlockSpec((1, D), tl),
            pl.BlockSpec((None, D, fc), lambda i, k: (layer, 0, k)),
            pl.BlockSpec((None, fc, D), lambda i, k: (layer, k, 0)),
        ],
        out_specs=(pl.BlockSpec((bm, D), fr), pl.BlockSpec((NT, D), tl)),
        out_shape=(jax.ShapeDtypeStruct((NF, D), F32), jax.ShapeDtypeStruct((NT, D), F32)),
        scratch_shapes=[pltpu.VMEM((bm, D), BF16)],
        compiler_params=pltpu.CompilerParams(
            dimension_semantics=("arbitrary", "arbitrary"), vmem_limit_bytes=VMEM_LIMIT),
        name="ffn",
    )(x_f, x_t, g2, w_up, w_down)


def kernel(x_prompt, x_sample, cache_k, cache_v, state_C, state_n, state_m, meta_tokens, rel_bias, norm1_g, w_in, b_i, b_f, q_norm_g, k_norm_g, lambda_q1, lambda_k1, lambda_q2, lambda_k2, subln_g, mnorm_g, w_out, norm2_g, w_up, w_down):
    B, S, D = x_prompt.shape
    Bd, Td, _ = x_sample.shape
    depth = w_in.shape[0]
    Pc = cache_k.shape[2]
    assert Td == N_META and S % ATT_TILE == 0 and S % M_TILE == 0 and Pc % LANES == 0
    assert ATT_TILE % CHUNK == 0 and CHUNK & (CHUNK - 1) == 0 and ATT_TILE >= MAX_DISTANCE

    n_frames = B * S
    n_meta = B * N_META
    n_samp = Bd * Td
    assert n_meta + n_samp <= TAIL_ROWS
    srow = n_meta // N_META

    meta = jnp.broadcast_to(meta_tokens.astype(F32)[None], (B, N_META, D)).reshape(n_meta, D)
    x_f = x_prompt.astype(F32).reshape(n_frames, D)
    x_t = jnp.concatenate([meta, x_sample.astype(F32).reshape(n_samp, D),
                           jnp.zeros((TAIL_ROWS - n_meta - n_samp, D), F32)], axis=0)

    bias = _bias_tiles(rel_bias.astype(F32), Pc, Td)
    ck = cache_k.astype(F32)
    cv = cache_v.astype(F32)
    zeros_pad_att = jnp.zeros((TAIL_ROWS - n_meta - n_samp, ATT_V), F32)

    g_off = 2 * ATT_QK + ATT_V + 2 * M_QK + M_V
    w_in_t = jnp.swapaxes(w_in, 1, 2)
    w_in_b, w_out_b, w_up_b, w_down_b = (w.astype(BF16) for w in (w_in_t, w_out, w_up, w_down))
    outs = {name: [] for name in ("kp", "vp", "Cp", "np", "mp", "ks", "vs", "Cs", "ns", "ms")}

    for l in range(depth):
        lam_init = 0.8 - 0.6 * math.exp(-0.3 * l)
        w_og = w_in_t[l, g_off + 2 * H_M:, :].astype(BF16)
        w_gate = jnp.pad(w_in[l, :, g_off:g_off + 2 * H_M], ((0, 0), (0, LANES - 2 * H_M))).astype(BF16)
        gate_bias = jnp.pad(jnp.concatenate([b_i[l], b_f[l]]).astype(F32), (0, LANES - 2 * H_M))[None]
        lam_vecs = [v[l].astype(F32)[None] for v in (lambda_q1, lambda_k1, lambda_q2, lambda_k2)]
        sg = subln_g[l].astype(F32)[None]

        z_f, z_t, g_f, g_t, k_p, v_p, k_s, v_s = _inproj(
            x_f, x_t, norm1_g[l].astype(F32)[None], w_in_b, l, w_og, w_gate, gate_bias,
            q_norm_g[l].astype(F32)[None], k_norm_g[l].astype(F32)[None], B, Bd)

        att_f = _attn_prompt(z_f, z_t, bias, lam_vecs, sg, B, S, lam_init)
        att_m = _attn_small(z_t, 0, B, bias[3], lam_vecs, sg, lam_init)
        att_s = _attn_small(z_t, srow, Bd, bias[4], lam_vecs, sg, lam_init, cache=(ck, cv, l))
        att_t = jnp.concatenate([att_m, att_s, zeros_pad_att], axis=0)

        hm_m, hm_f, C_p, n_p, m_p = _mlstm(z_t, g_t, 0, B, frames=(z_f, g_f, S))
        init = (state_C[l].astype(F32), state_n[l].astype(F32)[:, :, None, :],
                jnp.broadcast_to(state_m[l].astype(F32)[:, :, None, None], (Bd, H_M, 8, LANES)))
        hm_s, C_s, n_s, m_s = _mlstm(z_t, g_t, srow, Bd, init=init)
        hm_t = jnp.concatenate([hm_m, hm_s, zeros_pad_att], axis=0)

        x_f, x_t = _merge(x_f, x_t, att_f, att_t, hm_f, hm_t, z_f, z_t,
                          mnorm_g[l].astype(F32)[None], w_out_b, l)
        x_f, x_t = _ffn(x_f, x_t, norm2_g[l].astype(F32)[None], w_up_b, w_down_b, l)

        outs["kp"].append(k_p)
        outs["vp"].append(v_p)
        outs["Cp"].append(C_p)
        outs["np"].append(n_p[:, :, 0, :])
        outs["mp"].append(m_p[:, :, 0, 0])
        outs["ks"].append(k_s.reshape(Bd, Td, H_ATT, 2 * DQK_ATT))
        outs["vs"].append(v_s.reshape(Bd, Td, H_ATT, DV_ATT))
        outs["Cs"].append(C_s)
        outs["ns"].append(n_s[:, :, 0, :])
        outs["ms"].append(m_s[:, :, 0, 0])

    y_prompt = x_f.reshape(B, S, D)
    y_sample = x_t[n_meta:n_meta + n_samp].reshape(Bd, Td, D)
    return (y_prompt, y_sample,
            jnp.stack(outs["kp"]), jnp.stack(outs["vp"]), jnp.stack(outs["Cp"]), jnp.stack(outs["np"]),
            jnp.stack(outs["mp"]),
            jnp.stack(outs["ks"]), jnp.stack(outs["vs"]), jnp.stack(outs["Cs"]), jnp.stack(outs["ns"]),
            jnp.stack(outs["ms"]))
```

```python
import functools
import math

import jax
import jax.numpy as jnp
from jax import lax
from jax.experimental import pallas as pl
from jax.experimental.pallas import tpu as pltpu

F32 = jnp.float32
BF16 = jnp.bfloat16

CHUNK = 64
N_META = 16
H_ATT = 4
DQK_ATT = 128
DV_ATT = 256
H_M = 4
DQK_M = 128
DV_M = 256
N_BUCKETS = 32
MAX_DISTANCE = 128
EPS = 1e-6
NEG = -1e30
LOG2E = math.log2(math.e)

ATT_QK = H_ATT * 2 * DQK_ATT
ATT_V = H_ATT * DV_ATT
M_QK = H_M * DQK_M
M_V = H_M * DV_M
N_MAIN = 2 * ATT_QK + ATT_V + 2 * M_QK + M_V + M_V

LANES = 128
ATT_TILE = 256
M_TILE = 256
TAIL_ROWS = 256
PROJ_ROWS = 1024
PROJ_COLS = 512
MERGE_ROWS = 512
CACHE_CHUNK = 1024
FF_ROWS = 512
FF_COLS = 1024
VMEM_LIMIT = 56 * 1024 * 1024

_NT = (((1,), (1,)), ((), ()))


def _dot(a, b):
    return jnp.dot(a, b, preferred_element_type=F32)


def _dot_nt(a, b):
    return lax.dot_general(a, b, _NT, preferred_element_type=F32)


def _log_sigmoid(x):
    return jnp.minimum(x, 0.0) - jnp.log1p(jnp.exp(-jnp.abs(x)))


def _col(a, idx):
    lane = lax.broadcasted_iota(jnp.int32, a.shape, 1)
    return jnp.sum(jnp.where(lane == idx, a, 0.0), axis=1, keepdims=True)


def _rms(x, gain):
    ms = jnp.mean(x * x, axis=-1, keepdims=True)
    return x * lax.rsqrt(ms + EPS) * gain


def _bias_from_rel(rel, rb_ref, h):
    nb = N_BUCKETS // 2
    max_exact = nb // 2
    ret = jnp.where(rel > 0, nb, 0)
    n = jnp.abs(rel)
    nf = jnp.maximum(n, 1).astype(F32)
    large = max_exact + (jnp.log(nf / max_exact) / math.log(MAX_DISTANCE / max_exact)
                         * (nb - max_exact)).astype(jnp.int32)
    large = jnp.minimum(large, nb - 1)
    bucket = ret + jnp.where(n < max_exact, n, large)
    out = jnp.zeros(rel.shape, F32)
    for b in range(N_BUCKETS):
        out = jnp.where(bucket == b, rb_ref[b, h], out)
    return out


def _bias_kernel(rb_ref, diag_ref, prev_ref, meta_ref, mm_ref, samp_ref, *, past_len, dec_seq):
    h = pl.program_id(0)
    shift = CHUNK.bit_length() - 1
    T = ATT_TILE
    far = _bias_from_rel(jnp.full((8, LANES), -(2 * MAX_DISTANCE + 1), jnp.int32), rb_ref, h)[0:1, 0:1]

    qi = lax.broadcasted_iota(jnp.int32, (T, T), 0)
    kj = lax.broadcasted_iota(jnp.int32, (T, T), 1)
    bias = (_bias_from_rel(kj - qi, rb_ref, h) - far) * LOG2E
    diag_ref[...] = jnp.where((kj >> shift) <= (qi >> shift), bias, NEG)
    prev_ref[...] = (_bias_from_rel(kj - T - qi, rb_ref, h) - far) * LOG2E

    qi = lax.broadcasted_iota(jnp.int32, (T, LANES), 0)
    kj = lax.broadcasted_iota(jnp.int32, (T, LANES), 1)
    for var in range(2):
        bias = (_bias_from_rel(kj - (N_META + qi + var * T), rb_ref, h) - far) * LOG2E
        meta_ref[var] = jnp.where(kj < N_META, bias, NEG)

    qi = lax.broadcasted_iota(jnp.int32, (N_META, LANES), 0)
    kj = lax.broadcasted_iota(jnp.int32, (N_META, LANES), 1)
    mm_ref[...] = jnp.where(kj < N_META, _bias_from_rel(kj - qi, rb_ref, h) * LOG2E, NEG)

    width = samp_ref.shape[-1]
    qi = lax.broadcasted_iota(jnp.int32, (dec_seq, width), 0) + past_len
    kj = lax.broadcasted_iota(jnp.int32, (dec_seq, width), 1)
    bias = _bias_from_rel(kj - qi, rb_ref, h) * LOG2E
    ok = ((kj >> shift) <= (qi >> shift)) & (kj < past_len + dec_seq)
    samp_ref[...] = jnp.where(ok, bias, NEG)


def _bias_tiles(rel_bias, past_len, dec_seq):
    T = ATT_TILE
    sw = past_len + LANES
    out_shape = (
        jax.ShapeDtypeStruct((H_ATT, T, T), F32),
        jax.ShapeDtypeStruct((H_ATT, T, T), F32),
        jax.ShapeDtypeStruct((H_ATT, 2, T, LANES), F32),
        jax.ShapeDtypeStruct((H_ATT, N_META, LANES), F32),
        jax.ShapeDtypeStruct((H_ATT, dec_seq, sw), F32),
    )
    out_specs = (
        pl.BlockSpec((None, T, T), lambda h: (h, 0, 0)),
        pl.BlockSpec((None, T, T), lambda h: (h, 0, 0)),
        pl.BlockSpec((None, 2, T, LANES), lambda h: (h, 0, 0, 0)),
        pl.BlockSpec((None, N_META, LANES), lambda h: (h, 0, 0)),
        pl.BlockSpec((None, dec_seq, sw), lambda h: (h, 0, 0)),
    )
    return pl.pallas_call(
        functools.partial(_bias_kernel, past_len=past_len, dec_seq=dec_seq),
        grid=(H_ATT,),
        in_specs=[pl.BlockSpec(memory_space=pltpu.SMEM)],
        out_specs=out_specs,
        out_shape=out_shape,
        name="rel_bias_tiles",
    )(rel_bias)


def _kv_copies(stage, sem, dst_prompt, dst_sample, i, *, tail, n_batch, frames_per_batch, n_sample_rows):
    if not tail:
        rows = stage.shape[0]
        tiles_per_batch = frames_per_batch // rows
        b = i // tiles_per_batch
        t = i - b * tiles_per_batch
        return [pltpu.make_async_copy(stage, dst_prompt.at[b, pl.ds(N_META + t * rows, rows)], sem.at[0])]
    copies = [pltpu.make_async_copy(stage.at[pl.ds(b * N_META, N_META)], dst_prompt.at[b, pl.ds(0, N_META)],
                                    sem.at[b]) for b in range(n_batch)]
    copies.append(pltpu.make_async_copy(stage.at[pl.ds(n_batch * N_META, n_sample_rows)], dst_sample,
                                        sem.at[n_batch]))
    return copies


def _inproj_kernel(xf_ref, xt_ref, g1_ref, wa_ref, wb_ref, wg_ref, gb_ref, qg_ref, kg_ref,
                   zf_ref, zt_ref, gf_ref, gt_ref, kp_ref, vp_ref, ks_ref, vs_ref,
                   xn_s, y2d, stage, sem, *, n_frame_tiles, n_batch, frames_per_batch, n_sample_rows):
    i = pl.program_id(0)
    j = pl.program_id(1)
    bn = wa_ref.shape[0]
    assert ATT_QK % bn == 0 and ATT_V % bn == 0 and M_QK % bn == 0 and M_V % bn == 0 and bn % DV_ATT == 0
    j_k = ATT_QK // bn
    j_v = j_k + ATT_QK // bn
    j_mq = j_v + ATT_V // bn
    j_mk = j_mq + M_QK // bn
    j_mv = j_mk + M_QK // bn
    j_og = j_mv + M_V // bn

    def tile(x_ref, z_ref, g_ref, tail):
        nrows = x_ref.shape[0]
        st = stage.at[pl.ds(0, nrows)]
        copies = functools.partial(_kv_copies, st, sem, i=i, tail=tail, n_batch=n_batch,
                                   frames_per_batch=frames_per_batch, n_sample_rows=n_sample_rows)

        @pl.when(j == 0)
        def _():
            xb = _rms(x_ref[...], g1_ref[...]).astype(BF16)
            xn_s[0:nrows, :] = xb
            gates = _dot(xb, wg_ref[...]) + gb_ref[...]
            lane = lax.broadcasted_iota(jnp.int32, gates.shape, 1)
            g_ref[...] = jnp.where(lane < H_M, gates, _log_sigmoid(gates))

        acc = _dot_nt(xn_s[0:nrows, :], jnp.where(j < j_og, wa_ref[...].astype(BF16), wb_ref[...]))

        def group_norm(gain_ref, scale):
            parts = []
            for gi in range(bn // DQK_ATT):
                y = _rms(acc[:, gi * DQK_ATT:(gi + 1) * DQK_ATT], gain_ref[...])
                parts.append(y if scale is None else y * scale)
            return jnp.concatenate(parts, axis=1)

        def stage_heads(y, jj, j0, dst_prompt, dst_sample):
            c0 = (jj - j0) * bn
            y2d[0:nrows, c0:c0 + bn] = y
            if jj == j0 + ATT_QK // bn - 1:
                if j0 != j_k:
                    for c in copies(kp_ref, ks_ref):
                        c.wait()
                st[...] = y2d[0:nrows, :].reshape(nrows, H_ATT, DV_ATT)
                for c in copies(dst_prompt, dst_sample):
                    c.start()

        @pl.when(j < j_k)
        def _():
            z_ref[...] = group_norm(qg_ref, DQK_ATT ** -0.5 * LOG2E)

        for jj in range(j_k, j_v):
            @pl.when(j == jj)
            def _(jj=jj):
                y = group_norm(kg_ref, None)
                z_ref[...] = y
                stage_heads(y, jj, j_k, kp_ref, ks_ref)

        for jj in range(j_v, j_mq):
            @pl.when(j == jj)
            def _(jj=jj):
                z_ref[...] = acc
                stage_heads(acc, jj, j_v, vp_ref, vs_ref)

        @pl.when(j == j_mq)
        def _():
            for c in copies(vp_ref, vs_ref):
                c.wait()

        @pl.when(((j >= j_mq) & (j < j_mk)) | ((j >= j_mv) & (j < j_og)))
        def _():
            z_ref[...] = acc

        @pl.when((j >= j_mk) & (j < j_mv))
        def _():
            z_ref[...] = acc * (DQK_M ** -0.5)

        @pl.when(j >= j_og)
        def _():
            z_ref[...] = jax.nn.sigmoid(acc)

    @pl.when(i < n_frame_tiles)
    def _():
        tile(xf_ref, zf_ref, gf_ref, False)

    @pl.when(i == n_frame_tiles)
    def _():
        tile(xt_ref, zt_ref, gt_ref, True)


def _inproj(x_f, x_t, g1, w_all, layer, w_og, w_gate, gate_bias, qg, kg, n_batch, n_sample_streams):
    NF, D = x_f.shape
    NT = x_t.shape[0]
    bm, bn = PROJ_ROWS, PROJ_COLS
    nft = NF // bm
    nj = N_MAIN // bn
    nja = (N_MAIN - M_V) // bn
    S = NF // n_batch
    L = N_META + S
    n_sample_rows = n_sample_streams * N_META
    assert NF % bm == 0 and S % bm == 0 and N_MAIN % bn == 0 and NT <= bm
    fr = lambda i: jnp.minimum(i, nft - 1)
    any_spec = pl.BlockSpec(memory_space=pl.ANY)
    return pl.pallas_call(
        functools.partial(_inproj_kernel, n_frame_tiles=nft, n_batch=n_batch, frames_per_batch=S,
                          n_sample_rows=n_sample_rows),
        grid=(nft + 1, nj),
        in_specs=[
            pl.BlockSpec((bm, D), lambda i, j: (fr(i), 0)),
            pl.BlockSpec((NT, D), lambda i, j: (0, 0)),
            pl.BlockSpec((1, D), lambda i, j: (0, 0)),
            pl.BlockSpec((None, bn, D), lambda i, j: (layer, jnp.minimum(j, nja - 1), 0)),
            pl.BlockSpec((bn, D), lambda i, j: (jnp.maximum(j - nja, 0), 0)),
            pl.BlockSpec((D, LANES), lambda i, j: (0, 0)),
            pl.BlockSpec((1, LANES), lambda i, j: (0, 0)),
            pl.BlockSpec((1, DQK_ATT), lambda i, j: (0, 0)),
            pl.BlockSpec((1, DQK_ATT), lambda i, j: (0, 0)),
        ],
        out_specs=(
            pl.BlockSpec((bm, bn), lambda i, j: (fr(i), jnp.where(i < nft, j, nj - 1))),
            pl.BlockSpec((NT, bn), lambda i, j: (0, jnp.where(i < nft, 0, j))),
            pl.BlockSpec((bm, LANES), lambda i, j: (fr(i), 0)),
            pl.BlockSpec((NT, LANES), lambda i, j: (0, 0)),
            any_spec, any_spec, any_spec, any_spec,
        ),
        out_shape=(
            jax.ShapeDtypeStruct((NF, N_MAIN), F32),
            jax.ShapeDtypeStruct((NT, N_MAIN), F32),
            jax.ShapeDtypeStruct((NF, LANES), F32),
            jax.ShapeDtypeStruct((NT, LANES), F32),
            jax.ShapeDtypeStruct((n_batch, L, H_ATT, 2 * DQK_ATT), F32),
            jax.ShapeDtypeStruct((n_batch, L, H_ATT, DV_ATT), F32),
            jax.ShapeDtypeStruct((n_sample_rows, H_ATT, 2 * DQK_ATT), F32),
            jax.ShapeDtypeStruct((n_sample_rows, H_ATT, DV_ATT), F32),
        ),
        scratch_shapes=[
            pltpu.VMEM((bm, D), BF16),
            pltpu.VMEM((bm, ATT_QK), F32),
            pltpu.VMEM((bm, H_ATT, DV_ATT), F32),
            pltpu.SemaphoreType.DMA((n_batch + 1,)),
        ],
        compiler_params=pltpu.CompilerParams(
            dimension_semantics=("arbitrary", "arbitrary"), vmem_limit_bytes=VMEM_LIMIT),
        name="in_proj",
    )(x_f, x_t, g1, w_all, w_og, w_gate, gate_bias, qg, kg)


def _diff_lambda(lq1_ref, lk1_ref, lq2_ref, lk2_ref, lam_init):
    a = jnp.sum(lq1_ref[...] * lk1_ref[...], axis=-1, keepdims=True)
    b = jnp.sum(lq2_ref[...] * lk2_ref[...], axis=-1, keepdims=True)
    return jnp.exp(a) - jnp.exp(b) + lam_init


def _subln(o, sg_ref, lam_init):
    return _rms(o, sg_ref[...]) * (1.0 - lam_init)


def _pad_rows(dst, src):
    dst[...] = jnp.zeros(dst.shape, dst.dtype)
    dst[0:src.shape[0], :] = src[...].astype(dst.dtype)


def _attn_prompt_kernel(q_ref, k_ref, v_ref, km_ref, vm_ref, bd_ref, bp_ref, bm_ref,
                        lq1_ref, lk1_ref, lq2_ref, lk2_ref, sg_ref, o_ref,
                        kb_s, vb_s, kmb_s, vmb_s, *, lam_init):
    T = ATT_TILE
    n_tiles = q_ref.shape[0] // T
    kb_s[...] = k_ref[...].astype(BF16)
    vb_s[...] = v_ref[...].astype(BF16)
    _pad_rows(kmb_s, km_ref)
    _pad_rows(vmb_s, vm_ref)
    lam = _diff_lambda(lq1_ref, lk1_ref, lq2_ref, lk2_ref, lam_init)

    for j in range(n_tiles):
        rows = slice(j * T, (j + 1) * T)
        regions = [(kmb_s, vmb_s, slice(0, LANES), bm_ref[min(j, 1)])]
        if j >= 2:
            regions.append((kb_s, vb_s, slice(0, (j - 1) * T), None))
        if j >= 1:
            regions.append((kb_s, vb_s, slice((j - 1) * T, j * T), bp_ref[...]))
        regions.append((kb_s, vb_s, rows, bd_ref[...]))

        probs, denom = [], []
        for c in range(2):
            cols = slice(c * DQK_ATT, (c + 1) * DQK_ATT)
            qc = q_ref[rows, cols].astype(BF16)
            ss = [_dot_nt(qc, kr[rs, cols]) for kr, _, rs, _ in regions]
            ss = [s if br is None else s + br for s, (_, _, _, br) in zip(ss, regions)]
            m = functools.reduce(jnp.maximum, [jnp.max(s, axis=-1, keepdims=True) for s in ss])
            pc = [jnp.exp2(s - m) for s in ss]
            denom.append(functools.reduce(jnp.add, [jnp.sum(p, axis=-1, keepdims=True) for p in pc]))
            probs.append(pc)
        a0 = 1.0 / denom[0]
        a1 = lam / denom[1]
        o = None
        for r, (_, vr, rs, _) in enumerate(regions):
            w = (probs[0][r] * a0 - probs[1][r] * a1).astype(BF16)
            t = _dot(w, vr[rs, :])
            o = t if o is None else o + t
        o_ref[rows, :] = _subln(o, sg_ref, lam_init)


def _attn_prompt(z_f, z_t, bias, lam_vecs, sg, B, S, lam_init):
    T = ATT_TILE
    qkw = 2 * DQK_ATT
    kcol = ATT_QK // qkw
    vcol = 2 * ATT_QK // DV_ATT
    b_diag, b_prev, b_meta, _, _ = bias
    vec = pl.BlockSpec((1, DQK_ATT), lambda b, h: (0, 0))
    return pl.pallas_call(
        functools.partial(_attn_prompt_kernel, lam_init=lam_init),
        grid=(B, H_ATT),
        in_specs=[
            pl.BlockSpec((S, qkw), lambda b, h: (b, h)),
            pl.BlockSpec((S, qkw), lambda b, h: (b, kcol + h)),
            pl.BlockSpec((S, DV_ATT), lambda b, h: (b, vcol + h)),
            pl.BlockSpec((N_META, qkw), lambda b, h: (b, kcol + h)),
            pl.BlockSpec((N_META, DV_ATT), lambda b, h: (b, vcol + h)),
            pl.BlockSpec((None, T, T), lambda b, h: (h, 0, 0)),
            pl.BlockSpec((None, T, T), lambda b, h: (h, 0, 0)),
            pl.BlockSpec((None, 2, T, LANES), lambda b, h: (h, 0, 0, 0)),
            vec, vec, vec, vec,
            pl.BlockSpec((1, DV_ATT), lambda b, h: (0, 0)),
        ],
        out_specs=pl.BlockSpec((S, DV_ATT), lambda b, h: (b, h)),
        out_shape=jax.ShapeDtypeStruct((B * S, ATT_V), F32),
        scratch_shapes=[
            pltpu.VMEM((S, qkw), BF16),
            pltpu.VMEM((S, DV_ATT), BF16),
            pltpu.VMEM((LANES, qkw), BF16),
            pltpu.VMEM((LANES, DV_ATT), BF16),
        ],
        compiler_params=pltpu.CompilerParams(
            dimension_semantics=("arbitrary", "arbitrary"), vmem_limit_bytes=VMEM_LIMIT),
        name="attn_prompt",
    )(z_f, z_f, z_f, z_t, z_t, b_diag, b_prev, b_meta, *lam_vecs, sg)


def _attn_small_kernel(*refs, lam_init, has_cache):
    if has_cache:
        (q_ref, kc_ref, vc_ref, kn_ref, vn_ref, b_ref,
         lq1_ref, lk1_ref, lq2_ref, lk2_ref, sg_ref, o_ref, kpad, vpad, kb_s, vb_s) = refs
        past_len = kb_s.shape[0]
        chunk = kc_ref.shape[0]
        c = pl.program_id(1)
        r0 = pl.multiple_of(c * chunk, chunk)
        kb_s[pl.ds(r0, chunk), :] = kc_ref[...].reshape(chunk, ATT_QK).astype(BF16)
        vb_s[pl.ds(r0, chunk), :] = vc_ref[...].reshape(chunk, ATT_V).astype(BF16)

        @pl.when(c == past_len // chunk - 1)
        def _():
            _attn_small_body(q_ref, kn_ref, vn_ref, b_ref, lq1_ref, lk1_ref, lq2_ref, lk2_ref, sg_ref, o_ref,
                             kpad, vpad, kb_s, vb_s, lam_init=lam_init, past_len=past_len)
    else:
        (q_ref, kn_ref, vn_ref, b_ref,
         lq1_ref, lk1_ref, lq2_ref, lk2_ref, sg_ref, o_ref, kpad, vpad) = refs
        _attn_small_body(q_ref, kn_ref, vn_ref, b_ref, lq1_ref, lk1_ref, lq2_ref, lk2_ref, sg_ref, o_ref,
                         kpad, vpad, None, None, lam_init=lam_init, past_len=0)


def _attn_small_body(q_ref, kn_ref, vn_ref, b_ref, lq1_ref, lk1_ref, lq2_ref, lk2_ref, sg_ref, o_ref,
                     kpad, vpad, kb_s, vb_s, *, lam_init, past_len):
    has_cache = past_len > 0
    lam = _diff_lambda(lq1_ref, lk1_ref, lq2_ref, lk2_ref, lam_init)
    _pad_rows(kpad, kn_ref)
    _pad_rows(vpad, vn_ref)
    for h in range(H_ATT):
        vsl = slice(h * DV_ATT, (h + 1) * DV_ATT)
        outs = []
        for c in range(2):
            sl = slice(h * 2 * DQK_ATT + c * DQK_ATT, h * 2 * DQK_ATT + (c + 1) * DQK_ATT)
            qc = q_ref[:, sl].astype(BF16)
            s_new = _dot_nt(qc, kpad[:, sl]) + b_ref[h, :, past_len:]
            m = jnp.max(s_new, axis=-1, keepdims=True)
            if has_cache:
                s_old = _dot_nt(qc, kb_s[:, sl]) + b_ref[h, :, :past_len]
                m = jnp.maximum(m, jnp.max(s_old, axis=-1, keepdims=True))
            p_new = jnp.exp2(s_new - m)
            l = jnp.sum(p_new, axis=-1, keepdims=True)
            o = _dot(p_new.astype(BF16), vpad[:, vsl])
            if has_cache:
                p_old = jnp.exp2(s_old - m)
                l = l + jnp.sum(p_old, axis=-1, keepdims=True)
                o = o + _dot(p_old.astype(BF16), vb_s[:, vsl])
            outs.append(o / l)
        o_ref[:, vsl] = _subln(outs[0] - lam * outs[1], sg_ref, lam_init)


def _attn_small(z_t, row0, n_streams, bias, lam_vecs, sg, lam_init, cache=None):
    vec = pl.BlockSpec((1, DQK_ATT), lambda b, c: (0, 0))
    in_specs = [pl.BlockSpec((N_META, ATT_QK), lambda b, c: (row0 + b, 0))]
    args = [z_t]
    scratch = [pltpu.VMEM((LANES, ATT_QK), BF16), pltpu.VMEM((LANES, ATT_V), BF16)]
    n_chunks = 1
    if cache is not None:
        ck, cv, layer = cache
        past_len = ck.shape[2]
        chunk = min(past_len, CACHE_CHUNK)
        assert past_len % chunk == 0
        n_chunks = past_len // chunk
        in_specs += [
            pl.BlockSpec((None, None, chunk, H_ATT, 2 * DQK_ATT), lambda b, c: (layer, b, c, 0, 0)),
            pl.BlockSpec((None, None, chunk, H_ATT, DV_ATT), lambda b, c: (layer, b, c, 0, 0))]
        args += [ck, cv]
        scratch += [pltpu.VMEM((past_len, ATT_QK), BF16), pltpu.VMEM((past_len, ATT_V), BF16)]
    in_specs += [
        pl.BlockSpec((N_META, ATT_QK), lambda b, c: (row0 + b, 1)),
        pl.BlockSpec((N_META, ATT_V), lambda b, c: (row0 + b, 2 * ATT_QK // ATT_V)),
        pl.BlockSpec(bias.shape, lambda b, c: (0, 0, 0)),
        vec, vec, vec, vec,
        pl.BlockSpec((1, DV_ATT), lambda b, c: (0, 0)),
    ]
    args += [z_t, z_t, bias, *lam_vecs, sg]
    return pl.pallas_call(
        functools.partial(_attn_small_kernel, lam_init=lam_init, has_cache=cache is not None),
        grid=(n_streams, n_chunks),
        in_specs=in_specs,
        out_specs=pl.BlockSpec((N_META, ATT_V), lambda b, c: (b, 0)),
        out_shape=jax.ShapeDtypeStruct((n_streams * N_META, ATT_V), F32),
        scratch_shapes=scratch,
        compiler_params=pltpu.CompilerParams(
            dimension_semantics=("arbitrary", "arbitrary"), vmem_limit_bytes=VMEM_LIMIT),
        name="attn_cached" if cache is not None else "attn_meta",
    )(*args)


def _mlstm_kernel(*refs, n_blocks, has_init):
    refs = list(refs)
    qh_ref, kh_ref, vh_ref, gh_ref = refs[:4]
    pos = 4
    if n_blocks:
        qf_ref, kf_ref, vf_ref, gf_ref = refs[pos:pos + 4]
        pos += 4
    if has_init:
        c0_ref, n0_ref, m0_ref = refs[pos:pos + 3]
        pos += 3
    hh_ref = refs[pos]
    pos += 1
    if n_blocks:
        hf_ref = refs[pos]
        pos += 1
    c_ref, n_ref, m_ref = refs[pos:pos + 3]
    qs, ks, vs, gs = refs[pos + 3:pos + 7]

    t = pl.program_id(1)
    T = M_TILE
    block = functools.partial(_mlstm_block, c_ref=c_ref, n_ref=n_ref, m_ref=m_ref)

    @pl.when(t == 0)
    def _():
        if has_init:
            c_ref[...] = c0_ref[...]
            n_ref[...] = n0_ref[...]
            m_ref[...] = m0_ref[...]
        else:
            c_ref[...] = jnp.zeros(c_ref.shape, F32)
            n_ref[...] = jnp.zeros(n_ref.shape, F32)
            m_ref[...] = jnp.zeros(m_ref.shape, F32)
        nh = qh_ref.shape[0]
        for dst, src in ((qs, qh_ref), (ks, kh_ref), (vs, vh_ref)):
            dst[...] = jnp.zeros(dst.shape, F32)
            dst[0:nh, :] = src[...]
        lane = lax.broadcasted_iota(jnp.int32, gs.shape, 1)
        gs[...] = jnp.where(lane < H_M, NEG, 0.0)
        gs[0:nh, :] = gh_ref[...]

        block(qs, ks, vs, gs, hh_ref, nh)

    if n_blocks:
        @pl.when(t > 0)
        def _():
            block(qf_ref, kf_ref, vf_ref, gf_ref, hf_ref, T)


def _mlstm_block(q_src, k_src, v_src, g_src, h_dst, n_out, c_ref, n_ref, m_ref):
    T = M_TILE
    g = g_src[...]
    row = lax.broadcasted_iota(jnp.int32, (T, T), 0)
    colid = lax.broadcasted_iota(jnp.int32, (T, T), 1)
    causal = row >= colid
    tril = jnp.where(causal, 1.0, 0.0).astype(F32)
    cum = jnp.dot(tril, g, preferred_element_type=F32, precision=lax.Precision.HIGHEST)
    g_t = g.T
    cum_t = cum.T
    last = cum[T - 1:T, :]

    for h in range(H_M):
        q = q_src[:, h * DQK_M:(h + 1) * DQK_M]
        k = k_src[:, h * DQK_M:(h + 1) * DQK_M]
        v = v_src[:, h * DV_M:(h + 1) * DV_M]
        qb = q.astype(BF16)
        kb = k.astype(BF16)
        m_old = m_ref[h, 0:1, 0:1]
        c_old = c_ref[h]
        n_old = n_ref[h]
        bh_c = _col(cum, H_M + h)
        ih_c = _col(g, h)
        bh_r = cum_t[H_M + h:H_M + h + 1, :]
        ih_r = g_t[h:h + 1, :]
        d = jnp.where(causal, bh_c - bh_r + ih_r, NEG)
        inter = bh_c + m_old
        mt = jnp.maximum(inter, jnp.max(d, axis=-1, keepdims=True))
        a = jnp.exp(d - mt) * _dot_nt(qb, kb)
        iw = jnp.exp(inter - mt)
        num = _dot(a.astype(BF16), v.astype(BF16)) + iw * _dot_nt(qb, c_old.astype(BF16))
        den = jnp.sum(a, axis=-1, keepdims=True) + iw * jnp.sum(q * n_old, axis=-1, keepdims=True)
        den = jnp.maximum(jnp.abs(den), jnp.exp(-mt))
        hout = num / den
        h_dst[:, h * DV_M:(h + 1) * DV_M] = hout[0:n_out, :]

        bl = _col(last, H_M + h)
        g_c = bl - bh_c + ih_c
        m_new = jnp.maximum(bl + m_old, jnp.max(g_c, axis=0, keepdims=True))
        sw = jnp.exp(g_c - m_new)
        dec = jnp.exp(bl + m_old - m_new)
        c_ref[h] = dec * c_old + _dot(v.T.astype(BF16), (sw * k).astype(BF16))
        n_ref[h] = dec * n_old + jnp.sum(sw * k, axis=0, keepdims=True)
        m_ref[h] = jnp.broadcast_to(m_new, m_ref.shape[1:])


def _mlstm(z_t, g_t, row0, n_streams, frames=None, init=None):
    T = M_TILE
    qcol = (2 * ATT_QK + ATT_V) // M_QK
    vcol = (2 * ATT_QK + ATT_V + 2 * M_QK) // M_V
    nh = N_META
    in_specs = [
        pl.BlockSpec((nh, M_QK), lambda b, t: (row0 + b, qcol)),
        pl.BlockSpec((nh, M_QK), lambda b, t: (row0 + b, qcol + 1)),
        pl.BlockSpec((nh, M_V), lambda b, t: (row0 + b, vcol)),
        pl.BlockSpec((nh, LANES), lambda b, t: (row0 + b, 0)),
    ]
    args = [z_t, z_t, z_t, g_t]
    n_blocks = 0
    if frames is not None:
        z_f, g_f, fps = frames
        n_blocks = fps // T
        fidx = lambda b, t: b * n_blocks + jnp.maximum(t - 1, 0)
        in_specs += [
            pl.BlockSpec((T, M_QK), lambda b, t: (fidx(b, t), qcol)),
            pl.BlockSpec((T, M_QK), lambda b, t: (fidx(b, t), qcol + 1)),
            pl.BlockSpec((T, M_V), lambda b, t: (fidx(b, t), vcol)),
            pl.BlockSpec((T, LANES), lambda b, t: (fidx(b, t), 0)),
        ]
        args += [z_f, z_f, z_f, g_f]
    state_specs = (
        pl.BlockSpec((None, H_M, DV_M, DQK_M), lambda b, t: (b, 0, 0, 0)),
        pl.BlockSpec((None, H_M, 1, DQK_M), lambda b, t: (b, 0, 0, 0)),
        pl.BlockSpec((None, H_M, 8, LANES), lambda b, t: (b, 0, 0, 0)),
    )
    if init is not None:
        in_specs += list(state_specs)
        args += list(init)
    out_specs = [pl.BlockSpec((nh, M_V), lambda b, t: (b, 0))]
    out_shape = [jax.ShapeDtypeStruct((n_streams * nh, M_V), F32)]
    if n_blocks:
        out_specs.append(pl.BlockSpec((T, M_V), lambda b, t: (fidx(b, t), 0)))
        out_shape.append(jax.ShapeDtypeStruct((n_streams * fps, M_V), F32))
    out_specs += list(state_specs)
    out_shape += [
        jax.ShapeDtypeStruct((n_streams, H_M, DV_M, DQK_M), F32),
        jax.ShapeDtypeStruct((n_streams, H_M, 1, DQK_M), F32),
        jax.ShapeDtypeStruct((n_streams, H_M, 8, LANES), F32),
    ]
    return pl.pallas_call(
        functools.partial(_mlstm_kernel, n_blocks=n_blocks, has_init=init is not None),
        grid=(n_streams, 1 + n_blocks),
        in_specs=in_specs,
        out_specs=tuple(out_specs),
        out_shape=tuple(out_shape),
        scratch_shapes=[
            pltpu.VMEM((T, M_QK), F32), pltpu.VMEM((T, M_QK), F32),
            pltpu.VMEM((T, M_V), F32), pltpu.VMEM((T, LANES), F32),
        ],
        compiler_params=pltpu.CompilerParams(
            dimension_semantics=("arbitrary", "arbitrary"), vmem_limit_bytes=VMEM_LIMIT),
        name="mlstm_prompt" if n_blocks else "mlstm_cached",
    )(*args)


def _merge_kernel(xf_ref, xt_ref, attf_ref, attt_ref, hmf_ref, hmt_ref, zf_ref, zt_ref, mg_ref, wo_ref,
                  of_ref, ot_ref, *, n_frame_tiles):
    i = pl.program_id(0)

    def tile(x_ref, att_ref, hm_ref, og_ref, o_ref):
        parts = []
        for h in range(H_M):
            sl = slice(h * DV_M, (h + 1) * DV_M)
            parts.append(_rms(hm_ref[:, sl], mg_ref[:, sl]) * og_ref[:, sl])
        hmn = jnp.concatenate(parts, axis=1).astype(BF16)
        o_ref[...] = (x_ref[...] + _dot(att_ref[...].astype(BF16), wo_ref[0:ATT_V, :])
                      + _dot(hmn, wo_ref[ATT_V:, :]))

    @pl.when(i < n_frame_tiles)
    def _():
        tile(xf_ref, attf_ref, hmf_ref, zf_ref, of_ref)

    @pl.when(i == n_frame_tiles)
    def _():
        tile(xt_ref, attt_ref, hmt_ref, zt_ref, ot_ref)


def _merge(x_f, x_t, att_f, att_t, hm_f, hm_t, z_f, z_t, mg, w_out, layer):
    NF, D = x_f.shape
    NT = x_t.shape[0]
    bm = MERGE_ROWS
    nft = NF // bm
    assert NF % bm == 0
    ogcol = (N_MAIN - M_V) // M_V
    fr = lambda i: (jnp.minimum(i, nft - 1), 0)
    tl = lambda i: (0, 0)
    once = dict(pipeline_mode=pl.Buffered(1))
    return pl.pallas_call(
        functools.partial(_merge_kernel, n_frame_tiles=nft),
        grid=(nft + 1,),
        in_specs=[
            pl.BlockSpec((bm, D), fr), pl.BlockSpec((NT, D), tl, **once),
            pl.BlockSpec((bm, ATT_V), fr), pl.BlockSpec((NT, ATT_V), tl, **once),
            pl.BlockSpec((bm, M_V), fr), pl.BlockSpec((NT, M_V), tl, **once),
            pl.BlockSpec((bm, M_V), lambda i: (jnp.minimum(i, nft - 1), ogcol)),
            pl.BlockSpec((NT, M_V), lambda i: (0, ogcol), **once),
            pl.BlockSpec((1, M_V), tl),
            pl.BlockSpec((None, D, D), lambda i: (layer, 0, 0), **once),
        ],
        out_specs=(pl.BlockSpec((bm, D), fr), pl.BlockSpec((NT, D), tl)),
        out_shape=(jax.ShapeDtypeStruct((NF, D), F32), jax.ShapeDtypeStruct((NT, D), F32)),
        compiler_params=pltpu.CompilerParams(
            dimension_semantics=("arbitrary",), vmem_limit_bytes=VMEM_LIMIT),
        name="merge_out_proj",
    )(x_f, x_t, att_f, att_t, hm_f, hm_t, z_f, z_t, mg, w_out)


def _ffn_kernel(xf_ref, xt_ref, g2_ref, wu_ref, wd_ref, of_ref, ot_ref, xn_s, *, n_frame_tiles):
    i = pl.program_id(0)
    k = pl.program_id(1)

    def tile(x_ref, o_ref):
        nrows = x_ref.shape[0]

        @pl.when(k == 0)
        def _():
            x = x_ref[...]
            xn_s[0:nrows, :] = _rms(x, g2_ref[...]).astype(BF16)
            o_ref[...] = x

        u = jnp.maximum(_dot(xn_s[0:nrows, :], wu_ref[...]), 0.0)
        o_ref[...] += _dot((u * u).astype(BF16), wd_ref[...])

    @pl.when(i < n_frame_tiles)
    def _():
        tile(xf_ref, of_ref)

    @pl.when(i == n_frame_tiles)
    def _():
        tile(xt_ref, ot_ref)


def _ffn(x_f, x_t, g2, w_up, w_down, layer):
    NF, D = x_f.shape
    NT = x_t.shape[0]
    FF = w_up.shape[2]
    bm, fc = FF_ROWS, FF_COLS
    nft = NF // bm
    assert NF % bm == 0 and FF % fc == 0 and NT <= bm
    fr = lambda i, k: (jnp.minimum(i, nft - 1), 0)
    tl = lambda i, k: (0, 0)
    return pl.pallas_call(
        functools.partial(_ffn_kernel, n_frame_tiles=nft),
        grid=(nft + 1, FF // fc),
        in_specs=[
            pl.BlockSpec((bm, D), fr),
            pl.BlockSpec((NT, D), tl),
            pl.BlockSpec((1, D), tl),
            pl.BlockSpec((None, D, fc), lambda i, k: (layer, 0, k)),
            pl.BlockSpec((None, fc, D), lambda i, k: (layer, k, 0)),
        ],
        out_specs=(pl.BlockSpec((bm, D), fr), pl.BlockSpec((NT, D), tl)),
        out_shape=(jax.ShapeDtypeStruct((NF, D), F32), jax.ShapeDtypeStruct((NT, D), F32)),
        scratch_shapes=[pltpu.VMEM((bm, D), BF16)],
        compiler_params=pltpu.CompilerParams(
            dimension_semantics=("arbitrary", "arbitrary"), vmem_limit_bytes=VMEM_LIMIT),
        name="ffn",
    )(x_f, x_t, g2, w_up, w_down)


def kernel(x_prompt, x_sample, cache_k, cache_v, state_C, state_n, state_m, meta_tokens, rel_bias, norm1_g, w_in, b_i, b_f, q_norm_g, k_norm_g, lambda_q1, lambda_k1, lambda_q2, lambda_k2, subln_g, mnorm_g, w_out, norm2_g, w_up, w_down):
    B, S, D = x_prompt.shape
    Bd, Td, _ = x_sample.shape
    depth = w_in.shape[0]
    Pc = cache_k.shape[2]
    assert Td == N_META and S % ATT_TILE == 0 and S % M_TILE == 0 and Pc % LANES == 0
    assert ATT_TILE % CHUNK == 0 and CHUNK & (CHUNK - 1) == 0 and ATT_TILE >= MAX_DISTANCE

    n_frames = B * S
    n_meta = B * N_META
    n_samp = Bd * Td
    assert n_meta + n_samp <= TAIL_ROWS
    srow = n_meta // N_META

    meta = jnp.broadcast_to(meta_tokens.astype(F32)[None], (B, N_META, D)).reshape(n_meta, D)
    x_f = x_prompt.astype(F32).reshape(n_frames, D)
    x_t = jnp.concatenate([meta, x_sample.astype(F32).reshape(n_samp, D),
                           jnp.zeros((TAIL_ROWS - n_meta - n_samp, D), F32)], axis=0)

    bias = _bias_tiles(rel_bias.astype(F32), Pc, Td)
    ck = cache_k.astype(F32)
    cv = cache_v.astype(F32)
    zeros_pad_att = jnp.zeros((TAIL_ROWS - n_meta - n_samp, ATT_V), F32)

    g_off = 2 * ATT_QK + ATT_V + 2 * M_QK + M_V
    w_in_t = jnp.swapaxes(w_in.astype(F32), 1, 2)
    w_out_b, w_up_b, w_down_b = (w.astype(BF16) for w in (w_out, w_up, w_down))
    outs = {name: [] for name in ("kp", "vp", "Cp", "np", "mp", "ks", "vs", "Cs", "ns", "ms")}

    for l in range(depth):
        lam_init = 0.8 - 0.6 * math.exp(-0.3 * l)
        w_og = w_in_t[l, g_off + 2 * H_M:, :].astype(BF16)
        w_gate = jnp.pad(w_in[l, :, g_off:g_off + 2 * H_M], ((0, 0), (0, LANES - 2 * H_M))).astype(BF16)
        gate_bias = jnp.pad(jnp.concatenate([b_i[l], b_f[l]]).astype(F32), (0, LANES - 2 * H_M))[None]
        lam_vecs = [v[l].astype(F32)[None] for v in (lambda_q1, lambda_k1, lambda_q2, lambda_k2)]
        sg = subln_g[l].astype(F32)[None]

        z_f, z_t, g_f, g_t, k_p, v_p, k_s, v_s = _inproj(
            x_f, x_t, norm1_g[l].astype(F32)[None], w_in_t, l, w_og, w_gate, gate_bias,
            q_norm_g[l].astype(F32)[None], k_norm_g[l].astype(F32)[None], B, Bd)

        att_f = _attn_prompt(z_f, z_t, bias, lam_vecs, sg, B, S, lam_init)
        att_m = _attn_small(z_t, 0, B, bias[3], lam_vecs, sg, lam_init)
        att_s = _attn_small(z_t, srow, Bd, bias[4], lam_vecs, sg, lam_init, cache=(ck, cv, l))
        att_t = jnp.concatenate([att_m, att_s, zeros_pad_att], axis=0)

        hm_m, hm_f, C_p, n_p, m_p = _mlstm(z_t, g_t, 0, B, frames=(z_f, g_f, S))
        init = (state_C[l].astype(F32), state_n[l].astype(F32)[:, :, None, :],
                jnp.broadcast_to(state_m[l].astype(F32)[:, :, None, None], (Bd, H_M, 8, LANES)))
        hm_s, C_s, n_s, m_s = _mlstm(z_t, g_t, srow, Bd, init=init)
        hm_t = jnp.concatenate([hm_m, hm_s, zeros_pad_att], axis=0)

        x_f, x_t = _merge(x_f, x_t, att_f, att_t, hm_f, hm_t, z_f, z_t,
                          mnorm_g[l].astype(F32)[None], w_out_b, l)
        x_f, x_t = _ffn(x_f, x_t, norm2_g[l].astype(F32)[None], w_up_b, w_down_b, l)

        outs["kp"].append(k_p)
        outs["vp"].append(v_p)
        outs["Cp"].append(C_p)
        outs["np"].append(n_p[:, :, 0, :])
        outs["mp"].append(m_p[:, :, 0, 0])
        outs["ks"].append(k_s.reshape(Bd, Td, H_ATT, 2 * DQK_ATT))
        outs["vs"].append(v_s.reshape(Bd, Td, H_ATT, DV_ATT))
        outs["Cs"].append(C_s)
        outs["ns"].append(n_s[:, :, 0, :])
        outs["ms"].append(m_s[:, :, 0, 0])

    y_prompt = x_f.reshape(B, S, D)
    y_sample = x_t[n_meta:n_meta + n_samp].reshape(Bd, Td, D)
    return (y_prompt, y_sample,
            jnp.stack(outs["kp"]), jnp.stack(outs["vp"]), jnp.stack(outs["Cp"]), jnp.stack(outs["np"]),
            jnp.stack(outs["mp"]),
            jnp.stack(outs["ks"]), jnp.stack(outs["vs"]), jnp.stack(outs["Cs"]), jnp.stack(outs["ns"]),
            jnp.stack(outs["ms"]))
```

```python
import functools
import math

import jax
import jax.numpy as jnp
from jax import lax
from jax.experimental import pallas as pl
from jax.experimental.pallas import tpu as pltpu

F32 = jnp.float32
BF16 = jnp.bfloat16

CHUNK = 64
N_META = 16
H_ATT = 4
DQK_ATT = 128
DV_ATT = 256
H_M = 4
DQK_M = 128
DV_M = 256
N_BUCKETS = 32
MAX_DISTANCE = 128
EPS = 1e-6
NEG = -1e30
LOG2E = math.log2(math.e)

ATT_QK = H_ATT * 2 * DQK_ATT
ATT_V = H_ATT * DV_ATT
M_QK = H_M * DQK_M
M_V = H_M * DV_M
N_MAIN = 2 * ATT_QK + ATT_V + 2 * M_QK + M_V + M_V

LANES = 128
ATT_TILE = 256
M_TILE = 256
TAIL_ROWS = 256
PROJ_ROWS = 1024
PROJ_COLS = 512
MERGE_ROWS = 512
CACHE_CHUNK = 1024
FF_ROWS = 512
FF_COLS = 1024
VMEM_LIMIT = 56 * 1024 * 1024

_NT = (((1,), (1,)), ((), ()))


def _dot(a, b):
    return jnp.dot(a, b, preferred_element_type=F32)


def _dot_nt(a, b):
    return lax.dot_general(a, b, _NT, preferred_element_type=F32)


def _log_sigmoid(x):
    return jnp.minimum(x, 0.0) - jnp.log1p(jnp.exp(-jnp.abs(x)))


def _col(a, idx):
    lane = lax.broadcasted_iota(jnp.int32, a.shape, 1)
    return jnp.sum(jnp.where(lane == idx, a, 0.0), axis=1, keepdims=True)


def _rms(x, gain):
    ms = jnp.mean(x * x, axis=-1, keepdims=True)
    return x * lax.rsqrt(ms + EPS) * gain


def _bias_from_rel(rel, rb_ref, h):
    nb = N_BUCKETS // 2
    max_exact = nb // 2
    ret = jnp.where(rel > 0, nb, 0)
    n = jnp.abs(rel)
    nf = jnp.maximum(n, 1).astype(F32)
    large = max_exact + (jnp.log(nf / max_exact) / math.log(MAX_DISTANCE / max_exact)
                         * (nb - max_exact)).astype(jnp.int32)
    large = jnp.minimum(large, nb - 1)
    bucket = ret + jnp.where(n < max_exact, n, large)
    out = jnp.zeros(rel.shape, F32)
    for b in range(N_BUCKETS):
        out = jnp.where(bucket == b, rb_ref[b, h], out)
    return out


def _bias_kernel(rb_ref, diag_ref, prev_ref, meta_ref, mm_ref, samp_ref, *, past_len, dec_seq):
    h = pl.program_id(0)
    shift = CHUNK.bit_length() - 1
    T = ATT_TILE
    far = _bias_from_rel(jnp.full((8, LANES), -(2 * MAX_DISTANCE + 1), jnp.int32), rb_ref, h)[0:1, 0:1]

    qi = lax.broadcasted_iota(jnp.int32, (T, T), 0)
    kj = lax.broadcasted_iota(jnp.int32, (T, T), 1)
    bias = (_bias_from_rel(kj - qi, rb_ref, h) - far) * LOG2E
    diag_ref[...] = jnp.where((kj >> shift) <= (qi >> shift), bias, NEG)
    prev_ref[...] = (_bias_from_rel(kj - T - qi, rb_ref, h) - far) * LOG2E

    qi = lax.broadcasted_iota(jnp.int32, (T, LANES), 0)
    kj = lax.broadcasted_iota(jnp.int32, (T, LANES), 1)
    for var in range(2):
        bias = (_bias_from_rel(kj - (N_META + qi + var * T), rb_ref, h) - far) * LOG2E
        meta_ref[var] = jnp.where(kj < N_META, bias, NEG)

    qi = lax.broadcasted_iota(jnp.int32, (N_META, LANES), 0)
    kj = lax.broadcasted_iota(jnp.int32, (N_META, LANES), 1)
    mm_ref[...] = jnp.where(kj < N_META, _bias_from_rel(kj - qi, rb_ref, h) * LOG2E, NEG)

    width = samp_ref.shape[-1]
    qi = lax.broadcasted_iota(jnp.int32, (dec_seq, width), 0) + past_len
    kj = lax.broadcasted_iota(jnp.int32, (dec_seq, width), 1)
    bias = _bias_from_rel(kj - qi, rb_ref, h) * LOG2E
    ok = ((kj >> shift) <= (qi >> shift)) & (kj < past_len + dec_seq)
    samp_ref[...] = jnp.where(ok, bias, NEG)


def _bias_tiles(rel_bias, past_len, dec_seq):
    T = ATT_TILE
    sw = past_len + LANES
    out_shape = (
        jax.ShapeDtypeStruct((H_ATT, T, T), F32),
        jax.ShapeDtypeStruct((H_ATT, T, T), F32),
        jax.ShapeDtypeStruct((H_ATT, 2, T, LANES), F32),
        jax.ShapeDtypeStruct((H_ATT, N_META, LANES), F32),
        jax.ShapeDtypeStruct((H_ATT, dec_seq, sw), F32),
    )
    out_specs = (
        pl.BlockSpec((None, T, T), lambda h: (h, 0, 0)),
        pl.BlockSpec((None, T, T), lambda h: (h, 0, 0)),
        pl.BlockSpec((None, 2, T, LANES), lambda h: (h, 0, 0, 0)),
        pl.BlockSpec((None, N_META, LANES), lambda h: (h, 0, 0)),
        pl.BlockSpec((None, dec_seq, sw), lambda h: (h, 0, 0)),
    )
    return pl.pallas_call(
        functools.partial(_bias_kernel, past_len=past_len, dec_seq=dec_seq),
        grid=(H_ATT,),
        in_specs=[pl.BlockSpec(memory_space=pltpu.SMEM)],
        out_specs=out_specs,
        out_shape=out_shape,
        name="rel_bias_tiles",
    )(rel_bias)


def _kv_copies(stage, sem, dst_prompt, dst_sample, i, *, tail, n_batch, frames_per_batch, n_sample_rows):
    if not tail:
        rows = stage.shape[0]
        tiles_per_batch = frames_per_batch // rows
        b = i // tiles_per_batch
        t = i - b * tiles_per_batch
        return [pltpu.make_async_copy(stage, dst_prompt.at[b, pl.ds(N_META + t * rows, rows)], sem.at[0])]
    copies = [pltpu.make_async_copy(stage.at[pl.ds(b * N_META, N_META)], dst_prompt.at[b, pl.ds(0, N_META)],
                                    sem.at[b]) for b in range(n_batch)]
    copies.append(pltpu.make_async_copy(stage.at[pl.ds(n_batch * N_META, n_sample_rows)], dst_sample,
                                        sem.at[n_batch]))
    return copies


def _inproj_kernel(xf_ref, xt_ref, g1_ref, wa_ref, wg_ref, gb_ref, qg_ref, kg_ref,
                   zf_ref, zt_ref, gf_ref, gt_ref, kp_ref, vp_ref, ks_ref, vs_ref,
                   xn_s, y2d, stage, sem, *, n_frame_tiles, n_batch, frames_per_batch, n_sample_rows):
    i = pl.program_id(0)
    j = pl.program_id(1)
    bn = wa_ref.shape[1]
    assert ATT_QK % bn == 0 and ATT_V % bn == 0 and M_QK % bn == 0 and M_V % bn == 0 and bn % DV_ATT == 0
    j_k = ATT_QK // bn
    j_v = j_k + ATT_QK // bn
    j_mq = j_v + ATT_V // bn
    j_mk = j_mq + M_QK // bn
    j_mv = j_mk + M_QK // bn
    j_og = j_mv + M_V // bn

    def tile(x_ref, z_ref, g_ref, tail):
        nrows = x_ref.shape[0]
        st = stage.at[pl.ds(0, nrows)]
        copies = functools.partial(_kv_copies, st, sem, i=i, tail=tail, n_batch=n_batch,
                                   frames_per_batch=frames_per_batch, n_sample_rows=n_sample_rows)

        @pl.when(j == 0)
        def _():
            xb = _rms(x_ref[...], g1_ref[...]).astype(BF16)
            xn_s[0:nrows, :] = xb
            gates = _dot_nt(xb, wg_ref[0].astype(BF16)) + gb_ref[...]
            lane = lax.broadcasted_iota(jnp.int32, gates.shape, 1)
            g_ref[...] = jnp.where(lane < H_M, gates, _log_sigmoid(gates))

        acc = _dot_nt(xn_s[0:nrows, :], wa_ref[0].astype(BF16))

        def group_norm(gain_ref, scale):
            parts = []
            for gi in range(bn // DQK_ATT):
                y = _rms(acc[:, gi * DQK_ATT:(gi + 1) * DQK_ATT], gain_ref[...])
                parts.append(y if scale is None else y * scale)
            return jnp.concatenate(parts, axis=1)

        def stage_heads(y, jj, j0, dst_prompt, dst_sample):
            c0 = (jj - j0) * bn
            y2d[0:nrows, c0:c0 + bn] = y
            if jj == j0 + ATT_QK // bn - 1:
                if j0 != j_k:
                    for c in copies(kp_ref, ks_ref):
                        c.wait()
                st[...] = y2d[0:nrows, :].reshape(nrows, H_ATT, DV_ATT)
                for c in copies(dst_prompt, dst_sample):
                    c.start()

        @pl.when(j < j_k)
        def _():
            z_ref[...] = group_norm(qg_ref, DQK_ATT ** -0.5 * LOG2E)

        for jj in range(j_k, j_v):
            @pl.when(j == jj)
            def _(jj=jj):
                y = group_norm(kg_ref, None)
                z_ref[...] = y
                stage_heads(y, jj, j_k, kp_ref, ks_ref)

        for jj in range(j_v, j_mq):
            @pl.when(j == jj)
            def _(jj=jj):
                z_ref[...] = acc
                stage_heads(acc, jj, j_v, vp_ref, vs_ref)

        @pl.when(j == j_mq)
        def _():
            for c in copies(vp_ref, vs_ref):
                c.wait()

        @pl.when(((j >= j_mq) & (j < j_mk)) | ((j >= j_mv) & (j < j_og)))
        def _():
            z_ref[...] = acc

        @pl.when((j >= j_mk) & (j < j_mv))
        def _():
            z_ref[...] = acc * (DQK_M ** -0.5)

        @pl.when(j >= j_og)
        def _():
            z_ref[...] = jax.nn.sigmoid(acc)

    @pl.when(i < n_frame_tiles)
    def _():
        tile(xf_ref, zf_ref, gf_ref, False)

    @pl.when(i == n_frame_tiles)
    def _():
        tile(xt_ref, zt_ref, gt_ref, True)


def _inproj(x_f, x_t, g1, w_all, layer, gate_bias, qg, kg, n_batch, n_sample_streams):
    NF, D = x_f.shape
    NT = x_t.shape[0]
    bm, bn = PROJ_ROWS, PROJ_COLS
    nft = NF // bm
    nj = N_MAIN // bn
    g_row = N_MAIN - M_V
    assert g_row % bn == 0 and g_row % 8 == 0 and w_all.shape[1] >= g_row + LANES

    def w_window(i, j):
        row = j * bn + jnp.where(j * bn >= g_row, 2 * H_M, 0)
        return (layer, pl.multiple_of(row, 8), 0)

    S = NF // n_batch
    L = N_META + S
    n_sample_rows = n_sample_streams * N_META
    assert NF % bm == 0 and S % bm == 0 and N_MAIN % bn == 0 and NT <= bm
    fr = lambda i: jnp.minimum(i, nft - 1)
    any_spec = pl.BlockSpec(memory_space=pl.ANY)
    return pl.pallas_call(
        functools.partial(_inproj_kernel, n_frame_tiles=nft, n_batch=n_batch, frames_per_batch=S,
                          n_sample_rows=n_sample_rows),
        grid=(nft + 1, nj),
        in_specs=[
            pl.BlockSpec((bm, D), lambda i, j: (fr(i), 0)),
            pl.BlockSpec((NT, D), lambda i, j: (0, 0)),
            pl.BlockSpec((1, D), lambda i, j: (0, 0)),
            pl.BlockSpec((pl.Element(1), pl.Element(bn), pl.Element(D)), w_window),
            pl.BlockSpec((pl.Element(1), pl.Element(LANES), pl.Element(D)), lambda i, j: (layer, g_row, 0)),
            pl.BlockSpec((1, LANES), lambda i, j: (0, 0)),
            pl.BlockSpec((1, DQK_ATT), lambda i, j: (0, 0)),
            pl.BlockSpec((1, DQK_ATT), lambda i, j: (0, 0)),
        ],
        out_specs=(
            pl.BlockSpec((bm, bn), lambda i, j: (fr(i), jnp.where(i < nft, j, nj - 1))),
            pl.BlockSpec((NT, bn), lambda i, j: (0, jnp.where(i < nft, 0, j))),
            pl.BlockSpec((bm, LANES), lambda i, j: (fr(i), 0)),
            pl.BlockSpec((NT, LANES), lambda i, j: (0, 0)),
            any_spec, any_spec, any_spec, any_spec,
        ),
        out_shape=(
            jax.ShapeDtypeStruct((NF, N_MAIN), F32),
            jax.ShapeDtypeStruct((NT, N_MAIN), F32),
            jax.ShapeDtypeStruct((NF, LANES), F32),
            jax.ShapeDtypeStruct((NT, LANES), F32),
            jax.ShapeDtypeStruct((n_batch, L, H_ATT, 2 * DQK_ATT), F32),
            jax.ShapeDtypeStruct((n_batch, L, H_ATT, DV_ATT), F32),
            jax.ShapeDtypeStruct((n_sample_rows, H_ATT, 2 * DQK_ATT), F32),
            jax.ShapeDtypeStruct((n_sample_rows, H_ATT, DV_ATT), F32),
        ),
        scratch_shapes=[
            pltpu.VMEM((bm, D), BF16),
            pltpu.VMEM((bm, ATT_QK), F32),
            pltpu.VMEM((bm, H_ATT, DV_ATT), F32),
            pltpu.SemaphoreType.DMA((n_batch + 1,)),
        ],
        compiler_params=pltpu.CompilerParams(
            dimension_semantics=("arbitrary", "arbitrary"), vmem_limit_bytes=VMEM_LIMIT),
        name="in_proj",
    )(x_f, x_t, g1, w_all, w_all, gate_bias, qg, kg)


def _diff_lambda(lq1_ref, lk1_ref, lq2_ref, lk2_ref, lam_init):
    a = jnp.sum(lq1_ref[...] * lk1_ref[...], axis=-1, keepdims=True)
    b = jnp.sum(lq2_ref[...] * lk2_ref[...], axis=-1, keepdims=True)
    return jnp.exp(a) - jnp.exp(b) + lam_init


def _subln(o, sg_ref, lam_init):
    return _rms(o, sg_ref[...]) * (1.0 - lam_init)


def _pad_rows(dst, src):
    dst[...] = jnp.zeros(dst.shape, dst.dtype)
    dst[0:src.shape[0], :] = src[...].astype(dst.dtype)


def _attn_prompt_kernel(q_ref, k_ref, v_ref, km_ref, vm_ref, bd_ref, bp_ref, bm_ref,
                        lq1_ref, lk1_ref, lq2_ref, lk2_ref, sg_ref, o_ref,
                        kb_s, vb_s, kmb_s, vmb_s, *, lam_init):
    T = ATT_TILE
    n_tiles = q_ref.shape[0] // T
    kb_s[...] = k_ref[...].astype(BF16)
    vb_s[...] = v_ref[...].astype(BF16)
    _pad_rows(kmb_s, km_ref)
    _pad_rows(vmb_s, vm_ref)
    lam = _diff_lambda(lq1_ref, lk1_ref, lq2_ref, lk2_ref, lam_init)

    for j in range(n_tiles):
        rows = slice(j * T, (j + 1) * T)
        regions = [(kmb_s, vmb_s, slice(0, LANES), bm_ref[min(j, 1)])]
        if j >= 2:
            regions.append((kb_s, vb_s, slice(0, (j - 1) * T), None))
        if j >= 1:
            regions.append((kb_s, vb_s, slice((j - 1) * T, j * T), bp_ref[...]))
        regions.append((kb_s, vb_s, rows, bd_ref[...]))

        probs, denom = [], []
        for c in range(2):
            cols = slice(c * DQK_ATT, (c + 1) * DQK_ATT)
            qc = q_ref[rows, cols].astype(BF16)
            ss = [_dot_nt(qc, kr[rs, cols]) for kr, _, rs, _ in regions]
            ss = [s if br is None else s + br for s, (_, _, _, br) in zip(ss, regions)]
            m = functools.reduce(jnp.maximum, [jnp.max(s, axis=-1, keepdims=True) for s in ss])
            pc = [jnp.exp2(s - m) for s in ss]
            denom.append(functools.reduce(jnp.add, [jnp.sum(p, axis=-1, keepdims=True) for p in pc]))
            probs.append(pc)
        a0 = 1.0 / denom[0]
        a1 = lam / denom[1]
        o = None
        for r, (_, vr, rs, _) in enumerate(regions):
            w = (probs[0][r] * a0 - probs[1][r] * a1).astype(BF16)
            t = _dot(w, vr[rs, :])
            o = t if o is None else o + t
        o_ref[rows, :] = _subln(o, sg_ref, lam_init)


def _attn_prompt(z_f, z_t, bias, lam_vecs, sg, B, S, lam_init):
    T = ATT_TILE
    qkw = 2 * DQK_ATT
    kcol = ATT_QK // qkw
    vcol = 2 * ATT_QK // DV_ATT
    b_diag, b_prev, b_meta, _, _ = bias
    vec = pl.BlockSpec((1, DQK_ATT), lambda b, h: (0, 0))
    return pl.pallas_call(
        functools.partial(_attn_prompt_kernel, lam_init=lam_init),
        grid=(B, H_ATT),
        in_specs=[
            pl.BlockSpec((S, qkw), lambda b, h: (b, h)),
            pl.BlockSpec((S, qkw), lambda b, h: (b, kcol + h)),
            pl.BlockSpec((S, DV_ATT), lambda b, h: (b, vcol + h)),
            pl.BlockSpec((N_META, qkw), lambda b, h: (b, kcol + h)),
            pl.BlockSpec((N_META, DV_ATT), lambda b, h: (b, vcol + h)),
            pl.BlockSpec((None, T, T), lambda b, h: (h, 0, 0)),
            pl.BlockSpec((None, T, T), lambda b, h: (h, 0, 0)),
            pl.BlockSpec((None, 2, T, LANES), lambda b, h: (h, 0, 0, 0)),
            vec, vec, vec, vec,
            pl.BlockSpec((1, DV_ATT), lambda b, h: (0, 0)),
        ],
        out_specs=pl.BlockSpec((S, DV_ATT), lambda b, h: (b, h)),
        out_shape=jax.ShapeDtypeStruct((B * S, ATT_V), F32),
        scratch_shapes=[
            pltpu.VMEM((S, qkw), BF16),
            pltpu.VMEM((S, DV_ATT), BF16),
            pltpu.VMEM((LANES, qkw), BF16),
            pltpu.VMEM((LANES, DV_ATT), BF16),
        ],
        compiler_params=pltpu.CompilerParams(
            dimension_semantics=("arbitrary", "arbitrary"), vmem_limit_bytes=VMEM_LIMIT),
        name="attn_prompt",
    )(z_f, z_f, z_f, z_t, z_t, b_diag, b_prev, b_meta, *lam_vecs, sg)


def _attn_small_kernel(*refs, lam_init, has_cache):
    if has_cache:
        (q_ref, kc_ref, vc_ref, kn_ref, vn_ref, b_ref,
         lq1_ref, lk1_ref, lq2_ref, lk2_ref, sg_ref, o_ref, kpad, vpad, kb_s, vb_s) = refs
        past_len = kb_s.shape[0]
        chunk = kc_ref.shape[0]
        c = pl.program_id(1)
        r0 = pl.multiple_of(c * chunk, chunk)
        kb_s[pl.ds(r0, chunk), :] = kc_ref[...].reshape(chunk, ATT_QK).astype(BF16)
        vb_s[pl.ds(r0, chunk), :] = vc_ref[...].reshape(chunk, ATT_V).astype(BF16)

        @pl.when(c == past_len // chunk - 1)
        def _():
            _attn_small_body(q_ref, kn_ref, vn_ref, b_ref, lq1_ref, lk1_ref, lq2_ref, lk2_ref, sg_ref, o_ref,
                             kpad, vpad, kb_s, vb_s, lam_init=lam_init, past_len=past_len)
    else:
        (q_ref, kn_ref, vn_ref, b_ref,
         lq1_ref, lk1_ref, lq2_ref, lk2_ref, sg_ref, o_ref, kpad, vpad) = refs
        _attn_small_body(q_ref, kn_ref, vn_ref, b_ref, lq1_ref, lk1_ref, lq2_ref, lk2_ref, sg_ref, o_ref,
                         kpad, vpad, None, None, lam_init=lam_init, past_len=0)


def _attn_small_body(q_ref, kn_ref, vn_ref, b_ref, lq1_ref, lk1_ref, lq2_ref, lk2_ref, sg_ref, o_ref,
                     kpad, vpad, kb_s, vb_s, *, lam_init, past_len):
    has_cache = past_len > 0
    lam = _diff_lambda(lq1_ref, lk1_ref, lq2_ref, lk2_ref, lam_init)
    _pad_rows(kpad, kn_ref)
    _pad_rows(vpad, vn_ref)
    for h in range(H_ATT):
        vsl = slice(h * DV_ATT, (h + 1) * DV_ATT)
        outs = []
        for c in range(2):
            sl = slice(h * 2 * DQK_ATT + c * DQK_ATT, h * 2 * DQK_ATT + (c + 1) * DQK_ATT)
            qc = q_ref[:, sl].astype(BF16)
            s_new = _dot_nt(qc, kpad[:, sl]) + b_ref[h, :, past_len:]
            m = jnp.max(s_new, axis=-1, keepdims=True)
            if has_cache:
                s_old = _dot_nt(qc, kb_s[:, sl]) + b_ref[h, :, :past_len]
                m = jnp.maximum(m, jnp.max(s_old, axis=-1, keepdims=True))
            p_new = jnp.exp2(s_new - m)
            l = jnp.sum(p_new, axis=-1, keepdims=True)
            o = _dot(p_new.astype(BF16), vpad[:, vsl])
            if has_cache:
                p_old = jnp.exp2(s_old - m)
                l = l + jnp.sum(p_old, axis=-1, keepdims=True)
                o = o + _dot(p_old.astype(BF16), vb_s[:, vsl])
            outs.append(o / l)
        o_ref[:, vsl] = _subln(outs[0] - lam * outs[1], sg_ref, lam_init)


def _attn_small(z_t, row0, n_streams, bias, lam_vecs, sg, lam_init, cache=None):
    vec = pl.BlockSpec((1, DQK_ATT), lambda b, c: (0, 0))
    in_specs = [pl.BlockSpec((N_META, ATT_QK), lambda b, c: (row0 + b, 0))]
    args = [z_t]
    scratch = [pltpu.VMEM((LANES, ATT_QK), BF16), pltpu.VMEM((LANES, ATT_V), BF16)]
    n_chunks = 1
    if cache is not None:
        ck, cv, layer = cache
        past_len = ck.shape[2]
        chunk = min(past_len, CACHE_CHUNK)
        assert past_len % chunk == 0
        n_chunks = past_len // chunk
        in_specs += [
            pl.BlockSpec((None, None, chunk, H_ATT, 2 * DQK_ATT), lambda b, c: (layer, b, c, 0, 0)),
            pl.BlockSpec((None, None, chunk, H_ATT, DV_ATT), lambda b, c: (layer, b, c, 0, 0))]
        args += [ck, cv]
        scratch += [pltpu.VMEM((past_len, ATT_QK), BF16), pltpu.VMEM((past_len, ATT_V), BF16)]
    in_specs += [
        pl.BlockSpec((N_META, ATT_QK), lambda b, c: (row0 + b, 1)),
        pl.BlockSpec((N_META, ATT_V), lambda b, c: (row0 + b, 2 * ATT_QK // ATT_V)),
        pl.BlockSpec(bias.shape, lambda b, c: (0, 0, 0)),
        vec, vec, vec, vec,
        pl.BlockSpec((1, DV_ATT), lambda b, c: (0, 0)),
    ]
    args += [z_t, z_t, bias, *lam_vecs, sg]
    return pl.pallas_call(
        functools.partial(_attn_small_kernel, lam_init=lam_init, has_cache=cache is not None),
        grid=(n_streams, n_chunks),
        in_specs=in_specs,
        out_specs=pl.BlockSpec((N_META, ATT_V), lambda b, c: (b, 0)),
        out_shape=jax.ShapeDtypeStruct((n_streams * N_META, ATT_V), F32),
        scratch_shapes=scratch,
        compiler_params=pltpu.CompilerParams(
            dimension_semantics=("arbitrary", "arbitrary"), vmem_limit_bytes=VMEM_LIMIT),
        name="attn_cached" if cache is not None else "attn_meta",
    )(*args)


def _mlstm_kernel(*refs, n_blocks, has_init):
    refs = list(refs)
    qh_ref, kh_ref, vh_ref, gh_ref = refs[:4]
    pos = 4
    if n_blocks:
        qf_ref, kf_ref, vf_ref, gf_ref = refs[pos:pos + 4]
        pos += 4
    if has_init:
        c0_ref, n0_ref, m0_ref = refs[pos:pos + 3]
        pos += 3
    hh_ref = refs[pos]
    pos += 1
    if n_blocks:
        hf_ref = refs[pos]
        pos += 1
    c_ref, n_ref, m_ref = refs[pos:pos + 3]
    qs, ks, vs, gs = refs[pos + 3:pos + 7]

    t = pl.program_id(1)
    T = M_TILE
    block = functools.partial(_mlstm_block, c_ref=c_ref, n_ref=n_ref, m_ref=m_ref)

    @pl.when(t == 0)
    def _():
        if has_init:
            c_ref[...] = c0_ref[...]
            n_ref[...] = n0_ref[...]
            m_ref[...] = m0_ref[...]
        else:
            c_ref[...] = jnp.zeros(c_ref.shape, F32)
            n_ref[...] = jnp.zeros(n_ref.shape, F32)
            m_ref[...] = jnp.zeros(m_ref.shape, F32)
        nh = qh_ref.shape[0]
        for dst, src in ((qs, qh_ref), (ks, kh_ref), (vs, vh_ref)):
            dst[...] = jnp.zeros(dst.shape, F32)
            dst[0:nh, :] = src[...]
        lane = lax.broadcasted_iota(jnp.int32, gs.shape, 1)
        gs[...] = jnp.where(lane < H_M, NEG, 0.0)
        gs[0:nh, :] = gh_ref[...]

        block(qs, ks, vs, gs, hh_ref, nh)

    if n_blocks:
        @pl.when(t > 0)
        def _():
            block(qf_ref, kf_ref, vf_ref, gf_ref, hf_ref, T)


def _mlstm_block(q_src, k_src, v_src, g_src, h_dst, n_out, c_ref, n_ref, m_ref):
    T = M_TILE
    g = g_src[...]
    row = lax.broadcasted_iota(jnp.int32, (T, T), 0)
    colid = lax.broadcasted_iota(jnp.int32, (T, T), 1)
    causal = row >= colid
    tril = jnp.where(causal, 1.0, 0.0).astype(F32)
    cum = jnp.dot(tril, g, preferred_element_type=F32, precision=lax.Precision.HIGHEST)
    g_t = g.T
    cum_t = cum.T
    last = cum[T - 1:T, :]

    for h in range(H_M):
        q = q_src[:, h * DQK_M:(h + 1) * DQK_M]
        k = k_src[:, h * DQK_M:(h + 1) * DQK_M]
        v = v_src[:, h * DV_M:(h + 1) * DV_M]
        qb = q.astype(BF16)
        kb = k.astype(BF16)
        m_old = m_ref[h, 0:1, 0:1]
        c_old = c_ref[h]
        n_old = n_ref[h]
        bh_c = _col(cum, H_M + h)
        ih_c = _col(g, h)
        bh_r = cum_t[H_M + h:H_M + h + 1, :]
        ih_r = g_t[h:h + 1, :]
        d = jnp.where(causal, bh_c - bh_r + ih_r, NEG)
        inter = bh_c + m_old
        mt = jnp.maximum(inter, jnp.max(d, axis=-1, keepdims=True))
        a = jnp.exp(d - mt) * _dot_nt(qb, kb)
        iw = jnp.exp(inter - mt)
        num = _dot(a.astype(BF16), v.astype(BF16)) + iw * _dot_nt(qb, c_old.astype(BF16))
        den = jnp.sum(a, axis=-1, keepdims=True) + iw * jnp.sum(q * n_old, axis=-1, keepdims=True)
        den = jnp.maximum(jnp.abs(den), jnp.exp(-mt))
        hout = num / den
        h_dst[:, h * DV_M:(h + 1) * DV_M] = hout[0:n_out, :]

        bl = _col(last, H_M + h)
        g_c = bl - bh_c + ih_c
        m_new = jnp.maximum(bl + m_old, jnp.max(g_c, axis=0, keepdims=True))
        sw = jnp.exp(g_c - m_new)
        dec = jnp.exp(bl + m_old - m_new)
        c_ref[h] = dec * c_old + _dot(v.T.astype(BF16), (sw * k).astype(BF16))
        n_ref[h] = dec * n_old + jnp.sum(sw * k, axis=0, keepdims=True)
        m_ref[h] = jnp.broadcast_to(m_new, m_ref.shape[1:])


def _mlstm(z_t, g_t, row0, n_streams, frames=None, init=None):
    T = M_TILE
    qcol = (2 * ATT_QK + ATT_V) // M_QK
    vcol = (2 * ATT_QK + ATT_V + 2 * M_QK) // M_V
    nh = N_META
    in_specs = [
        pl.BlockSpec((nh, M_QK), lambda b, t: (row0 + b, qcol)),
        pl.BlockSpec((nh, M_QK), lambda b, t: (row0 + b, qcol + 1)),
        pl.BlockSpec((nh, M_V), lambda b, t: (row0 + b, vcol)),
        pl.BlockSpec((nh, LANES), lambda b, t: (row0 + b, 0)),
    ]
    args = [z_t, z_t, z_t, g_t]
    n_blocks = 0
    if frames is not None:
        z_f, g_f, fps = frames
        n_blocks = fps // T
        fidx = lambda b, t: b * n_blocks + jnp.maximum(t - 1, 0)
        in_specs += [
            pl.BlockSpec((T, M_QK), lambda b, t: (fidx(b, t), qcol)),
            pl.BlockSpec((T, M_QK), lambda b, t: (fidx(b, t), qcol + 1)),
            pl.BlockSpec((T, M_V), lambda b, t: (fidx(b, t), vcol)),
            pl.BlockSpec((T, LANES), lambda b, t: (fidx(b, t), 0)),
        ]
        args += [z_f, z_f, z_f, g_f]
    state_specs = (
        pl.BlockSpec((None, H_M, DV_M, DQK_M), lambda b, t: (b, 0, 0, 0)),
        pl.BlockSpec((None, H_M, 1, DQK_M), lambda b, t: (b, 0, 0, 0)),
        pl.BlockSpec((None, H_M, 8, LANES), lambda b, t: (b, 0, 0, 0)),
    )
    if init is not None:
        in_specs += list(state_specs)
        args += list(init)
    out_specs = [pl.BlockSpec((nh, M_V), lambda b, t: (b, 0))]
    out_shape = [jax.ShapeDtypeStruct((n_streams * nh, M_V), F32)]
    if n_blocks:
        out_specs.append(pl.BlockSpec((T, M_V), lambda b, t: (fidx(b, t), 0)))
        out_shape.append(jax.ShapeDtypeStruct((n_streams * fps, M_V), F32))
    out_specs += list(state_specs)
    out_shape += [
        jax.ShapeDtypeStruct((n_streams, H_M, DV_M, DQK_M), F32),
        jax.ShapeDtypeStruct((n_streams, H_M, 1, DQK_M), F32),
        jax.ShapeDtypeStruct((n_streams, H_M, 8, LANES), F32),
    ]
    return pl.pallas_call(
        functools.partial(_mlstm_kernel, n_blocks=n_blocks, has_init=init is not None),
        grid=(n_streams, 1 + n_blocks),
        in_specs=in_specs,
        out_specs=tuple(out_specs),
        out_shape=tuple(out_shape),
        scratch_shapes=[
            pltpu.VMEM((T, M_QK), F32), pltpu.VMEM((T, M_QK), F32),
            pltpu.VMEM((T, M_V), F32), pltpu.VMEM((T, LANES), F32),
        ],
        compiler_params=pltpu.CompilerParams(
            dimension_semantics=("arbitrary", "arbitrary"), vmem_limit_bytes=VMEM_LIMIT),
        name="mlstm_prompt" if n_blocks else "mlstm_cached",
    )(*args)


def _merge_kernel(xf_ref, xt_ref, attf_ref, attt_ref, hmf_ref, hmt_ref, zf_ref, zt_ref, mg_ref, wo_ref,
                  of_ref, ot_ref, *, n_frame_tiles):
    i = pl.program_id(0)

    def tile(x_ref, att_ref, hm_ref, og_ref, o_ref):
        parts = []
        for h in range(H_M):
            sl = slice(h * DV_M, (h + 1) * DV_M)
            parts.append(_rms(hm_ref[:, sl], mg_ref[:, sl]) * og_ref[:, sl])
        hmn = jnp.concatenate(parts, axis=1).astype(BF16)
        o_ref[...] = (x_ref[...] + _dot(att_ref[...].astype(BF16), wo_ref[0:ATT_V, :])
                      + _dot(hmn, wo_ref[ATT_V:, :]))

    @pl.when(i < n_frame_tiles)
    def _():
        tile(xf_ref, attf_ref, hmf_ref, zf_ref, of_ref)

    @pl.when(i == n_frame_tiles)
    def _():
        tile(xt_ref, attt_ref, hmt_ref, zt_ref, ot_ref)


def _merge(x_f, x_t, att_f, att_t, hm_f, hm_t, z_f, z_t, mg, w_out, layer):
    NF, D = x_f.shape
    NT = x_t.shape[0]
    bm = MERGE_ROWS
    nft = NF // bm
    assert NF % bm == 0
    ogcol = (N_MAIN - M_V) // M_V
    fr = lambda i: (jnp.minimum(i, nft - 1), 0)
    tl = lambda i: (0, 0)
    once = dict(pipeline_mode=pl.Buffered(1))
    return pl.pallas_call(
        functools.partial(_merge_kernel, n_frame_tiles=nft),
        grid=(nft + 1,),
        in_specs=[
            pl.BlockSpec((bm, D), fr), pl.BlockSpec((NT, D), tl, **once),
            pl.BlockSpec((bm, ATT_V), fr), pl.BlockSpec((NT, ATT_V), tl, **once),
            pl.BlockSpec((bm, M_V), fr), pl.BlockSpec((NT, M_V), tl, **once),
            pl.BlockSpec((bm, M_V), lambda i: (jnp.minimum(i, nft - 1), ogcol)),
            pl.BlockSpec((NT, M_V), lambda i: (0, ogcol), **once),
            pl.BlockSpec((1, M_V), tl),
            pl.BlockSpec((None, D, D), lambda i: (layer, 0, 0), **once),
        ],
        out_specs=(pl.BlockSpec((bm, D), fr), pl.BlockSpec((NT, D), tl)),
        out_shape=(jax.ShapeDtypeStruct((NF, D), F32), jax.ShapeDtypeStruct((NT, D), F32)),
        compiler_params=pltpu.CompilerParams(
            dimension_semantics=("arbitrary",), vmem_limit_bytes=VMEM_LIMIT),
        name="merge_out_proj",
    )(x_f, x_t, att_f, att_t, hm_f, hm_t, z_f, z_t, mg, w_out)


def _ffn_kernel(xf_ref, xt_ref, g2_ref, wu_ref, wd_ref, of_ref, ot_ref, xn_s, *, n_frame_tiles):
    i = pl.program_id(0)
    k = pl.program_id(1)

    def tile(x_ref, o_ref):
        nrows = x_ref.shape[0]

        @pl.when(k == 0)
        def _():
            x = x_ref[...]
            xn_s[0:nrows, :] = _rms(x, g2_ref[...]).astype(BF16)
            o_ref[...] = x

        u = jnp.maximum(_dot(xn_s[0:nrows, :], wu_ref[...]), 0.0)
        o_ref[...] += _dot((u * u).astype(BF16), wd_ref[...])

    @pl.when(i < n_frame_tiles)
    def _():
        tile(xf_ref, of_ref)

    @pl.when(i == n_frame_tiles)
    def _():
        tile(xt_ref, ot_ref)


def _ffn(x_f, x_t, g2, w_up, w_down, layer):
    NF, D = x_f.shape
    NT = x_t.shape[0]
    FF = w_up.shape[2]
    bm, fc = FF_ROWS, FF_COLS
    nft = NF // bm
    assert NF % bm == 0 and FF % fc == 0 and NT <= bm
    fr = lambda i, k: (jnp.minimum(i, nft - 1), 0)
    tl = lambda i, k: (0, 0)
    return pl.pallas_call(
        functools.partial(_ffn_kernel, n_frame_tiles=nft),
        grid=(nft + 1, FF // fc),
        in_specs=[
            pl.BlockSpec((bm, D), fr),
            pl.BlockSpec((NT, D), tl),
            pl.BlockSpec((1, D), tl),
            pl.BlockSpec((None, D, fc), lambda i, k: (layer, 0, k)),
            pl.BlockSpec((None, fc, D), lambda i, k: (layer, k, 0)),
        ],
        out_specs=(pl.BlockSpec((bm, D), fr), pl.BlockSpec((NT, D), tl)),
        out_shape=(jax.ShapeDtypeStruct((NF, D), F32), jax.ShapeDtypeStruct((NT, D), F32)),
        scratch_shapes=[pltpu.VMEM((bm, D), BF16)],
        compiler_params=pltpu.CompilerParams(
            dimension_semantics=("arbitrary", "arbitrary"), vmem_limit_bytes=VMEM_LIMIT),
        name="ffn",
    )(x_f, x_t, g2, w_up, w_down)


def kernel(x_prompt, x_sample, cache_k, cache_v, state_C, state_n, state_m, meta_tokens, rel_bias, norm1_g, w_in, b_i, b_f, q_norm_g, k_norm_g, lambda_q1, lambda_k1, lambda_q2, lambda_k2, subln_g, mnorm_g, w_out, norm2_g, w_up, w_down):
    B, S, D = x_prompt.shape
    Bd, Td, _ = x_sample.shape
    depth = w_in.shape[0]
    Pc = cache_k.shape[2]
    assert Td == N_META and S % ATT_TILE == 0 and S % M_TILE == 0 and Pc % LANES == 0
    assert ATT_TILE % CHUNK == 0 and CHUNK & (CHUNK - 1) == 0 and ATT_TILE >= MAX_DISTANCE

    n_frames = B * S
    n_meta = B * N_META
    n_samp = Bd * Td
    assert n_meta + n_samp <= TAIL_ROWS
    srow = n_meta // N_META

    meta = jnp.broadcast_to(meta_tokens.astype(F32)[None], (B, N_META, D)).reshape(n_meta, D)
    x_f = x_prompt.astype(F32).reshape(n_frames, D)
    x_t = jnp.concatenate([meta, x_sample.astype(F32).reshape(n_samp, D),
                           jnp.zeros((TAIL_ROWS - n_meta - n_samp, D), F32)], axis=0)

    bias = _bias_tiles(rel_bias.astype(F32), Pc, Td)
    ck = cache_k.astype(F32)
    cv = cache_v.astype(F32)
    zeros_pad_att = jnp.zeros((TAIL_ROWS - n_meta - n_samp, ATT_V), F32)

    g_off = 2 * ATT_QK + ATT_V + 2 * M_QK + M_V
    w_in_t = jnp.swapaxes(w_in.astype(F32), 1, 2)
    w_out_b, w_up_b, w_down_b = (w.astype(BF16) for w in (w_out, w_up, w_down))
    outs = {name: [] for name in ("kp", "vp", "Cp", "np", "mp", "ks", "vs", "Cs", "ns", "ms")}

    for l in range(depth):
        lam_init = 0.8 - 0.6 * math.exp(-0.3 * l)
        gate_bias = jnp.pad(jnp.concatenate([b_i[l], b_f[l]]).astype(F32), (0, LANES - 2 * H_M))[None]
        lam_vecs = [v[l].astype(F32)[None] for v in (lambda_q1, lambda_k1, lambda_q2, lambda_k2)]
        sg = subln_g[l].astype(F32)[None]

        z_f, z_t, g_f, g_t, k_p, v_p, k_s, v_s = _inproj(
            x_f, x_t, norm1_g[l].astype(F32)[None], w_in_t, l, gate_bias,
            q_norm_g[l].astype(F32)[None], k_norm_g[l].astype(F32)[None], B, Bd)

        att_f = _attn_prompt(z_f, z_t, bias, lam_vecs, sg, B, S, lam_init)
        att_m = _attn_small(z_t, 0, B, bias[3], lam_vecs, sg, lam_init)
        att_s = _attn_small(z_t, srow, Bd, bias[4], lam_vecs, sg, lam_init, cache=(ck, cv, l))
        att_t = jnp.concatenate([att_m, att_s, zeros_pad_att], axis=0)

        hm_m, hm_f, C_p, n_p, m_p = _mlstm(z_t, g_t, 0, B, frames=(z_f, g_f, S))
        init = (state_C[l].astype(F32), state_n[l].astype(F32)[:, :, None, :],
                jnp.broadcast_to(state_m[l].astype(F32)[:, :, None, None], (Bd, H_M, 8, LANES)))
        hm_s, C_s, n_s, m_s = _mlstm(z_t, g_t, srow, Bd, init=init)
        hm_t = jnp.concatenate([hm_m, hm_s, zeros_pad_att], axis=0)

        x_f, x_t = _merge(x_f, x_t, att_f, att_t, hm_f, hm_t, z_f, z_t,
                          mnorm_g[l].astype(F32)[None], w_out_b, l)
        x_f, x_t = _ffn(x_f, x_t, norm2_g[l].astype(F32)[None], w_up_b, w_down_b, l)

        outs["kp"].append(k_p)
        outs["vp"].append(v_p)
        outs["Cp"].append(C_p)
        outs["np"].append(n_p[:, :, 0, :])
        outs["mp"].append(m_p[:, :, 0, 0])
        outs["ks"].append(k_s.reshape(Bd, Td, H_ATT, 2 * DQK_ATT))
        outs["vs"].append(v_s.reshape(Bd, Td, H_ATT, DV_ATT))
        outs["Cs"].append(C_s)
        outs["ns"].append(n_s[:, :, 0, :])
        outs["ms"].append(m_s[:, :, 0, 0])

    y_prompt = x_f.reshape(B, S, D)
    y_sample = x_t[n_meta:n_meta + n_samp].reshape(Bd, Td, D)
    return (y_prompt, y_sample,
            jnp.stack(outs["kp"]), jnp.stack(outs["vp"]), jnp.stack(outs["Cp"]), jnp.stack(outs["np"]),
            jnp.stack(outs["mp"]),
            jnp.stack(outs["ks"]), jnp.stack(outs["vs"]), jnp.stack(outs["Cs"]), jnp.stack(outs["ns"]),
            jnp.stack(outs["ms"]))
```

```python
import functools
import math

import jax
import jax.numpy as jnp
from jax import lax
from jax.experimental import pallas as pl
from jax.experimental.pallas import tpu as pltpu

F32 = jnp.float32
BF16 = jnp.bfloat16

CHUNK = 64
N_META = 16
H_ATT = 4
DQK_ATT = 128
DV_ATT = 256
H_M = 4
DQK_M = 128
DV_M = 256
N_BUCKETS = 32
MAX_DISTANCE = 128
EPS = 1e-6
NEG = -1e30
LOG2E = math.log2(math.e)

ATT_QK = H_ATT * 2 * DQK_ATT
ATT_V = H_ATT * DV_ATT
M_QK = H_M * DQK_M
M_V = H_M * DV_M
N_MAIN = 2 * ATT_QK + ATT_V + 2 * M_QK + M_V + M_V

LANES = 128
ATT_TILE = 256
M_TILE = 256
TAIL_ROWS = 256
PROJ_ROWS = 1024
PROJ_COLS = 512
MERGE_ROWS = 512
CACHE_CHUNK = 1024
FF_ROWS = 512
FF_COLS = 1024
VMEM_LIMIT = 56 * 1024 * 1024

_NT = (((1,), (1,)), ((), ()))


def _dot(a, b):
    return jnp.dot(a, b, preferred_element_type=F32)


def _dot_nt(a, b):
    return lax.dot_general(a, b, _NT, preferred_element_type=F32)


def _log_sigmoid(x):
    return jnp.minimum(x, 0.0) - jnp.log1p(jnp.exp(-jnp.abs(x)))


def _col(a, idx):
    lane = lax.broadcasted_iota(jnp.int32, a.shape, 1)
    return jnp.sum(jnp.where(lane == idx, a, 0.0), axis=1, keepdims=True)


def _rms(x, gain):
    ms = jnp.mean(x * x, axis=-1, keepdims=True)
    return x * lax.rsqrt(ms + EPS) * gain


def _bias_from_rel(rel, rb_ref, h):
    nb = N_BUCKETS // 2
    max_exact = nb // 2
    ret = jnp.where(rel > 0, nb, 0)
    n = jnp.abs(rel)
    nf = jnp.maximum(n, 1).astype(F32)
    large = max_exact + (jnp.log(nf / max_exact) / math.log(MAX_DISTANCE / max_exact)
                         * (nb - max_exact)).astype(jnp.int32)
    large = jnp.minimum(large, nb - 1)
    bucket = ret + jnp.where(n < max_exact, n, large)
    out = jnp.zeros(rel.shape, F32)
    for b in range(N_BUCKETS):
        out = jnp.where(bucket == b, rb_ref[b, h], out)
    return out


def _bias_kernel(rb_ref, diag_ref, prev_ref, meta_ref, mm_ref, samp_ref, *, past_len, dec_seq):
    h = pl.program_id(0)
    shift = CHUNK.bit_length() - 1
    T = ATT_TILE
    far = _bias_from_rel(jnp.full((8, LANES), -(2 * MAX_DISTANCE + 1), jnp.int32), rb_ref, h)[0:1, 0:1]

    qi = lax.broadcasted_iota(jnp.int32, (T, T), 0)
    kj = lax.broadcasted_iota(jnp.int32, (T, T), 1)
    bias = (_bias_from_rel(kj - qi, rb_ref, h) - far) * LOG2E
    diag_ref[...] = jnp.where((kj >> shift) <= (qi >> shift), bias, NEG)
    prev_ref[...] = (_bias_from_rel(kj - T - qi, rb_ref, h) - far) * LOG2E

    qi = lax.broadcasted_iota(jnp.int32, (T, LANES), 0)
    kj = lax.broadcasted_iota(jnp.int32, (T, LANES), 1)
    for var in range(2):
        bias = (_bias_from_rel(kj - (N_META + qi + var * T), rb_ref, h) - far) * LOG2E
        meta_ref[var] = jnp.where(kj < N_META, bias, NEG)

    qi = lax.broadcasted_iota(jnp.int32, (N_META, LANES), 0)
    kj = lax.broadcasted_iota(jnp.int32, (N_META, LANES), 1)
    mm_ref[...] = jnp.where(kj < N_META, _bias_from_rel(kj - qi, rb_ref, h) * LOG2E, NEG)

    width = samp_ref.shape[-1]
    qi = lax.broadcasted_iota(jnp.int32, (dec_seq, width), 0) + past_len
    kj = lax.broadcasted_iota(jnp.int32, (dec_seq, width), 1)
    bias = _bias_from_rel(kj - qi, rb_ref, h) * LOG2E
    ok = ((kj >> shift) <= (qi >> shift)) & (kj < past_len + dec_seq)
    samp_ref[...] = jnp.where(ok, bias, NEG)


def _bias_tiles(rel_bias, past_len, dec_seq):
    T = ATT_TILE
    sw = past_len + LANES
    out_shape = (
        jax.ShapeDtypeStruct((H_ATT, T, T), F32),
        jax.ShapeDtypeStruct((H_ATT, T, T), F32),
        jax.ShapeDtypeStruct((H_ATT, 2, T, LANES), F32),
        jax.ShapeDtypeStruct((H_ATT, N_META, LANES), F32),
        jax.ShapeDtypeStruct((H_ATT, dec_seq, sw), F32),
    )
    out_specs = (
        pl.BlockSpec((None, T, T), lambda h: (h, 0, 0)),
        pl.BlockSpec((None, T, T), lambda h: (h, 0, 0)),
        pl.BlockSpec((None, 2, T, LANES), lambda h: (h, 0, 0, 0)),
        pl.BlockSpec((None, N_META, LANES), lambda h: (h, 0, 0)),
        pl.BlockSpec((None, dec_seq, sw), lambda h: (h, 0, 0)),
    )
    return pl.pallas_call(
        functools.partial(_bias_kernel, past_len=past_len, dec_seq=dec_seq),
        grid=(H_ATT,),
        in_specs=[pl.BlockSpec(memory_space=pltpu.SMEM)],
        out_specs=out_specs,
        out_shape=out_shape,
        name="rel_bias_tiles",
    )(rel_bias)


def _zero_fill_copies(zbuf, zsem, cache_refs, depth):
    rows = zbuf.shape[0]
    copies = []
    for dst in cache_refs:
        lead = dst.shape[1:-3]
        n_rows = dst.shape[-3]
        for l in range(1, depth):
            for b in range(lead[0] if lead else 1):
                view = dst.at[l, b] if lead else dst.at[l]
                for r0 in range(0, n_rows, rows):
                    n = min(rows, n_rows - r0)
                    copies.append(pltpu.make_async_copy(zbuf.at[pl.ds(0, n)], view.at[pl.ds(r0, n)],
                                                        zsem.at[len(copies)]))
    return copies


def _kv_copies(stage, sem, dst_prompt, dst_sample, i, *, layer, tail, n_batch, frames_per_batch, n_sample_rows):
    dst_prompt = dst_prompt.at[layer]
    dst_sample = dst_sample.at[layer]
    if not tail:
        rows = stage.shape[0]
        tiles_per_batch = frames_per_batch // rows
        b = i // tiles_per_batch
        t = i - b * tiles_per_batch
        return [pltpu.make_async_copy(stage, dst_prompt.at[b, pl.ds(N_META + t * rows, rows)], sem.at[0])]
    copies = [pltpu.make_async_copy(stage.at[pl.ds(b * N_META, N_META)], dst_prompt.at[b, pl.ds(0, N_META)],
                                    sem.at[b]) for b in range(n_batch)]
    copies.append(pltpu.make_async_copy(stage.at[pl.ds(n_batch * N_META, n_sample_rows)], dst_sample,
                                        sem.at[n_batch]))
    return copies


def _inproj_kernel(*refs, layer, depth, n_frame_tiles, n_batch, frames_per_batch, n_sample_rows):
    xf_ref, xt_ref, g1_ref, wa_ref, wg_ref, gb_ref, qg_ref, kg_ref = refs[:8]
    (zf_ref, zt_ref, gf_ref, gt_ref, kp_ref, vp_ref, ks_ref, vs_ref,
     xn_s, y2d, stage, sem, zbuf, zsem) = refs[-14:]
    i = pl.program_id(0)
    j = pl.program_id(1)
    bn = wa_ref.shape[1]

    if layer == 0 and depth > 1:
        zero_copies = _zero_fill_copies(zbuf, zsem, (kp_ref, vp_ref, ks_ref, vs_ref), depth)

        @pl.when((i == 0) & (j == 0))
        def _():
            zbuf[...] = jnp.zeros(zbuf.shape, F32)
            for c in zero_copies:
                c.start()

        @pl.when((i == pl.num_programs(0) - 1) & (j == pl.num_programs(1) - 1))
        def _():
            for c in zero_copies:
                c.wait()

    assert ATT_QK % bn == 0 and ATT_V % bn == 0 and M_QK % bn == 0 and M_V % bn == 0 and bn % DV_ATT == 0
    j_k = ATT_QK // bn
    j_v = j_k + ATT_QK // bn
    j_mq = j_v + ATT_V // bn
    j_mk = j_mq + M_QK // bn
    j_mv = j_mk + M_QK // bn
    j_og = j_mv + M_V // bn

    def tile(x_ref, z_ref, g_ref, tail):
        nrows = x_ref.shape[0]
        st = stage.at[pl.ds(0, nrows)]
        copies = functools.partial(_kv_copies, st, sem, i=i, layer=layer, tail=tail, n_batch=n_batch,
                                   frames_per_batch=frames_per_batch, n_sample_rows=n_sample_rows)

        @pl.when(j == 0)
        def _():
            xb = _rms(x_ref[...], g1_ref[...]).astype(BF16)
            xn_s[0:nrows, :] = xb
            gates = _dot_nt(xb, wg_ref[0].astype(BF16)) + gb_ref[...]
            lane = lax.broadcasted_iota(jnp.int32, gates.shape, 1)
            g_ref[...] = jnp.where(lane < H_M, gates, _log_sigmoid(gates))

        acc = _dot_nt(xn_s[0:nrows, :], wa_ref[0].astype(BF16))

        def group_norm(gain_ref, scale):
            parts = []
            for gi in range(bn // DQK_ATT):
                y = _rms(acc[:, gi * DQK_ATT:(gi + 1) * DQK_ATT], gain_ref[...])
                parts.append(y if scale is None else y * scale)
            return jnp.concatenate(parts, axis=1)

        def stage_heads(y, jj, j0, dst_prompt, dst_sample):
            c0 = (jj - j0) * bn
            y2d[0:nrows, c0:c0 + bn] = y
            if jj == j0 + ATT_QK // bn - 1:
                if j0 != j_k:
                    for c in copies(kp_ref, ks_ref):
                        c.wait()
                st[...] = y2d[0:nrows, :].reshape(nrows, H_ATT, DV_ATT)
                for c in copies(dst_prompt, dst_sample):
                    c.start()

        @pl.when(j < j_k)
        def _():
            z_ref[...] = group_norm(qg_ref, DQK_ATT ** -0.5 * LOG2E)

        for jj in range(j_k, j_v):
            @pl.when(j == jj)
            def _(jj=jj):
                y = group_norm(kg_ref, None)
                z_ref[...] = y
                stage_heads(y, jj, j_k, kp_ref, ks_ref)

        for jj in range(j_v, j_mq):
            @pl.when(j == jj)
            def _(jj=jj):
                z_ref[...] = acc
                stage_heads(acc, jj, j_v, vp_ref, vs_ref)

        @pl.when(j == j_mq)
        def _():
            for c in copies(vp_ref, vs_ref):
                c.wait()

        @pl.when(((j >= j_mq) & (j < j_mk)) | ((j >= j_mv) & (j < j_og)))
        def _():
            z_ref[...] = acc

        @pl.when((j >= j_mk) & (j < j_mv))
        def _():
            z_ref[...] = acc * (DQK_M ** -0.5)

        @pl.when(j >= j_og)
        def _():
            z_ref[...] = jax.nn.sigmoid(acc)

    @pl.when(i < n_frame_tiles)
    def _():
        tile(xf_ref, zf_ref, gf_ref, False)

    @pl.when(i == n_frame_tiles)
    def _():
        tile(xt_ref, zt_ref, gt_ref, True)


def _inproj(x_f, x_t, g1, w_all, layer, gate_bias, qg, kg, n_batch, n_sample_streams, caches):
    depth = w_all.shape[0]
    NF, D = x_f.shape
    NT = x_t.shape[0]
    bm, bn = PROJ_ROWS, PROJ_COLS
    nft = NF // bm
    nj = N_MAIN // bn
    g_row = N_MAIN - M_V
    assert g_row % bn == 0 and g_row % 8 == 0 and w_all.shape[1] >= g_row + LANES

    def w_window(i, j):
        row = j * bn + jnp.where(j * bn >= g_row, 2 * H_M, 0)
        return (layer, pl.multiple_of(row, 8), 0)

    S = NF // n_batch
    L = N_META + S
    n_sample_rows = n_sample_streams * N_META
    assert NF % bm == 0 and S % bm == 0 and N_MAIN % bn == 0 and NT <= bm
    fr = lambda i: jnp.minimum(i, nft - 1)
    any_spec = pl.BlockSpec(memory_space=pl.ANY)
    assert (caches is None) == (layer == 0)
    prev = () if caches is None else tuple(caches)
    n_in = 8
    zero_rows = 256
    n_zero = (depth - 1) * (2 * n_batch * pl.cdiv(L, zero_rows) + 2 * pl.cdiv(n_sample_rows, zero_rows))
    return pl.pallas_call(
        functools.partial(_inproj_kernel, layer=layer, depth=depth, n_frame_tiles=nft, n_batch=n_batch,
                          frames_per_batch=S, n_sample_rows=n_sample_rows),
        grid=(nft + 1, nj),
        input_output_aliases={n_in + c: 4 + c for c in range(len(prev))},
        in_specs=[
            pl.BlockSpec((bm, D), lambda i, j: (fr(i), 0)),
            pl.BlockSpec((NT, D), lambda i, j: (0, 0)),
            pl.BlockSpec((1, D), lambda i, j: (0, 0)),
            pl.BlockSpec((pl.Element(1), pl.Element(bn), pl.Element(D)), w_window),
            pl.BlockSpec((pl.Element(1), pl.Element(LANES), pl.Element(D)), lambda i, j: (layer, g_row, 0)),
            pl.BlockSpec((1, LANES), lambda i, j: (0, 0)),
            pl.BlockSpec((1, DQK_ATT), lambda i, j: (0, 0)),
            pl.BlockSpec((1, DQK_ATT), lambda i, j: (0, 0)),
        ] + [any_spec] * len(prev),
        out_specs=(
            pl.BlockSpec((bm, bn), lambda i, j: (fr(i), jnp.where(i < nft, j, nj - 1))),
            pl.BlockSpec((NT, bn), lambda i, j: (0, jnp.where(i < nft, 0, j))),
            pl.BlockSpec((bm, LANES), lambda i, j: (fr(i), 0)),
            pl.BlockSpec((NT, LANES), lambda i, j: (0, 0)),
            any_spec, any_spec, any_spec, any_spec,
        ),
        out_shape=(
            jax.ShapeDtypeStruct((NF, N_MAIN), F32),
            jax.ShapeDtypeStruct((NT, N_MAIN), F32),
            jax.ShapeDtypeStruct((NF, LANES), F32),
            jax.ShapeDtypeStruct((NT, LANES), F32),
            jax.ShapeDtypeStruct((depth, n_batch, L, H_ATT, 2 * DQK_ATT), F32),
            jax.ShapeDtypeStruct((depth, n_batch, L, H_ATT, DV_ATT), F32),
            jax.ShapeDtypeStruct((depth, n_sample_rows, H_ATT, 2 * DQK_ATT), F32),
            jax.ShapeDtypeStruct((depth, n_sample_rows, H_ATT, DV_ATT), F32),
        ),
        scratch_shapes=[
            pltpu.VMEM((bm, D), BF16),
            pltpu.VMEM((bm, ATT_QK), F32),
            pltpu.VMEM((bm, H_ATT, DV_ATT), F32),
            pltpu.SemaphoreType.DMA((n_batch + 1,)),
            pltpu.VMEM((zero_rows, H_ATT, DV_ATT), F32),
            pltpu.SemaphoreType.DMA((max(n_zero, 1),)),
        ],
        compiler_params=pltpu.CompilerParams(
            dimension_semantics=("arbitrary", "arbitrary"), vmem_limit_bytes=VMEM_LIMIT),
        name="in_proj",
    )(x_f, x_t, g1, w_all, w_all, gate_bias, qg, kg, *prev)


def _diff_lambda(lq1_ref, lk1_ref, lq2_ref, lk2_ref, lam_init):
    a = jnp.sum(lq1_ref[...] * lk1_ref[...], axis=-1, keepdims=True)
    b = jnp.sum(lq2_ref[...] * lk2_ref[...], axis=-1, keepdims=True)
    return jnp.exp(a) - jnp.exp(b) + lam_init


def _subln(o, sg_ref, lam_init):
    return _rms(o, sg_ref[...]) * (1.0 - lam_init)


def _pad_rows(dst, src):
    dst[...] = jnp.zeros(dst.shape, dst.dtype)
    dst[0:src.shape[0], :] = src[...].astype(dst.dtype)


def _attn_prompt_kernel(q_ref, k_ref, v_ref, km_ref, vm_ref, bd_ref, bp_ref, bm_ref,
                        lq1_ref, lk1_ref, lq2_ref, lk2_ref, sg_ref, o_ref,
                        kb_s, vb_s, kmb_s, vmb_s, *, lam_init):
    T = ATT_TILE
    n_tiles = q_ref.shape[0] // T
    kb_s[...] = k_ref[...].astype(BF16)
    vb_s[...] = v_ref[...].astype(BF16)
    _pad_rows(kmb_s, km_ref)
    _pad_rows(vmb_s, vm_ref)
    lam = _diff_lambda(lq1_ref, lk1_ref, lq2_ref, lk2_ref, lam_init)

    for j in range(n_tiles):
        rows = slice(j * T, (j + 1) * T)
        regions = [(kmb_s, vmb_s, slice(0, LANES), bm_ref[min(j, 1)])]
        if j >= 2:
            regions.append((kb_s, vb_s, slice(0, (j - 1) * T), None))
        if j >= 1:
            regions.append((kb_s, vb_s, slice((j - 1) * T, j * T), bp_ref[...]))
        regions.append((kb_s, vb_s, rows, bd_ref[...]))

        probs, denom = [], []
        for c in range(2):
            cols = slice(c * DQK_ATT, (c + 1) * DQK_ATT)
            qc = q_ref[rows, cols].astype(BF16)
            ss = [_dot_nt(qc, kr[rs, cols]) for kr, _, rs, _ in regions]
            ss = [s if br is None else s + br for s, (_, _, _, br) in zip(ss, regions)]
            m = functools.reduce(jnp.maximum, [jnp.max(s, axis=-1, keepdims=True) for s in ss])
            pc = [jnp.exp2(s - m) for s in ss]
            denom.append(functools.reduce(jnp.add, [jnp.sum(p, axis=-1, keepdims=True) for p in pc]))
            probs.append(pc)
        a0 = 1.0 / denom[0]
        a1 = lam / denom[1]
        o = None
        for r, (_, vr, rs, _) in enumerate(regions):
            w = (probs[0][r] * a0 - probs[1][r] * a1).astype(BF16)
            t = _dot(w, vr[rs, :])
            o = t if o is None else o + t
        o_ref[rows, :] = _subln(o, sg_ref, lam_init)


def _attn_prompt(z_f, z_t, bias, lam_vecs, sg, B, S, lam_init):
    T = ATT_TILE
    qkw = 2 * DQK_ATT
    kcol = ATT_QK // qkw
    vcol = 2 * ATT_QK // DV_ATT
    b_diag, b_prev, b_meta, _, _ = bias
    vec = pl.BlockSpec((1, DQK_ATT), lambda b, h: (0, 0))
    return pl.pallas_call(
        functools.partial(_attn_prompt_kernel, lam_init=lam_init),
        grid=(B, H_ATT),
        in_specs=[
            pl.BlockSpec((S, qkw), lambda b, h: (b, h)),
            pl.BlockSpec((S, qkw), lambda b, h: (b, kcol + h)),
            pl.BlockSpec((S, DV_ATT), lambda b, h: (b, vcol + h)),
            pl.BlockSpec((N_META, qkw), lambda b, h: (b, kcol + h)),
            pl.BlockSpec((N_META, DV_ATT), lambda b, h: (b, vcol + h)),
            pl.BlockSpec((None, T, T), lambda b, h: (h, 0, 0)),
            pl.BlockSpec((None, T, T), lambda b, h: (h, 0, 0)),
            pl.BlockSpec((None, 2, T, LANES), lambda b, h: (h, 0, 0, 0)),
            vec, vec, vec, vec,
            pl.BlockSpec((1, DV_ATT), lambda b, h: (0, 0)),
        ],
        out_specs=pl.BlockSpec((S, DV_ATT), lambda b, h: (b, h)),
        out_shape=jax.ShapeDtypeStruct((B * S, ATT_V), F32),
        scratch_shapes=[
            pltpu.VMEM((S, qkw), BF16),
            pltpu.VMEM((S, DV_ATT), BF16),
            pltpu.VMEM((LANES, qkw), BF16),
            pltpu.VMEM((LANES, DV_ATT), BF16),
        ],
        compiler_params=pltpu.CompilerParams(
            dimension_semantics=("arbitrary", "arbitrary"), vmem_limit_bytes=VMEM_LIMIT),
        name="attn_prompt",
    )(z_f, z_f, z_f, z_t, z_t, b_diag, b_prev, b_meta, *lam_vecs, sg)


def _attn_small_kernel(*refs, lam_init, has_cache):
    if has_cache:
        (q_ref, kc_ref, vc_ref, kn_ref, vn_ref, b_ref,
         lq1_ref, lk1_ref, lq2_ref, lk2_ref, sg_ref, o_ref, kpad, vpad, kb_s, vb_s) = refs
        past_len = kb_s.shape[0]
        chunk = kc_ref.shape[0]
        c = pl.program_id(1)
        r0 = pl.multiple_of(c * chunk, chunk)
        kb_s[pl.ds(r0, chunk), :] = kc_ref[...].reshape(chunk, ATT_QK).astype(BF16)
        vb_s[pl.ds(r0, chunk), :] = vc_ref[...].reshape(chunk, ATT_V).astype(BF16)

        @pl.when(c == past_len // chunk - 1)
        def _():
            _attn_small_body(q_ref, kn_ref, vn_ref, b_ref, lq1_ref, lk1_ref, lq2_ref, lk2_ref, sg_ref, o_ref,
                             kpad, vpad, kb_s, vb_s, lam_init=lam_init, past_len=past_len)
    else:
        (q_ref, kn_ref, vn_ref, b_ref,
         lq1_ref, lk1_ref, lq2_ref, lk2_ref, sg_ref, o_ref, kpad, vpad) = refs
        _attn_small_body(q_ref, kn_ref, vn_ref, b_ref, lq1_ref, lk1_ref, lq2_ref, lk2_ref, sg_ref, o_ref,
                         kpad, vpad, None, None, lam_init=lam_init, past_len=0)


def _attn_small_body(q_ref, kn_ref, vn_ref, b_ref, lq1_ref, lk1_ref, lq2_ref, lk2_ref, sg_ref, o_ref,
                     kpad, vpad, kb_s, vb_s, *, lam_init, past_len):
    has_cache = past_len > 0
    lam = _diff_lambda(lq1_ref, lk1_ref, lq2_ref, lk2_ref, lam_init)
    _pad_rows(kpad, kn_ref)
    _pad_rows(vpad, vn_ref)
    for h in range(H_ATT):
        vsl = slice(h * DV_ATT, (h + 1) * DV_ATT)
        outs = []
        for c in range(2):
            sl = slice(h * 2 * DQK_ATT + c * DQK_ATT, h * 2 * DQK_ATT + (c + 1) * DQK_ATT)
            qc = q_ref[:, sl].astype(BF16)
            s_new = _dot_nt(qc, kpad[:, sl]) + b_ref[h, :, past_len:]
            m = jnp.max(s_new, axis=-1, keepdims=True)
            if has_cache:
                s_old = _dot_nt(qc, kb_s[:, sl]) + b_ref[h, :, :past_len]
                m = jnp.maximum(m, jnp.max(s_old, axis=-1, keepdims=True))
            p_new = jnp.exp2(s_new - m)
            l = jnp.sum(p_new, axis=-1, keepdims=True)
            o = _dot(p_new.astype(BF16), vpad[:, vsl])
            if has_cache:
                p_old = jnp.exp2(s_old - m)
                l = l + jnp.sum(p_old, axis=-1, keepdims=True)
                o = o + _dot(p_old.astype(BF16), vb_s[:, vsl])
            outs.append(o / l)
        o_ref[:, vsl] = _subln(outs[0] - lam * outs[1], sg_ref, lam_init)


def _attn_small(z_t, row0, n_streams, bias, lam_vecs, sg, lam_init, cache=None):
    vec = pl.BlockSpec((1, DQK_ATT), lambda b, c: (0, 0))
    in_specs = [pl.BlockSpec((N_META, ATT_QK), lambda b, c: (row0 + b, 0))]
    args = [z_t]
    scratch = [pltpu.VMEM((LANES, ATT_QK), BF16), pltpu.VMEM((LANES, ATT_V), BF16)]
    n_chunks = 1
    if cache is not None:
        ck, cv, layer = cache
        past_len = ck.shape[2]
        chunk = min(past_len, CACHE_CHUNK)
        assert past_len % chunk == 0
        n_chunks = past_len // chunk
        in_specs += [
            pl.BlockSpec((None, None, chunk, H_ATT, 2 * DQK_ATT), lambda b, c: (layer, b, c, 0, 0)),
            pl.BlockSpec((None, None, chunk, H_ATT, DV_ATT), lambda b, c: (layer, b, c, 0, 0))]
        args += [ck, cv]
        scratch += [pltpu.VMEM((past_len, ATT_QK), BF16), pltpu.VMEM((past_len, ATT_V), BF16)]
    in_specs += [
        pl.BlockSpec((N_META, ATT_QK), lambda b, c: (row0 + b, 1)),
        pl.BlockSpec((N_META, ATT_V), lambda b, c: (row0 + b, 2 * ATT_QK // ATT_V)),
        pl.BlockSpec(bias.shape, lambda b, c: (0, 0, 0)),
        vec, vec, vec, vec,
        pl.BlockSpec((1, DV_ATT), lambda b, c: (0, 0)),
    ]
    args += [z_t, z_t, bias, *lam_vecs, sg]
    return pl.pallas_call(
        functools.partial(_attn_small_kernel, lam_init=lam_init, has_cache=cache is not None),
        grid=(n_streams, n_chunks),
        in_specs=in_specs,
        out_specs=pl.BlockSpec((N_META, ATT_V), lambda b, c: (b, 0)),
        out_shape=jax.ShapeDtypeStruct((n_streams * N_META, ATT_V), F32),
        scratch_shapes=scratch,
        compiler_params=pltpu.CompilerParams(
            dimension_semantics=("arbitrary", "arbitrary"), vmem_limit_bytes=VMEM_LIMIT),
        name="attn_cached" if cache is not None else "attn_meta",
    )(*args)


def _mlstm_kernel(*refs, n_blocks, has_init):
    refs = list(refs)
    qh_ref, kh_ref, vh_ref, gh_ref = refs[:4]
    pos = 4
    if n_blocks:
        qf_ref, kf_ref, vf_ref, gf_ref = refs[pos:pos + 4]
        pos += 4
    if has_init:
        c0_ref, n0_ref, m0_ref = refs[pos:pos + 3]
        pos += 3
    hh_ref = refs[pos]
    pos += 1
    if n_blocks:
        hf_ref = refs[pos]
        pos += 1
    c_ref, n_ref, m_ref = refs[pos:pos + 3]
    qs, ks, vs, gs = refs[pos + 3:pos + 7]

    t = pl.program_id(1)
    T = M_TILE
    block = functools.partial(_mlstm_block, c_ref=c_ref, n_ref=n_ref, m_ref=m_ref)

    @pl.when(t == 0)
    def _():
        if has_init:
            c_ref[...] = c0_ref[...]
            n_ref[...] = n0_ref[...]
            m_ref[...] = m0_ref[...]
        else:
            c_ref[...] = jnp.zeros(c_ref.shape, F32)
            n_ref[...] = jnp.zeros(n_ref.shape, F32)
            m_ref[...] = jnp.zeros(m_ref.shape, F32)
        nh = qh_ref.shape[0]
        for dst, src in ((qs, qh_ref), (ks, kh_ref), (vs, vh_ref)):
            dst[...] = jnp.zeros(dst.shape, F32)
            dst[0:nh, :] = src[...]
        lane = lax.broadcasted_iota(jnp.int32, gs.shape, 1)
        gs[...] = jnp.where(lane < H_M, NEG, 0.0)
        gs[0:nh, :] = gh_ref[...]

        block(qs, ks, vs, gs, hh_ref, nh)

    if n_blocks:
        @pl.when(t > 0)
        def _():
            block(qf_ref, kf_ref, vf_ref, gf_ref, hf_ref, T)


def _mlstm_block(q_src, k_src, v_src, g_src, h_dst, n_out, c_ref, n_ref, m_ref):
    T = M_TILE
    g = g_src[...]
    row = lax.broadcasted_iota(jnp.int32, (T, T), 0)
    colid = lax.broadcasted_iota(jnp.int32, (T, T), 1)
    causal = row >= colid
    tril = jnp.where(causal, 1.0, 0.0).astype(F32)
    cum = jnp.dot(tril, g, preferred_element_type=F32, precision=lax.Precision.HIGHEST)
    g_t = g.T
    cum_t = cum.T
    last = cum[T - 1:T, :]

    for h in range(H_M):
        q = q_src[:, h * DQK_M:(h + 1) * DQK_M]
        k = k_src[:, h * DQK_M:(h + 1) * DQK_M]
        v = v_src[:, h * DV_M:(h + 1) * DV_M]
        qb = q.astype(BF16)
        kb = k.astype(BF16)
        m_old = m_ref[h, 0:1, 0:1]
        c_old = c_ref[h]
        n_old = n_ref[h]
        bh_c = _col(cum, H_M + h)
        ih_c = _col(g, h)
        bh_r = cum_t[H_M + h:H_M + h + 1, :]
        ih_r = g_t[h:h + 1, :]
        d = jnp.where(causal, bh_c - bh_r + ih_r, NEG)
        inter = bh_c + m_old
        mt = jnp.maximum(inter, jnp.max(d, axis=-1, keepdims=True))
        a = jnp.exp(d - mt) * _dot_nt(qb, kb)
        iw = jnp.exp(inter - mt)
        num = _dot(a.astype(BF16), v.astype(BF16)) + iw * _dot_nt(qb, c_old.astype(BF16))
        den = jnp.sum(a, axis=-1, keepdims=True) + iw * jnp.sum(q * n_old, axis=-1, keepdims=True)
        den = jnp.maximum(jnp.abs(den), jnp.exp(-mt))
        hout = num / den
        h_dst[:, h * DV_M:(h + 1) * DV_M] = hout[0:n_out, :]

        bl = _col(last, H_M + h)
        g_c = bl - bh_c + ih_c
        m_new = jnp.maximum(bl + m_old, jnp.max(g_c, axis=0, keepdims=True))
        sw = jnp.exp(g_c - m_new)
        dec = jnp.exp(bl + m_old - m_new)
        c_ref[h] = dec * c_old + _dot(v.T.astype(BF16), (sw * k).astype(BF16))
        n_ref[h] = dec * n_old + jnp.sum(sw * k, axis=0, keepdims=True)
        m_ref[h] = jnp.broadcast_to(m_new, m_ref.shape[1:])


def _mlstm(z_t, g_t, row0, n_streams, frames=None, init=None):
    T = M_TILE
    qcol = (2 * ATT_QK + ATT_V) // M_QK
    vcol = (2 * ATT_QK + ATT_V + 2 * M_QK) // M_V
    nh = N_META
    in_specs = [
        pl.BlockSpec((nh, M_QK), lambda b, t: (row0 + b, qcol)),
        pl.BlockSpec((nh, M_QK), lambda b, t: (row0 + b, qcol + 1)),
        pl.BlockSpec((nh, M_V), lambda b, t: (row0 + b, vcol)),
        pl.BlockSpec((nh, LANES), lambda b, t: (row0 + b, 0)),
    ]
    args = [z_t, z_t, z_t, g_t]
    n_blocks = 0
    if frames is not None:
        z_f, g_f, fps = frames
        n_blocks = fps // T
        fidx = lambda b, t: b * n_blocks + jnp.maximum(t - 1, 0)
        in_specs += [
            pl.BlockSpec((T, M_QK), lambda b, t: (fidx(b, t), qcol)),
            pl.BlockSpec((T, M_QK), lambda b, t: (fidx(b, t), qcol + 1)),
            pl.BlockSpec((T, M_V), lambda b, t: (fidx(b, t), vcol)),
            pl.BlockSpec((T, LANES), lambda b, t: (fidx(b, t), 0)),
        ]
        args += [z_f, z_f, z_f, g_f]
    state_specs = (
        pl.BlockSpec((None, H_M, DV_M, DQK_M), lambda b, t: (b, 0, 0, 0)),
        pl.BlockSpec((None, H_M, 1, DQK_M), lambda b, t: (b, 0, 0, 0)),
        pl.BlockSpec((None, H_M, 8, LANES), lambda b, t: (b, 0, 0, 0)),
    )
    if init is not None:
        in_specs += list(state_specs)
        args += list(init)
    out_specs = [pl.BlockSpec((nh, M_V), lambda b, t: (b, 0))]
    out_shape = [jax.ShapeDtypeStruct((n_streams * nh, M_V), F32)]
    if n_blocks:
        out_specs.append(pl.BlockSpec((T, M_V), lambda b, t: (fidx(b, t), 0)))
        out_shape.append(jax.ShapeDtypeStruct((n_streams * fps, M_V), F32))
    out_specs += list(state_specs)
    out_shape += [
        jax.ShapeDtypeStruct((n_streams, H_M, DV_M, DQK_M), F32),
        jax.ShapeDtypeStruct((n_streams, H_M, 1, DQK_M), F32),
        jax.ShapeDtypeStruct((n_streams, H_M, 8, LANES), F32),
    ]
    return pl.pallas_call(
        functools.partial(_mlstm_kernel, n_blocks=n_blocks, has_init=init is not None),
        grid=(n_streams, 1 + n_blocks),
        in_specs=in_specs,
        out_specs=tuple(out_specs),
        out_shape=tuple(out_shape),
        scratch_shapes=[
            pltpu.VMEM((T, M_QK), F32), pltpu.VMEM((T, M_QK), F32),
            pltpu.VMEM((T, M_V), F32), pltpu.VMEM((T, LANES), F32),
        ],
        compiler_params=pltpu.CompilerParams(
            dimension_semantics=("arbitrary", "arbitrary"), vmem_limit_bytes=VMEM_LIMIT),
        name="mlstm_prompt" if n_blocks else "mlstm_cached",
    )(*args)


def _merge_kernel(xf_ref, xt_ref, attf_ref, attt_ref, hmf_ref, hmt_ref, zf_ref, zt_ref, mg_ref, wo_ref,
                  of_ref, ot_ref, *, n_frame_tiles):
    i = pl.program_id(0)

    def tile(x_ref, att_ref, hm_ref, og_ref, o_ref):
        parts = []
        for h in range(H_M):
            sl = slice(h * DV_M, (h + 1) * DV_M)
            parts.append(_rms(hm_ref[:, sl], mg_ref[:, sl]) * og_ref[:, sl])
        hmn = jnp.concatenate(parts, axis=1).astype(BF16)
        o_ref[...] = (x_ref[...] + _dot(att_ref[...].astype(BF16), wo_ref[0:ATT_V, :])
                      + _dot(hmn, wo_ref[ATT_V:, :]))

    @pl.when(i < n_frame_tiles)
    def _():
        tile(xf_ref, attf_ref, hmf_ref, zf_ref, of_ref)

    @pl.when(i == n_frame_tiles)
    def _():
        tile(xt_ref, attt_ref, hmt_ref, zt_ref, ot_ref)


def _merge(x_f, x_t, att_f, att_t, hm_f, hm_t, z_f, z_t, mg, w_out, layer):
    NF, D = x_f.shape
    NT = x_t.shape[0]
    bm = MERGE_ROWS
    nft = NF // bm
    assert NF % bm == 0
    ogcol = (N_MAIN - M_V) // M_V
    fr = lambda i: (jnp.minimum(i, nft - 1), 0)
    tl = lambda i: (0, 0)
    once = dict(pipeline_mode=pl.Buffered(1))
    return pl.pallas_call(
        functools.partial(_merge_kernel, n_frame_tiles=nft),
        grid=(nft + 1,),
        in_specs=[
            pl.BlockSpec((bm, D), fr), pl.BlockSpec((NT, D), tl, **once),
            pl.BlockSpec((bm, ATT_V), fr), pl.BlockSpec((NT, ATT_V), tl, **once),
            pl.BlockSpec((bm, M_V), fr), pl.BlockSpec((NT, M_V), tl, **once),
            pl.BlockSpec((bm, M_V), lambda i: (jnp.minimum(i, nft - 1), ogcol)),
            pl.BlockSpec((NT, M_V), lambda i: (0, ogcol), **once),
            pl.BlockSpec((1, M_V), tl),
            pl.BlockSpec((None, D, D), lambda i: (layer, 0, 0), **once),
        ],
        out_specs=(pl.BlockSpec((bm, D), fr), pl.BlockSpec((NT, D), tl)),
        out_shape=(jax.ShapeDtypeStruct((NF, D), F32), jax.ShapeDtypeStruct((NT, D), F32)),
        compiler_params=pltpu.CompilerParams(
            dimension_semantics=("arbitrary",), vmem_limit_bytes=VMEM_LIMIT),
        name="merge_out_proj",
    )(x_f, x_t, att_f, att_t, hm_f, hm_t, z_f, z_t, mg, w_out)


def _ffn_kernel(xf_ref, xt_ref, g2_ref, wu_ref, wd_ref, of_ref, ot_ref, xn_s, *, n_frame_tiles):
    i = pl.program_id(0)
    k = pl.program_id(1)

    def tile(x_ref, o_ref):
        nrows = x_ref.shape[0]

        @pl.when(k == 0)
        def _():
            x = x_ref[...]
            xn_s[0:nrows, :] = _rms(x, g2_ref[...]).astype(BF16)
            o_ref[...] = x

        u = jnp.maximum(_dot(xn_s[0:nrows, :], wu_ref[...]), 0.0)
        o_ref[...] += _dot((u * u).astype(BF16), wd_ref[...])

    @pl.when(i < n_frame_tiles)
    def _():
        tile(xf_ref, of_ref)

    @pl.when(i == n_frame_tiles)
    def _():
        tile(xt_ref, ot_ref)


def _ffn(x_f, x_t, g2, w_up, w_down, layer):
    NF, D = x_f.shape
    NT = x_t.shape[0]
    FF = w_up.shape[2]
    bm, fc = FF_ROWS, FF_COLS
    nft = NF // bm
    assert NF % bm == 0 and FF % fc == 0 and NT <= bm
    fr = lambda i, k: (jnp.minimum(i, nft - 1), 0)
    tl = lambda i, k: (0, 0)
    return pl.pallas_call(
        functools.partial(_ffn_kernel, n_frame_tiles=nft),
        grid=(nft + 1, FF // fc),
        in_specs=[
            pl.BlockSpec((bm, D), fr),
            pl.BlockSpec((NT, D), tl),
            pl.BlockSpec((1, D), tl),
            pl.BlockSpec((None, D, fc), lambda i, k: (layer, 0, k)),
            pl.BlockSpec((None, fc, D), lambda i, k: (layer, k, 0)),
        ],
        out_specs=(pl.BlockSpec((bm, D), fr), pl.BlockSpec((NT, D), tl)),
        out_shape=(jax.ShapeDtypeStruct((NF, D), F32), jax.ShapeDtypeStruct((NT, D), F32)),
        scratch_shapes=[pltpu.VMEM((bm, D), BF16)],
        compiler_params=pltpu.CompilerParams(
            dimension_semantics=("arbitrary", "arbitrary"), vmem_limit_bytes=VMEM_LIMIT),
        name="ffn",
    )(x_f, x_t, g2, w_up, w_down)


def kernel(x_prompt, x_sample, cache_k, cache_v, state_C, state_n, state_m, meta_tokens, rel_bias, norm1_g, w_in, b_i, b_f, q_norm_g, k_norm_g, lambda_q1, lambda_k1, lambda_q2, lambda_k2, subln_g, mnorm_g, w_out, norm2_g, w_up, w_down):
    B, S, D = x_prompt.shape
    Bd, Td, _ = x_sample.shape
    depth = w_in.shape[0]
    Pc = cache_k.shape[2]
    assert Td == N_META and S % ATT_TILE == 0 and S % M_TILE == 0 and Pc % LANES == 0
    assert ATT_TILE % CHUNK == 0 and CHUNK & (CHUNK - 1) == 0 and ATT_TILE >= MAX_DISTANCE

    n_frames = B * S
    n_meta = B * N_META
    n_samp = Bd * Td
    assert n_meta + n_samp <= TAIL_ROWS
    srow = n_meta // N_META

    meta = jnp.broadcast_to(meta_tokens.astype(F32)[None], (B, N_META, D)).reshape(n_meta, D)
    x_f = x_prompt.astype(F32).reshape(n_frames, D)
    x_t = jnp.concatenate([meta, x_sample.astype(F32).reshape(n_samp, D),
                           jnp.zeros((TAIL_ROWS - n_meta - n_samp, D), F32)], axis=0)

    bias = _bias_tiles(rel_bias.astype(F32), Pc, Td)
    ck = cache_k.astype(F32)
    cv = cache_v.astype(F32)
    zeros_pad_att = jnp.zeros((TAIL_ROWS - n_meta - n_samp, ATT_V), F32)

    g_off = 2 * ATT_QK + ATT_V + 2 * M_QK + M_V
    w_in_t = jnp.swapaxes(w_in.astype(F32), 1, 2)
    w_out_b, w_up_b, w_down_b = (w.astype(BF16) for w in (w_out, w_up, w_down))
    outs = {name: [] for name in ("Cp", "np", "mp", "Cs", "ns", "ms")}
    caches = None

    for l in range(depth):
        lam_init = 0.8 - 0.6 * math.exp(-0.3 * l)
        gate_bias = jnp.pad(jnp.concatenate([b_i[l], b_f[l]]).astype(F32), (0, LANES - 2 * H_M))[None]
        lam_vecs = [v[l].astype(F32)[None] for v in (lambda_q1, lambda_k1, lambda_q2, lambda_k2)]
        sg = subln_g[l].astype(F32)[None]

        z_f, z_t, g_f, g_t, *caches = _inproj(
            x_f, x_t, norm1_g[l].astype(F32)[None], w_in_t, l, gate_bias,
            q_norm_g[l].astype(F32)[None], k_norm_g[l].astype(F32)[None], B, Bd, caches)

        att_f = _attn_prompt(z_f, z_t, bias, lam_vecs, sg, B, S, lam_init)
        att_m = _attn_small(z_t, 0, B, bias[3], lam_vecs, sg, lam_init)
        att_s = _attn_small(z_t, srow, Bd, bias[4], lam_vecs, sg, lam_init, cache=(ck, cv, l))
        att_t = jnp.concatenate([att_m, att_s, zeros_pad_att], axis=0)

        hm_m, hm_f, C_p, n_p, m_p = _mlstm(z_t, g_t, 0, B, frames=(z_f, g_f, S))
        init = (state_C[l].astype(F32), state_n[l].astype(F32)[:, :, None, :],
                jnp.broadcast_to(state_m[l].astype(F32)[:, :, None, None], (Bd, H_M, 8, LANES)))
        hm_s, C_s, n_s, m_s = _mlstm(z_t, g_t, srow, Bd, init=init)
        hm_t = jnp.concatenate([hm_m, hm_s, zeros_pad_att], axis=0)

        x_f, x_t = _merge(x_f, x_t, att_f, att_t, hm_f, hm_t, z_f, z_t,
                          mnorm_g[l].astype(F32)[None], w_out_b, l)
        x_f, x_t = _ffn(x_f, x_t, norm2_g[l].astype(F32)[None], w_up_b, w_down_b, l)

        outs["Cp"].append(C_p)
        outs["np"].append(n_p[:, :, 0, :])
        outs["mp"].append(m_p[:, :, 0, 0])
        outs["Cs"].append(C_s)
        outs["ns"].append(n_s[:, :, 0, :])
        outs["ms"].append(m_s[:, :, 0, 0])

    y_prompt = x_f.reshape(B, S, D)
    y_sample = x_t[n_meta:n_meta + n_samp].reshape(Bd, Td, D)
    k_p, v_p, k_s, v_s = caches
    return (y_prompt, y_sample,
            k_p, v_p, jnp.stack(outs["Cp"]), jnp.stack(outs["np"]), jnp.stack(outs["mp"]),
            k_s.reshape(depth, Bd, Td, H_ATT, 2 * DQK_ATT), v_s.reshape(depth, Bd, Td, H_ATT, DV_ATT),
            jnp.stack(outs["Cs"]), jnp.stack(outs["ns"]), jnp.stack(outs["ms"]))
```

```python
import functools
import math

import jax
import jax.numpy as jnp
from jax import lax
from jax.experimental import pallas as pl
from jax.experimental.pallas import tpu as pltpu

F32 = jnp.float32
BF16 = jnp.bfloat16

CHUNK = 64
N_META = 16
H_ATT = 4
DQK_ATT = 128
DV_ATT = 256
H_M = 4
DQK_M = 128
DV_M = 256
N_BUCKETS = 32
MAX_DISTANCE = 128
EPS = 1e-6
NEG = -1e30
LOG2E = math.log2(math.e)

ATT_QK = H_ATT * 2 * DQK_ATT
ATT_V = H_ATT * DV_ATT
M_QK = H_M * DQK_M
M_V = H_M * DV_M
N_MAIN = 2 * ATT_QK + ATT_V + 2 * M_QK + M_V + M_V

LANES = 128
ATT_TILE = 256
M_TILE = 256
TAIL_ROWS = 256
PROJ_ROWS = 1024
PROJ_COLS = 512
MERGE_ROWS = 512
CACHE_CHUNK = 1024
FF_ROWS = 512
FF_COLS = 1024
VMEM_LIMIT = 56 * 1024 * 1024

_NT = (((1,), (1,)), ((), ()))


def _dot(a, b):
    return jnp.dot(a, b, preferred_element_type=F32)


def _dot_nt(a, b):
    return lax.dot_general(a, b, _NT, preferred_element_type=F32)


def _log_sigmoid(x):
    return jnp.minimum(x, 0.0) - jnp.log1p(jnp.exp(-jnp.abs(x)))


def _col(a, idx):
    lane = lax.broadcasted_iota(jnp.int32, a.shape, 1)
    return jnp.sum(jnp.where(lane == idx, a, 0.0), axis=1, keepdims=True)


def _rms(x, gain):
    ms = jnp.mean(x * x, axis=-1, keepdims=True)
    return x * lax.rsqrt(ms + EPS) * gain


def _bias_from_rel(rel, rb_ref, h):
    nb = N_BUCKETS // 2
    max_exact = nb // 2
    ret = jnp.where(rel > 0, nb, 0)
    n = jnp.abs(rel)
    nf = jnp.maximum(n, 1).astype(F32)
    large = max_exact + (jnp.log(nf / max_exact) / math.log(MAX_DISTANCE / max_exact)
                         * (nb - max_exact)).astype(jnp.int32)
    large = jnp.minimum(large, nb - 1)
    bucket = ret + jnp.where(n < max_exact, n, large)
    out = jnp.zeros(rel.shape, F32)
    for b in range(N_BUCKETS):
        out = jnp.where(bucket == b, rb_ref[b, h], out)
    return out


def _bias_kernel(rb_ref, diag_ref, prev_ref, meta_ref, mm_ref, samp_ref, *, past_len, dec_seq):
    h = pl.program_id(0)
    shift = CHUNK.bit_length() - 1
    T = ATT_TILE
    far = _bias_from_rel(jnp.full((8, LANES), -(2 * MAX_DISTANCE + 1), jnp.int32), rb_ref, h)[0:1, 0:1]

    qi = lax.broadcasted_iota(jnp.int32, (T, T), 0)
    kj = lax.broadcasted_iota(jnp.int32, (T, T), 1)
    bias = (_bias_from_rel(kj - qi, rb_ref, h) - far) * LOG2E
    diag_ref[...] = jnp.where((kj >> shift) <= (qi >> shift), bias, NEG)
    prev_ref[...] = (_bias_from_rel(kj - T - qi, rb_ref, h) - far) * LOG2E

    qi = lax.broadcasted_iota(jnp.int32, (T, LANES), 0)
    kj = lax.broadcasted_iota(jnp.int32, (T, LANES), 1)
    for var in range(2):
        bias = (_bias_from_rel(kj - (N_META + qi + var * T), rb_ref, h) - far) * LOG2E
        meta_ref[var] = jnp.where(kj < N_META, bias, NEG)

    qi = lax.broadcasted_iota(jnp.int32, (N_META, LANES), 0)
    kj = lax.broadcasted_iota(jnp.int32, (N_META, LANES), 1)
    mm_ref[...] = jnp.where(kj < N_META, _bias_from_rel(kj - qi, rb_ref, h) * LOG2E, NEG)

    width = samp_ref.shape[-1]
    qi = lax.broadcasted_iota(jnp.int32, (dec_seq, width), 0) + past_len
    kj = lax.broadcasted_iota(jnp.int32, (dec_seq, width), 1)
    bias = _bias_from_rel(kj - qi, rb_ref, h) * LOG2E
    ok = ((kj >> shift) <= (qi >> shift)) & (kj < past_len + dec_seq)
    samp_ref[...] = jnp.where(ok, bias, NEG)


def _bias_tiles(rel_bias, past_len, dec_seq):
    T = ATT_TILE
    sw = past_len + LANES
    out_shape = (
        jax.ShapeDtypeStruct((H_ATT, T, T), F32),
        jax.ShapeDtypeStruct((H_ATT, T, T), F32),
        jax.ShapeDtypeStruct((H_ATT, 2, T, LANES), F32),
        jax.ShapeDtypeStruct((H_ATT, N_META, LANES), F32),
        jax.ShapeDtypeStruct((H_ATT, dec_seq, sw), F32),
    )
    out_specs = (
        pl.BlockSpec((None, T, T), lambda h: (h, 0, 0)),
        pl.BlockSpec((None, T, T), lambda h: (h, 0, 0)),
        pl.BlockSpec((None, 2, T, LANES), lambda h: (h, 0, 0, 0)),
        pl.BlockSpec((None, N_META, LANES), lambda h: (h, 0, 0)),
        pl.BlockSpec((None, dec_seq, sw), lambda h: (h, 0, 0)),
    )
    return pl.pallas_call(
        functools.partial(_bias_kernel, past_len=past_len, dec_seq=dec_seq),
        grid=(H_ATT,),
        in_specs=[pl.BlockSpec(memory_space=pltpu.SMEM)],
        out_specs=out_specs,
        out_shape=out_shape,
        name="rel_bias_tiles",
    )(rel_bias)


def _zero_fill_copies(zbuf, zsem, cache_refs, depth):
    rows = zbuf.shape[0]
    copies = []
    for dst in cache_refs:
        lead = dst.shape[1:-3]
        n_rows = dst.shape[-3]
        for l in range(1, depth):
            for b in range(lead[0] if lead else 1):
                view = dst.at[l, b] if lead else dst.at[l]
                for r0 in range(0, n_rows, rows):
                    n = min(rows, n_rows - r0)
                    copies.append(pltpu.make_async_copy(zbuf.at[pl.ds(0, n)], view.at[pl.ds(r0, n)],
                                                        zsem.at[len(copies)]))
    return copies


def _kv_copies(stage, sem, dst_prompt, dst_sample, i, *, layer, tail, n_batch, frames_per_batch, n_sample_rows):
    dst_prompt = dst_prompt.at[layer]
    dst_sample = dst_sample.at[layer]
    if not tail:
        rows = stage.shape[0]
        tiles_per_batch = frames_per_batch // rows
        b = i // tiles_per_batch
        t = i - b * tiles_per_batch
        return [pltpu.make_async_copy(stage, dst_prompt.at[b, pl.ds(N_META + t * rows, rows)], sem.at[0])]
    copies = [pltpu.make_async_copy(stage.at[pl.ds(b * N_META, N_META)], dst_prompt.at[b, pl.ds(0, N_META)],
                                    sem.at[b]) for b in range(n_batch)]
    copies.append(pltpu.make_async_copy(stage.at[pl.ds(n_batch * N_META, n_sample_rows)], dst_sample,
                                        sem.at[n_batch]))
    return copies


def _inproj_kernel(*refs, layer, depth, n_frame_tiles, n_batch, frames_per_batch, n_sample_rows):
    xf_ref, xt_ref, g1_ref, wa_ref, wg_ref, gb_ref, qg_ref, kg_ref = refs[:8]
    (zf_ref, zt_ref, gf_ref, gt_ref, kp_ref, vp_ref, ks_ref, vs_ref,
     xn_s, accbuf, stage, sem, zbuf, zsem) = refs[-14:]
    i = pl.program_id(0)
    j = pl.program_id(1)
    bn = wa_ref.shape[1]

    if layer == 0 and depth > 1:
        zero_copies = _zero_fill_copies(zbuf, zsem, (kp_ref, vp_ref, ks_ref, vs_ref), depth)

        @pl.when((i == 0) & (j == 0))
        def _():
            zbuf[...] = jnp.zeros(zbuf.shape, F32)
            for c in zero_copies:
                c.start()

        @pl.when((i == pl.num_programs(0) - 1) & (j == pl.num_programs(1) - 1))
        def _():
            for c in zero_copies:
                c.wait()

    assert ATT_QK % bn == 0 and ATT_V % bn == 0 and M_QK % bn == 0 and M_V % bn == 0 and bn % DV_ATT == 0
    j_k = ATT_QK // bn
    j_v = j_k + ATT_QK // bn
    j_mq = j_v + ATT_V // bn
    j_mk = j_mq + M_QK // bn
    j_mv = j_mk + M_QK // bn
    j_og = j_mv + M_V // bn

    nj = j_og + M_V // bn
    heads_per_tile = bn // DV_ATT

    def tile(x_ref, z_ref, g_ref, tail, lag):
        nrows = x_ref.shape[0]
        st = stage.at[pl.ds(0, nrows)]
        copies = functools.partial(_kv_copies, st, sem, i=i, layer=layer, tail=tail, n_batch=n_batch,
                                   frames_per_batch=frames_per_batch, n_sample_rows=n_sample_rows)

        @pl.when(j == 0)
        def _():
            xb = _rms(x_ref[...], g1_ref[...]).astype(BF16)
            xn_s[0:nrows, :] = xb
            gates = _dot_nt(xb, wg_ref[0].astype(BF16)) + gb_ref[...]
            lane = lax.broadcasted_iota(jnp.int32, gates.shape, 1)
            g_ref[...] = jnp.where(lane < H_M, gates, _log_sigmoid(gates))

        def matmul():
            return _dot_nt(xn_s[0:nrows, :], wa_ref[0].astype(BF16))

        def group_norm(acc, gain_ref, scale):
            parts = []
            for gi in range(bn // DQK_ATT):
                y = _rms(acc[:, gi * DQK_ATT:(gi + 1) * DQK_ATT], gain_ref[...])
                parts.append(y if scale is None else y * scale)
            return jnp.concatenate(parts, axis=1)

        def stage_heads(y, e, e0, dst_prompt, dst_sample):
            if e == e0 and e0 != j_k:
                for c in copies(kp_ref, ks_ref):
                    c.wait()
            h0 = (e - e0) * heads_per_tile
            st[:, h0:h0 + heads_per_tile, :] = y.reshape(nrows, heads_per_tile, DV_ATT)
            if e == e0 + ATT_QK // bn - 1:
                for c in copies(dst_prompt, dst_sample):
                    c.start()

        def epilogue(e, acc):
            if e < j_k:
                z_ref[...] = group_norm(acc, qg_ref, DQK_ATT ** -0.5 * LOG2E)
            elif e < j_v:
                y = group_norm(acc, kg_ref, None)
                z_ref[...] = y
                stage_heads(y, e, j_k, kp_ref, ks_ref)
            elif e < j_mq:
                z_ref[...] = acc
                stage_heads(acc, e, j_v, vp_ref, vs_ref)
            else:
                if e == j_mq:
                    for c in copies(vp_ref, vs_ref):
                        c.wait()
                if j_mk <= e < j_mv:
                    z_ref[...] = acc * (DQK_M ** -0.5)
                elif e >= j_og:
                    z_ref[...] = jax.nn.sigmoid(acc)
                else:
                    z_ref[...] = acc

        for jj in range(nj + lag):
            @pl.when(j == jj)
            def _(jj=jj):
                if lag == 0:
                    epilogue(jj, matmul())
                else:
                    if jj < nj:
                        accbuf[jj % 2] = matmul()
                    if jj >= lag:
                        epilogue(jj - lag, accbuf[(jj - lag) % 2])

    @pl.when(i < n_frame_tiles)
    def _():
        tile(xf_ref, zf_ref, gf_ref, False, 1)

    @pl.when(i == n_frame_tiles)
    def _():
        tile(xt_ref, zt_ref, gt_ref, True, 0)


def _inproj(x_f, x_t, g1, w_all, layer, gate_bias, qg, kg, n_batch, n_sample_streams, caches):
    depth = w_all.shape[0]
    NF, D = x_f.shape
    NT = x_t.shape[0]
    bm, bn = PROJ_ROWS, PROJ_COLS
    nft = NF // bm
    nj = N_MAIN // bn
    g_row = N_MAIN - M_V
    assert g_row % bn == 0 and g_row % 8 == 0 and w_all.shape[1] >= g_row + LANES

    def w_window(i, j):
        jc = jnp.minimum(j, nj - 1)
        row = jc * bn + jnp.where(jc * bn >= g_row, 2 * H_M, 0)
        return (layer, pl.multiple_of(row, 8), 0)

    S = NF // n_batch
    L = N_META + S
    n_sample_rows = n_sample_streams * N_META
    assert NF % bm == 0 and S % bm == 0 and N_MAIN % bn == 0 and NT <= bm
    fr = lambda i: jnp.minimum(i, nft - 1)
    any_spec = pl.BlockSpec(memory_space=pl.ANY)
    assert (caches is None) == (layer == 0)
    prev = () if caches is None else tuple(caches)
    n_in = 8
    zero_rows = 256
    n_zero = (depth - 1) * (2 * n_batch * pl.cdiv(L, zero_rows) + 2 * pl.cdiv(n_sample_rows, zero_rows))
    return pl.pallas_call(
        functools.partial(_inproj_kernel, layer=layer, depth=depth, n_frame_tiles=nft, n_batch=n_batch,
                          frames_per_batch=S, n_sample_rows=n_sample_rows),
        grid=(nft + 1, nj + 1),
        input_output_aliases={n_in + c: 4 + c for c in range(len(prev))},
        in_specs=[
            pl.BlockSpec((bm, D), lambda i, j: (fr(i), 0)),
            pl.BlockSpec((NT, D), lambda i, j: (0, 0)),
            pl.BlockSpec((1, D), lambda i, j: (0, 0)),
            pl.BlockSpec((pl.Element(1), pl.Element(bn), pl.Element(D)), w_window),
            pl.BlockSpec((pl.Element(1), pl.Element(LANES), pl.Element(D)), lambda i, j: (layer, g_row, 0)),
            pl.BlockSpec((1, LANES), lambda i, j: (0, 0)),
            pl.BlockSpec((1, DQK_ATT), lambda i, j: (0, 0)),
            pl.BlockSpec((1, DQK_ATT), lambda i, j: (0, 0)),
        ] + [any_spec] * len(prev),
        out_specs=(
            pl.BlockSpec((bm, bn), lambda i, j: (fr(i), jnp.where(i < nft, jnp.maximum(j - 1, 0), nj - 1))),
            pl.BlockSpec((NT, bn), lambda i, j: (0, jnp.where(i < nft, 0, jnp.minimum(j, nj - 1)))),
            pl.BlockSpec((bm, LANES), lambda i, j: (fr(i), 0)),
            pl.BlockSpec((NT, LANES), lambda i, j: (0, 0)),
            any_spec, any_spec, any_spec, any_spec,
        ),
        out_shape=(
            jax.ShapeDtypeStruct((NF, N_MAIN), F32),
            jax.ShapeDtypeStruct((NT, N_MAIN), F32),
            jax.ShapeDtypeStruct((NF, LANES), F32),
            jax.ShapeDtypeStruct((NT, LANES), F32),
            jax.ShapeDtypeStruct((depth, n_batch, L, H_ATT, 2 * DQK_ATT), F32),
            jax.ShapeDtypeStruct((depth, n_batch, L, H_ATT, DV_ATT), F32),
            jax.ShapeDtypeStruct((depth, n_sample_rows, H_ATT, 2 * DQK_ATT), F32),
            jax.ShapeDtypeStruct((depth, n_sample_rows, H_ATT, DV_ATT), F32),
        ),
        scratch_shapes=[
            pltpu.VMEM((bm, D), BF16),
            pltpu.VMEM((2, bm, bn), F32),
            pltpu.VMEM((bm, H_ATT, DV_ATT), F32),
            pltpu.SemaphoreType.DMA((n_batch + 1,)),
            pltpu.VMEM((zero_rows, H_ATT, DV_ATT), F32),
            pltpu.SemaphoreType.DMA((max(n_zero, 1),)),
        ],
        compiler_params=pltpu.CompilerParams(
            dimension_semantics=("arbitrary", "arbitrary"), vmem_limit_bytes=VMEM_LIMIT),
        name="in_proj",
    )(x_f, x_t, g1, w_all, w_all, gate_bias, qg, kg, *prev)


def _diff_lambda(lq1_ref, lk1_ref, lq2_ref, lk2_ref, lam_init):
    a = jnp.sum(lq1_ref[...] * lk1_ref[...], axis=-1, keepdims=True)
    b = jnp.sum(lq2_ref[...] * lk2_ref[...], axis=-1, keepdims=True)
    return jnp.exp(a) - jnp.exp(b) + lam_init


def _subln(o, sg_ref, lam_init):
    return _rms(o, sg_ref[...]) * (1.0 - lam_init)


def _pad_rows(dst, src):
    dst[...] = jnp.zeros(dst.shape, dst.dtype)
    dst[0:src.shape[0], :] = src[...].astype(dst.dtype)


def _attn_prompt_kernel(q_ref, k_ref, v_ref, km_ref, vm_ref, bd_ref, bp_ref, bm_ref,
                        lq1_ref, lk1_ref, lq2_ref, lk2_ref, sg_ref, o_ref,
                        kb_s, vb_s, kmb_s, vmb_s, *, lam_init):
    T = ATT_TILE
    n_tiles = q_ref.shape[0] // T
    kb_s[...] = k_ref[...].astype(BF16)
    vb_s[...] = v_ref[...].astype(BF16)
    _pad_rows(kmb_s, km_ref)
    _pad_rows(vmb_s, vm_ref)
    lam = _diff_lambda(lq1_ref, lk1_ref, lq2_ref, lk2_ref, lam_init)

    for j in range(n_tiles):
        rows = slice(j * T, (j + 1) * T)
        regions = [(kmb_s, vmb_s, slice(0, LANES), bm_ref[min(j, 1)])]
        if j >= 2:
            regions.append((kb_s, vb_s, slice(0, (j - 1) * T), None))
        if j >= 1:
            regions.append((kb_s, vb_s, slice((j - 1) * T, j * T), bp_ref[...]))
        regions.append((kb_s, vb_s, rows, bd_ref[...]))

        probs, denom = [], []
        for c in range(2):
            cols = slice(c * DQK_ATT, (c + 1) * DQK_ATT)
            qc = q_ref[rows, cols].astype(BF16)
            ss = [_dot_nt(qc, kr[rs, cols]) for kr, _, rs, _ in regions]
            ss = [s if br is None else s + br for s, (_, _, _, br) in zip(ss, regions)]
            m = functools.reduce(jnp.maximum, [jnp.max(s, axis=-1, keepdims=True) for s in ss])
            pc = [jnp.exp2(s - m) for s in ss]
            denom.append(functools.reduce(jnp.add, [jnp.sum(p, axis=-1, keepdims=True) for p in pc]))
            probs.append(pc)
        a0 = 1.0 / denom[0]
        a1 = lam / denom[1]
        o = None
        for r, (_, vr, rs, _) in enumerate(regions):
            w = (probs[0][r] * a0 - probs[1][r] * a1).astype(BF16)
            t = _dot(w, vr[rs, :])
            o = t if o is None else o + t
        o_ref[rows, :] = _subln(o, sg_ref, lam_init)


def _attn_prompt(z_f, z_t, bias, lam_vecs, sg, B, S, lam_init):
    T = ATT_TILE
    qkw = 2 * DQK_ATT
    kcol = ATT_QK // qkw
    vcol = 2 * ATT_QK // DV_ATT
    b_diag, b_prev, b_meta, _, _ = bias
    vec = pl.BlockSpec((1, DQK_ATT), lambda b, h: (0, 0))
    return pl.pallas_call(
        functools.partial(_attn_prompt_kernel, lam_init=lam_init),
        grid=(B, H_ATT),
        in_specs=[
            pl.BlockSpec((S, qkw), lambda b, h: (b, h)),
            pl.BlockSpec((S, qkw), lambda b, h: (b, kcol + h)),
            pl.BlockSpec((S, DV_ATT), lambda b, h: (b, vcol + h)),
            pl.BlockSpec((N_META, qkw), lambda b, h: (b, kcol + h)),
            pl.BlockSpec((N_META, DV_ATT), lambda b, h: (b, vcol + h)),
            pl.BlockSpec((None, T, T), lambda b, h: (h, 0, 0)),
            pl.BlockSpec((None, T, T), lambda b, h: (h, 0, 0)),
            pl.BlockSpec((None, 2, T, LANES), lambda b, h: (h, 0, 0, 0)),
            vec, vec, vec, vec,
            pl.BlockSpec((1, DV_ATT), lambda b, h: (0, 0)),
        ],
        out_specs=pl.BlockSpec((S, DV_ATT), lambda b, h: (b, h)),
        out_shape=jax.ShapeDtypeStruct((B * S, ATT_V), F32),
        scratch_shapes=[
            pltpu.VMEM((S, qkw), BF16),
            pltpu.VMEM((S, DV_ATT), BF16),
            pltpu.VMEM((LANES, qkw), BF16),
            pltpu.VMEM((LANES, DV_ATT), BF16),
        ],
        compiler_params=pltpu.CompilerParams(
            dimension_semantics=("arbitrary", "arbitrary"), vmem_limit_bytes=VMEM_LIMIT),
        name="attn_prompt",
    )(z_f, z_f, z_f, z_t, z_t, b_diag, b_prev, b_meta, *lam_vecs, sg)


def _attn_small_kernel(*refs, lam_init, has_cache):
    if has_cache:
        (q_ref, kc_ref, vc_ref, kn_ref, vn_ref, b_ref,
         lq1_ref, lk1_ref, lq2_ref, lk2_ref, sg_ref, o_ref, kpad, vpad, kb_s, vb_s) = refs
        past_len = kb_s.shape[0]
        chunk = kc_ref.shape[0]
        c = pl.program_id(1)
        r0 = pl.multiple_of(c * chunk, chunk)
        kb_s[pl.ds(r0, chunk), :] = kc_ref[...].reshape(chunk, ATT_QK).astype(BF16)
        vb_s[pl.ds(r0, chunk), :] = vc_ref[...].reshape(chunk, ATT_V).astype(BF16)

        @pl.when(c == past_len // chunk - 1)
        def _():
            _attn_small_body(q_ref, kn_ref, vn_ref, b_ref, lq1_ref, lk1_ref, lq2_ref, lk2_ref, sg_ref, o_ref,
                             kpad, vpad, kb_s, vb_s, lam_init=lam_init, past_len=past_len)
    else:
        (q_ref, kn_ref, vn_ref, b_ref,
         lq1_ref, lk1_ref, lq2_ref, lk2_ref, sg_ref, o_ref, kpad, vpad) = refs
        _attn_small_body(q_ref, kn_ref, vn_ref, b_ref, lq1_ref, lk1_ref, lq2_ref, lk2_ref, sg_ref, o_ref,
                         kpad, vpad, None, None, lam_init=lam_init, past_len=0)


def _attn_small_body(q_ref, kn_ref, vn_ref, b_ref, lq1_ref, lk1_ref, lq2_ref, lk2_ref, sg_ref, o_ref,
                     kpad, vpad, kb_s, vb_s, *, lam_init, past_len):
    has_cache = past_len > 0
    lam = _diff_lambda(lq1_ref, lk1_ref, lq2_ref, lk2_ref, lam_init)
    _pad_rows(kpad, kn_ref)
    _pad_rows(vpad, vn_ref)
    for h in range(H_ATT):
        vsl = slice(h * DV_ATT, (h + 1) * DV_ATT)
        outs = []
        for c in range(2):
            sl = slice(h * 2 * DQK_ATT + c * DQK_ATT, h * 2 * DQK_ATT + (c + 1) * DQK_ATT)
            qc = q_ref[:, sl].astype(BF16)
            s_new = _dot_nt(qc, kpad[:, sl]) + b_ref[h, :, past_len:]
            m = jnp.max(s_new, axis=-1, keepdims=True)
            if has_cache:
                s_old = _dot_nt(qc, kb_s[:, sl]) + b_ref[h, :, :past_len]
                m = jnp.maximum(m, jnp.max(s_old, axis=-1, keepdims=True))
            p_new = jnp.exp2(s_new - m)
            l = jnp.sum(p_new, axis=-1, keepdims=True)
            o = _dot(p_new.astype(BF16), vpad[:, vsl])
            if has_cache:
                p_old = jnp.exp2(s_old - m)
                l = l + jnp.sum(p_old, axis=-1, keepdims=True)
                o = o + _dot(p_old.astype(BF16), vb_s[:, vsl])
            outs.append(o / l)
        o_ref[:, vsl] = _subln(outs[0] - lam * outs[1], sg_ref, lam_init)


def _attn_small(z_t, row0, n_streams, bias, lam_vecs, sg, lam_init, cache=None):
    vec = pl.BlockSpec((1, DQK_ATT), lambda b, c: (0, 0))
    in_specs = [pl.BlockSpec((N_META, ATT_QK), lambda b, c: (row0 + b, 0))]
    args = [z_t]
    scratch = [pltpu.VMEM((LANES, ATT_QK), BF16), pltpu.VMEM((LANES, ATT_V), BF16)]
    n_chunks = 1
    if cache is not None:
        ck, cv, layer = cache
        past_len = ck.shape[2]
        chunk = min(past_len, CACHE_CHUNK)
        assert past_len % chunk == 0
        n_chunks = past_len // chunk
        in_specs += [
            pl.BlockSpec((None, None, chunk, H_ATT, 2 * DQK_ATT), lambda b, c: (layer, b, c, 0, 0)),
            pl.BlockSpec((None, None, chunk, H_ATT, DV_ATT), lambda b, c: (layer, b, c, 0, 0))]
        args += [ck, cv]
        scratch += [pltpu.VMEM((past_len, ATT_QK), BF16), pltpu.VMEM((past_len, ATT_V), BF16)]
    in_specs += [
        pl.BlockSpec((N_META, ATT_QK), lambda b, c: (row0 + b, 1)),
        pl.BlockSpec((N_META, ATT_V), lambda b, c: (row0 + b, 2 * ATT_QK // ATT_V)),
        pl.BlockSpec(bias.shape, lambda b, c: (0, 0, 0)),
        vec, vec, vec, vec,
        pl.BlockSpec((1, DV_ATT), lambda b, c: (0, 0)),
    ]
    args += [z_t, z_t, bias, *lam_vecs, sg]
    return pl.pallas_call(
        functools.partial(_attn_small_kernel, lam_init=lam_init, has_cache=cache is not None),
        grid=(n_streams, n_chunks),
        in_specs=in_specs,
        out_specs=pl.BlockSpec((N_META, ATT_V), lambda b, c: (b, 0)),
        out_shape=jax.ShapeDtypeStruct((n_streams * N_META, ATT_V), F32),
        scratch_shapes=scratch,
        compiler_params=pltpu.CompilerParams(
            dimension_semantics=("arbitrary", "arbitrary"), vmem_limit_bytes=VMEM_LIMIT),
        name="attn_cached" if cache is not None else "attn_meta",
    )(*args)


def _mlstm_kernel(*refs, n_blocks, has_init):
    refs = list(refs)
    qh_ref, kh_ref, vh_ref, gh_ref = refs[:4]
    pos = 4
    if n_blocks:
        qf_ref, kf_ref, vf_ref, gf_ref = refs[pos:pos + 4]
        pos += 4
    if has_init:
        c0_ref, n0_ref, m0_ref = refs[pos:pos + 3]
        pos += 3
    hh_ref = refs[pos]
    pos += 1
    if n_blocks:
        hf_ref = refs[pos]
        pos += 1
    c_ref, n_ref, m_ref = refs[pos:pos + 3]
    qs, ks, vs, gs = refs[pos + 3:pos + 7]

    t = pl.program_id(1)
    T = M_TILE
    block = functools.partial(_mlstm_block, c_ref=c_ref, n_ref=n_ref, m_ref=m_ref)

    @pl.when(t == 0)
    def _():
        if has_init:
            c_ref[...] = c0_ref[...]
            n_ref[...] = n0_ref[...]
            m_ref[...] = m0_ref[...]
        else:
            c_ref[...] = jnp.zeros(c_ref.shape, F32)
            n_ref[...] = jnp.zeros(n_ref.shape, F32)
            m_ref[...] = jnp.zeros(m_ref.shape, F32)
        nh = qh_ref.shape[0]
        for dst, src in ((qs, qh_ref), (ks, kh_ref), (vs, vh_ref)):
            dst[...] = jnp.zeros(dst.shape, F32)
            dst[0:nh, :] = src[...]
        lane = lax.broadcasted_iota(jnp.int32, gs.shape, 1)
        gs[...] = jnp.where(lane < H_M, NEG, 0.0)
        gs[0:nh, :] = gh_ref[...]

        block(qs, ks, vs, gs, hh_ref, nh)

    if n_blocks:
        @pl.when(t > 0)
        def _():
            block(qf_ref, kf_ref, vf_ref, gf_ref, hf_ref, T)


def _mlstm_block(q_src, k_src, v_src, g_src, h_dst, n_out, c_ref, n_ref, m_ref):
    T = M_TILE
    g = g_src[...]
    row = lax.broadcasted_iota(jnp.int32, (T, T), 0)
    colid = lax.broadcasted_iota(jnp.int32, (T, T), 1)
    causal = row >= colid
    tril = jnp.where(causal, 1.0, 0.0).astype(F32)
    cum = jnp.dot(tril, g, preferred_element_type=F32, precision=lax.Precision.HIGHEST)
    g_t = g.T
    cum_t = cum.T
    last = cum[T - 1:T, :]

    for h in range(H_M):
        q = q_src[:, h * DQK_M:(h + 1) * DQK_M]
        k = k_src[:, h * DQK_M:(h + 1) * DQK_M]
        v = v_src[:, h * DV_M:(h + 1) * DV_M]
        qb = q.astype(BF16)
        kb = k.astype(BF16)
        m_old = m_ref[h, 0:1, 0:1]
        c_old = c_ref[h]
        n_old = n_ref[h]
        bh_c = _col(cum, H_M + h)
        ih_c = _col(g, h)
        bh_r = cum_t[H_M + h:H_M + h + 1, :]
        ih_r = g_t[h:h + 1, :]
        d = jnp.where(causal, bh_c - bh_r + ih_r, NEG)
        inter = bh_c + m_old
        mt = jnp.maximum(inter, jnp.max(d, axis=-1, keepdims=True))
        a = jnp.exp(d - mt) * _dot_nt(qb, kb)
        iw = jnp.exp(inter - mt)
        num = _dot(a.astype(BF16), v.astype(BF16)) + iw * _dot_nt(qb, c_old.astype(BF16))
        den = jnp.sum(a, axis=-1, keepdims=True) + iw * jnp.sum(q * n_old, axis=-1, keepdims=True)
        den = jnp.maximum(jnp.abs(den), jnp.exp(-mt))
        hout = num / den
        h_dst[:, h * DV_M:(h + 1) * DV_M] = hout[0:n_out, :]

        bl = _col(last, H_M + h)
        g_c = bl - bh_c + ih_c
        m_new = jnp.maximum(bl + m_old, jnp.max(g_c, axis=0, keepdims=True))
        sw = jnp.exp(g_c - m_new)
        dec = jnp.exp(bl + m_old - m_new)
        c_ref[h] = dec * c_old + _dot(v.T.astype(BF16), (sw * k).astype(BF16))
        n_ref[h] = dec * n_old + jnp.sum(sw * k, axis=0, keepdims=True)
        m_ref[h] = jnp.broadcast_to(m_new, m_ref.shape[1:])


def _mlstm(z_t, g_t, row0, n_streams, frames=None, init=None):
    T = M_TILE
    qcol = (2 * ATT_QK + ATT_V) // M_QK
    vcol = (2 * ATT_QK + ATT_V + 2 * M_QK) // M_V
    nh = N_META
    in_specs = [
        pl.BlockSpec((nh, M_QK), lambda b, t: (row0 + b, qcol)),
        pl.BlockSpec((nh, M_QK), lambda b, t: (row0 + b, qcol + 1)),
        pl.BlockSpec((nh, M_V), lambda b, t: (row0 + b, vcol)),
        pl.BlockSpec((nh, LANES), lambda b, t: (row0 + b, 0)),
    ]
    args = [z_t, z_t, z_t, g_t]
    n_blocks = 0
    if frames is not None:
        z_f, g_f, fps = frames
        n_blocks = fps // T
        fidx = lambda b, t: b * n_blocks + jnp.maximum(t - 1, 0)
        in_specs += [
            pl.BlockSpec((T, M_QK), lambda b, t: (fidx(b, t), qcol)),
            pl.BlockSpec((T, M_QK), lambda b, t: (fidx(b, t), qcol + 1)),
            pl.BlockSpec((T, M_V), lambda b, t: (fidx(b, t), vcol)),
            pl.BlockSpec((T, LANES), lambda b, t: (fidx(b, t), 0)),
        ]
        args += [z_f, z_f, z_f, g_f]
    state_specs = (
        pl.BlockSpec((None, H_M, DV_M, DQK_M), lambda b, t: (b, 0, 0, 0)),
        pl.BlockSpec((None, H_M, 1, DQK_M), lambda b, t: (b, 0, 0, 0)),
        pl.BlockSpec((None, H_M, 8, LANES), lambda b, t: (b, 0, 0, 0)),
    )
    if init is not None:
        in_specs += list(state_specs)
        args += list(init)
    out_specs = [pl.BlockSpec((nh, M_V), lambda b, t: (b, 0))]
    out_shape = [jax.ShapeDtypeStruct((n_streams * nh, M_V), F32)]
    if n_blocks:
        out_specs.append(pl.BlockSpec((T, M_V), lambda b, t: (fidx(b, t), 0)))
        out_shape.append(jax.ShapeDtypeStruct((n_streams * fps, M_V), F32))
    out_specs += list(state_specs)
    out_shape += [
        jax.ShapeDtypeStruct((n_streams, H_M, DV_M, DQK_M), F32),
        jax.ShapeDtypeStruct((n_streams, H_M, 1, DQK_M), F32),
        jax.ShapeDtypeStruct((n_streams, H_M, 8, LANES), F32),
    ]
    return pl.pallas_call(
        functools.partial(_mlstm_kernel, n_blocks=n_blocks, has_init=init is not None),
        grid=(n_streams, 1 + n_blocks),
        in_specs=in_specs,
        out_specs=tuple(out_specs),
        out_shape=tuple(out_shape),
        scratch_shapes=[
            pltpu.VMEM((T, M_QK), F32), pltpu.VMEM((T, M_QK), F32),
            pltpu.VMEM((T, M_V), F32), pltpu.VMEM((T, LANES), F32),
        ],
        compiler_params=pltpu.CompilerParams(
            dimension_semantics=("arbitrary", "arbitrary"), vmem_limit_bytes=VMEM_LIMIT),
        name="mlstm_prompt" if n_blocks else "mlstm_cached",
    )(*args)


def _merge_kernel(xf_ref, xt_ref, attf_ref, attt_ref, hmf_ref, hmt_ref, zf_ref, zt_ref, mg_ref, wo_ref,
                  of_ref, ot_ref, *, n_frame_tiles):
    i = pl.program_id(0)

    def tile(x_ref, att_ref, hm_ref, og_ref, o_ref):
        parts = []
        for h in range(H_M):
            sl = slice(h * DV_M, (h + 1) * DV_M)
            parts.append(_rms(hm_ref[:, sl], mg_ref[:, sl]) * og_ref[:, sl])
        hmn = jnp.concatenate(parts, axis=1).astype(BF16)
        o_ref[...] = (x_ref[...] + _dot(att_ref[...].astype(BF16), wo_ref[0:ATT_V, :])
                      + _dot(hmn, wo_ref[ATT_V:, :]))

    @pl.when(i < n_frame_tiles)
    def _():
        tile(xf_ref, attf_ref, hmf_ref, zf_ref, of_ref)

    @pl.when(i == n_frame_tiles)
    def _():
        tile(xt_ref, attt_ref, hmt_ref, zt_ref, ot_ref)


def _merge(x_f, x_t, att_f, att_t, hm_f, hm_t, z_f, z_t, mg, w_out, layer):
    NF, D = x_f.shape
    NT = x_t.shape[0]
    bm = MERGE_ROWS
    nft = NF // bm
    assert NF % bm == 0
    ogcol = (N_MAIN - M_V) // M_V
    fr = lambda i: (jnp.minimum(i, nft - 1), 0)
    tl = lambda i: (0, 0)
    once = dict(pipeline_mode=pl.Buffered(1))
    return pl.pallas_call(
        functools.partial(_merge_kernel, n_frame_tiles=nft),
        grid=(nft + 1,),
        in_specs=[
            pl.BlockSpec((bm, D), fr), pl.BlockSpec((NT, D), tl, **once),
            pl.BlockSpec((bm, ATT_V), fr), pl.BlockSpec((NT, ATT_V), tl, **once),
            pl.BlockSpec((bm, M_V), fr), pl.BlockSpec((NT, M_V), tl, **once),
            pl.BlockSpec((bm, M_V), lambda i: (jnp.minimum(i, nft - 1), ogcol)),
            pl.BlockSpec((NT, M_V), lambda i: (0, ogcol), **once),
            pl.BlockSpec((1, M_V), tl),
            pl.BlockSpec((None, D, D), lambda i: (layer, 0, 0), **once),
        ],
        out_specs=(pl.BlockSpec((bm, D), fr), pl.BlockSpec((NT, D), tl)),
        out_shape=(jax.ShapeDtypeStruct((NF, D), F32), jax.ShapeDtypeStruct((NT, D), F32)),
        compiler_params=pltpu.CompilerParams(
            dimension_semantics=("arbitrary",), vmem_limit_bytes=VMEM_LIMIT),
        name="merge_out_proj",
    )(x_f, x_t, att_f, att_t, hm_f, hm_t, z_f, z_t, mg, w_out)


def _ffn_kernel(xf_ref, xt_ref, g2_ref, wu_ref, wd_ref, of_ref, ot_ref, xn_s, *, n_frame_tiles):
    i = pl.program_id(0)
    k = pl.program_id(1)

    def tile(x_ref, o_ref):
        nrows = x_ref.shape[0]

        @pl.when(k == 0)
        def _():
            x = x_ref[...]
            xn_s[0:nrows, :] = _rms(x, g2_ref[...]).astype(BF16)
            o_ref[...] = x

        u = jnp.maximum(_dot(xn_s[0:nrows, :], wu_ref[...]), 0.0)
        o_ref[...] += _dot((u * u).astype(BF16), wd_ref[...])

    @pl.when(i < n_frame_tiles)
    def _():
        tile(xf_ref, of_ref)

    @pl.when(i == n_frame_tiles)
    def _():
        tile(xt_ref, ot_ref)


def _ffn(x_f, x_t, g2, w_up, w_down, layer):
    NF, D = x_f.shape
    NT = x_t.shape[0]
    FF = w_up.shape[2]
    bm, fc = FF_ROWS, FF_COLS
    nft = NF // bm
    assert NF % bm == 0 and FF % fc == 0 and NT <= bm
    fr = lambda i, k: (jnp.minimum(i, nft - 1), 0)
    tl = lambda i, k: (0, 0)
    return pl.pallas_call(
        functools.partial(_ffn_kernel, n_frame_tiles=nft),
        grid=(nft + 1, FF // fc),
        in_specs=[
            pl.BlockSpec((bm, D), fr),
            pl.BlockSpec((NT, D), tl),
            pl.BlockSpec((1, D), tl),
            pl.BlockSpec((None, D, fc), lambda i, k: (layer, 0, k)),
            pl.BlockSpec((None, fc, D), lambda i, k: (layer, k, 0)),
        ],
        out_specs=(pl.BlockSpec((bm, D), fr), pl.BlockSpec((NT, D), tl)),
        out_shape=(jax.ShapeDtypeStruct((NF, D), F32), jax.ShapeDtypeStruct((NT, D), F32)),
        scratch_shapes=[pltpu.VMEM((bm, D), BF16)],
        compiler_params=pltpu.CompilerParams(
            dimension_semantics=("arbitrary", "arbitrary"), vmem_limit_bytes=VMEM_LIMIT),
        name="ffn",
    )(x_f, x_t, g2, w_up, w_down)


def kernel(x_prompt, x_sample, cache_k, cache_v, state_C, state_n, state_m, meta_tokens, rel_bias, norm1_g, w_in, b_i, b_f, q_norm_g, k_norm_g, lambda_q1, lambda_k1, lambda_q2, lambda_k2, subln_g, mnorm_g, w_out, norm2_g, w_up, w_down):
    B, S, D = x_prompt.shape
    Bd, Td, _ = x_sample.shape
    depth = w_in.shape[0]
    Pc = cache_k.shape[2]
    assert Td == N_META and S % ATT_TILE == 0 and S % M_TILE == 0 and Pc % LANES == 0
    assert ATT_TILE % CHUNK == 0 and CHUNK & (CHUNK - 1) == 0 and ATT_TILE >= MAX_DISTANCE

    n_frames = B * S
    n_meta = B * N_META
    n_samp = Bd * Td
    assert n_meta + n_samp <= TAIL_ROWS
    srow = n_meta // N_META

    meta = jnp.broadcast_to(meta_tokens.astype(F32)[None], (B, N_META, D)).reshape(n_meta, D)
    x_f = x_prompt.astype(F32).reshape(n_frames, D)
    x_t = jnp.concatenate([meta, x_sample.astype(F32).reshape(n_samp, D),
                           jnp.zeros((TAIL_ROWS - n_meta - n_samp, D), F32)], axis=0)

    bias = _bias_tiles(rel_bias.astype(F32), Pc, Td)
    ck = cache_k.astype(F32)
    cv = cache_v.astype(F32)
    zeros_pad_att = jnp.zeros((TAIL_ROWS - n_meta - n_samp, ATT_V), F32)

    g_off = 2 * ATT_QK + ATT_V + 2 * M_QK + M_V
    w_in_t = jnp.swapaxes(w_in.astype(F32), 1, 2)
    w_out_b, w_up_b, w_down_b = (w.astype(BF16) for w in (w_out, w_up, w_down))
    outs = {name: [] for name in ("Cp", "np", "mp", "Cs", "ns", "ms")}
    caches = None

    for l in range(depth):
        lam_init = 0.8 - 0.6 * math.exp(-0.3 * l)
        gate_bias = jnp.pad(jnp.concatenate([b_i[l], b_f[l]]).astype(F32), (0, LANES - 2 * H_M))[None]
        lam_vecs = [v[l].astype(F32)[None] for v in (lambda_q1, lambda_k1, lambda_q2, lambda_k2)]
        sg = subln_g[l].astype(F32)[None]

        z_f, z_t, g_f, g_t, *caches = _inproj(
            x_f, x_t, norm1_g[l].astype(F32)[None], w_in_t, l, gate_bias,
            q_norm_g[l].astype(F32)[None], k_norm_g[l].astype(F32)[None], B, Bd, caches)

        att_f = _attn_prompt(z_f, z_t, bias, lam_vecs, sg, B, S, lam_init)
        att_m = _attn_small(z_t, 0, B, bias[3], lam_vecs, sg, lam_init)
        att_s = _attn_small(z_t, srow, Bd, bias[4], lam_vecs, sg, lam_init, cache=(ck, cv, l))
        att_t = jnp.concatenate([att_m, att_s, zeros_pad_att], axis=0)

        hm_m, hm_f, C_p, n_p, m_p = _mlstm(z_t, g_t, 0, B, frames=(z_f, g_f, S))
        init = (state_C[l].astype(F32), state_n[l].astype(F32)[:, :, None, :],
                jnp.broadcast_to(state_m[l].astype(F32)[:, :, None, None], (Bd, H_M, 8, LANES)))
        hm_s, C_s, n_s, m_s = _mlstm(z_t, g_t, srow, Bd, init=init)
        hm_t = jnp.concatenate([hm_m, hm_s, zeros_pad_att], axis=0)

        x_f, x_t = _merge(x_f, x_t, att_f, att_t, hm_f, hm_t, z_f, z_t,
                          mnorm_g[l].astype(F32)[None], w_out_b, l)
        x_f, x_t = _ffn(x_f, x_t, norm2_g[l].astype(F32)[None], w_up_b, w_down_b, l)

        outs["Cp"].append(C_p)
        outs["np"].append(n_p[:, :, 0, :])
        outs["mp"].append(m_p[:, :, 0, 0])
        outs["Cs"].append(C_s)
        outs["ns"].append(n_s[:, :, 0, :])
        outs["ms"].append(m_s[:, :, 0, 0])

    y_prompt = x_f.reshape(B, S, D)
    y_sample = x_t[n_meta:n_meta + n_samp].reshape(Bd, Td, D)
    k_p, v_p, k_s, v_s = caches
    return (y_prompt, y_sample,
            k_p, v_p, jnp.stack(outs["Cp"]), jnp.stack(outs["np"]), jnp.stack(outs["mp"]),
            k_s.reshape(depth, Bd, Td, H_ATT, 2 * DQK_ATT), v_s.reshape(depth, Bd, Td, H_ATT, DV_ATT),
            jnp.stack(outs["Cs"]), jnp.stack(outs["ns"]), jnp.stack(outs["ms"]))
```

```python
import functools
import math

import jax
import jax.numpy as jnp
from jax import lax
from jax.experimental import pallas as pl
from jax.experimental.pallas import tpu as pltpu

F32 = jnp.float32
BF16 = jnp.bfloat16

CHUNK = 64
N_META = 16
H_ATT = 4
DQK_ATT = 128
DV_ATT = 256
H_M = 4
DQK_M = 128
DV_M = 256
N_BUCKETS = 32
MAX_DISTANCE = 128
EPS = 1e-6
NEG = -1e30
LOG2E = math.log2(math.e)

ATT_QK = H_ATT * 2 * DQK_ATT
ATT_V = H_ATT * DV_ATT
M_QK = H_M * DQK_M
M_V = H_M * DV_M
N_MAIN = 2 * ATT_QK + ATT_V + 2 * M_QK + M_V + M_V
N_ATT = 2 * ATT_QK + ATT_V

LANES = 128
ATT_TILE = 256
M_TILE = 256
TAIL_ROWS = 256
PROJ_ROWS = 1024
PROJ_COLS = 512
MERGE_ROWS = 512
CACHE_CHUNK = 1024
FF_ROWS = 512
FF_COLS = 1024
VMEM_LIMIT = 56 * 1024 * 1024

_NT = (((1,), (1,)), ((), ()))


def _dot(a, b):
    return jnp.dot(a, b, preferred_element_type=F32)


def _dot_nt(a, b):
    return lax.dot_general(a, b, _NT, preferred_element_type=F32)


def _log_sigmoid(x):
    return jnp.minimum(x, 0.0) - jnp.log1p(jnp.exp(-jnp.abs(x)))


def _col(a, idx):
    lane = lax.broadcasted_iota(jnp.int32, a.shape, 1)
    return jnp.sum(jnp.where(lane == idx, a, 0.0), axis=1, keepdims=True)


def _rms(x, gain):
    ms = jnp.mean(x * x, axis=-1, keepdims=True)
    return x * lax.rsqrt(ms + EPS) * gain


def _bias_from_rel(rel, rb_ref, h):
    nb = N_BUCKETS // 2
    max_exact = nb // 2
    ret = jnp.where(rel > 0, nb, 0)
    n = jnp.abs(rel)
    nf = jnp.maximum(n, 1).astype(F32)
    large = max_exact + (jnp.log(nf / max_exact) / math.log(MAX_DISTANCE / max_exact)
                         * (nb - max_exact)).astype(jnp.int32)
    large = jnp.minimum(large, nb - 1)
    bucket = ret + jnp.where(n < max_exact, n, large)
    out = jnp.zeros(rel.shape, F32)
    for b in range(N_BUCKETS):
        out = jnp.where(bucket == b, rb_ref[b, h], out)
    return out


def _bias_kernel(rb_ref, diag_ref, prev_ref, meta_ref, mm_ref, samp_ref, *, past_len, dec_seq):
    h = pl.program_id(0)
    shift = CHUNK.bit_length() - 1
    T = ATT_TILE
    far = _bias_from_rel(jnp.full((8, LANES), -(2 * MAX_DISTANCE + 1), jnp.int32), rb_ref, h)[0:1, 0:1]

    qi = lax.broadcasted_iota(jnp.int32, (T, T), 0)
    kj = lax.broadcasted_iota(jnp.int32, (T, T), 1)
    bias = (_bias_from_rel(kj - qi, rb_ref, h) - far) * LOG2E
    diag_ref[...] = jnp.where((kj >> shift) <= (qi >> shift), bias, NEG)
    prev_ref[...] = (_bias_from_rel(kj - T - qi, rb_ref, h) - far) * LOG2E

    qi = lax.broadcasted_iota(jnp.int32, (T, LANES), 0)
    kj = lax.broadcasted_iota(jnp.int32, (T, LANES), 1)
    for var in range(2):
        bias = (_bias_from_rel(kj - (N_META + qi + var * T), rb_ref, h) - far) * LOG2E
        meta_ref[var] = jnp.where(kj < N_META, bias, NEG)

    qi = lax.broadcasted_iota(jnp.int32, (N_META, LANES), 0)
    kj = lax.broadcasted_iota(jnp.int32, (N_META, LANES), 1)
    mm_ref[...] = jnp.where(kj < N_META, _bias_from_rel(kj - qi, rb_ref, h) * LOG2E, NEG)

    width = samp_ref.shape[-1]
    qi = lax.broadcasted_iota(jnp.int32, (dec_seq, width), 0) + past_len
    kj = lax.broadcasted_iota(jnp.int32, (dec_seq, width), 1)
    bias = _bias_from_rel(kj - qi, rb_ref, h) * LOG2E
    ok = ((kj >> shift) <= (qi >> shift)) & (kj < past_len + dec_seq)
    samp_ref[...] = jnp.where(ok, bias, NEG)


def _bias_tiles(rel_bias, past_len, dec_seq):
    T = ATT_TILE
    sw = past_len + LANES
    out_shape = (
        jax.ShapeDtypeStruct((H_ATT, T, T), F32),
        jax.ShapeDtypeStruct((H_ATT, T, T), F32),
        jax.ShapeDtypeStruct((H_ATT, 2, T, LANES), F32),
        jax.ShapeDtypeStruct((H_ATT, N_META, LANES), F32),
        jax.ShapeDtypeStruct((H_ATT, dec_seq, sw), F32),
    )
    out_specs = (
        pl.BlockSpec((None, T, T), lambda h: (h, 0, 0)),
        pl.BlockSpec((None, T, T), lambda h: (h, 0, 0)),
        pl.BlockSpec((None, 2, T, LANES), lambda h: (h, 0, 0, 0)),
        pl.BlockSpec((None, N_META, LANES), lambda h: (h, 0, 0)),
        pl.BlockSpec((None, dec_seq, sw), lambda h: (h, 0, 0)),
    )
    return pl.pallas_call(
        functools.partial(_bias_kernel, past_len=past_len, dec_seq=dec_seq),
        grid=(H_ATT,),
        in_specs=[pl.BlockSpec(memory_space=pltpu.SMEM)],
        out_specs=out_specs,
        out_shape=out_shape,
        name="rel_bias_tiles",
    )(rel_bias)


def _zero_fill_copies(zbuf, zsem, cache_refs, depth):
    rows = zbuf.shape[0]
    copies = []
    for dst in cache_refs:
        lead = dst.shape[1:-3]
        n_rows = dst.shape[-3]
        for l in range(1, depth):
            for b in range(lead[0] if lead else 1):
                view = dst.at[l, b] if lead else dst.at[l]
                for r0 in range(0, n_rows, rows):
                    n = min(rows, n_rows - r0)
                    copies.append(pltpu.make_async_copy(zbuf.at[pl.ds(0, n)], view.at[pl.ds(r0, n)],
                                                        zsem.at[len(copies)]))
    return copies


def _kv_copies(stage, sem, dst_prompt, dst_sample, i, *, layer, tail, n_batch, frames_per_batch, n_sample_rows):
    dst_prompt = dst_prompt.at[layer]
    dst_sample = dst_sample.at[layer]
    if not tail:
        rows = stage.shape[0]
        tiles_per_batch = frames_per_batch // rows
        b = i // tiles_per_batch
        t = i - b * tiles_per_batch
        return [pltpu.make_async_copy(stage, dst_prompt.at[b, pl.ds(N_META + t * rows, rows)], sem.at[0])]
    copies = [pltpu.make_async_copy(stage.at[pl.ds(b * N_META, N_META)], dst_prompt.at[b, pl.ds(0, N_META)],
                                    sem.at[b]) for b in range(n_batch)]
    copies.append(pltpu.make_async_copy(stage.at[pl.ds(n_batch * N_META, n_sample_rows)], dst_sample,
                                        sem.at[n_batch]))
    return copies


def _inproj_kernel(*refs, layer, depth, n_frame_tiles, n_batch, frames_per_batch, n_sample_rows):
    xf_ref, xt_ref, g1_ref, wa_ref, wg_ref, gb_ref, qg_ref, kg_ref = refs[:8]
    (zaf_ref, zat_ref, zmf_ref, zmt_ref, gf_ref, gt_ref, kp_ref, vp_ref, ks_ref, vs_ref,
     xn_s, y2d, stage, sem, zbuf, zsem) = refs[-16:]
    i = pl.program_id(0)
    j = pl.program_id(1)
    bn = wa_ref.shape[1]

    if layer == 0 and depth > 1:
        zero_copies = _zero_fill_copies(zbuf, zsem, (kp_ref, vp_ref, ks_ref, vs_ref), depth)

        @pl.when((i == 0) & (j == 0))
        def _():
            zbuf[...] = jnp.zeros(zbuf.shape, F32)
            for c in zero_copies:
                c.start()

        @pl.when((i == pl.num_programs(0) - 1) & (j == pl.num_programs(1) - 1))
        def _():
            for c in zero_copies:
                c.wait()

    assert ATT_QK % bn == 0 and ATT_V % bn == 0 and M_QK % bn == 0 and M_V % bn == 0 and bn % DV_ATT == 0
    j_k = ATT_QK // bn
    j_v = j_k + ATT_QK // bn
    j_mq = j_v + ATT_V // bn
    j_mk = j_mq + M_QK // bn
    j_mv = j_mk + M_QK // bn
    j_og = j_mv + M_V // bn

    def tile(x_ref, za_ref, zm_ref, g_ref, tail):
        nrows = x_ref.shape[0]
        st = stage.at[pl.ds(0, nrows)]
        copies = functools.partial(_kv_copies, st, sem, i=i, layer=layer, tail=tail, n_batch=n_batch,
                                   frames_per_batch=frames_per_batch, n_sample_rows=n_sample_rows)

        @pl.when(j == 0)
        def _():
            xb = _rms(x_ref[...], g1_ref[...]).astype(BF16)
            xn_s[0:nrows, :] = xb
            gates = _dot_nt(xb, wg_ref[0].astype(BF16)) + gb_ref[...]
            lane = lax.broadcasted_iota(jnp.int32, gates.shape, 1)
            g_ref[...] = jnp.where(lane < H_M, gates, _log_sigmoid(gates))

        acc = _dot_nt(xn_s[0:nrows, :], wa_ref[0].astype(BF16))

        def group_norm(gain_ref, scale):
            parts = []
            for gi in range(bn // DQK_ATT):
                y = _rms(acc[:, gi * DQK_ATT:(gi + 1) * DQK_ATT], gain_ref[...])
                parts.append(y if scale is None else y * scale)
            return jnp.concatenate(parts, axis=1)

        def stage_heads(y, jj, j0, dst_prompt, dst_sample):
            c0 = (jj - j0) * bn
            y2d[0:nrows, c0:c0 + bn] = y
            if jj == j0 + ATT_QK // bn - 1:
                if j0 != j_k:
                    for c in copies(kp_ref, ks_ref):
                        c.wait()
                st[...] = y2d[0:nrows, :].reshape(nrows, H_ATT, DV_ATT)
                for c in copies(dst_prompt, dst_sample):
                    c.start()

        @pl.when(j < j_k)
        def _():
            za_ref[...] = group_norm(qg_ref, DQK_ATT ** -0.5 * LOG2E).astype(BF16)

        for jj in range(j_k, j_v):
            @pl.when(j == jj)
            def _(jj=jj):
                y = group_norm(kg_ref, None)
                za_ref[...] = y.astype(BF16)
                stage_heads(y, jj, j_k, kp_ref, ks_ref)

        for jj in range(j_v, j_mq):
            @pl.when(j == jj)
            def _(jj=jj):
                za_ref[...] = acc.astype(BF16)
                stage_heads(acc, jj, j_v, vp_ref, vs_ref)

        @pl.when(j == j_mq)
        def _():
            for c in copies(vp_ref, vs_ref):
                c.wait()

        @pl.when(((j >= j_mq) & (j < j_mk)) | ((j >= j_mv) & (j < j_og)))
        def _():
            zm_ref[...] = acc

        @pl.when((j >= j_mk) & (j < j_mv))
        def _():
            zm_ref[...] = acc * (DQK_M ** -0.5)

        @pl.when(j >= j_og)
        def _():
            zm_ref[...] = jax.nn.sigmoid(acc)

    @pl.when(i < n_frame_tiles)
    def _():
        tile(xf_ref, zaf_ref, zmf_ref, gf_ref, False)

    @pl.when(i == n_frame_tiles)
    def _():
        tile(xt_ref, zat_ref, zmt_ref, gt_ref, True)


def _inproj(x_f, x_t, g1, w_all, layer, gate_bias, qg, kg, n_batch, n_sample_streams, caches):
    depth = w_all.shape[0]
    NF, D = x_f.shape
    NT = x_t.shape[0]
    bm, bn = PROJ_ROWS, PROJ_COLS
    nft = NF // bm
    nj = N_MAIN // bn
    nja = N_ATT // bn
    g_row = N_MAIN - M_V
    assert g_row % bn == 0 and g_row % 8 == 0 and w_all.shape[1] >= g_row + LANES

    def w_window(i, j):
        row = j * bn + jnp.where(j * bn >= g_row, 2 * H_M, 0)
        return (layer, pl.multiple_of(row, 8), 0)

    S = NF // n_batch
    L = N_META + S
    n_sample_rows = n_sample_streams * N_META
    assert NF % bm == 0 and S % bm == 0 and N_MAIN % bn == 0 and NT <= bm
    fr = lambda i: jnp.minimum(i, nft - 1)
    any_spec = pl.BlockSpec(memory_space=pl.ANY)
    assert (caches is None) == (layer == 0)
    prev = () if caches is None else tuple(caches)
    n_in = 8
    zero_rows = 256
    n_zero = (depth - 1) * (2 * n_batch * pl.cdiv(L, zero_rows) + 2 * pl.cdiv(n_sample_rows, zero_rows))
    return pl.pallas_call(
        functools.partial(_inproj_kernel, layer=layer, depth=depth, n_frame_tiles=nft, n_batch=n_batch,
                          frames_per_batch=S, n_sample_rows=n_sample_rows),
        grid=(nft + 1, nj),
        input_output_aliases={n_in + c: 6 + c for c in range(len(prev))},
        in_specs=[
            pl.BlockSpec((bm, D), lambda i, j: (fr(i), 0)),
            pl.BlockSpec((NT, D), lambda i, j: (0, 0), pipeline_mode=pl.Buffered(1)),
            pl.BlockSpec((1, D), lambda i, j: (0, 0)),
            pl.BlockSpec((pl.Element(1), pl.Element(bn), pl.Element(D)), w_window),
            pl.BlockSpec((pl.Element(1), pl.Element(LANES), pl.Element(D)), lambda i, j: (layer, g_row, 0)),
            pl.BlockSpec((1, LANES), lambda i, j: (0, 0)),
            pl.BlockSpec((1, DQK_ATT), lambda i, j: (0, 0)),
            pl.BlockSpec((1, DQK_ATT), lambda i, j: (0, 0)),
        ] + [any_spec] * len(prev),
        out_specs=(
            pl.BlockSpec((bm, bn), lambda i, j: (fr(i), jnp.where(i < nft, jnp.minimum(j, nja - 1), nja - 1))),
            pl.BlockSpec((NT, bn), lambda i, j: (0, jnp.where(i < nft, 0, jnp.minimum(j, nja - 1)))),
            pl.BlockSpec((bm, bn), lambda i, j: (fr(i), jnp.where(i < nft, jnp.maximum(j - nja, 0), nj - nja - 1))),
            pl.BlockSpec((NT, bn), lambda i, j: (0, jnp.where(i < nft, 0, jnp.maximum(j - nja, 0)))),
            pl.BlockSpec((bm, LANES), lambda i, j: (fr(i), 0)),
            pl.BlockSpec((NT, LANES), lambda i, j: (0, 0)),
            any_spec, any_spec, any_spec, any_spec,
        ),
        out_shape=(
            jax.ShapeDtypeStruct((NF, N_ATT), BF16),
            jax.ShapeDtypeStruct((NT, N_ATT), BF16),
            jax.ShapeDtypeStruct((NF, N_MAIN - N_ATT), F32),
            jax.ShapeDtypeStruct((NT, N_MAIN - N_ATT), F32),
            jax.ShapeDtypeStruct((NF, LANES), F32),
            jax.ShapeDtypeStruct((NT, LANES), F32),
            jax.ShapeDtypeStruct((depth, n_batch, L, H_ATT, 2 * DQK_ATT), F32),
            jax.ShapeDtypeStruct((depth, n_batch, L, H_ATT, DV_ATT), F32),
            jax.ShapeDtypeStruct((depth, n_sample_rows, H_ATT, 2 * DQK_ATT), F32),
            jax.ShapeDtypeStruct((depth, n_sample_rows, H_ATT, DV_ATT), F32),
        ),
        scratch_shapes=[
            pltpu.VMEM((bm, D), BF16),
            pltpu.VMEM((bm, ATT_QK), F32),
            pltpu.VMEM((bm, H_ATT, DV_ATT), F32),
            pltpu.SemaphoreType.DMA((n_batch + 1,)),
            pltpu.VMEM((zero_rows, H_ATT, DV_ATT), F32),
            pltpu.SemaphoreType.DMA((max(n_zero, 1),)),
        ],
        compiler_params=pltpu.CompilerParams(
            dimension_semantics=("arbitrary", "arbitrary"), vmem_limit_bytes=VMEM_LIMIT),
        name="in_proj",
    )(x_f, x_t, g1, w_all, w_all, gate_bias, qg, kg, *prev)


def _diff_lambda(lq1_ref, lk1_ref, lq2_ref, lk2_ref, lam_init):
    a = jnp.sum(lq1_ref[...] * lk1_ref[...], axis=-1, keepdims=True)
    b = jnp.sum(lq2_ref[...] * lk2_ref[...], axis=-1, keepdims=True)
    return jnp.exp(a) - jnp.exp(b) + lam_init


def _subln(o, sg_ref, lam_init):
    return _rms(o, sg_ref[...]) * (1.0 - lam_init)


def _pad_rows(dst, src):
    dst[...] = jnp.zeros(dst.shape, dst.dtype)
    dst[0:src.shape[0], :] = src[...].astype(dst.dtype)


def _attn_prompt_kernel(q_ref, k_ref, v_ref, km_ref, vm_ref, bd_ref, bp_ref, bm_ref,
                        lq1_ref, lk1_ref, lq2_ref, lk2_ref, sg_ref, o_ref,
                        kmb_s, vmb_s, *, lam_init):
    T = ATT_TILE
    n_tiles = q_ref.shape[0] // T
    kb_s, vb_s = k_ref, v_ref
    _pad_rows(kmb_s, km_ref)
    _pad_rows(vmb_s, vm_ref)
    lam = _diff_lambda(lq1_ref, lk1_ref, lq2_ref, lk2_ref, lam_init)

    for j in range(n_tiles):
        rows = slice(j * T, (j + 1) * T)
        regions = [(kmb_s, vmb_s, slice(0, LANES), bm_ref[min(j, 1)])]
        if j >= 2:
            regions.append((kb_s, vb_s, slice(0, (j - 1) * T), None))
        if j >= 1:
            regions.append((kb_s, vb_s, slice((j - 1) * T, j * T), bp_ref[...]))
        regions.append((kb_s, vb_s, rows, bd_ref[...]))

        probs, denom = [], []
        for c in range(2):
            cols = slice(c * DQK_ATT, (c + 1) * DQK_ATT)
            qc = q_ref[rows, cols]
            ss = [_dot_nt(qc, kr[rs, cols]) for kr, _, rs, _ in regions]
            ss = [s if br is None else s + br for s, (_, _, _, br) in zip(ss, regions)]
            m = functools.reduce(jnp.maximum, [jnp.max(s, axis=-1, keepdims=True) for s in ss])
            pc = [jnp.exp2(s - m) for s in ss]
            denom.append(functools.reduce(jnp.add, [jnp.sum(p, axis=-1, keepdims=True) for p in pc]))
            probs.append(pc)
        a0 = 1.0 / denom[0]
        a1 = lam / denom[1]
        o = None
        for r, (_, vr, rs, _) in enumerate(regions):
            w = (probs[0][r] * a0 - probs[1][r] * a1).astype(BF16)
            t = _dot(w, vr[rs, :])
            o = t if o is None else o + t
        o_ref[rows, :] = _subln(o, sg_ref, lam_init)


def _attn_prompt(z_f, z_t, bias, lam_vecs, sg, B, S, lam_init):
    T = ATT_TILE
    qkw = 2 * DQK_ATT
    kcol = ATT_QK // qkw
    vcol = 2 * ATT_QK // DV_ATT
    b_diag, b_prev, b_meta, _, _ = bias
    vec = pl.BlockSpec((1, DQK_ATT), lambda b, h: (0, 0))
    return pl.pallas_call(
        functools.partial(_attn_prompt_kernel, lam_init=lam_init),
        grid=(B, H_ATT),
        in_specs=[
            pl.BlockSpec((S, qkw), lambda b, h: (b, h)),
            pl.BlockSpec((S, qkw), lambda b, h: (b, kcol + h)),
            pl.BlockSpec((S, DV_ATT), lambda b, h: (b, vcol + h)),
            pl.BlockSpec((N_META, qkw), lambda b, h: (b, kcol + h)),
            pl.BlockSpec((N_META, DV_ATT), lambda b, h: (b, vcol + h)),
            pl.BlockSpec((None, T, T), lambda b, h: (h, 0, 0)),
            pl.BlockSpec((None, T, T), lambda b, h: (h, 0, 0)),
            pl.BlockSpec((None, 2, T, LANES), lambda b, h: (h, 0, 0, 0)),
            vec, vec, vec, vec,
            pl.BlockSpec((1, DV_ATT), lambda b, h: (0, 0)),
        ],
        out_specs=pl.BlockSpec((S, DV_ATT), lambda b, h: (b, h)),
        out_shape=jax.ShapeDtypeStruct((B * S, ATT_V), F32),
        scratch_shapes=[
            pltpu.VMEM((LANES, qkw), BF16),
            pltpu.VMEM((LANES, DV_ATT), BF16),
        ],
        compiler_params=pltpu.CompilerParams(
            dimension_semantics=("arbitrary", "arbitrary"), vmem_limit_bytes=VMEM_LIMIT),
        name="attn_prompt",
    )(z_f, z_f, z_f, z_t, z_t, b_diag, b_prev, b_meta, *lam_vecs, sg)


def _attn_small_kernel(*refs, lam_init, has_cache):
    if has_cache:
        (q_ref, kc_ref, vc_ref, kn_ref, vn_ref, b_ref,
         lq1_ref, lk1_ref, lq2_ref, lk2_ref, sg_ref, o_ref, kpad, vpad, kb_s, vb_s) = refs
        past_len = kb_s.shape[0]
        chunk = kc_ref.shape[0]
        c = pl.program_id(1)
        r0 = pl.multiple_of(c * chunk, chunk)
        kb_s[pl.ds(r0, chunk), :] = kc_ref[...].reshape(chunk, ATT_QK).astype(BF16)
        vb_s[pl.ds(r0, chunk), :] = vc_ref[...].reshape(chunk, ATT_V).astype(BF16)

        @pl.when(c == past_len // chunk - 1)
        def _():
            _attn_small_body(q_ref, kn_ref, vn_ref, b_ref, lq1_ref, lk1_ref, lq2_ref, lk2_ref, sg_ref, o_ref,
                             kpad, vpad, kb_s, vb_s, lam_init=lam_init, past_len=past_len)
    else:
        (q_ref, kn_ref, vn_ref, b_ref,
         lq1_ref, lk1_ref, lq2_ref, lk2_ref, sg_ref, o_ref, kpad, vpad) = refs
        _attn_small_body(q_ref, kn_ref, vn_ref, b_ref, lq1_ref, lk1_ref, lq2_ref, lk2_ref, sg_ref, o_ref,
                         kpad, vpad, None, None, lam_init=lam_init, past_len=0)


def _attn_small_body(q_ref, kn_ref, vn_ref, b_ref, lq1_ref, lk1_ref, lq2_ref, lk2_ref, sg_ref, o_ref,
                     kpad, vpad, kb_s, vb_s, *, lam_init, past_len):
    has_cache = past_len > 0
    lam = _diff_lambda(lq1_ref, lk1_ref, lq2_ref, lk2_ref, lam_init)
    _pad_rows(kpad, kn_ref)
    _pad_rows(vpad, vn_ref)
    for h in range(H_ATT):
        vsl = slice(h * DV_ATT, (h + 1) * DV_ATT)
        outs = []
        for c in range(2):
            sl = slice(h * 2 * DQK_ATT + c * DQK_ATT, h * 2 * DQK_ATT + (c + 1) * DQK_ATT)
            qc = q_ref[:, sl]
            s_new = _dot_nt(qc, kpad[:, sl]) + b_ref[h, :, past_len:]
            m = jnp.max(s_new, axis=-1, keepdims=True)
            if has_cache:
                s_old = _dot_nt(qc, kb_s[:, sl]) + b_ref[h, :, :past_len]
                m = jnp.maximum(m, jnp.max(s_old, axis=-1, keepdims=True))
            p_new = jnp.exp2(s_new - m)
            l = jnp.sum(p_new, axis=-1, keepdims=True)
            o = _dot(p_new.astype(BF16), vpad[:, vsl])
            if has_cache:
                p_old = jnp.exp2(s_old - m)
                l = l + jnp.sum(p_old, axis=-1, keepdims=True)
                o = o + _dot(p_old.astype(BF16), vb_s[:, vsl])
            outs.append(o / l)
        o_ref[:, vsl] = _subln(outs[0] - lam * outs[1], sg_ref, lam_init)


def _attn_small(z_t, row0, n_streams, bias, lam_vecs, sg, lam_init, cache=None):
    vec = pl.BlockSpec((1, DQK_ATT), lambda b, c: (0, 0))
    in_specs = [pl.BlockSpec((N_META, ATT_QK), lambda b, c: (row0 + b, 0))]
    args = [z_t]
    scratch = [pltpu.VMEM((LANES, ATT_QK), BF16), pltpu.VMEM((LANES, ATT_V), BF16)]
    n_chunks = 1
    if cache is not None:
        ck, cv, layer = cache
        past_len = ck.shape[2]
        chunk = min(past_len, CACHE_CHUNK)
        assert past_len % chunk == 0
        n_chunks = past_len // chunk
        in_specs += [
            pl.BlockSpec((None, None, chunk, H_ATT, 2 * DQK_ATT), lambda b, c: (layer, b, c, 0, 0)),
            pl.BlockSpec((None, None, chunk, H_ATT, DV_ATT), lambda b, c: (layer, b, c, 0, 0))]
        args += [ck, cv]
        scratch += [pltpu.VMEM((past_len, ATT_QK), BF16), pltpu.VMEM((past_len, ATT_V), BF16)]
    in_specs += [
        pl.BlockSpec((N_META, ATT_QK), lambda b, c: (row0 + b, 1)),
        pl.BlockSpec((N_META, ATT_V), lambda b, c: (row0 + b, 2 * ATT_QK // ATT_V)),
        pl.BlockSpec(bias.shape, lambda b, c: (0, 0, 0)),
        vec, vec, vec, vec,
        pl.BlockSpec((1, DV_ATT), lambda b, c: (0, 0)),
    ]
    args += [z_t, z_t, bias, *lam_vecs, sg]
    return pl.pallas_call(
        functools.partial(_attn_small_kernel, lam_init=lam_init, has_cache=cache is not None),
        grid=(n_streams, n_chunks),
        in_specs=in_specs,
        out_specs=pl.BlockSpec((N_META, ATT_V), lambda b, c: (b, 0)),
        out_shape=jax.ShapeDtypeStruct((n_streams * N_META, ATT_V), F32),
        scratch_shapes=scratch,
        compiler_params=pltpu.CompilerParams(
            dimension_semantics=("arbitrary", "arbitrary"), vmem_limit_bytes=VMEM_LIMIT),
        name="attn_cached" if cache is not None else "attn_meta",
    )(*args)


def _mlstm_kernel(*refs, n_blocks, has_init):
    refs = list(refs)
    qh_ref, kh_ref, vh_ref, gh_ref = refs[:4]
    pos = 4
    if n_blocks:
        qf_ref, kf_ref, vf_ref, gf_ref = refs[pos:pos + 4]
        pos += 4
    if has_init:
        c0_ref, n0_ref, m0_ref = refs[pos:pos + 3]
        pos += 3
    hh_ref = refs[pos]
    pos += 1
    if n_blocks:
        hf_ref = refs[pos]
        pos += 1
    c_ref, n_ref, m_ref = refs[pos:pos + 3]
    qs, ks, vs, gs = refs[pos + 3:pos + 7]

    t = pl.program_id(1)
    T = M_TILE
    block = functools.partial(_mlstm_block, c_ref=c_ref, n_ref=n_ref, m_ref=m_ref)

    @pl.when(t == 0)
    def _():
        if has_init:
            c_ref[...] = c0_ref[...]
            n_ref[...] = n0_ref[...]
            m_ref[...] = m0_ref[...]
        else:
            c_ref[...] = jnp.zeros(c_ref.shape, F32)
            n_ref[...] = jnp.zeros(n_ref.shape, F32)
            m_ref[...] = jnp.zeros(m_ref.shape, F32)
        nh = qh_ref.shape[0]
        for dst, src in ((qs, qh_ref), (ks, kh_ref), (vs, vh_ref)):
            dst[...] = jnp.zeros(dst.shape, F32)
            dst[0:nh, :] = src[...]
        lane = lax.broadcasted_iota(jnp.int32, gs.shape, 1)
        gs[...] = jnp.where(lane < H_M, NEG, 0.0)
        gs[0:nh, :] = gh_ref[...]

        block(qs, ks, vs, gs, hh_ref, nh)

    if n_blocks:
        @pl.when(t > 0)
        def _():
            block(qf_ref, kf_ref, vf_ref, gf_ref, hf_ref, T)


def _mlstm_block(q_src, k_src, v_src, g_src, h_dst, n_out, c_ref, n_ref, m_ref):
    T = M_TILE
    g = g_src[...]
    row = lax.broadcasted_iota(jnp.int32, (T, T), 0)
    colid = lax.broadcasted_iota(jnp.int32, (T, T), 1)
    causal = row >= colid
    tril = jnp.where(causal, 1.0, 0.0).astype(F32)
    cum = jnp.dot(tril, g, preferred_element_type=F32, precision=lax.Precision.HIGHEST)
    g_t = g.T
    cum_t = cum.T
    last = cum[T - 1:T, :]

    for h in range(H_M):
        q = q_src[:, h * DQK_M:(h + 1) * DQK_M]
        k = k_src[:, h * DQK_M:(h + 1) * DQK_M]
        v = v_src[:, h * DV_M:(h + 1) * DV_M]
        qb = q.astype(BF16)
        kb = k.astype(BF16)
        m_old = m_ref[h, 0:1, 0:1]
        c_old = c_ref[h]
        n_old = n_ref[h]
        bh_c = _col(cum, H_M + h)
        ih_c = _col(g, h)
        bh_r = cum_t[H_M + h:H_M + h + 1, :]
        ih_r = g_t[h:h + 1, :]
        d = jnp.where(causal, bh_c - bh_r + ih_r, NEG)
        inter = bh_c + m_old
        mt = jnp.maximum(inter, jnp.max(d, axis=-1, keepdims=True))
        a = jnp.exp(d - mt) * _dot_nt(qb, kb)
        iw = jnp.exp(inter - mt)
        num = _dot(a.astype(BF16), v.astype(BF16)) + iw * _dot_nt(qb, c_old.astype(BF16))
        den = jnp.sum(a, axis=-1, keepdims=True) + iw * jnp.sum(q * n_old, axis=-1, keepdims=True)
        den = jnp.maximum(jnp.abs(den), jnp.exp(-mt))
        hout = num / den
        h_dst[:, h * DV_M:(h + 1) * DV_M] = hout[0:n_out, :]

        bl = _col(last, H_M + h)
        g_c = bl - bh_c + ih_c
        m_new = jnp.maximum(bl + m_old, jnp.max(g_c, axis=0, keepdims=True))
        sw = jnp.exp(g_c - m_new)
        dec = jnp.exp(bl + m_old - m_new)
        c_ref[h] = dec * c_old + _dot(v.T.astype(BF16), (sw * k).astype(BF16))
        n_ref[h] = dec * n_old + jnp.sum(sw * k, axis=0, keepdims=True)
        m_ref[h] = jnp.broadcast_to(m_new, m_ref.shape[1:])


def _mlstm(z_t, g_t, row0, n_streams, frames=None, init=None):
    T = M_TILE
    qcol = 0
    vcol = 2 * M_QK // M_V
    nh = N_META
    in_specs = [
        pl.BlockSpec((nh, M_QK), lambda b, t: (row0 + b, qcol)),
        pl.BlockSpec((nh, M_QK), lambda b, t: (row0 + b, qcol + 1)),
        pl.BlockSpec((nh, M_V), lambda b, t: (row0 + b, vcol)),
        pl.BlockSpec((nh, LANES), lambda b, t: (row0 + b, 0)),
    ]
    args = [z_t, z_t, z_t, g_t]
    n_blocks = 0
    if frames is not None:
        z_f, g_f, fps = frames
        n_blocks = fps // T
        fidx = lambda b, t: b * n_blocks + jnp.maximum(t - 1, 0)
        in_specs += [
            pl.BlockSpec((T, M_QK), lambda b, t: (fidx(b, t), qcol)),
            pl.BlockSpec((T, M_QK), lambda b, t: (fidx(b, t), qcol + 1)),
            pl.BlockSpec((T, M_V), lambda b, t: (fidx(b, t), vcol)),
            pl.BlockSpec((T, LANES), lambda b, t: (fidx(b, t), 0)),
        ]
        args += [z_f, z_f, z_f, g_f]
    state_specs = (
        pl.BlockSpec((None, H_M, DV_M, DQK_M), lambda b, t: (b, 0, 0, 0)),
        pl.BlockSpec((None, H_M, 1, DQK_M), lambda b, t: (b, 0, 0, 0)),
        pl.BlockSpec((None, H_M, 8, LANES), lambda b, t: (b, 0, 0, 0)),
    )
    if init is not None:
        in_specs += list(state_specs)
        args += list(init)
    out_specs = [pl.BlockSpec((nh, M_V), lambda b, t: (b, 0))]
    out_shape = [jax.ShapeDtypeStruct((n_streams * nh, M_V), F32)]
    if n_blocks:
        out_specs.append(pl.BlockSpec((T, M_V), lambda b, t: (fidx(b, t), 0)))
        out_shape.append(jax.ShapeDtypeStruct((n_streams * fps, M_V), F32))
    out_specs += list(state_specs)
    out_shape += [
        jax.ShapeDtypeStruct((n_streams, H_M, DV_M, DQK_M), F32),
        jax.ShapeDtypeStruct((n_streams, H_M, 1, DQK_M), F32),
        jax.ShapeDtypeStruct((n_streams, H_M, 8, LANES), F32),
    ]
    return pl.pallas_call(
        functools.partial(_mlstm_kernel, n_blocks=n_blocks, has_init=init is not None),
        grid=(n_streams, 1 + n_blocks),
        in_specs=in_specs,
        out_specs=tuple(out_specs),
        out_shape=tuple(out_shape),
        scratch_shapes=[
            pltpu.VMEM((T, M_QK), F32), pltpu.VMEM((T, M_QK), F32),
            pltpu.VMEM((T, M_V), F32), pltpu.VMEM((T, LANES), F32),
        ],
        compiler_params=pltpu.CompilerParams(
            dimension_semantics=("arbitrary", "arbitrary"), vmem_limit_bytes=VMEM_LIMIT),
        name="mlstm_prompt" if n_blocks else "mlstm_cached",
    )(*args)


def _merge_kernel(xf_ref, xt_ref, attf_ref, attt_ref, hmf_ref, hmt_ref, zf_ref, zt_ref, mg_ref, wo_ref,
                  of_ref, ot_ref, *, n_frame_tiles):
    i = pl.program_id(0)

    def tile(x_ref, att_ref, hm_ref, og_ref, o_ref):
        parts = []
        for h in range(H_M):
            sl = slice(h * DV_M, (h + 1) * DV_M)
            parts.append(_rms(hm_ref[:, sl], mg_ref[:, sl]) * og_ref[:, sl])
        hmn = jnp.concatenate(parts, axis=1).astype(BF16)
        o_ref[...] = (x_ref[...] + _dot(att_ref[...].astype(BF16), wo_ref[0:ATT_V, :])
                      + _dot(hmn, wo_ref[ATT_V:, :]))

    @pl.when(i < n_frame_tiles)
    def _():
        tile(xf_ref, attf_ref, hmf_ref, zf_ref, of_ref)

    @pl.when(i == n_frame_tiles)
    def _():
        tile(xt_ref, attt_ref, hmt_ref, zt_ref, ot_ref)


def _merge(x_f, x_t, att_f, att_t, hm_f, hm_t, z_f, z_t, mg, w_out, layer):
    NF, D = x_f.shape
    NT = x_t.shape[0]
    bm = MERGE_ROWS
    nft = NF // bm
    assert NF % bm == 0
    ogcol = (N_MAIN - N_ATT - M_V) // M_V
    fr = lambda i: (jnp.minimum(i, nft - 1), 0)
    tl = lambda i: (0, 0)
    once = dict(pipeline_mode=pl.Buffered(1))
    return pl.pallas_call(
        functools.partial(_merge_kernel, n_frame_tiles=nft),
        grid=(nft + 1,),
        in_specs=[
            pl.BlockSpec((bm, D), fr), pl.BlockSpec((NT, D), tl, **once),
            pl.BlockSpec((bm, ATT_V), fr), pl.BlockSpec((NT, ATT_V), tl, **once),
            pl.BlockSpec((bm, M_V), fr), pl.BlockSpec((NT, M_V), tl, **once),
            pl.BlockSpec((bm, M_V), lambda i: (jnp.minimum(i, nft - 1), ogcol)),
            pl.BlockSpec((NT, M_V), lambda i: (0, ogcol), **once),
            pl.BlockSpec((1, M_V), tl),
            pl.BlockSpec((None, D, D), lambda i: (layer, 0, 0), **once),
        ],
        out_specs=(pl.BlockSpec((bm, D), fr), pl.BlockSpec((NT, D), tl)),
        out_shape=(jax.ShapeDtypeStruct((NF, D), F32), jax.ShapeDtypeStruct((NT, D), F32)),
        compiler_params=pltpu.CompilerParams(
            dimension_semantics=("arbitrary",), vmem_limit_bytes=VMEM_LIMIT),
        name="merge_out_proj",
    )(x_f, x_t, att_f, att_t, hm_f, hm_t, z_f, z_t, mg, w_out)


def _ffn_kernel(xf_ref, xt_ref, g2_ref, wu_ref, wd_ref, of_ref, ot_ref, xn_s, *, n_frame_tiles):
    i = pl.program_id(0)
    k = pl.program_id(1)

    def tile(x_ref, o_ref):
        nrows = x_ref.shape[0]

        @pl.when(k == 0)
        def _():
            x = x_ref[...]
            xn_s[0:nrows, :] = _rms(x, g2_ref[...]).astype(BF16)
            o_ref[...] = x

        u = jnp.maximum(_dot(xn_s[0:nrows, :], wu_ref[...]), 0.0)
        o_ref[...] += _dot((u * u).astype(BF16), wd_ref[...])

    @pl.when(i < n_frame_tiles)
    def _():
        tile(xf_ref, of_ref)

    @pl.when(i == n_frame_tiles)
    def _():
        tile(xt_ref, ot_ref)


def _ffn(x_f, x_t, g2, w_up, w_down, layer):
    NF, D = x_f.shape
    NT = x_t.shape[0]
    FF = w_up.shape[2]
    bm, fc = FF_ROWS, FF_COLS
    nft = NF // bm
    assert NF % bm == 0 and FF % fc == 0 and NT <= bm
    fr = lambda i, k: (jnp.minimum(i, nft - 1), 0)
    tl = lambda i, k: (0, 0)
    return pl.pallas_call(
        functools.partial(_ffn_kernel, n_frame_tiles=nft),
        grid=(nft + 1, FF // fc),
        in_specs=[
            pl.BlockSpec((bm, D), fr),
            pl.BlockSpec((NT, D), tl),
            pl.BlockSpec((1, D), tl),
            pl.BlockSpec((None, D, fc), lambda i, k: (layer, 0, k)),
            pl.BlockSpec((None, fc, D), lambda i, k: (layer, k, 0)),
        ],
        out_specs=(pl.BlockSpec((bm, D), fr), pl.BlockSpec((NT, D), tl)),
        out_shape=(jax.ShapeDtypeStruct((NF, D), F32), jax.ShapeDtypeStruct((NT, D), F32)),
        scratch_shapes=[pltpu.VMEM((bm, D), BF16)],
        compiler_params=pltpu.CompilerParams(
            dimension_semantics=("arbitrary", "arbitrary"), vmem_limit_bytes=VMEM_LIMIT),
        name="ffn",
    )(x_f, x_t, g2, w_up, w_down)


def kernel(x_prompt, x_sample, cache_k, cache_v, state_C, state_n, state_m, meta_tokens, rel_bias, norm1_g, w_in, b_i, b_f, q_norm_g, k_norm_g, lambda_q1, lambda_k1, lambda_q2, lambda_k2, subln_g, mnorm_g, w_out, norm2_g, w_up, w_down):
    B, S, D = x_prompt.shape
    Bd, Td, _ = x_sample.shape
    depth = w_in.shape[0]
    Pc = cache_k.shape[2]
    assert Td == N_META and S % ATT_TILE == 0 and S % M_TILE == 0 and Pc % LANES == 0
    assert ATT_TILE % CHUNK == 0 and CHUNK & (CHUNK - 1) == 0 and ATT_TILE >= MAX_DISTANCE

    n_frames = B * S
    n_meta = B * N_META
    n_samp = Bd * Td
    assert n_meta + n_samp <= TAIL_ROWS
    srow = n_meta // N_META

    meta = jnp.broadcast_to(meta_tokens.astype(F32)[None], (B, N_META, D)).reshape(n_meta, D)
    x_f = x_prompt.astype(F32).reshape(n_frames, D)
    x_t = jnp.concatenate([meta, x_sample.astype(F32).reshape(n_samp, D),
                           jnp.zeros((TAIL_ROWS - n_meta - n_samp, D), F32)], axis=0)

    bias = _bias_tiles(rel_bias.astype(F32), Pc, Td)
    ck = cache_k.astype(F32)
    cv = cache_v.astype(F32)
    zeros_pad_att = jnp.zeros((TAIL_ROWS - n_meta - n_samp, ATT_V), F32)

    g_off = 2 * ATT_QK + ATT_V + 2 * M_QK + M_V
    w_in_t = jnp.swapaxes(w_in.astype(F32), 1, 2)
    w_out_b, w_up_b, w_down_b = (w.astype(BF16) for w in (w_out, w_up, w_down))
    outs = {name: [] for name in ("Cp", "np", "mp", "Cs", "ns", "ms")}
    caches = None

    for l in range(depth):
        lam_init = 0.8 - 0.6 * math.exp(-0.3 * l)
        gate_bias = jnp.pad(jnp.concatenate([b_i[l], b_f[l]]).astype(F32), (0, LANES - 2 * H_M))[None]
        lam_vecs = [v[l].astype(F32)[None] for v in (lambda_q1, lambda_k1, lambda_q2, lambda_k2)]
        sg = subln_g[l].astype(F32)[None]

        za_f, za_t, zm_f, zm_t, g_f, g_t, *caches = _inproj(
            x_f, x_t, norm1_g[l].astype(F32)[None], w_in_t, l, gate_bias,
            q_norm_g[l].astype(F32)[None], k_norm_g[l].astype(F32)[None], B, Bd, caches)

        att_f = _attn_prompt(za_f, za_t, bias, lam_vecs, sg, B, S, lam_init)
        att_m = _attn_small(za_t, 0, B, bias[3], lam_vecs, sg, lam_init)
        att_s = _attn_small(za_t, srow, Bd, bias[4], lam_vecs, sg, lam_init, cache=(ck, cv, l))
        att_t = jnp.concatenate([att_m, att_s, zeros_pad_att], axis=0)

        hm_m, hm_f, C_p, n_p, m_p = _mlstm(zm_t, g_t, 0, B, frames=(zm_f, g_f, S))
        init = (state_C[l].astype(F32), state_n[l].astype(F32)[:, :, None, :],
                jnp.broadcast_to(state_m[l].astype(F32)[:, :, None, None], (Bd, H_M, 8, LANES)))
        hm_s, C_s, n_s, m_s = _mlstm(zm_t, g_t, srow, Bd, init=init)
        hm_t = jnp.concatenate([hm_m, hm_s, zeros_pad_att], axis=0)

        x_f, x_t = _merge(x_f, x_t, att_f, att_t, hm_f, hm_t, zm_f, zm_t,
                          mnorm_g[l].astype(F32)[None], w_out_b, l)
        x_f, x_t = _ffn(x_f, x_t, norm2_g[l].astype(F32)[None], w_up_b, w_down_b, l)

        outs["Cp"].append(C_p)
        outs["np"].append(n_p[:, :, 0, :])
        outs["mp"].append(m_p[:, :, 0, 0])
        outs["Cs"].append(C_s)
        outs["ns"].append(n_s[:, :, 0, :])
        outs["ms"].append(m_s[:, :, 0, 0])

    y_prompt = x_f.reshape(B, S, D)
    y_sample = x_t[n_meta:n_meta + n_samp].reshape(Bd, Td, D)
    k_p, v_p, k_s, v_s = caches
    return (y_prompt, y_sample,
            k_p, v_p, jnp.stack(outs["Cp"]), jnp.stack(outs["np"]), jnp.stack(outs["mp"]),
            k_s.reshape(depth, Bd, Td, H_ATT, 2 * DQK_ATT), v_s.reshape(depth, Bd, Td, H_ATT, DV_ATT),
            jnp.stack(outs["Cs"]), jnp.stack(outs["ns"]), jnp.stack(outs["ms"]))
```

```python
import functools
import math

import jax
import jax.numpy as jnp
from jax import lax
from jax.experimental import pallas as pl
from jax.experimental.pallas import tpu as pltpu

F32 = jnp.float32
BF16 = jnp.bfloat16

CHUNK = 64
N_META = 16
H_ATT = 4
DQK_ATT = 128
DV_ATT = 256
H_M = 4
DQK_M = 128
DV_M = 256
N_BUCKETS = 32
MAX_DISTANCE = 128
EPS = 1e-6
NEG = -1e30
LOG2E = math.log2(math.e)

ATT_QK = H_ATT * 2 * DQK_ATT
ATT_V = H_ATT * DV_ATT
M_QK = H_M * DQK_M
M_V = H_M * DV_M
N_MAIN = 2 * ATT_QK + ATT_V + 2 * M_QK + M_V + M_V
N_ATT = 2 * ATT_QK + ATT_V

LANES = 128
ATT_TILE = 256
M_TILE = 256
TAIL_ROWS = 256
PROJ_ROWS = 1024
PROJ_COLS = 512
MERGE_ROWS = 512
CACHE_CHUNK = 1024
FF_ROWS = 512
FF_COLS = 1024
VMEM_LIMIT = 56 * 1024 * 1024

_NT = (((1,), (1,)), ((), ()))


def _dot(a, b):
    return jnp.dot(a, b, preferred_element_type=F32)


def _dot_nt(a, b):
    return lax.dot_general(a, b, _NT, preferred_element_type=F32)


def _log_sigmoid(x):
    return jnp.minimum(x, 0.0) - jnp.log1p(jnp.exp(-jnp.abs(x)))


def _col(a, idx):
    lane = lax.broadcasted_iota(jnp.int32, a.shape, 1)
    return jnp.sum(jnp.where(lane == idx, a, 0.0), axis=1, keepdims=True)


def _rms(x, gain):
    ms = jnp.mean(x * x, axis=-1, keepdims=True)
    return x * lax.rsqrt(ms + EPS) * gain


def _bias_from_rel(rel, rb_ref, h):
    nb = N_BUCKETS // 2
    max_exact = nb // 2
    ret = jnp.where(rel > 0, nb, 0)
    n = jnp.abs(rel)
    nf = jnp.maximum(n, 1).astype(F32)
    large = max_exact + (jnp.log(nf / max_exact) / math.log(MAX_DISTANCE / max_exact)
                         * (nb - max_exact)).astype(jnp.int32)
    large = jnp.minimum(large, nb - 1)
    bucket = ret + jnp.where(n < max_exact, n, large)
    out = jnp.zeros(rel.shape, F32)
    for b in range(N_BUCKETS):
        out = jnp.where(bucket == b, rb_ref[b, h], out)
    return out


def _bias_kernel(rb_ref, diag_ref, prev_ref, meta_ref, mm_ref, samp_ref, *, past_len, dec_seq):
    h = pl.program_id(0)
    shift = CHUNK.bit_length() - 1
    T = ATT_TILE
    far = _bias_from_rel(jnp.full((8, LANES), -(2 * MAX_DISTANCE + 1), jnp.int32), rb_ref, h)[0:1, 0:1]

    qi = lax.broadcasted_iota(jnp.int32, (T, T), 0)
    kj = lax.broadcasted_iota(jnp.int32, (T, T), 1)
    bias = (_bias_from_rel(kj - qi, rb_ref, h) - far) * LOG2E
    diag_ref[...] = jnp.where((kj >> shift) <= (qi >> shift), bias, NEG)
    prev_ref[...] = (_bias_from_rel(kj - T - qi, rb_ref, h) - far) * LOG2E

    qi = lax.broadcasted_iota(jnp.int32, (T, LANES), 0)
    kj = lax.broadcasted_iota(jnp.int32, (T, LANES), 1)
    for var in range(2):
        bias = (_bias_from_rel(kj - (N_META + qi + var * T), rb_ref, h) - far) * LOG2E
        meta_ref[var] = jnp.where(kj < N_META, bias, NEG)

    qi = lax.broadcasted_iota(jnp.int32, (N_META, LANES), 0)
    kj = lax.broadcasted_iota(jnp.int32, (N_META, LANES), 1)
    mm_ref[...] = jnp.where(kj < N_META, _bias_from_rel(kj - qi, rb_ref, h) * LOG2E, NEG)

    width = samp_ref.shape[-1]
    qi = lax.broadcasted_iota(jnp.int32, (dec_seq, width), 0) + past_len
    kj = lax.broadcasted_iota(jnp.int32, (dec_seq, width), 1)
    bias = _bias_from_rel(kj - qi, rb_ref, h) * LOG2E
    ok = ((kj >> shift) <= (qi >> shift)) & (kj < past_len + dec_seq)
    samp_ref[...] = jnp.where(ok, bias, NEG)


def _bias_tiles(rel_bias, past_len, dec_seq):
    T = ATT_TILE
    sw = past_len + LANES
    out_shape = (
        jax.ShapeDtypeStruct((H_ATT, T, T), F32),
        jax.ShapeDtypeStruct((H_ATT, T, T), F32),
        jax.ShapeDtypeStruct((H_ATT, 2, T, LANES), F32),
        jax.ShapeDtypeStruct((H_ATT, N_META, LANES), F32),
        jax.ShapeDtypeStruct((H_ATT, dec_seq, sw), F32),
    )
    out_specs = (
        pl.BlockSpec((None, T, T), lambda h: (h, 0, 0)),
        pl.BlockSpec((None, T, T), lambda h: (h, 0, 0)),
        pl.BlockSpec((None, 2, T, LANES), lambda h: (h, 0, 0, 0)),
        pl.BlockSpec((None, N_META, LANES), lambda h: (h, 0, 0)),
        pl.BlockSpec((None, dec_seq, sw), lambda h: (h, 0, 0)),
    )
    return pl.pallas_call(
        functools.partial(_bias_kernel, past_len=past_len, dec_seq=dec_seq),
        grid=(H_ATT,),
        in_specs=[pl.BlockSpec(memory_space=pltpu.SMEM)],
        out_specs=out_specs,
        out_shape=out_shape,
        name="rel_bias_tiles",
    )(rel_bias)


def _zero_fill_copies(zbuf, zsem, cache_refs, depth):
    rows = zbuf.shape[0]
    copies = []
    for dst in cache_refs:
        lead = dst.shape[1:-3]
        n_rows = dst.shape[-3]
        for l in range(1, depth):
            for b in range(lead[0] if lead else 1):
                view = dst.at[l, b] if lead else dst.at[l]
                for r0 in range(0, n_rows, rows):
                    n = min(rows, n_rows - r0)
                    copies.append(pltpu.make_async_copy(zbuf.at[pl.ds(0, n)], view.at[pl.ds(r0, n)],
                                                        zsem.at[len(copies)]))
    return copies


def _kv_copies(stage, sem, dst_prompt, dst_sample, i, *, layer, tail, n_batch, frames_per_batch, n_sample_rows):
    dst_prompt = dst_prompt.at[layer]
    dst_sample = dst_sample.at[layer]
    if not tail:
        rows = stage.shape[0]
        tiles_per_batch = frames_per_batch // rows
        b = i // tiles_per_batch
        t = i - b * tiles_per_batch
        return [pltpu.make_async_copy(stage, dst_prompt.at[b, pl.ds(N_META + t * rows, rows)], sem.at[0])]
    copies = [pltpu.make_async_copy(stage.at[pl.ds(b * N_META, N_META)], dst_prompt.at[b, pl.ds(0, N_META)],
                                    sem.at[b]) for b in range(n_batch)]
    copies.append(pltpu.make_async_copy(stage.at[pl.ds(n_batch * N_META, n_sample_rows)], dst_sample,
                                        sem.at[n_batch]))
    return copies


def _inproj_kernel(*refs, layer, depth, n_frame_tiles, n_batch, frames_per_batch, n_sample_rows):
    xf_ref, xt_ref, g1_ref, wa_ref, wg_ref, gb_ref, qg_ref, kg_ref = refs[:8]
    (zaf_ref, zat_ref, zmf_ref, zmt_ref, gf_ref, gt_ref, kp_ref, vp_ref, ks_ref, vs_ref,
     xn_s, y2d, stage, sem, zbuf, zsem) = refs[-16:]
    i = pl.program_id(0)
    j = pl.program_id(1)
    bn = wa_ref.shape[1]

    if layer == 0 and depth > 1:
        zero_copies = _zero_fill_copies(zbuf, zsem, (kp_ref, vp_ref, ks_ref, vs_ref), depth)

        @pl.when((i == 0) & (j == 0))
        def _():
            zbuf[...] = jnp.zeros(zbuf.shape, F32)
            for c in zero_copies:
                c.start()

        @pl.when((i == pl.num_programs(0) - 1) & (j == pl.num_programs(1) - 1))
        def _():
            for c in zero_copies:
                c.wait()

    assert ATT_QK % bn == 0 and ATT_V % bn == 0 and M_QK % bn == 0 and M_V % bn == 0 and bn % DV_ATT == 0
    j_k = ATT_QK // bn
    j_v = j_k + ATT_QK // bn
    j_mq = j_v + ATT_V // bn
    j_mk = j_mq + M_QK // bn
    j_mv = j_mk + M_QK // bn
    j_og = j_mv + M_V // bn

    def tile(x_ref, za_ref, zm_ref, g_ref, tail):
        nrows = x_ref.shape[0]
        st = stage.at[pl.ds(0, nrows)]
        copies = functools.partial(_kv_copies, st, sem, i=i, layer=layer, tail=tail, n_batch=n_batch,
                                   frames_per_batch=frames_per_batch, n_sample_rows=n_sample_rows)

        @pl.when(j == 0)
        def _():
            xb = _rms(x_ref[...], g1_ref[...]).astype(BF16)
            xn_s[0:nrows, :] = xb
            gates = _dot_nt(xb, wg_ref[0].astype(BF16)) + gb_ref[...]
            lane = lax.broadcasted_iota(jnp.int32, gates.shape, 1)
            g_ref[...] = jnp.where(lane < H_M, gates, _log_sigmoid(gates))

        acc = _dot_nt(xn_s[0:nrows, :], wa_ref[0].astype(BF16))

        def group_norm(gain_ref, scale):
            parts = []
            for gi in range(bn // DQK_ATT):
                y = _rms(acc[:, gi * DQK_ATT:(gi + 1) * DQK_ATT], gain_ref[...])
                parts.append(y if scale is None else y * scale)
            return jnp.concatenate(parts, axis=1)

        def stage_heads(y, jj, j0, dst_prompt, dst_sample):
            c0 = (jj - j0) * bn
            y2d[0:nrows, c0:c0 + bn] = y
            if jj == j0 + ATT_QK // bn - 1:
                if j0 != j_k:
                    for c in copies(kp_ref, ks_ref):
                        c.wait()
                st[...] = y2d[0:nrows, :].reshape(nrows, H_ATT, DV_ATT)
                for c in copies(dst_prompt, dst_sample):
                    c.start()

        @pl.when(j < j_k)
        def _():
            za_ref[...] = group_norm(qg_ref, DQK_ATT ** -0.5 * LOG2E).astype(BF16)

        for jj in range(j_k, j_v):
            @pl.when(j == jj)
            def _(jj=jj):
                y = group_norm(kg_ref, None)
                za_ref[...] = y.astype(BF16)
                stage_heads(y, jj, j_k, kp_ref, ks_ref)

        for jj in range(j_v, j_mq):
            @pl.when(j == jj)
            def _(jj=jj):
                za_ref[...] = acc.astype(BF16)
                stage_heads(acc, jj, j_v, vp_ref, vs_ref)

        @pl.when(j == j_mq)
        def _():
            for c in copies(vp_ref, vs_ref):
                c.wait()

        @pl.when(((j >= j_mq) & (j < j_mk)) | ((j >= j_mv) & (j < j_og)))
        def _():
            zm_ref[...] = acc

        @pl.when((j >= j_mk) & (j < j_mv))
        def _():
            zm_ref[...] = acc * (DQK_M ** -0.5)

        @pl.when(j >= j_og)
        def _():
            zm_ref[...] = jax.nn.sigmoid(acc)

    @pl.when(i < n_frame_tiles)
    def _():
        tile(xf_ref, zaf_ref, zmf_ref, gf_ref, False)

    @pl.when(i == n_frame_tiles)
    def _():
        tile(xt_ref, zat_ref, zmt_ref, gt_ref, True)


def _inproj(x_f, x_t, g1, w_all, layer, gate_bias, qg, kg, n_batch, n_sample_streams, caches):
    depth = w_all.shape[0]
    NF, D = x_f.shape
    NT = x_t.shape[0]
    bm, bn = PROJ_ROWS, PROJ_COLS
    nft = NF // bm
    nj = N_MAIN // bn
    nja = N_ATT // bn
    g_row = N_MAIN - M_V
    assert g_row % bn == 0 and g_row % 8 == 0 and w_all.shape[1] >= g_row + LANES

    def w_window(i, j):
        row = j * bn + jnp.where(j * bn >= g_row, 2 * H_M, 0)
        return (layer, pl.multiple_of(row, 8), 0)

    S = NF // n_batch
    L = N_META + S
    n_sample_rows = n_sample_streams * N_META
    assert NF % bm == 0 and S % bm == 0 and N_MAIN % bn == 0 and NT <= bm
    fr = lambda i: jnp.minimum(i, nft - 1)
    any_spec = pl.BlockSpec(memory_space=pl.ANY)
    assert (caches is None) == (layer == 0)
    prev = () if caches is None else tuple(caches)
    n_in = 8
    zero_rows = 256
    n_zero = (depth - 1) * (2 * n_batch * pl.cdiv(L, zero_rows) + 2 * pl.cdiv(n_sample_rows, zero_rows))
    return pl.pallas_call(
        functools.partial(_inproj_kernel, layer=layer, depth=depth, n_frame_tiles=nft, n_batch=n_batch,
                          frames_per_batch=S, n_sample_rows=n_sample_rows),
        grid=(nft + 1, nj),
        input_output_aliases={n_in + c: 6 + c for c in range(len(prev))},
        in_specs=[
            pl.BlockSpec((bm, D), lambda i, j: (fr(i), 0)),
            pl.BlockSpec((NT, D), lambda i, j: (0, 0), pipeline_mode=pl.Buffered(1)),
            pl.BlockSpec((1, D), lambda i, j: (0, 0)),
            pl.BlockSpec((pl.Element(1), pl.Element(bn), pl.Element(D)), w_window),
            pl.BlockSpec((pl.Element(1), pl.Element(LANES), pl.Element(D)), lambda i, j: (layer, g_row, 0)),
            pl.BlockSpec((1, LANES), lambda i, j: (0, 0)),
            pl.BlockSpec((1, DQK_ATT), lambda i, j: (0, 0)),
            pl.BlockSpec((1, DQK_ATT), lambda i, j: (0, 0)),
        ] + [any_spec] * len(prev),
        out_specs=(
            pl.BlockSpec((bm, bn), lambda i, j: (fr(i), jnp.where(i < nft, jnp.minimum(j, nja - 1), nja - 1))),
            pl.BlockSpec((NT, bn), lambda i, j: (0, jnp.where(i < nft, 0, jnp.minimum(j, nja - 1)))),
            pl.BlockSpec((bm, bn), lambda i, j: (fr(i), jnp.where(i < nft, jnp.maximum(j - nja, 0), nj - nja - 1))),
            pl.BlockSpec((NT, bn), lambda i, j: (0, jnp.where(i < nft, 0, jnp.maximum(j - nja, 0)))),
            pl.BlockSpec((bm, LANES), lambda i, j: (fr(i), 0)),
            pl.BlockSpec((NT, LANES), lambda i, j: (0, 0)),
            any_spec, any_spec, any_spec, any_spec,
        ),
        out_shape=(
            jax.ShapeDtypeStruct((NF, N_ATT), BF16),
            jax.ShapeDtypeStruct((NT, N_ATT), BF16),
            jax.ShapeDtypeStruct((NF, N_MAIN - N_ATT), F32),
            jax.ShapeDtypeStruct((NT, N_MAIN - N_ATT), F32),
            jax.ShapeDtypeStruct((NF, LANES), F32),
            jax.ShapeDtypeStruct((NT, LANES), F32),
            jax.ShapeDtypeStruct((depth, n_batch, L, H_ATT, 2 * DQK_ATT), F32),
            jax.ShapeDtypeStruct((depth, n_batch, L, H_ATT, DV_ATT), F32),
            jax.ShapeDtypeStruct((depth, n_sample_rows, H_ATT, 2 * DQK_ATT), F32),
            jax.ShapeDtypeStruct((depth, n_sample_rows, H_ATT, DV_ATT), F32),
        ),
        scratch_shapes=[
            pltpu.VMEM((bm, D), BF16),
            pltpu.VMEM((bm, ATT_QK), F32),
            pltpu.VMEM((bm, H_ATT, DV_ATT), F32),
            pltpu.SemaphoreType.DMA((n_batch + 1,)),
            pltpu.VMEM((zero_rows, H_ATT, DV_ATT), F32),
            pltpu.SemaphoreType.DMA((max(n_zero, 1),)),
        ],
        compiler_params=pltpu.CompilerParams(
            dimension_semantics=("arbitrary", "arbitrary"), vmem_limit_bytes=VMEM_LIMIT),
        name="in_proj",
    )(x_f, x_t, g1, w_all, w_all, gate_bias, qg, kg, *prev)


def _diff_lambda(lq1_ref, lk1_ref, lq2_ref, lk2_ref, lam_init):
    a = jnp.sum(lq1_ref[...] * lk1_ref[...], axis=-1, keepdims=True)
    b = jnp.sum(lq2_ref[...] * lk2_ref[...], axis=-1, keepdims=True)
    return jnp.exp(a) - jnp.exp(b) + lam_init


def _subln(o, sg_ref, lam_init):
    return _rms(o, sg_ref[...]) * (1.0 - lam_init)


def _pad_rows(dst, src):
    dst[...] = jnp.zeros(dst.shape, dst.dtype)
    dst[0:src.shape[0], :] = src[...].astype(dst.dtype)


def _attn_prompt_kernel(q_ref, k_ref, v_ref, km_ref, vm_ref, bd_ref, bp_ref, bm_ref,
                        lq1_ref, lk1_ref, lq2_ref, lk2_ref, sg_ref, o_ref,
                        kmb_s, vmb_s, *, lam_init):
    T = ATT_TILE
    n_tiles = q_ref.shape[0] // T
    kb_s, vb_s = k_ref, v_ref
    _pad_rows(kmb_s, km_ref)
    _pad_rows(vmb_s, vm_ref)
    lam = _diff_lambda(lq1_ref, lk1_ref, lq2_ref, lk2_ref, lam_init)

    for j in range(n_tiles):
        rows = slice(j * T, (j + 1) * T)
        regions = [(kmb_s, vmb_s, slice(0, LANES), bm_ref[min(j, 1)])]
        if j >= 2:
            regions.append((kb_s, vb_s, slice(0, (j - 1) * T), None))
        if j >= 1:
            regions.append((kb_s, vb_s, slice((j - 1) * T, j * T), bp_ref[...]))
        regions.append((kb_s, vb_s, rows, bd_ref[...]))

        outs = []
        for c in range(2):
            cols = slice(c * DQK_ATT, (c + 1) * DQK_ATT)
            qc = q_ref[rows, cols]
            ss = [_dot_nt(qc, kr[rs, cols]) for kr, _, rs, _ in regions]
            ss = [s if br is None else s + br for s, (_, _, _, br) in zip(ss, regions)]
            m = functools.reduce(jnp.maximum, [jnp.max(s, axis=-1, keepdims=True) for s in ss])
            pc = [jnp.exp2(s - m) for s in ss]
            denom = functools.reduce(jnp.add, [jnp.sum(p, axis=-1, keepdims=True) for p in pc])
            pv = functools.reduce(jnp.add, [_dot(p.astype(BF16), vr[rs, :])
                                            for p, (_, vr, rs, _) in zip(pc, regions)])
            outs.append(pv / denom)
        o_ref[rows, :] = _subln(outs[0] - lam * outs[1], sg_ref, lam_init)


def _attn_prompt(z_f, z_t, bias, lam_vecs, sg, B, S, lam_init):
    T = ATT_TILE
    qkw = 2 * DQK_ATT
    kcol = ATT_QK // qkw
    vcol = 2 * ATT_QK // DV_ATT
    b_diag, b_prev, b_meta, _, _ = bias
    vec = pl.BlockSpec((1, DQK_ATT), lambda b, h: (0, 0))
    return pl.pallas_call(
        functools.partial(_attn_prompt_kernel, lam_init=lam_init),
        grid=(B, H_ATT),
        in_specs=[
            pl.BlockSpec((S, qkw), lambda b, h: (b, h)),
            pl.BlockSpec((S, qkw), lambda b, h: (b, kcol + h)),
            pl.BlockSpec((S, DV_ATT), lambda b, h: (b, vcol + h)),
            pl.BlockSpec((N_META, qkw), lambda b, h: (b, kcol + h)),
            pl.BlockSpec((N_META, DV_ATT), lambda b, h: (b, vcol + h)),
            pl.BlockSpec((None, T, T), lambda b, h: (h, 0, 0)),
            pl.BlockSpec((None, T, T), lambda b, h: (h, 0, 0)),
            pl.BlockSpec((None, 2, T, LANES), lambda b, h: (h, 0, 0, 0)),
            vec, vec, vec, vec,
            pl.BlockSpec((1, DV_ATT), lambda b, h: (0, 0)),
        ],
        out_specs=pl.BlockSpec((S, DV_ATT), lambda b, h: (b, h)),
        out_shape=jax.ShapeDtypeStruct((B * S, ATT_V), F32),
        scratch_shapes=[
            pltpu.VMEM((LANES, qkw), BF16),
            pltpu.VMEM((LANES, DV_ATT), BF16),
        ],
        compiler_params=pltpu.CompilerParams(
            dimension_semantics=("arbitrary", "arbitrary"), vmem_limit_bytes=VMEM_LIMIT),
        name="attn_prompt",
    )(z_f, z_f, z_f, z_t, z_t, b_diag, b_prev, b_meta, *lam_vecs, sg)


def _attn_small_kernel(*refs, lam_init, has_cache):
    if has_cache:
        (q_ref, kc_ref, vc_ref, kn_ref, vn_ref, b_ref,
         lq1_ref, lk1_ref, lq2_ref, lk2_ref, sg_ref, o_ref, kpad, vpad, kb_s, vb_s) = refs
        past_len = kb_s.shape[0]
        chunk = kc_ref.shape[0]
        c = pl.program_id(1)
        r0 = pl.multiple_of(c * chunk, chunk)
        kb_s[pl.ds(r0, chunk), :] = kc_ref[...].reshape(chunk, ATT_QK).astype(BF16)
        vb_s[pl.ds(r0, chunk), :] = vc_ref[...].reshape(chunk, ATT_V).astype(BF16)

        @pl.when(c == past_len // chunk - 1)
        def _():
            _attn_small_body(q_ref, kn_ref, vn_ref, b_ref, lq1_ref, lk1_ref, lq2_ref, lk2_ref, sg_ref, o_ref,
                             kpad, vpad, kb_s, vb_s, lam_init=lam_init, past_len=past_len)
    else:
        (q_ref, kn_ref, vn_ref, b_ref,
         lq1_ref, lk1_ref, lq2_ref, lk2_ref, sg_ref, o_ref, kpad, vpad) = refs
        _attn_small_body(q_ref, kn_ref, vn_ref, b_ref, lq1_ref, lk1_ref, lq2_ref, lk2_ref, sg_ref, o_ref,
                         kpad, vpad, None, None, lam_init=lam_init, past_len=0)


def _attn_small_body(q_ref, kn_ref, vn_ref, b_ref, lq1_ref, lk1_ref, lq2_ref, lk2_ref, sg_ref, o_ref,
                     kpad, vpad, kb_s, vb_s, *, lam_init, past_len):
    has_cache = past_len > 0
    lam = _diff_lambda(lq1_ref, lk1_ref, lq2_ref, lk2_ref, lam_init)
    _pad_rows(kpad, kn_ref)
    _pad_rows(vpad, vn_ref)
    for h in range(H_ATT):
        vsl = slice(h * DV_ATT, (h + 1) * DV_ATT)
        outs = []
        for c in range(2):
            sl = slice(h * 2 * DQK_ATT + c * DQK_ATT, h * 2 * DQK_ATT + (c + 1) * DQK_ATT)
            qc = q_ref[:, sl]
            s_new = _dot_nt(qc, kpad[:, sl]) + b_ref[h, :, past_len:]
            m = jnp.max(s_new, axis=-1, keepdims=True)
            if has_cache:
                s_old = _dot_nt(qc, kb_s[:, sl]) + b_ref[h, :, :past_len]
                m = jnp.maximum(m, jnp.max(s_old, axis=-1, keepdims=True))
            p_new = jnp.exp2(s_new - m)
            l = jnp.sum(p_new, axis=-1, keepdims=True)
            o = _dot(p_new.astype(BF16), vpad[:, vsl])
            if has_cache:
                p_old = jnp.exp2(s_old - m)
                l = l + jnp.sum(p_old, axis=-1, keepdims=True)
                o = o + _dot(p_old.astype(BF16), vb_s[:, vsl])
            outs.append(o / l)
        o_ref[:, vsl] = _subln(outs[0] - lam * outs[1], sg_ref, lam_init)


def _attn_small(z_t, row0, n_streams, bias, lam_vecs, sg, lam_init, cache=None):
    vec = pl.BlockSpec((1, DQK_ATT), lambda b, c: (0, 0))
    in_specs = [pl.BlockSpec((N_META, ATT_QK), lambda b, c: (row0 + b, 0))]
    args = [z_t]
    scratch = [pltpu.VMEM((LANES, ATT_QK), BF16), pltpu.VMEM((LANES, ATT_V), BF16)]
    n_chunks = 1
    if cache is not None:
        ck, cv, layer = cache
        past_len = ck.shape[2]
        chunk = min(past_len, CACHE_CHUNK)
        assert past_len % chunk == 0
        n_chunks = past_len // chunk
        in_specs += [
            pl.BlockSpec((None, None, chunk, H_ATT, 2 * DQK_ATT), lambda b, c: (layer, b, c, 0, 0)),
            pl.BlockSpec((None, None, chunk, H_ATT, DV_ATT), lambda b, c: (layer, b, c, 0, 0))]
        args += [ck, cv]
        scratch += [pltpu.VMEM((past_len, ATT_QK), BF16), pltpu.VMEM((past_len, ATT_V), BF16)]
    in_specs += [
        pl.BlockSpec((N_META, ATT_QK), lambda b, c: (row0 + b, 1)),
        pl.BlockSpec((N_META, ATT_V), lambda b, c: (row0 + b, 2 * ATT_QK // ATT_V)),
        pl.BlockSpec(bias.shape, lambda b, c: (0, 0, 0)),
        vec, vec, vec, vec,
        pl.BlockSpec((1, DV_ATT), lambda b, c: (0, 0)),
    ]
    args += [z_t, z_t, bias, *lam_vecs, sg]
    return pl.pallas_call(
        functools.partial(_attn_small_kernel, lam_init=lam_init, has_cache=cache is not None),
        grid=(n_streams, n_chunks),
        in_specs=in_specs,
        out_specs=pl.BlockSpec((N_META, ATT_V), lambda b, c: (b, 0)),
        out_shape=jax.ShapeDtypeStruct((n_streams * N_META, ATT_V), F32),
        scratch_shapes=scratch,
        compiler_params=pltpu.CompilerParams(
            dimension_semantics=("arbitrary", "arbitrary"), vmem_limit_bytes=VMEM_LIMIT),
        name="attn_cached" if cache is not None else "attn_meta",
    )(*args)


def _mlstm_kernel(*refs, n_blocks, has_init):
    refs = list(refs)
    qh_ref, kh_ref, vh_ref, gh_ref = refs[:4]
    pos = 4
    if n_blocks:
        qf_ref, kf_ref, vf_ref, gf_ref = refs[pos:pos + 4]
        pos += 4
    if has_init:
        c0_ref, n0_ref, m0_ref = refs[pos:pos + 3]
        pos += 3
    hh_ref = refs[pos]
    pos += 1
    if n_blocks:
        hf_ref = refs[pos]
        pos += 1
    c_ref, n_ref, m_ref = refs[pos:pos + 3]
    qs, ks, vs, gs = refs[pos + 3:pos + 7]

    t = pl.program_id(1)
    T = M_TILE
    block = functools.partial(_mlstm_block, c_ref=c_ref, n_ref=n_ref, m_ref=m_ref)

    @pl.when(t == 0)
    def _():
        if has_init:
            c_ref[...] = c0_ref[...]
            n_ref[...] = n0_ref[...]
            m_ref[...] = m0_ref[...]
        else:
            c_ref[...] = jnp.zeros(c_ref.shape, F32)
            n_ref[...] = jnp.zeros(n_ref.shape, F32)
            m_ref[...] = jnp.zeros(m_ref.shape, F32)
        nh = qh_ref.shape[0]
        for dst, src in ((qs, qh_ref), (ks, kh_ref), (vs, vh_ref)):
            dst[...] = jnp.zeros(dst.shape, F32)
            dst[0:nh, :] = src[...]
        lane = lax.broadcasted_iota(jnp.int32, gs.shape, 1)
        gs[...] = jnp.where(lane < H_M, NEG, 0.0)
        gs[0:nh, :] = gh_ref[...]

        block(qs, ks, vs, gs, hh_ref, nh)

    if n_blocks:
        @pl.when(t > 0)
        def _():
            block(qf_ref, kf_ref, vf_ref, gf_ref, hf_ref, T)


def _mlstm_block(q_src, k_src, v_src, g_src, h_dst, n_out, c_ref, n_ref, m_ref):
    T = M_TILE
    g = g_src[...]
    row = lax.broadcasted_iota(jnp.int32, (T, T), 0)
    colid = lax.broadcasted_iota(jnp.int32, (T, T), 1)
    causal = row >= colid
    tril = jnp.where(causal, 1.0, 0.0).astype(F32)
    cum = jnp.dot(tril, g, preferred_element_type=F32, precision=lax.Precision.HIGHEST)
    g_t = g.T
    cum_t = cum.T
    last = cum[T - 1:T, :]

    for h in range(H_M):
        q = q_src[:, h * DQK_M:(h + 1) * DQK_M]
        k = k_src[:, h * DQK_M:(h + 1) * DQK_M]
        v = v_src[:, h * DV_M:(h + 1) * DV_M]
        qb = q.astype(BF16)
        kb = k.astype(BF16)
        m_old = m_ref[h, 0:1, 0:1]
        c_old = c_ref[h]
        n_old = n_ref[h]
        bh_c = _col(cum, H_M + h)
        ih_c = _col(g, h)
        bh_r = cum_t[H_M + h:H_M + h + 1, :]
        ih_r = g_t[h:h + 1, :]
        d = jnp.where(causal, bh_c - bh_r + ih_r, NEG)
        inter = bh_c + m_old
        mt = jnp.maximum(inter, jnp.max(d, axis=-1, keepdims=True))
        a = jnp.exp(d - mt) * _dot_nt(qb, kb)
        iw = jnp.exp(inter - mt)
        num = _dot(a.astype(BF16), v.astype(BF16)) + iw * _dot_nt(qb, c_old.astype(BF16))
        den = jnp.sum(a, axis=-1, keepdims=True) + iw * jnp.sum(q * n_old, axis=-1, keepdims=True)
        den = jnp.maximum(jnp.abs(den), jnp.exp(-mt))
        hout = num / den
        h_dst[:, h * DV_M:(h + 1) * DV_M] = hout[0:n_out, :]

        bl = _col(last, H_M + h)
        g_c = bl - bh_c + ih_c
        m_new = jnp.maximum(bl + m_old, jnp.max(g_c, axis=0, keepdims=True))
        sw = jnp.exp(g_c - m_new)
        dec = jnp.exp(bl + m_old - m_new)
        c_ref[h] = dec * c_old + _dot(v.T.astype(BF16), (sw * k).astype(BF16))
        n_ref[h] = dec * n_old + jnp.sum(sw * k, axis=0, keepdims=True)
        m_ref[h] = jnp.broadcast_to(m_new, m_ref.shape[1:])


def _mlstm(z_t, g_t, row0, n_streams, frames=None, init=None):
    T = M_TILE
    qcol = 0
    vcol = 2 * M_QK // M_V
    nh = N_META
    in_specs = [
        pl.BlockSpec((nh, M_QK), lambda b, t: (row0 + b, qcol)),
        pl.BlockSpec((nh, M_QK), lambda b, t: (row0 + b, qcol + 1)),
        pl.BlockSpec((nh, M_V), lambda b, t: (row0 + b, vcol)),
        pl.BlockSpec((nh, LANES), lambda b, t: (row0 + b, 0)),
    ]
    args = [z_t, z_t, z_t, g_t]
    n_blocks = 0
    if frames is not None:
        z_f, g_f, fps = frames
        n_blocks = fps // T
        fidx = lambda b, t: b * n_blocks + jnp.maximum(t - 1, 0)
        in_specs += [
            pl.BlockSpec((T, M_QK), lambda b, t: (fidx(b, t), qcol)),
            pl.BlockSpec((T, M_QK), lambda b, t: (fidx(b, t), qcol + 1)),
            pl.BlockSpec((T, M_V), lambda b, t: (fidx(b, t), vcol)),
            pl.BlockSpec((T, LANES), lambda b, t: (fidx(b, t), 0)),
        ]
        args += [z_f, z_f, z_f, g_f]
    state_specs = (
        pl.BlockSpec((None, H_M, DV_M, DQK_M), lambda b, t: (b, 0, 0, 0)),
        pl.BlockSpec((None, H_M, 1, DQK_M), lambda b, t: (b, 0, 0, 0)),
        pl.BlockSpec((None, H_M, 8, LANES), lambda b, t: (b, 0, 0, 0)),
    )
    if init is not None:
        in_specs += list(state_specs)
        args += list(init)
    out_specs = [pl.BlockSpec((nh, M_V), lambda b, t: (b, 0))]
    out_shape = [jax.ShapeDtypeStruct((n_streams * nh, M_V), F32)]
    if n_blocks:
        out_specs.append(pl.BlockSpec((T, M_V), lambda b, t: (fidx(b, t), 0)))
        out_shape.append(jax.ShapeDtypeStruct((n_streams * fps, M_V), F32))
    out_specs += list(state_specs)
    out_shape += [
        jax.ShapeDtypeStruct((n_streams, H_M, DV_M, DQK_M), F32),
        jax.ShapeDtypeStruct((n_streams, H_M, 1, DQK_M), F32),
        jax.ShapeDtypeStruct((n_streams, H_M, 8, LANES), F32),
    ]
    return pl.pallas_call(
        functools.partial(_mlstm_kernel, n_blocks=n_blocks, has_init=init is not None),
        grid=(n_streams, 1 + n_blocks),
        in_specs=in_specs,
        out_specs=tuple(out_specs),
        out_shape=tuple(out_shape),
        scratch_shapes=[
            pltpu.VMEM((T, M_QK), F32), pltpu.VMEM((T, M_QK), F32),
            pltpu.VMEM((T, M_V), F32), pltpu.VMEM((T, LANES), F32),
        ],
        compiler_params=pltpu.CompilerParams(
            dimension_semantics=("arbitrary", "arbitrary"), vmem_limit_bytes=VMEM_LIMIT),
        name="mlstm_prompt" if n_blocks else "mlstm_cached",
    )(*args)


def _merge_kernel(xf_ref, xt_ref, attf_ref, attt_ref, hmf_ref, hmt_ref, zf_ref, zt_ref, mg_ref, wo_ref,
                  of_ref, ot_ref, *, n_frame_tiles):
    i = pl.program_id(0)

    def tile(x_ref, att_ref, hm_ref, og_ref, o_ref):
        parts = []
        for h in range(H_M):
            sl = slice(h * DV_M, (h + 1) * DV_M)
            parts.append(_rms(hm_ref[:, sl], mg_ref[:, sl]) * og_ref[:, sl])
        hmn = jnp.concatenate(parts, axis=1).astype(BF16)
        o_ref[...] = (x_ref[...] + _dot(att_ref[...].astype(BF16), wo_ref[0:ATT_V, :])
                      + _dot(hmn, wo_ref[ATT_V:, :]))

    @pl.when(i < n_frame_tiles)
    def _():
        tile(xf_ref, attf_ref, hmf_ref, zf_ref, of_ref)

    @pl.when(i == n_frame_tiles)
    def _():
        tile(xt_ref, attt_ref, hmt_ref, zt_ref, ot_ref)


def _merge(x_f, x_t, att_f, att_t, hm_f, hm_t, z_f, z_t, mg, w_out, layer):
    NF, D = x_f.shape
    NT = x_t.shape[0]
    bm = MERGE_ROWS
    nft = NF // bm
    assert NF % bm == 0
    ogcol = (N_MAIN - N_ATT - M_V) // M_V
    fr = lambda i: (jnp.minimum(i, nft - 1), 0)
    tl = lambda i: (0, 0)
    once = dict(pipeline_mode=pl.Buffered(1))
    return pl.pallas_call(
        functools.partial(_merge_kernel, n_frame_tiles=nft),
        grid=(nft + 1,),
        in_specs=[
            pl.BlockSpec((bm, D), fr), pl.BlockSpec((NT, D), tl, **once),
            pl.BlockSpec((bm, ATT_V), fr), pl.BlockSpec((NT, ATT_V), tl, **once),
            pl.BlockSpec((bm, M_V), fr), pl.BlockSpec((NT, M_V), tl, **once),
            pl.BlockSpec((bm, M_V), lambda i: (jnp.minimum(i, nft - 1), ogcol)),
            pl.BlockSpec((NT, M_V), lambda i: (0, ogcol), **once),
            pl.BlockSpec((1, M_V), tl),
            pl.BlockSpec((None, D, D), lambda i: (layer, 0, 0), **once),
        ],
        out_specs=(pl.BlockSpec((bm, D), fr), pl.BlockSpec((NT, D), tl)),
        out_shape=(jax.ShapeDtypeStruct((NF, D), F32), jax.ShapeDtypeStruct((NT, D), F32)),
        compiler_params=pltpu.CompilerParams(
            dimension_semantics=("arbitrary",), vmem_limit_bytes=VMEM_LIMIT),
        name="merge_out_proj",
    )(x_f, x_t, att_f, att_t, hm_f, hm_t, z_f, z_t, mg, w_out)


def _ffn_kernel(xf_ref, xt_ref, g2_ref, wu_ref, wd_ref, of_ref, ot_ref, xn_s, *, n_frame_tiles):
    i = pl.program_id(0)
    k = pl.program_id(1)

    def tile(x_ref, o_ref):
        nrows = x_ref.shape[0]

        @pl.when(k == 0)
        def _():
            x = x_ref[...]
            xn_s[0:nrows, :] = _rms(x, g2_ref[...]).astype(BF16)
            o_ref[...] = x

        u = jnp.maximum(_dot(xn_s[0:nrows, :], wu_ref[...]), 0.0)
        o_ref[...] += _dot((u * u).astype(BF16), wd_ref[...])

    @pl.when(i < n_frame_tiles)
    def _():
        tile(xf_ref, of_ref)

    @pl.when(i == n_frame_tiles)
    def _():
        tile(xt_ref, ot_ref)


def _ffn(x_f, x_t, g2, w_up, w_down, layer):
    NF, D = x_f.shape
    NT = x_t.shape[0]
    FF = w_up.shape[2]
    bm, fc = FF_ROWS, FF_COLS
    nft = NF // bm
    assert NF % bm == 0 and FF % fc == 0 and NT <= bm
    fr = lambda i, k: (jnp.minimum(i, nft - 1), 0)
    tl = lambda i, k: (0, 0)
    return pl.pallas_call(
        functools.partial(_ffn_kernel, n_frame_tiles=nft),
        grid=(nft + 1, FF // fc),
        in_specs=[
            pl.BlockSpec((bm, D), fr),
            pl.BlockSpec((NT, D), tl),
            pl.BlockSpec((1, D), tl),
            pl.BlockSpec((None, D, fc), lambda i, k: (layer, 0, k)),
            pl.BlockSpec((None, fc, D), lambda i, k: (layer, k, 0)),
        ],
        out_specs=(pl.BlockSpec((bm, D), fr), pl.BlockSpec((NT, D), tl)),
        out_shape=(jax.ShapeDtypeStruct((NF, D), F32), jax.ShapeDtypeStruct((NT, D), F32)),
        scratch_shapes=[pltpu.VMEM((bm, D), BF16)],
        compiler_params=pltpu.CompilerParams(
            dimension_semantics=("arbitrary", "arbitrary"), vmem_limit_bytes=VMEM_LIMIT),
        name="ffn",
    )(x_f, x_t, g2, w_up, w_down)


def kernel(x_prompt, x_sample, cache_k, cache_v, state_C, state_n, state_m, meta_tokens, rel_bias, norm1_g, w_in, b_i, b_f, q_norm_g, k_norm_g, lambda_q1, lambda_k1, lambda_q2, lambda_k2, subln_g, mnorm_g, w_out, norm2_g, w_up, w_down):
    B, S, D = x_prompt.shape
    Bd, Td, _ = x_sample.shape
    depth = w_in.shape[0]
    Pc = cache_k.shape[2]
    assert Td == N_META and S % ATT_TILE == 0 and S % M_TILE == 0 and Pc % LANES == 0
    assert ATT_TILE % CHUNK == 0 and CHUNK & (CHUNK - 1) == 0 and ATT_TILE >= MAX_DISTANCE

    n_frames = B * S
    n_meta = B * N_META
    n_samp = Bd * Td
    assert n_meta + n_samp <= TAIL_ROWS
    srow = n_meta // N_META

    meta = jnp.broadcast_to(meta_tokens.astype(F32)[None], (B, N_META, D)).reshape(n_meta, D)
    x_f = x_prompt.astype(F32).reshape(n_frames, D)
    x_t = jnp.concatenate([meta, x_sample.astype(F32).reshape(n_samp, D),
                           jnp.zeros((TAIL_ROWS - n_meta - n_samp, D), F32)], axis=0)

    bias = _bias_tiles(rel_bias.astype(F32), Pc, Td)
    ck = cache_k.astype(F32)
    cv = cache_v.astype(F32)
    zeros_pad_att = jnp.zeros((TAIL_ROWS - n_meta - n_samp, ATT_V), F32)

    g_off = 2 * ATT_QK + ATT_V + 2 * M_QK + M_V
    w_in_t = jnp.swapaxes(w_in.astype(F32), 1, 2)
    w_out_b, w_up_b, w_down_b = (w.astype(BF16) for w in (w_out, w_up, w_down))
    outs = {name: [] for name in ("Cp", "np", "mp", "Cs", "ns", "ms")}
    caches = None

    for l in range(depth):
        lam_init = 0.8 - 0.6 * math.exp(-0.3 * l)
        gate_bias = jnp.pad(jnp.concatenate([b_i[l], b_f[l]]).astype(F32), (0, LANES - 2 * H_M))[None]
        lam_vecs = [v[l].astype(F32)[None] for v in (lambda_q1, lambda_k1, lambda_q2, lambda_k2)]
        sg = subln_g[l].astype(F32)[None]

        za_f, za_t, zm_f, zm_t, g_f, g_t, *caches = _inproj(
            x_f, x_t, norm1_g[l].astype(F32)[None], w_in_t, l, gate_bias,
            q_norm_g[l].astype(F32)[None], k_norm_g[l].astype(F32)[None], B, Bd, caches)

        att_f = _attn_prompt(za_f, za_t, bias, lam_vecs, sg, B, S, lam_init)
        att_m = _attn_small(za_t, 0, B, bias[3], lam_vecs, sg, lam_init)
        att_s = _attn_small(za_t, srow, Bd, bias[4], lam_vecs, sg, lam_init, cache=(ck, cv, l))
        att_t = jnp.concatenate([att_m, att_s, zeros_pad_att], axis=0)

        hm_m, hm_f, C_p, n_p, m_p = _mlstm(zm_t, g_t, 0, B, frames=(zm_f, g_f, S))
        init = (state_C[l].astype(F32), state_n[l].astype(F32)[:, :, None, :],
                jnp.broadcast_to(state_m[l].astype(F32)[:, :, None, None], (Bd, H_M, 8, LANES)))
        hm_s, C_s, n_s, m_s = _mlstm(zm_t, g_t, srow, Bd, init=init)
        hm_t = jnp.concatenate([hm_m, hm_s, zeros_pad_att], axis=0)

        x_f, x_t = _merge(x_f, x_t, att_f, att_t, hm_f, hm_t, zm_f, zm_t,
                          mnorm_g[l].astype(F32)[None], w_out_b, l)
        x_f, x_t = _ffn(x_f, x_t, norm2_g[l].astype(F32)[None], w_up_b, w_down_b, l)

        outs["Cp"].append(C_p)
        outs["np"].append(n_p[:, :, 0, :])
        outs["mp"].append(m_p[:, :, 0, 0])
        outs["Cs"].append(C_s)
        outs["ns"].append(n_s[:, :, 0, :])
        outs["ms"].append(m_s[:, :, 0, 0])

    y_prompt = x_f.reshape(B, S, D)
    y_sample = x_t[n_meta:n_meta + n_samp].reshape(Bd, Td, D)
    k_p, v_p, k_s, v_s = caches
    return (y_prompt, y_sample,
            k_p, v_p, jnp.stack(outs["Cp"]), jnp.stack(outs["np"]), jnp.stack(outs["mp"]),
            k_s.reshape(depth, Bd, Td, H_ATT, 2 * DQK_ATT), v_s.reshape(depth, Bd, Td, H_ATT, DV_ATT),
            jnp.stack(outs["Cs"]), jnp.stack(outs["ns"]), jnp.stack(outs["ms"]))
```

```python
import functools
import math

import jax
import jax.numpy as jnp
from jax import lax
from jax.experimental import pallas as pl
from jax.experimental.pallas import tpu as pltpu

F32 = jnp.float32
BF16 = jnp.bfloat16

CHUNK = 64
N_META = 16
H_ATT = 4
DQK_ATT = 128
DV_ATT = 256
H_M = 4
DQK_M = 128
DV_M = 256
N_BUCKETS = 32
MAX_DISTANCE = 128
EPS = 1e-6
NEG = -1e30
LOG2E = math.log2(math.e)

ATT_QK = H_ATT * 2 * DQK_ATT
ATT_V = H_ATT * DV_ATT
M_QK = H_M * DQK_M
M_V = H_M * DV_M
N_MAIN = 2 * ATT_QK + ATT_V + 2 * M_QK + M_V + M_V
N_ATT = 2 * ATT_QK + ATT_V

LANES = 128
ATT_TILE = 256
M_TILE = 256
TAIL_ROWS = 256
PROJ_ROWS = 1024
PROJ_COLS = 512
MERGE_ROWS = 512
CACHE_CHUNK = 1024
FF_ROWS = 512
FF_COLS = 1024
VMEM_LIMIT = 56 * 1024 * 1024

_NT = (((1,), (1,)), ((), ()))


def _dot(a, b):
    return jnp.dot(a, b, preferred_element_type=F32)


def _dot_nt(a, b):
    return lax.dot_general(a, b, _NT, preferred_element_type=F32)


def _log_sigmoid(x):
    return jnp.minimum(x, 0.0) - jnp.log1p(jnp.exp(-jnp.abs(x)))


def _col(a, idx):
    lane = lax.broadcasted_iota(jnp.int32, a.shape, 1)
    return jnp.sum(jnp.where(lane == idx, a, 0.0), axis=1, keepdims=True)


def _rms(x, gain):
    ms = jnp.mean(x * x, axis=-1, keepdims=True)
    return x * lax.rsqrt(ms + EPS) * gain


def _bias_from_rel(rel, rb_ref, h):
    nb = N_BUCKETS // 2
    max_exact = nb // 2
    ret = jnp.where(rel > 0, nb, 0)
    n = jnp.abs(rel)
    nf = jnp.maximum(n, 1).astype(F32)
    large = max_exact + (jnp.log(nf / max_exact) / math.log(MAX_DISTANCE / max_exact)
                         * (nb - max_exact)).astype(jnp.int32)
    large = jnp.minimum(large, nb - 1)
    bucket = ret + jnp.where(n < max_exact, n, large)
    out = jnp.zeros(rel.shape, F32)
    for b in range(N_BUCKETS):
        out = jnp.where(bucket == b, rb_ref[b, h], out)
    return out


def _bias_kernel(rb_ref, diag_ref, prev_ref, meta_ref, mm_ref, samp_ref, *, past_len, dec_seq):
    h = pl.program_id(0)
    shift = CHUNK.bit_length() - 1
    T = ATT_TILE
    far = _bias_from_rel(jnp.full((8, LANES), -(2 * MAX_DISTANCE + 1), jnp.int32), rb_ref, h)[0:1, 0:1]

    qi = lax.broadcasted_iota(jnp.int32, (T, T), 0)
    kj = lax.broadcasted_iota(jnp.int32, (T, T), 1)
    bias = (_bias_from_rel(kj - qi, rb_ref, h) - far) * LOG2E
    diag_ref[...] = jnp.where((kj >> shift) <= (qi >> shift), bias, NEG)
    prev_ref[...] = (_bias_from_rel(kj - T - qi, rb_ref, h) - far) * LOG2E

    qi = lax.broadcasted_iota(jnp.int32, (T, LANES), 0)
    kj = lax.broadcasted_iota(jnp.int32, (T, LANES), 1)
    for var in range(2):
        bias = (_bias_from_rel(kj - (N_META + qi + var * T), rb_ref, h) - far) * LOG2E
        meta_ref[var] = jnp.where(kj < N_META, bias, NEG)

    qi = lax.broadcasted_iota(jnp.int32, (N_META, LANES), 0)
    kj = lax.broadcasted_iota(jnp.int32, (N_META, LANES), 1)
    mm_ref[...] = jnp.where(kj < N_META, _bias_from_rel(kj - qi, rb_ref, h) * LOG2E, NEG)

    width = samp_ref.shape[-1]
    qi = lax.broadcasted_iota(jnp.int32, (dec_seq, width), 0) + past_len
    kj = lax.broadcasted_iota(jnp.int32, (dec_seq, width), 1)
    bias = _bias_from_rel(kj - qi, rb_ref, h) * LOG2E
    ok = ((kj >> shift) <= (qi >> shift)) & (kj < past_len + dec_seq)
    samp_ref[...] = jnp.where(ok, bias, NEG)


def _bias_tiles(rel_bias, past_len, dec_seq):
    T = ATT_TILE
    sw = past_len + LANES
    out_shape = (
        jax.ShapeDtypeStruct((H_ATT, T, T), F32),
        jax.ShapeDtypeStruct((H_ATT, T, T), F32),
        jax.ShapeDtypeStruct((H_ATT, 2, T, LANES), F32),
        jax.ShapeDtypeStruct((H_ATT, N_META, LANES), F32),
        jax.ShapeDtypeStruct((H_ATT, dec_seq, sw), F32),
    )
    out_specs = (
        pl.BlockSpec((None, T, T), lambda h: (h, 0, 0)),
        pl.BlockSpec((None, T, T), lambda h: (h, 0, 0)),
        pl.BlockSpec((None, 2, T, LANES), lambda h: (h, 0, 0, 0)),
        pl.BlockSpec((None, N_META, LANES), lambda h: (h, 0, 0)),
        pl.BlockSpec((None, dec_seq, sw), lambda h: (h, 0, 0)),
    )
    return pl.pallas_call(
        functools.partial(_bias_kernel, past_len=past_len, dec_seq=dec_seq),
        grid=(H_ATT,),
        in_specs=[pl.BlockSpec(memory_space=pltpu.SMEM)],
        out_specs=out_specs,
        out_shape=out_shape,
        name="rel_bias_tiles",
    )(rel_bias)


def _zero_fill_copies(zbuf, zsem, cache_refs, depth):
    rows = zbuf.shape[0]
    copies = []
    for dst in cache_refs:
        lead = dst.shape[1:-3]
        n_rows = dst.shape[-3]
        for l in range(1, depth):
            for b in range(lead[0] if lead else 1):
                view = dst.at[l, b] if lead else dst.at[l]
                for r0 in range(0, n_rows, rows):
                    n = min(rows, n_rows - r0)
                    copies.append(pltpu.make_async_copy(zbuf.at[pl.ds(0, n)], view.at[pl.ds(r0, n)],
                                                        zsem.at[len(copies)]))
    return copies


def _kv_copies(stage, sem, dst_prompt, dst_sample, i, *, layer, tail, n_batch, frames_per_batch, n_sample_rows):
    dst_prompt = dst_prompt.at[layer]
    dst_sample = dst_sample.at[layer]
    if not tail:
        rows = stage.shape[0]
        tiles_per_batch = frames_per_batch // rows
        b = i // tiles_per_batch
        t = i - b * tiles_per_batch
        return [pltpu.make_async_copy(stage, dst_prompt.at[b, pl.ds(N_META + t * rows, rows)], sem.at[0])]
    copies = [pltpu.make_async_copy(stage.at[pl.ds(b * N_META, N_META)], dst_prompt.at[b, pl.ds(0, N_META)],
                                    sem.at[b]) for b in range(n_batch)]
    copies.append(pltpu.make_async_copy(stage.at[pl.ds(n_batch * N_META, n_sample_rows)], dst_sample,
                                        sem.at[n_batch]))
    return copies


def _inproj_kernel(*refs, layer, depth, n_frame_tiles, n_batch, frames_per_batch, n_sample_rows):
    xf_ref, xt_ref, g1_ref, wa_ref, wg_ref, gb_ref, qg_ref, kg_ref = refs[:8]
    (zaf_ref, zat_ref, zmf_ref, zmt_ref, gf_ref, gt_ref, kp_ref, vp_ref, ks_ref, vs_ref,
     xn_s, y2d, stage, sem, zbuf, zsem) = refs[-16:]
    i = pl.program_id(0)
    j = pl.program_id(1)
    bn = wa_ref.shape[1]

    if layer == 0 and depth > 1:
        zero_copies = _zero_fill_copies(zbuf, zsem, (kp_ref, vp_ref, ks_ref, vs_ref), depth)

        @pl.when((i == 0) & (j == 0))
        def _():
            zbuf[...] = jnp.zeros(zbuf.shape, F32)
            for c in zero_copies:
                c.start()

        @pl.when((i == pl.num_programs(0) - 1) & (j == pl.num_programs(1) - 1))
        def _():
            for c in zero_copies:
                c.wait()

    assert ATT_QK % bn == 0 and ATT_V % bn == 0 and M_QK % bn == 0 and M_V % bn == 0 and bn % DV_ATT == 0
    j_k = ATT_QK // bn
    j_v = j_k + ATT_QK // bn
    j_mq = j_v + ATT_V // bn
    j_mk = j_mq + M_QK // bn
    j_mv = j_mk + M_QK // bn
    j_og = j_mv + M_V // bn

    def tile(x_ref, za_ref, zm_ref, g_ref, tail):
        nrows = x_ref.shape[0]
        st = stage.at[pl.ds(0, nrows)]
        copies = functools.partial(_kv_copies, st, sem, i=i, layer=layer, tail=tail, n_batch=n_batch,
                                   frames_per_batch=frames_per_batch, n_sample_rows=n_sample_rows)

        @pl.when(j == 0)
        def _():
            xb = _rms(x_ref[...], g1_ref[...]).astype(BF16)
            xn_s[0:nrows, :] = xb
            gates = _dot_nt(xb, wg_ref[0].astype(BF16)) + gb_ref[...]
            lane = lax.broadcasted_iota(jnp.int32, gates.shape, 1)
            g_ref[...] = jnp.where(lane < H_M, gates, _log_sigmoid(gates))

        acc = _dot_nt(xn_s[0:nrows, :], wa_ref[0].astype(BF16))

        def group_norm(gain_ref, scale):
            parts = []
            for gi in range(bn // DQK_ATT):
                y = _rms(acc[:, gi * DQK_ATT:(gi + 1) * DQK_ATT], gain_ref[...])
                parts.append(y if scale is None else y * scale)
            return jnp.concatenate(parts, axis=1)

        def stage_heads(y, jj, j0, dst_prompt, dst_sample):
            c0 = (jj - j0) * bn
            y2d[0:nrows, c0:c0 + bn] = y
            if jj == j0 + ATT_QK // bn - 1:
                if j0 != j_k:
                    for c in copies(kp_ref, ks_ref):
                        c.wait()
                st[...] = y2d[0:nrows, :].reshape(nrows, H_ATT, DV_ATT)
                for c in copies(dst_prompt, dst_sample):
                    c.start()

        @pl.when(j < j_k)
        def _():
            za_ref[...] = group_norm(qg_ref, DQK_ATT ** -0.5 * LOG2E).astype(BF16)

        for jj in range(j_k, j_v):
            @pl.when(j == jj)
            def _(jj=jj):
                y = group_norm(kg_ref, None)
                za_ref[...] = y.astype(BF16)
                stage_heads(y, jj, j_k, kp_ref, ks_ref)

        for jj in range(j_v, j_mq):
            @pl.when(j == jj)
            def _(jj=jj):
                za_ref[...] = acc.astype(BF16)
                stage_heads(acc, jj, j_v, vp_ref, vs_ref)

        @pl.when(j == j_mq)
        def _():
            for c in copies(vp_ref, vs_ref):
                c.wait()

        @pl.when(((j >= j_mq) & (j < j_mk)) | ((j >= j_mv) & (j < j_og)))
        def _():
            zm_ref[...] = acc

        @pl.when((j >= j_mk) & (j < j_mv))
        def _():
            zm_ref[...] = acc * (DQK_M ** -0.5)

        @pl.when(j >= j_og)
        def _():
            zm_ref[...] = jax.nn.sigmoid(acc)

    @pl.when(i < n_frame_tiles)
    def _():
        tile(xf_ref, zaf_ref, zmf_ref, gf_ref, False)

    @pl.when(i == n_frame_tiles)
    def _():
        tile(xt_ref, zat_ref, zmt_ref, gt_ref, True)


def _inproj(x_f, x_t, g1, w_all, layer, gate_bias, qg, kg, n_batch, n_sample_streams, caches):
    depth = w_all.shape[0]
    NF, D = x_f.shape
    NT = x_t.shape[0]
    bm, bn = PROJ_ROWS, PROJ_COLS
    nft = NF // bm
    nj = N_MAIN // bn
    nja = N_ATT // bn
    g_row = N_MAIN - M_V
    assert g_row % bn == 0 and g_row % 8 == 0 and w_all.shape[1] >= g_row + LANES

    def w_window(i, j):
        row = j * bn + jnp.where(j * bn >= g_row, 2 * H_M, 0)
        return (layer, pl.multiple_of(row, 8), 0)

    S = NF // n_batch
    L = N_META + S
    n_sample_rows = n_sample_streams * N_META
    assert NF % bm == 0 and S % bm == 0 and N_MAIN % bn == 0 and NT <= bm
    fr = lambda i: jnp.minimum(i, nft - 1)
    any_spec = pl.BlockSpec(memory_space=pl.ANY)
    assert (caches is None) == (layer == 0)
    prev = () if caches is None else tuple(caches)
    n_in = 8
    zero_rows = 256
    n_zero = (depth - 1) * (2 * n_batch * pl.cdiv(L, zero_rows) + 2 * pl.cdiv(n_sample_rows, zero_rows))
    return pl.pallas_call(
        functools.partial(_inproj_kernel, layer=layer, depth=depth, n_frame_tiles=nft, n_batch=n_batch,
                          frames_per_batch=S, n_sample_rows=n_sample_rows),
        grid=(nft + 1, nj),
        input_output_aliases={n_in + c: 6 + c for c in range(len(prev))},
        in_specs=[
            pl.BlockSpec((bm, D), lambda i, j: (fr(i), 0)),
            pl.BlockSpec((NT, D), lambda i, j: (0, 0), pipeline_mode=pl.Buffered(1)),
            pl.BlockSpec((1, D), lambda i, j: (0, 0)),
            pl.BlockSpec((pl.Element(1), pl.Element(bn), pl.Element(D)), w_window),
            pl.BlockSpec((pl.Element(1), pl.Element(LANES), pl.Element(D)), lambda i, j: (layer, g_row, 0)),
            pl.BlockSpec((1, LANES), lambda i, j: (0, 0)),
            pl.BlockSpec((1, DQK_ATT), lambda i, j: (0, 0)),
            pl.BlockSpec((1, DQK_ATT), lambda i, j: (0, 0)),
        ] + [any_spec] * len(prev),
        out_specs=(
            pl.BlockSpec((bm, bn), lambda i, j: (fr(i), jnp.where(i < nft, jnp.minimum(j, nja - 1), nja - 1))),
            pl.BlockSpec((NT, bn), lambda i, j: (0, jnp.where(i < nft, 0, jnp.minimum(j, nja - 1)))),
            pl.BlockSpec((bm, bn), lambda i, j: (fr(i), jnp.where(i < nft, jnp.maximum(j - nja, 0), nj - nja - 1))),
            pl.BlockSpec((NT, bn), lambda i, j: (0, jnp.where(i < nft, 0, jnp.maximum(j - nja, 0)))),
            pl.BlockSpec((bm, LANES), lambda i, j: (fr(i), 0)),
            pl.BlockSpec((NT, LANES), lambda i, j: (0, 0)),
            any_spec, any_spec, any_spec, any_spec,
        ),
        out_shape=(
            jax.ShapeDtypeStruct((NF, N_ATT), BF16),
            jax.ShapeDtypeStruct((NT, N_ATT), BF16),
            jax.ShapeDtypeStruct((NF, N_MAIN - N_ATT), F32),
            jax.ShapeDtypeStruct((NT, N_MAIN - N_ATT), F32),
            jax.ShapeDtypeStruct((NF, LANES), F32),
            jax.ShapeDtypeStruct((NT, LANES), F32),
            jax.ShapeDtypeStruct((depth, n_batch, L, H_ATT, 2 * DQK_ATT), F32),
            jax.ShapeDtypeStruct((depth, n_batch, L, H_ATT, DV_ATT), F32),
            jax.ShapeDtypeStruct((depth, n_sample_rows, H_ATT, 2 * DQK_ATT), F32),
            jax.ShapeDtypeStruct((depth, n_sample_rows, H_ATT, DV_ATT), F32),
        ),
        scratch_shapes=[
            pltpu.VMEM((bm, D), BF16),
            pltpu.VMEM((bm, ATT_QK), F32),
            pltpu.VMEM((bm, H_ATT, DV_ATT), F32),
            pltpu.SemaphoreType.DMA((n_batch + 1,)),
            pltpu.VMEM((zero_rows, H_ATT, DV_ATT), F32),
            pltpu.SemaphoreType.DMA((max(n_zero, 1),)),
        ],
        compiler_params=pltpu.CompilerParams(
            dimension_semantics=("arbitrary", "arbitrary"), vmem_limit_bytes=VMEM_LIMIT),
        name="in_proj",
    )(x_f, x_t, g1, w_all, w_all, gate_bias, qg, kg, *prev)


def _diff_lambda(lq1_ref, lk1_ref, lq2_ref, lk2_ref, lam_init):
    a = jnp.sum(lq1_ref[...] * lk1_ref[...], axis=-1, keepdims=True)
    b = jnp.sum(lq2_ref[...] * lk2_ref[...], axis=-1, keepdims=True)
    return jnp.exp(a) - jnp.exp(b) + lam_init


def _subln(o, sg_ref, lam_init):
    return _rms(o, sg_ref[...]) * (1.0 - lam_init)


def _pad_rows(dst, src):
    dst[...] = jnp.zeros(dst.shape, dst.dtype)
    dst[0:src.shape[0], :] = src[...].astype(dst.dtype)


def _attn_prompt_kernel(q_ref, k_ref, v_ref, km_ref, vm_ref, bd_ref, bp_ref, bm_ref,
                        lq1_ref, lk1_ref, lq2_ref, lk2_ref, sg_ref, o_ref,
                        kmb_s, vmb_s, *, lam_init):
    T = ATT_TILE
    n_tiles = q_ref.shape[0] // T
    kb_s, vb_s = k_ref, v_ref
    _pad_rows(kmb_s, km_ref)
    _pad_rows(vmb_s, vm_ref)
    lam = _diff_lambda(lq1_ref, lk1_ref, lq2_ref, lk2_ref, lam_init)

    for j in range(n_tiles):
        rows = slice(j * T, (j + 1) * T)
        regions = [(kmb_s, vmb_s, slice(0, LANES), bm_ref[min(j, 1)])]
        if j >= 2:
            regions.append((kb_s, vb_s, slice(0, (j - 1) * T), None))
        if j >= 1:
            regions.append((kb_s, vb_s, slice((j - 1) * T, j * T), bp_ref[...]))
        regions.append((kb_s, vb_s, rows, bd_ref[...]))

        outs = []
        for c in range(2):
            cols = slice(c * DQK_ATT, (c + 1) * DQK_ATT)
            qc = q_ref[rows, cols]
            ss = [_dot_nt(qc, kr[rs, cols]) for kr, _, rs, _ in regions]
            ss = [s if br is None else s + br for s, (_, _, _, br) in zip(ss, regions)]
            m = functools.reduce(jnp.maximum, [jnp.max(s, axis=-1, keepdims=True) for s in ss])
            pc = [jnp.exp2(s - m) for s in ss]
            denom = functools.reduce(jnp.add, [jnp.sum(p, axis=-1, keepdims=True) for p in pc])
            pv = functools.reduce(jnp.add, [_dot(p.astype(BF16), vr[rs, :])
                                            for p, (_, vr, rs, _) in zip(pc, regions)])
            outs.append(pv / denom)
        o_ref[rows, :] = _subln(outs[0] - lam * outs[1], sg_ref, lam_init)


def _attn_prompt(z_f, z_t, bias, lam_vecs, sg, B, S, lam_init):
    T = ATT_TILE
    qkw = 2 * DQK_ATT
    kcol = ATT_QK // qkw
    vcol = 2 * ATT_QK // DV_ATT
    b_diag, b_prev, b_meta, _, _ = bias
    vec = pl.BlockSpec((1, DQK_ATT), lambda b, h: (0, 0))
    return pl.pallas_call(
        functools.partial(_attn_prompt_kernel, lam_init=lam_init),
        grid=(B, H_ATT),
        in_specs=[
            pl.BlockSpec((S, qkw), lambda b, h: (b, h)),
            pl.BlockSpec((S, qkw), lambda b, h: (b, kcol + h)),
            pl.BlockSpec((S, DV_ATT), lambda b, h: (b, vcol + h)),
            pl.BlockSpec((N_META, qkw), lambda b, h: (b, kcol + h)),
            pl.BlockSpec((N_META, DV_ATT), lambda b, h: (b, vcol + h)),
            pl.BlockSpec((None, T, T), lambda b, h: (h, 0, 0)),
            pl.BlockSpec((None, T, T), lambda b, h: (h, 0, 0)),
            pl.BlockSpec((None, 2, T, LANES), lambda b, h: (h, 0, 0, 0)),
            vec, vec, vec, vec,
            pl.BlockSpec((1, DV_ATT), lambda b, h: (0, 0)),
        ],
        out_specs=pl.BlockSpec((S, DV_ATT), lambda b, h: (b, h)),
        out_shape=jax.ShapeDtypeStruct((B * S, ATT_V), F32),
        scratch_shapes=[
            pltpu.VMEM((LANES, qkw), BF16),
            pltpu.VMEM((LANES, DV_ATT), BF16),
        ],
        compiler_params=pltpu.CompilerParams(
            dimension_semantics=("arbitrary", "arbitrary"), vmem_limit_bytes=VMEM_LIMIT),
        name="attn_prompt",
    )(z_f, z_f, z_f, z_t, z_t, b_diag, b_prev, b_meta, *lam_vecs, sg)


def _attn_small_kernel(*refs, lam_init, has_cache):
    if has_cache:
        (q_ref, kc_ref, vc_ref, kn_ref, vn_ref, b_ref,
         lq1_ref, lk1_ref, lq2_ref, lk2_ref, sg_ref, o_ref, kpad, vpad, kb_s, vb_s) = refs
        past_len = kb_s.shape[0]
        chunk = kc_ref.shape[0]
        c = pl.program_id(1)
        r0 = pl.multiple_of(c * chunk, chunk)
        kb_s[pl.ds(r0, chunk), :] = kc_ref[...].reshape(chunk, ATT_QK).astype(BF16)
        vb_s[pl.ds(r0, chunk), :] = vc_ref[...].reshape(chunk, ATT_V).astype(BF16)

        @pl.when(c == past_len // chunk - 1)
        def _():
            _attn_small_body(q_ref, kn_ref, vn_ref, b_ref, lq1_ref, lk1_ref, lq2_ref, lk2_ref, sg_ref, o_ref,
                             kpad, vpad, kb_s, vb_s, lam_init=lam_init, past_len=past_len)
    else:
        (q_ref, kn_ref, vn_ref, b_ref,
         lq1_ref, lk1_ref, lq2_ref, lk2_ref, sg_ref, o_ref, kpad, vpad) = refs
        _attn_small_body(q_ref, kn_ref, vn_ref, b_ref, lq1_ref, lk1_ref, lq2_ref, lk2_ref, sg_ref, o_ref,
                         kpad, vpad, None, None, lam_init=lam_init, past_len=0)


def _attn_small_body(q_ref, kn_ref, vn_ref, b_ref, lq1_ref, lk1_ref, lq2_ref, lk2_ref, sg_ref, o_ref,
                     kpad, vpad, kb_s, vb_s, *, lam_init, past_len):
    has_cache = past_len > 0
    lam = _diff_lambda(lq1_ref, lk1_ref, lq2_ref, lk2_ref, lam_init)
    _pad_rows(kpad, kn_ref)
    _pad_rows(vpad, vn_ref)
    for h in range(H_ATT):
        vsl = slice(h * DV_ATT, (h + 1) * DV_ATT)
        outs = []
        for c in range(2):
            sl = slice(h * 2 * DQK_ATT + c * DQK_ATT, h * 2 * DQK_ATT + (c + 1) * DQK_ATT)
            qc = q_ref[:, sl]
            s_new = _dot_nt(qc, kpad[:, sl]) + b_ref[h, :, past_len:]
            m = jnp.max(s_new, axis=-1, keepdims=True)
            if has_cache:
                s_old = _dot_nt(qc, kb_s[:, sl]) + b_ref[h, :, :past_len]
                m = jnp.maximum(m, jnp.max(s_old, axis=-1, keepdims=True))
            p_new = jnp.exp2(s_new - m)
            l = jnp.sum(p_new, axis=-1, keepdims=True)
            o = _dot(p_new.astype(BF16), vpad[:, vsl])
            if has_cache:
                p_old = jnp.exp2(s_old - m)
                l = l + jnp.sum(p_old, axis=-1, keepdims=True)
                o = o + _dot(p_old.astype(BF16), vb_s[:, vsl])
            outs.append(o / l)
        o_ref[:, vsl] = _subln(outs[0] - lam * outs[1], sg_ref, lam_init)


def _attn_small(z_t, row0, n_streams, bias, lam_vecs, sg, lam_init, cache=None):
    vec = pl.BlockSpec((1, DQK_ATT), lambda b, c: (0, 0))
    in_specs = [pl.BlockSpec((N_META, ATT_QK), lambda b, c: (row0 + b, 0))]
    args = [z_t]
    scratch = [pltpu.VMEM((LANES, ATT_QK), BF16), pltpu.VMEM((LANES, ATT_V), BF16)]
    n_chunks = 1
    if cache is not None:
        ck, cv, layer = cache
        past_len = ck.shape[2]
        chunk = min(past_len, CACHE_CHUNK)
        assert past_len % chunk == 0
        n_chunks = past_len // chunk
        in_specs += [
            pl.BlockSpec((None, None, chunk, H_ATT, 2 * DQK_ATT), lambda b, c: (layer, b, c, 0, 0)),
            pl.BlockSpec((None, None, chunk, H_ATT, DV_ATT), lambda b, c: (layer, b, c, 0, 0))]
        args += [ck, cv]
        scratch += [pltpu.VMEM((past_len, ATT_QK), BF16), pltpu.VMEM((past_len, ATT_V), BF16)]
    in_specs += [
        pl.BlockSpec((N_META, ATT_QK), lambda b, c: (row0 + b, 1)),
        pl.BlockSpec((N_META, ATT_V), lambda b, c: (row0 + b, 2 * ATT_QK // ATT_V)),
        pl.BlockSpec(bias.shape, lambda b, c: (0, 0, 0)),
        vec, vec, vec, vec,
        pl.BlockSpec((1, DV_ATT), lambda b, c: (0, 0)),
    ]
    args += [z_t, z_t, bias, *lam_vecs, sg]
    return pl.pallas_call(
        functools.partial(_attn_small_kernel, lam_init=lam_init, has_cache=cache is not None),
        grid=(n_streams, n_chunks),
        in_specs=in_specs,
        out_specs=pl.BlockSpec((N_META, ATT_V), lambda b, c: (b, 0)),
        out_shape=jax.ShapeDtypeStruct((n_streams * N_META, ATT_V), F32),
        scratch_shapes=scratch,
        compiler_params=pltpu.CompilerParams(
            dimension_semantics=("arbitrary", "arbitrary"), vmem_limit_bytes=VMEM_LIMIT),
        name="attn_cached" if cache is not None else "attn_meta",
    )(*args)


def _mlstm_kernel(*refs, n_blocks, has_init):
    refs = list(refs)
    qh_ref, kh_ref, vh_ref, gh_ref = refs[:4]
    pos = 4
    if n_blocks:
        qf_ref, kf_ref, vf_ref, gf_ref = refs[pos:pos + 4]
        pos += 4
    if has_init:
        c0_ref, n0_ref, m0_ref = refs[pos:pos + 3]
        pos += 3
    hh_ref = refs[pos]
    pos += 1
    if n_blocks:
        hf_ref = refs[pos]
        pos += 1
    c_ref, n_ref, m_ref = refs[pos:pos + 3]
    qs, ks, vs, gs = refs[pos + 3:pos + 7]

    t = pl.program_id(1)
    block = functools.partial(_mlstm_block, c_ref=c_ref, n_ref=n_ref, m_ref=m_ref)

    @pl.when(t == 0)
    def _():
        if has_init:
            c_ref[...] = c0_ref[...]
            n_ref[...] = n0_ref[...]
            m_ref[...] = m0_ref[...]
        else:
            c_ref[...] = jnp.zeros(c_ref.shape, F32)
            n_ref[...] = jnp.zeros(n_ref.shape, F32)
            m_ref[...] = jnp.zeros(m_ref.shape, F32)
        nh = qh_ref.shape[0]
        for dst, src in ((qs, qh_ref), (ks, kh_ref), (vs, vh_ref)):
            dst[...] = jnp.zeros(dst.shape, F32)
            dst[0:nh, :] = src[...]
        lane = lax.broadcasted_iota(jnp.int32, gs.shape, 1)
        gs[...] = jnp.where(lane < H_M, NEG, 0.0)
        gs[0:nh, :] = gh_ref[...]

        block(qs, ks, vs, gs, hh_ref, nh)

    if n_blocks:
        @pl.when(t > 0)
        def _():
            block(qf_ref, kf_ref, vf_ref, gf_ref, hf_ref, qf_ref.shape[0])


def _mlstm_block(q_src, k_src, v_src, g_src, h_dst, n_out, c_ref, n_ref, m_ref):
    T = q_src.shape[0]
    g = g_src[...]
    row = lax.broadcasted_iota(jnp.int32, (T, T), 0)
    colid = lax.broadcasted_iota(jnp.int32, (T, T), 1)
    causal = row >= colid
    tril = jnp.where(causal, 1.0, 0.0).astype(F32)
    cum = jnp.dot(tril, g, preferred_element_type=F32, precision=lax.Precision.HIGHEST)
    g_t = g.T
    cum_t = cum.T
    last = cum[T - 1:T, :]

    for h in range(H_M):
        q = q_src[:, h * DQK_M:(h + 1) * DQK_M]
        k = k_src[:, h * DQK_M:(h + 1) * DQK_M]
        v = v_src[:, h * DV_M:(h + 1) * DV_M]
        qb = q.astype(BF16)
        kb = k.astype(BF16)
        m_old = m_ref[h, 0:1, 0:1]
        c_old = c_ref[h]
        n_old = n_ref[h]
        bh_c = _col(cum, H_M + h)
        ih_c = _col(g, h)
        bh_r = cum_t[H_M + h:H_M + h + 1, :]
        ih_r = g_t[h:h + 1, :]
        d = jnp.where(causal, bh_c - bh_r + ih_r, NEG)
        inter = bh_c + m_old
        mt = jnp.maximum(inter, jnp.max(d, axis=-1, keepdims=True))
        a = jnp.exp(d - mt) * _dot_nt(qb, kb)
        iw = jnp.exp(inter - mt)
        num = _dot(a.astype(BF16), v.astype(BF16)) + iw * _dot_nt(qb, c_old.astype(BF16))
        den = jnp.sum(a, axis=-1, keepdims=True) + iw * jnp.sum(q * n_old, axis=-1, keepdims=True)
        den = jnp.maximum(jnp.abs(den), jnp.exp(-mt))
        hout = num / den
        h_dst[:, h * DV_M:(h + 1) * DV_M] = hout[0:n_out, :]

        bl = _col(last, H_M + h)
        g_c = bl - bh_c + ih_c
        m_new = jnp.maximum(bl + m_old, jnp.max(g_c, axis=0, keepdims=True))
        sw = jnp.exp(g_c - m_new)
        dec = jnp.exp(bl + m_old - m_new)
        c_ref[h] = dec * c_old + _dot(v.T.astype(BF16), (sw * k).astype(BF16))
        n_ref[h] = dec * n_old + jnp.sum(sw * k, axis=0, keepdims=True)
        m_ref[h] = jnp.broadcast_to(m_new, m_ref.shape[1:])


def _mlstm(z_t, g_t, row0, n_streams, frames=None, init=None):
    T = M_TILE
    qcol = 0
    vcol = 2 * M_QK // M_V
    nh = N_META
    in_specs = [
        pl.BlockSpec((nh, M_QK), lambda b, t: (row0 + b, qcol)),
        pl.BlockSpec((nh, M_QK), lambda b, t: (row0 + b, qcol + 1)),
        pl.BlockSpec((nh, M_V), lambda b, t: (row0 + b, vcol)),
        pl.BlockSpec((nh, LANES), lambda b, t: (row0 + b, 0)),
    ]
    args = [z_t, z_t, z_t, g_t]
    n_blocks = 0
    if frames is not None:
        z_f, g_f, fps = frames
        n_blocks = fps // T
        fidx = lambda b, t: b * n_blocks + jnp.maximum(t - 1, 0)
        in_specs += [
            pl.BlockSpec((T, M_QK), lambda b, t: (fidx(b, t), qcol)),
            pl.BlockSpec((T, M_QK), lambda b, t: (fidx(b, t), qcol + 1)),
            pl.BlockSpec((T, M_V), lambda b, t: (fidx(b, t), vcol)),
            pl.BlockSpec((T, LANES), lambda b, t: (fidx(b, t), 0)),
        ]
        args += [z_f, z_f, z_f, g_f]
    state_specs = (
        pl.BlockSpec((None, H_M, DV_M, DQK_M), lambda b, t: (b, 0, 0, 0)),
        pl.BlockSpec((None, H_M, 1, DQK_M), lambda b, t: (b, 0, 0, 0)),
        pl.BlockSpec((None, H_M, 8, LANES), lambda b, t: (b, 0, 0, 0)),
    )
    if init is not None:
        in_specs += list(state_specs)
        args += list(init)
    out_specs = [pl.BlockSpec((nh, M_V), lambda b, t: (b, 0))]
    out_shape = [jax.ShapeDtypeStruct((n_streams * nh, M_V), F32)]
    if n_blocks:
        out_specs.append(pl.BlockSpec((T, M_V), lambda b, t: (fidx(b, t), 0)))
        out_shape.append(jax.ShapeDtypeStruct((n_streams * fps, M_V), F32))
    out_specs += list(state_specs)
    out_shape += [
        jax.ShapeDtypeStruct((n_streams, H_M, DV_M, DQK_M), F32),
        jax.ShapeDtypeStruct((n_streams, H_M, 1, DQK_M), F32),
        jax.ShapeDtypeStruct((n_streams, H_M, 8, LANES), F32),
    ]
    return pl.pallas_call(
        functools.partial(_mlstm_kernel, n_blocks=n_blocks, has_init=init is not None),
        grid=(n_streams, 1 + n_blocks),
        in_specs=in_specs,
        out_specs=tuple(out_specs),
        out_shape=tuple(out_shape),
        scratch_shapes=[
            pltpu.VMEM((LANES, M_QK), F32), pltpu.VMEM((LANES, M_QK), F32),
            pltpu.VMEM((LANES, M_V), F32), pltpu.VMEM((LANES, LANES), F32),
        ],
        compiler_params=pltpu.CompilerParams(
            dimension_semantics=("arbitrary", "arbitrary"), vmem_limit_bytes=VMEM_LIMIT),
        name="mlstm_prompt" if n_blocks else "mlstm_cached",
    )(*args)


def _merge_kernel(xf_ref, xt_ref, attf_ref, attt_ref, hmf_ref, hmt_ref, zf_ref, zt_ref, mg_ref, wo_ref,
                  of_ref, ot_ref, *, n_frame_tiles):
    i = pl.program_id(0)

    def tile(x_ref, att_ref, hm_ref, og_ref, o_ref):
        parts = []
        for h in range(H_M):
            sl = slice(h * DV_M, (h + 1) * DV_M)
            parts.append(_rms(hm_ref[:, sl], mg_ref[:, sl]) * og_ref[:, sl])
        hmn = jnp.concatenate(parts, axis=1).astype(BF16)
        o_ref[...] = (x_ref[...] + _dot(att_ref[...].astype(BF16), wo_ref[0:ATT_V, :])
                      + _dot(hmn, wo_ref[ATT_V:, :]))

    @pl.when(i < n_frame_tiles)
    def _():
        tile(xf_ref, attf_ref, hmf_ref, zf_ref, of_ref)

    @pl.when(i == n_frame_tiles)
    def _():
        tile(xt_ref, attt_ref, hmt_ref, zt_ref, ot_ref)


def _merge(x_f, x_t, att_f, att_t, hm_f, hm_t, z_f, z_t, mg, w_out, layer):
    NF, D = x_f.shape
    NT = x_t.shape[0]
    bm = MERGE_ROWS
    nft = NF // bm
    assert NF % bm == 0
    ogcol = (N_MAIN - N_ATT - M_V) // M_V
    fr = lambda i: (jnp.minimum(i, nft - 1), 0)
    tl = lambda i: (0, 0)
    once = dict(pipeline_mode=pl.Buffered(1))
    return pl.pallas_call(
        functools.partial(_merge_kernel, n_frame_tiles=nft),
        grid=(nft + 1,),
        in_specs=[
            pl.BlockSpec((bm, D), fr), pl.BlockSpec((NT, D), tl, **once),
            pl.BlockSpec((bm, ATT_V), fr), pl.BlockSpec((NT, ATT_V), tl, **once),
            pl.BlockSpec((bm, M_V), fr), pl.BlockSpec((NT, M_V), tl, **once),
            pl.BlockSpec((bm, M_V), lambda i: (jnp.minimum(i, nft - 1), ogcol)),
            pl.BlockSpec((NT, M_V), lambda i: (0, ogcol), **once),
            pl.BlockSpec((1, M_V), tl),
            pl.BlockSpec((None, D, D), lambda i: (layer, 0, 0), **once),
        ],
        out_specs=(pl.BlockSpec((bm, D), fr), pl.BlockSpec((NT, D), tl)),
        out_shape=(jax.ShapeDtypeStruct((NF, D), F32), jax.ShapeDtypeStruct((NT, D), F32)),
        compiler_params=pltpu.CompilerParams(
            dimension_semantics=("arbitrary",), vmem_limit_bytes=VMEM_LIMIT),
        name="merge_out_proj",
    )(x_f, x_t, att_f, att_t, hm_f, hm_t, z_f, z_t, mg, w_out)


def _ffn_kernel(xf_ref, xt_ref, g2_ref, wu_ref, wd_ref, of_ref, ot_ref, xn_s, *, n_frame_tiles):
    i = pl.program_id(0)
    k = pl.program_id(1)

    def tile(x_ref, o_ref):
        nrows = x_ref.shape[0]

        @pl.when(k == 0)
        def _():
            x = x_ref[...]
            xn_s[0:nrows, :] = _rms(x, g2_ref[...]).astype(BF16)
            o_ref[...] = x

        u = jnp.maximum(_dot(xn_s[0:nrows, :], wu_ref[...]), 0.0)
        o_ref[...] += _dot((u * u).astype(BF16), wd_ref[...])

    @pl.when(i < n_frame_tiles)
    def _():
        tile(xf_ref, of_ref)

    @pl.when(i == n_frame_tiles)
    def _():
        tile(xt_ref, ot_ref)


def _ffn(x_f, x_t, g2, w_up, w_down, layer):
    NF, D = x_f.shape
    NT = x_t.shape[0]
    FF = w_up.shape[2]
    bm, fc = FF_ROWS, FF_COLS
    nft = NF // bm
    assert NF % bm == 0 and FF % fc == 0 and NT <= bm
    fr = lambda i, k: (jnp.minimum(i, nft - 1), 0)
    tl = lambda i, k: (0, 0)
    return pl.pallas_call(
        functools.partial(_ffn_kernel, n_frame_tiles=nft),
        grid=(nft + 1, FF // fc),
        in_specs=[
            pl.BlockSpec((bm, D), fr),
            pl.BlockSpec((NT, D), tl),
            pl.BlockSpec((1, D), tl),
            pl.BlockSpec((None, D, fc), lambda i, k: (layer, 0, k)),
            pl.BlockSpec((None, fc, D), lambda i, k: (layer, k, 0)),
        ],
        out_specs=(pl.BlockSpec((bm, D), fr), pl.BlockSpec((NT, D), tl)),
        out_shape=(jax.ShapeDtypeStruct((NF, D), F32), jax.ShapeDtypeStruct((NT, D), F32)),
        scratch_shapes=[pltpu.VMEM((bm, D), BF16)],
        compiler_params=pltpu.CompilerParams(
            dimension_semantics=("arbitrary", "arbitrary"), vmem_limit_bytes=VMEM_LIMIT),
        name="ffn",
    )(x_f, x_t, g2, w_up, w_down)


def kernel(x_prompt, x_sample, cache_k, cache_v, state_C, state_n, state_m, meta_tokens, rel_bias, norm1_g, w_in, b_i, b_f, q_norm_g, k_norm_g, lambda_q1, lambda_k1, lambda_q2, lambda_k2, subln_g, mnorm_g, w_out, norm2_g, w_up, w_down):
    B, S, D = x_prompt.shape
    Bd, Td, _ = x_sample.shape
    depth = w_in.shape[0]
    Pc = cache_k.shape[2]
    assert Td == N_META and S % ATT_TILE == 0 and S % M_TILE == 0 and Pc % LANES == 0
    assert ATT_TILE % CHUNK == 0 and CHUNK & (CHUNK - 1) == 0 and ATT_TILE >= MAX_DISTANCE

    n_frames = B * S
    n_meta = B * N_META
    n_samp = Bd * Td
    assert n_meta + n_samp <= TAIL_ROWS
    srow = n_meta // N_META

    meta = jnp.broadcast_to(meta_tokens.astype(F32)[None], (B, N_META, D)).reshape(n_meta, D)
    x_f = x_prompt.astype(F32).reshape(n_frames, D)
    x_t = jnp.concatenate([meta, x_sample.astype(F32).reshape(n_samp, D),
                           jnp.zeros((TAIL_ROWS - n_meta - n_samp, D), F32)], axis=0)

    bias = _bias_tiles(rel_bias.astype(F32), Pc, Td)
    ck = cache_k.astype(F32)
    cv = cache_v.astype(F32)
    zeros_pad_att = jnp.zeros((TAIL_ROWS - n_meta - n_samp, ATT_V), F32)

    g_off = 2 * ATT_QK + ATT_V + 2 * M_QK + M_V
    w_in_t = jnp.swapaxes(w_in.astype(F32), 1, 2)
    w_out_b, w_up_b, w_down_b = (w.astype(BF16) for w in (w_out, w_up, w_down))
    outs = {name: [] for name in ("Cp", "np", "mp", "Cs", "ns", "ms")}
    caches = None

    for l in range(depth):
        lam_init = 0.8 - 0.6 * math.exp(-0.3 * l)
        gate_bias = jnp.pad(jnp.concatenate([b_i[l], b_f[l]]).astype(F32), (0, LANES - 2 * H_M))[None]
        lam_vecs = [v[l].astype(F32)[None] for v in (lambda_q1, lambda_k1, lambda_q2, lambda_k2)]
        sg = subln_g[l].astype(F32)[None]

        za_f, za_t, zm_f, zm_t, g_f, g_t, *caches = _inproj(
            x_f, x_t, norm1_g[l].astype(F32)[None], w_in_t, l, gate_bias,
            q_norm_g[l].astype(F32)[None], k_norm_g[l].astype(F32)[None], B, Bd, caches)

        att_f = _attn_prompt(za_f, za_t, bias, lam_vecs, sg, B, S, lam_init)
        att_m = _attn_small(za_t, 0, B, bias[3], lam_vecs, sg, lam_init)
        att_s = _attn_small(za_t, srow, Bd, bias[4], lam_vecs, sg, lam_init, cache=(ck, cv, l))
        att_t = jnp.concatenate([att_m, att_s, zeros_pad_att], axis=0)

        hm_m, hm_f, C_p, n_p, m_p = _mlstm(zm_t, g_t, 0, B, frames=(zm_f, g_f, S))
        init = (state_C[l].astype(F32), state_n[l].astype(F32)[:, :, None, :],
                jnp.broadcast_to(state_m[l].astype(F32)[:, :, None, None], (Bd, H_M, 8, LANES)))
        hm_s, C_s, n_s, m_s = _mlstm(zm_t, g_t, srow, Bd, init=init)
        hm_t = jnp.concatenate([hm_m, hm_s, zeros_pad_att], axis=0)

        x_f, x_t = _merge(x_f, x_t, att_f, att_t, hm_f, hm_t, zm_f, zm_t,
                          mnorm_g[l].astype(F32)[None], w_out_b, l)
        x_f, x_t = _ffn(x_f, x_t, norm2_g[l].astype(F32)[None], w_up_b, w_down_b, l)

        outs["Cp"].append(C_p)
        outs["np"].append(n_p[:, :, 0, :])
        outs["mp"].append(m_p[:, :, 0, 0])
        outs["Cs"].append(C_s)
        outs["ns"].append(n_s[:, :, 0, :])
        outs["ms"].append(m_s[:, :, 0, 0])

    y_prompt = x_f.reshape(B, S, D)
    y_sample = x_t[n_meta:n_meta + n_samp].reshape(Bd, Td, D)
    k_p, v_p, k_s, v_s = caches
    return (y_prompt, y_sample,
            k_p, v_p, jnp.stack(outs["Cp"]), jnp.stack(outs["np"]), jnp.stack(outs["mp"]),
            k_s.reshape(depth, Bd, Td, H_ATT, 2 * DQK_ATT), v_s.reshape(depth, Bd, Td, H_ATT, DV_ATT),
            jnp.stack(outs["Cs"]), jnp.stack(outs["ns"]), jnp.stack(outs["ms"]))
```

```python
import functools
import math

import jax
import jax.numpy as jnp
from jax import lax
from jax.experimental import pallas as pl
from jax.experimental.pallas import tpu as pltpu

F32 = jnp.float32
BF16 = jnp.bfloat16

CHUNK = 64
N_META = 16
H_ATT = 4
DQK_ATT = 128
DV_ATT = 256
H_M = 4
DQK_M = 128
DV_M = 256
N_BUCKETS = 32
MAX_DISTANCE = 128
EPS = 1e-6
NEG = -1e30
LOG2E = math.log2(math.e)

ATT_QK = H_ATT * 2 * DQK_ATT
ATT_V = H_ATT * DV_ATT
M_QK = H_M * DQK_M
M_V = H_M * DV_M
N_MAIN = 2 * ATT_QK + ATT_V + 2 * M_QK + M_V + M_V
N_ATT = 2 * ATT_QK + ATT_V

LANES = 128
ATT_TILE = 256
M_TILE = 256
TAIL_ROWS = 256
PROJ_ROWS = 1024
PROJ_COLS = 512
MERGE_ROWS = 512
CACHE_CHUNK = 1024
FF_ROWS = 512
FF_COLS = 1024
ZERO_FILL_ROWS = 256
V7X_VMEM_BYTES = 64 * 1024 * 1024
VMEM_LIMIT = V7X_VMEM_BYTES * 7 // 8

_NT = (((1,), (1,)), ((), ()))


def _dot(a, b):
    return jnp.dot(a, b, preferred_element_type=F32)


def _dot_nt(a, b):
    return lax.dot_general(a, b, _NT, preferred_element_type=F32)


def _log_sigmoid(x):
    return jnp.minimum(x, 0.0) - jnp.log1p(jnp.exp(-jnp.abs(x)))


def _col(a, idx):
    lane = lax.broadcasted_iota(jnp.int32, a.shape, 1)
    return jnp.sum(jnp.where(lane == idx, a, 0.0), axis=1, keepdims=True)


def _rms(x, gain):
    ms = jnp.mean(x * x, axis=-1, keepdims=True)
    return x * lax.rsqrt(ms + EPS) * gain


def _bias_from_rel(rel, rb_ref, h):
    nb = N_BUCKETS // 2
    max_exact = nb // 2
    ret = jnp.where(rel > 0, nb, 0)
    n = jnp.abs(rel)
    nf = jnp.maximum(n, 1).astype(F32)
    large = max_exact + (jnp.log(nf / max_exact) / math.log(MAX_DISTANCE / max_exact)
                         * (nb - max_exact)).astype(jnp.int32)
    large = jnp.minimum(large, nb - 1)
    bucket = ret + jnp.where(n < max_exact, n, large)
    out = jnp.zeros(rel.shape, F32)
    for b in range(N_BUCKETS):
        out = jnp.where(bucket == b, rb_ref[b, h], out)
    return out


def _bias_kernel(rb_ref, diag_ref, prev_ref, meta_ref, mm_ref, samp_ref, *, past_len, dec_seq):
    h = pl.program_id(0)
    shift = CHUNK.bit_length() - 1
    T = ATT_TILE
    far = _bias_from_rel(jnp.full((8, LANES), -(2 * MAX_DISTANCE + 1), jnp.int32), rb_ref, h)[0:1, 0:1]

    qi = lax.broadcasted_iota(jnp.int32, (T, T), 0)
    kj = lax.broadcasted_iota(jnp.int32, (T, T), 1)
    bias = (_bias_from_rel(kj - qi, rb_ref, h) - far) * LOG2E
    diag_ref[...] = jnp.where((kj >> shift) <= (qi >> shift), bias, NEG)
    prev_ref[...] = (_bias_from_rel(kj - T - qi, rb_ref, h) - far) * LOG2E

    qi = lax.broadcasted_iota(jnp.int32, (T, LANES), 0)
    kj = lax.broadcasted_iota(jnp.int32, (T, LANES), 1)
    for var in range(2):
        bias = (_bias_from_rel(kj - (N_META + qi + var * T), rb_ref, h) - far) * LOG2E
        meta_ref[var] = jnp.where(kj < N_META, bias, NEG)

    qi = lax.broadcasted_iota(jnp.int32, (N_META, LANES), 0)
    kj = lax.broadcasted_iota(jnp.int32, (N_META, LANES), 1)
    mm_ref[...] = jnp.where(kj < N_META, _bias_from_rel(kj - qi, rb_ref, h) * LOG2E, NEG)

    width = samp_ref.shape[-1]
    qi = lax.broadcasted_iota(jnp.int32, (dec_seq, width), 0) + past_len
    kj = lax.broadcasted_iota(jnp.int32, (dec_seq, width), 1)
    bias = _bias_from_rel(kj - qi, rb_ref, h) * LOG2E
    ok = ((kj >> shift) <= (qi >> shift)) & (kj < past_len + dec_seq)
    samp_ref[...] = jnp.where(ok, bias, NEG)


def _bias_tiles(rel_bias, past_len, dec_seq):
    T = ATT_TILE
    sw = past_len + LANES
    out_shape = (
        jax.ShapeDtypeStruct((H_ATT, T, T), F32),
        jax.ShapeDtypeStruct((H_ATT, T, T), F32),
        jax.ShapeDtypeStruct((H_ATT, 2, T, LANES), F32),
        jax.ShapeDtypeStruct((H_ATT, N_META, LANES), F32),
        jax.ShapeDtypeStruct((H_ATT, dec_seq, sw), F32),
    )
    out_specs = (
        pl.BlockSpec((None, T, T), lambda h: (h, 0, 0)),
        pl.BlockSpec((None, T, T), lambda h: (h, 0, 0)),
        pl.BlockSpec((None, 2, T, LANES), lambda h: (h, 0, 0, 0)),
        pl.BlockSpec((None, N_META, LANES), lambda h: (h, 0, 0)),
        pl.BlockSpec((None, dec_seq, sw), lambda h: (h, 0, 0)),
    )
    return pl.pallas_call(
        functools.partial(_bias_kernel, past_len=past_len, dec_seq=dec_seq),
        grid=(H_ATT,),
        in_specs=[pl.BlockSpec(memory_space=pltpu.SMEM)],
        out_specs=out_specs,
        out_shape=out_shape,
        name="rel_bias_tiles",
    )(rel_bias)


def _zero_fill_copies(zbuf, zsem, cache_refs, depth):
    rows = zbuf.shape[0]
    copies = []
    for dst in cache_refs:
        lead = dst.shape[1:-3]
        n_rows = dst.shape[-3]
        for l in range(1, depth):
            for b in range(lead[0] if lead else 1):
                view = dst.at[l, b] if lead else dst.at[l]
                for r0 in range(0, n_rows, rows):
                    n = min(rows, n_rows - r0)
                    copies.append(pltpu.make_async_copy(zbuf.at[pl.ds(0, n)], view.at[pl.ds(r0, n)],
                                                        zsem.at[len(copies)]))
    return copies


def _kv_copies(stage, sem, dst_prompt, dst_sample, i, *, layer, tail, n_batch, frames_per_batch, n_sample_rows):
    dst_prompt = dst_prompt.at[layer]
    dst_sample = dst_sample.at[layer]
    if not tail:
        rows = stage.shape[0]
        tiles_per_batch = frames_per_batch // rows
        b = i // tiles_per_batch
        t = i - b * tiles_per_batch
        return [pltpu.make_async_copy(stage, dst_prompt.at[b, pl.ds(N_META + t * rows, rows)], sem.at[0])]
    copies = [pltpu.make_async_copy(stage.at[pl.ds(b * N_META, N_META)], dst_prompt.at[b, pl.ds(0, N_META)],
                                    sem.at[b]) for b in range(n_batch)]
    copies.append(pltpu.make_async_copy(stage.at[pl.ds(n_batch * N_META, n_sample_rows)], dst_sample,
                                        sem.at[n_batch]))
    return copies


def _inproj_kernel(*refs, layer, depth, n_frame_tiles, n_batch, frames_per_batch, n_sample_rows):
    xf_ref, xt_ref, g1_ref, wa_ref, wg_ref, gb_ref, qg_ref, kg_ref = refs[:8]
    (zaf_ref, zat_ref, zmf_ref, zmt_ref, gf_ref, gt_ref, kp_ref, vp_ref, ks_ref, vs_ref,
     xn_s, y2d, stage, sem, zbuf, zsem) = refs[-16:]
    i = pl.program_id(0)
    j = pl.program_id(1)
    bn = wa_ref.shape[1]

    if layer == 0 and depth > 1:
        zero_copies = _zero_fill_copies(zbuf, zsem, (kp_ref, vp_ref, ks_ref, vs_ref), depth)

        @pl.when((i == 0) & (j == 0))
        def _():
            zbuf[...] = jnp.zeros(zbuf.shape, F32)
            for c in zero_copies:
                c.start()

        @pl.when((i == pl.num_programs(0) - 1) & (j == pl.num_programs(1) - 1))
        def _():
            for c in zero_copies:
                c.wait()

    assert ATT_QK % bn == 0 and ATT_V % bn == 0 and M_QK % bn == 0 and M_V % bn == 0 and bn % DV_ATT == 0
    j_k = ATT_QK // bn
    j_v = j_k + ATT_QK // bn
    j_mq = j_v + ATT_V // bn
    j_mk = j_mq + M_QK // bn
    j_mv = j_mk + M_QK // bn
    j_og = j_mv + M_V // bn

    def tile(x_ref, za_ref, zm_ref, g_ref, tail):
        nrows = x_ref.shape[0]
        st = stage.at[pl.ds(0, nrows)]
        copies = functools.partial(_kv_copies, st, sem, i=i, layer=layer, tail=tail, n_batch=n_batch,
                                   frames_per_batch=frames_per_batch, n_sample_rows=n_sample_rows)

        @pl.when(j == 0)
        def _():
            xb = _rms(x_ref[...], g1_ref[...]).astype(BF16)
            xn_s[0:nrows, :] = xb
            gates = _dot_nt(xb, wg_ref[0].astype(BF16)) + gb_ref[...]
            lane = lax.broadcasted_iota(jnp.int32, gates.shape, 1)
            g_ref[...] = jnp.where(lane < H_M, gates, _log_sigmoid(gates))

        acc = _dot_nt(xn_s[0:nrows, :], wa_ref[0].astype(BF16))

        def group_norm(gain_ref, scale):
            parts = []
            for gi in range(bn // DQK_ATT):
                y = _rms(acc[:, gi * DQK_ATT:(gi + 1) * DQK_ATT], gain_ref[...])
                parts.append(y if scale is None else y * scale)
            return jnp.concatenate(parts, axis=1)

        def stage_heads(y, jj, j0, dst_prompt, dst_sample):
            c0 = (jj - j0) * bn
            y2d[0:nrows, c0:c0 + bn] = y
            if jj == j0 + ATT_QK // bn - 1:
                if j0 != j_k:
                    for c in copies(kp_ref, ks_ref):
                        c.wait()
                st[...] = y2d[0:nrows, :].reshape(nrows, H_ATT, DV_ATT)
                for c in copies(dst_prompt, dst_sample):
                    c.start()

        @pl.when(j < j_k)
        def _():
            za_ref[...] = group_norm(qg_ref, DQK_ATT ** -0.5 * LOG2E).astype(BF16)

        for jj in range(j_k, j_v):
            @pl.when(j == jj)
            def _(jj=jj):
                y = group_norm(kg_ref, None)
                za_ref[...] = y.astype(BF16)
                stage_heads(y, jj, j_k, kp_ref, ks_ref)

        for jj in range(j_v, j_mq):
            @pl.when(j == jj)
            def _(jj=jj):
                za_ref[...] = acc.astype(BF16)
                stage_heads(acc, jj, j_v, vp_ref, vs_ref)

        @pl.when(j == j_mq)
        def _():
            for c in copies(vp_ref, vs_ref):
                c.wait()

        @pl.when(((j >= j_mq) & (j < j_mk)) | ((j >= j_mv) & (j < j_og)))
        def _():
            zm_ref[...] = acc

        @pl.when((j >= j_mk) & (j < j_mv))
        def _():
            zm_ref[...] = acc * (DQK_M ** -0.5)

        @pl.when(j >= j_og)
        def _():
            zm_ref[...] = jax.nn.sigmoid(acc)

    @pl.when(i < n_frame_tiles)
    def _():
        tile(xf_ref, zaf_ref, zmf_ref, gf_ref, False)

    @pl.when(i == n_frame_tiles)
    def _():
        tile(xt_ref, zat_ref, zmt_ref, gt_ref, True)


def _inproj(x_f, x_t, g1, w_all, layer, gate_bias, qg, kg, n_batch, n_sample_streams, caches):
    depth = w_all.shape[0]
    NF, D = x_f.shape
    NT = x_t.shape[0]
    bm, bn = PROJ_ROWS, PROJ_COLS
    nft = NF // bm
    nj = N_MAIN // bn
    nja = N_ATT // bn
    g_row = N_MAIN - M_V
    assert g_row % bn == 0 and g_row % 8 == 0 and w_all.shape[1] >= g_row + LANES

    def w_window(i, j):
        row = j * bn + jnp.where(j * bn >= g_row, 2 * H_M, 0)
        return (layer, pl.multiple_of(row, 8), 0)

    S = NF // n_batch
    L = N_META + S
    n_sample_rows = n_sample_streams * N_META
    assert NF % bm == 0 and S % bm == 0 and N_MAIN % bn == 0 and NT <= bm
    fr = lambda i: jnp.minimum(i, nft - 1)
    any_spec = pl.BlockSpec(memory_space=pl.ANY)
    assert (caches is None) == (layer == 0)
    prev = () if caches is None else tuple(caches)
    n_in = 8
    zero_rows = ZERO_FILL_ROWS
    n_zero = (depth - 1) * (2 * n_batch * pl.cdiv(L, zero_rows) + 2 * pl.cdiv(n_sample_rows, zero_rows))
    return pl.pallas_call(
        functools.partial(_inproj_kernel, layer=layer, depth=depth, n_frame_tiles=nft, n_batch=n_batch,
                          frames_per_batch=S, n_sample_rows=n_sample_rows),
        grid=(nft + 1, nj),
        input_output_aliases={n_in + c: 6 + c for c in range(len(prev))},
        in_specs=[
            pl.BlockSpec((bm, D), lambda i, j: (fr(i), 0)),
            pl.BlockSpec((NT, D), lambda i, j: (0, 0), pipeline_mode=pl.Buffered(1)),
            pl.BlockSpec((1, D), lambda i, j: (0, 0)),
            pl.BlockSpec((pl.Element(1), pl.Element(bn), pl.Element(D)), w_window),
            pl.BlockSpec((pl.Element(1), pl.Element(LANES), pl.Element(D)), lambda i, j: (layer, g_row, 0)),
            pl.BlockSpec((1, LANES), lambda i, j: (0, 0)),
            pl.BlockSpec((1, DQK_ATT), lambda i, j: (0, 0)),
            pl.BlockSpec((1, DQK_ATT), lambda i, j: (0, 0)),
        ] + [any_spec] * len(prev),
        out_specs=(
            pl.BlockSpec((bm, bn), lambda i, j: (fr(i), jnp.where(i < nft, jnp.minimum(j, nja - 1), nja - 1))),
            pl.BlockSpec((NT, bn), lambda i, j: (0, jnp.where(i < nft, 0, jnp.minimum(j, nja - 1)))),
            pl.BlockSpec((bm, bn), lambda i, j: (fr(i), jnp.where(i < nft, jnp.maximum(j - nja, 0), nj - nja - 1))),
            pl.BlockSpec((NT, bn), lambda i, j: (0, jnp.where(i < nft, 0, jnp.maximum(j - nja, 0)))),
            pl.BlockSpec((bm, LANES), lambda i, j: (fr(i), 0)),
            pl.BlockSpec((NT, LANES), lambda i, j: (0, 0)),
            any_spec, any_spec, any_spec, any_spec,
        ),
        out_shape=(
            jax.ShapeDtypeStruct((NF, N_ATT), BF16),
            jax.ShapeDtypeStruct((NT, N_ATT), BF16),
            jax.ShapeDtypeStruct((NF, N_MAIN - N_ATT), F32),
            jax.ShapeDtypeStruct((NT, N_MAIN - N_ATT), F32),
            jax.ShapeDtypeStruct((NF, LANES), F32),
            jax.ShapeDtypeStruct((NT, LANES), F32),
            jax.ShapeDtypeStruct((depth, n_batch, L, H_ATT, 2 * DQK_ATT), F32),
            jax.ShapeDtypeStruct((depth, n_batch, L, H_ATT, DV_ATT), F32),
            jax.ShapeDtypeStruct((depth, n_sample_rows, H_ATT, 2 * DQK_ATT), F32),
            jax.ShapeDtypeStruct((depth, n_sample_rows, H_ATT, DV_ATT), F32),
        ),
        scratch_shapes=[
            pltpu.VMEM((bm, D), BF16),
            pltpu.VMEM((bm, ATT_QK), F32),
            pltpu.VMEM((bm, H_ATT, DV_ATT), F32),
            pltpu.SemaphoreType.DMA((n_batch + 1,)),
            pltpu.VMEM((zero_rows, H_ATT, DV_ATT), F32),
            pltpu.SemaphoreType.DMA((max(n_zero, 1),)),
        ],
        compiler_params=pltpu.CompilerParams(
            dimension_semantics=("arbitrary", "arbitrary"), vmem_limit_bytes=VMEM_LIMIT),
        name="in_proj",
    )(x_f, x_t, g1, w_all, w_all, gate_bias, qg, kg, *prev)


def _diff_lambda(lq1_ref, lk1_ref, lq2_ref, lk2_ref, lam_init):
    a = jnp.sum(lq1_ref[...] * lk1_ref[...], axis=-1, keepdims=True)
    b = jnp.sum(lq2_ref[...] * lk2_ref[...], axis=-1, keepdims=True)
    return jnp.exp(a) - jnp.exp(b) + lam_init


def _subln(o, sg_ref, lam_init):
    return _rms(o, sg_ref[...]) * (1.0 - lam_init)


def _pad_rows(dst, src):
    dst[...] = jnp.zeros(dst.shape, dst.dtype)
    dst[0:src.shape[0], :] = src[...].astype(dst.dtype)


def _attn_prompt_kernel(q_ref, k_ref, v_ref, km_ref, vm_ref, bd_ref, bp_ref, bm_ref,
                        lq1_ref, lk1_ref, lq2_ref, lk2_ref, sg_ref, o_ref,
                        kmb_s, vmb_s, *, lam_init):
    T = ATT_TILE
    n_tiles = q_ref.shape[0] // T
    kb_s, vb_s = k_ref, v_ref
    _pad_rows(kmb_s, km_ref)
    _pad_rows(vmb_s, vm_ref)
    lam = _diff_lambda(lq1_ref, lk1_ref, lq2_ref, lk2_ref, lam_init)

    for j in range(n_tiles):
        rows = slice(j * T, (j + 1) * T)
        regions = [(kmb_s, vmb_s, slice(0, LANES), bm_ref[min(j, 1)])]
        if j >= 2:
            regions.append((kb_s, vb_s, slice(0, (j - 1) * T), None))
        if j >= 1:
            regions.append((kb_s, vb_s, slice((j - 1) * T, j * T), bp_ref[...]))
        regions.append((kb_s, vb_s, rows, bd_ref[...]))

        outs = []
        for c in range(2):
            cols = slice(c * DQK_ATT, (c + 1) * DQK_ATT)
            qc = q_ref[rows, cols]
            ss = [_dot_nt(qc, kr[rs, cols]) for kr, _, rs, _ in regions]
            ss = [s if br is None else s + br for s, (_, _, _, br) in zip(ss, regions)]
            m = functools.reduce(jnp.maximum, [jnp.max(s, axis=-1, keepdims=True) for s in ss])
            pc = [jnp.exp2(s - m) for s in ss]
            denom = functools.reduce(jnp.add, [jnp.sum(p, axis=-1, keepdims=True) for p in pc])
            pv = functools.reduce(jnp.add, [_dot(p.astype(BF16), vr[rs, :])
                                            for p, (_, vr, rs, _) in zip(pc, regions)])
            outs.append(pv / denom)
        o_ref[rows, :] = _subln(outs[0] - lam * outs[1], sg_ref, lam_init)


def _attn_prompt(z_f, z_t, bias, lam_vecs, sg, B, S, lam_init):
    T = ATT_TILE
    qkw = 2 * DQK_ATT
    kcol = ATT_QK // qkw
    vcol = 2 * ATT_QK // DV_ATT
    b_diag, b_prev, b_meta, _, _ = bias
    vec = pl.BlockSpec((1, DQK_ATT), lambda b, h: (0, 0))
    return pl.pallas_call(
        functools.partial(_attn_prompt_kernel, lam_init=lam_init),
        grid=(B, H_ATT),
        in_specs=[
            pl.BlockSpec((S, qkw), lambda b, h: (b, h)),
            pl.BlockSpec((S, qkw), lambda b, h: (b, kcol + h)),
            pl.BlockSpec((S, DV_ATT), lambda b, h: (b, vcol + h)),
            pl.BlockSpec((N_META, qkw), lambda b, h: (b, kcol + h)),
            pl.BlockSpec((N_META, DV_ATT), lambda b, h: (b, vcol + h)),
            pl.BlockSpec((None, T, T), lambda b, h: (h, 0, 0)),
            pl.BlockSpec((None, T, T), lambda b, h: (h, 0, 0)),
            pl.BlockSpec((None, 2, T, LANES), lambda b, h: (h, 0, 0, 0)),
            vec, vec, vec, vec,
            pl.BlockSpec((1, DV_ATT), lambda b, h: (0, 0)),
        ],
        out_specs=pl.BlockSpec((S, DV_ATT), lambda b, h: (b, h)),
        out_shape=jax.ShapeDtypeStruct((B * S, ATT_V), F32),
        scratch_shapes=[
            pltpu.VMEM((LANES, qkw), BF16),
            pltpu.VMEM((LANES, DV_ATT), BF16),
        ],
        compiler_params=pltpu.CompilerParams(
            dimension_semantics=("arbitrary", "arbitrary"), vmem_limit_bytes=VMEM_LIMIT),
        name="attn_prompt",
    )(z_f, z_f, z_f, z_t, z_t, b_diag, b_prev, b_meta, *lam_vecs, sg)


def _attn_small_kernel(*refs, lam_init, has_cache):
    if has_cache:
        (q_ref, kc_ref, vc_ref, kn_ref, vn_ref, b_ref,
         lq1_ref, lk1_ref, lq2_ref, lk2_ref, sg_ref, o_ref, kpad, vpad, kb_s, vb_s) = refs
        past_len = kb_s.shape[0]
        chunk = kc_ref.shape[0]
        c = pl.program_id(1)
        r0 = pl.multiple_of(c * chunk, chunk)
        kb_s[pl.ds(r0, chunk), :] = kc_ref[...].reshape(chunk, ATT_QK).astype(BF16)
        vb_s[pl.ds(r0, chunk), :] = vc_ref[...].reshape(chunk, ATT_V).astype(BF16)

        @pl.when(c == past_len // chunk - 1)
        def _():
            _attn_small_body(q_ref, kn_ref, vn_ref, b_ref, lq1_ref, lk1_ref, lq2_ref, lk2_ref, sg_ref, o_ref,
                             kpad, vpad, kb_s, vb_s, lam_init=lam_init, past_len=past_len)
    else:
        (q_ref, kn_ref, vn_ref, b_ref,
         lq1_ref, lk1_ref, lq2_ref, lk2_ref, sg_ref, o_ref, kpad, vpad) = refs
        _attn_small_body(q_ref, kn_ref, vn_ref, b_ref, lq1_ref, lk1_ref, lq2_ref, lk2_ref, sg_ref, o_ref,
                         kpad, vpad, None, None, lam_init=lam_init, past_len=0)


def _attn_small_body(q_ref, kn_ref, vn_ref, b_ref, lq1_ref, lk1_ref, lq2_ref, lk2_ref, sg_ref, o_ref,
                     kpad, vpad, kb_s, vb_s, *, lam_init, past_len):
    has_cache = past_len > 0
    lam = _diff_lambda(lq1_ref, lk1_ref, lq2_ref, lk2_ref, lam_init)
    _pad_rows(kpad, kn_ref)
    _pad_rows(vpad, vn_ref)
    for h in range(H_ATT):
        vsl = slice(h * DV_ATT, (h + 1) * DV_ATT)
        outs = []
        for c in range(2):
            sl = slice(h * 2 * DQK_ATT + c * DQK_ATT, h * 2 * DQK_ATT + (c + 1) * DQK_ATT)
            qc = q_ref[:, sl]
            s_new = _dot_nt(qc, kpad[:, sl]) + b_ref[h, :, past_len:]
            m = jnp.max(s_new, axis=-1, keepdims=True)
            if has_cache:
                s_old = _dot_nt(qc, kb_s[:, sl]) + b_ref[h, :, :past_len]
                m = jnp.maximum(m, jnp.max(s_old, axis=-1, keepdims=True))
            p_new = jnp.exp2(s_new - m)
            l = jnp.sum(p_new, axis=-1, keepdims=True)
            o = _dot(p_new.astype(BF16), vpad[:, vsl])
            if has_cache:
                p_old = jnp.exp2(s_old - m)
                l = l + jnp.sum(p_old, axis=-1, keepdims=True)
                o = o + _dot(p_old.astype(BF16), vb_s[:, vsl])
            outs.append(o / l)
        o_ref[:, vsl] = _subln(outs[0] - lam * outs[1], sg_ref, lam_init)


def _attn_small(z_t, row0, n_streams, bias, lam_vecs, sg, lam_init, cache=None):
    vec = pl.BlockSpec((1, DQK_ATT), lambda b, c: (0, 0))
    in_specs = [pl.BlockSpec((N_META, ATT_QK), lambda b, c: (row0 + b, 0))]
    args = [z_t]
    scratch = [pltpu.VMEM((LANES, ATT_QK), BF16), pltpu.VMEM((LANES, ATT_V), BF16)]
    n_chunks = 1
    if cache is not None:
        ck, cv, layer = cache
        past_len = ck.shape[2]
        chunk = min(past_len, CACHE_CHUNK)
        assert past_len % chunk == 0
        n_chunks = past_len // chunk
        in_specs += [
            pl.BlockSpec((None, None, chunk, H_ATT, 2 * DQK_ATT), lambda b, c: (layer, b, c, 0, 0)),
            pl.BlockSpec((None, None, chunk, H_ATT, DV_ATT), lambda b, c: (layer, b, c, 0, 0))]
        args += [ck, cv]
        scratch += [pltpu.VMEM((past_len, ATT_QK), BF16), pltpu.VMEM((past_len, ATT_V), BF16)]
    in_specs += [
        pl.BlockSpec((N_META, ATT_QK), lambda b, c: (row0 + b, 1)),
        pl.BlockSpec((N_META, ATT_V), lambda b, c: (row0 + b, 2 * ATT_QK // ATT_V)),
        pl.BlockSpec(bias.shape, lambda b, c: (0, 0, 0)),
        vec, vec, vec, vec,
        pl.BlockSpec((1, DV_ATT), lambda b, c: (0, 0)),
    ]
    args += [z_t, z_t, bias, *lam_vecs, sg]
    return pl.pallas_call(
        functools.partial(_attn_small_kernel, lam_init=lam_init, has_cache=cache is not None),
        grid=(n_streams, n_chunks),
        in_specs=in_specs,
        out_specs=pl.BlockSpec((N_META, ATT_V), lambda b, c: (b, 0)),
        out_shape=jax.ShapeDtypeStruct((n_streams * N_META, ATT_V), F32),
        scratch_shapes=scratch,
        compiler_params=pltpu.CompilerParams(
            dimension_semantics=("arbitrary", "arbitrary"), vmem_limit_bytes=VMEM_LIMIT),
        name="attn_cached" if cache is not None else "attn_meta",
    )(*args)


def _mlstm_kernel(*refs, n_blocks, has_init):
    refs = list(refs)
    qh_ref, kh_ref, vh_ref, gh_ref = refs[:4]
    pos = 4
    if n_blocks:
        qf_ref, kf_ref, vf_ref, gf_ref = refs[pos:pos + 4]
        pos += 4
    if has_init:
        c0_ref, n0_ref, m0_ref = refs[pos:pos + 3]
        pos += 3
    hh_ref = refs[pos]
    pos += 1
    if n_blocks:
        hf_ref = refs[pos]
        pos += 1
    c_ref, n_ref, m_ref = refs[pos:pos + 3]
    qs, ks, vs, gs = refs[pos + 3:pos + 7]

    t = pl.program_id(1)
    block = functools.partial(_mlstm_block, c_ref=c_ref, n_ref=n_ref, m_ref=m_ref)

    @pl.when(t == 0)
    def _():
        if has_init:
            c_ref[...] = c0_ref[...]
            n_ref[...] = n0_ref[...]
            m_ref[...] = m0_ref[...]
        else:
            c_ref[...] = jnp.zeros(c_ref.shape, F32)
            n_ref[...] = jnp.zeros(n_ref.shape, F32)
            m_ref[...] = jnp.zeros(m_ref.shape, F32)
        nh = qh_ref.shape[0]
        for dst, src in ((qs, qh_ref), (ks, kh_ref), (vs, vh_ref)):
            dst[...] = jnp.zeros(dst.shape, F32)
            dst[0:nh, :] = src[...]
        lane = lax.broadcasted_iota(jnp.int32, gs.shape, 1)
        gs[...] = jnp.where(lane < H_M, NEG, 0.0)
        gs[0:nh, :] = gh_ref[...]

        block(qs, ks, vs, gs, hh_ref, nh)

    if n_blocks:
        @pl.when(t > 0)
        def _():
            block(qf_ref, kf_ref, vf_ref, gf_ref, hf_ref, qf_ref.shape[0])


def _mlstm_block(q_src, k_src, v_src, g_src, h_dst, n_out, c_ref, n_ref, m_ref):
    T = q_src.shape[0]
    g = g_src[...]
    row = lax.broadcasted_iota(jnp.int32, (T, T), 0)
    colid = lax.broadcasted_iota(jnp.int32, (T, T), 1)
    causal = row >= colid
    tril = jnp.where(causal, 1.0, 0.0).astype(F32)
    cum = jnp.dot(tril, g, preferred_element_type=F32, precision=lax.Precision.HIGHEST)
    g_t = g.T
    cum_t = cum.T
    last = cum[T - 1:T, :]

    for h in range(H_M):
        q = q_src[:, h * DQK_M:(h + 1) * DQK_M]
        k = k_src[:, h * DQK_M:(h + 1) * DQK_M]
        v = v_src[:, h * DV_M:(h + 1) * DV_M]
        qb = q.astype(BF16)
        kb = k.astype(BF16)
        m_old = m_ref[h, 0:1, 0:1]
        c_old = c_ref[h]
        n_old = n_ref[h]
        bh_c = _col(cum, H_M + h)
        ih_c = _col(g, h)
        bh_r = cum_t[H_M + h:H_M + h + 1, :]
        ih_r = g_t[h:h + 1, :]
        d = jnp.where(causal, bh_c - bh_r + ih_r, NEG)
        inter = bh_c + m_old
        mt = jnp.maximum(inter, jnp.max(d, axis=-1, keepdims=True))
        a = jnp.exp(d - mt) * _dot_nt(qb, kb)
        iw = jnp.exp(inter - mt)
        num = _dot(a.astype(BF16), v.astype(BF16)) + iw * _dot_nt(qb, c_old.astype(BF16))
        den = jnp.sum(a, axis=-1, keepdims=True) + iw * jnp.sum(q * n_old, axis=-1, keepdims=True)
        den = jnp.maximum(jnp.abs(den), jnp.exp(-mt))
        hout = num / den
        h_dst[:, h * DV_M:(h + 1) * DV_M] = hout[0:n_out, :]

        bl = _col(last, H_M + h)
        g_c = bl - bh_c + ih_c
        m_new = jnp.maximum(bl + m_old, jnp.max(g_c, axis=0, keepdims=True))
        sw = jnp.exp(g_c - m_new)
        dec = jnp.exp(bl + m_old - m_new)
        c_ref[h] = dec * c_old + _dot(v.T.astype(BF16), (sw * k).astype(BF16))
        n_ref[h] = dec * n_old + jnp.sum(sw * k, axis=0, keepdims=True)
        m_ref[h] = jnp.broadcast_to(m_new, m_ref.shape[1:])


def _mlstm(z_t, g_t, row0, n_streams, frames=None, init=None):
    T = M_TILE
    qcol = 0
    vcol = 2 * M_QK // M_V
    nh = N_META
    in_specs = [
        pl.BlockSpec((nh, M_QK), lambda b, t: (row0 + b, qcol)),
        pl.BlockSpec((nh, M_QK), lambda b, t: (row0 + b, qcol + 1)),
        pl.BlockSpec((nh, M_V), lambda b, t: (row0 + b, vcol)),
        pl.BlockSpec((nh, LANES), lambda b, t: (row0 + b, 0)),
    ]
    args = [z_t, z_t, z_t, g_t]
    n_blocks = 0
    if frames is not None:
        z_f, g_f, fps = frames
        n_blocks = fps // T
        fidx = lambda b, t: b * n_blocks + jnp.maximum(t - 1, 0)
        in_specs += [
            pl.BlockSpec((T, M_QK), lambda b, t: (fidx(b, t), qcol)),
            pl.BlockSpec((T, M_QK), lambda b, t: (fidx(b, t), qcol + 1)),
            pl.BlockSpec((T, M_V), lambda b, t: (fidx(b, t), vcol)),
            pl.BlockSpec((T, LANES), lambda b, t: (fidx(b, t), 0)),
        ]
        args += [z_f, z_f, z_f, g_f]
    state_specs = (
        pl.BlockSpec((None, H_M, DV_M, DQK_M), lambda b, t: (b, 0, 0, 0)),
        pl.BlockSpec((None, H_M, 1, DQK_M), lambda b, t: (b, 0, 0, 0)),
        pl.BlockSpec((None, H_M, 8, LANES), lambda b, t: (b, 0, 0, 0)),
    )
    if init is not None:
        in_specs += list(state_specs)
        args += list(init)
    out_specs = [pl.BlockSpec((nh, M_V), lambda b, t: (b, 0))]
    out_shape = [jax.ShapeDtypeStruct((n_streams * nh, M_V), F32)]
    if n_blocks:
        out_specs.append(pl.BlockSpec((T, M_V), lambda b, t: (fidx(b, t), 0)))
        out_shape.append(jax.ShapeDtypeStruct((n_streams * fps, M_V), F32))
    out_specs += list(state_specs)
    out_shape += [
        jax.ShapeDtypeStruct((n_streams, H_M, DV_M, DQK_M), F32),
        jax.ShapeDtypeStruct((n_streams, H_M, 1, DQK_M), F32),
        jax.ShapeDtypeStruct((n_streams, H_M, 8, LANES), F32),
    ]
    return pl.pallas_call(
        functools.partial(_mlstm_kernel, n_blocks=n_blocks, has_init=init is not None),
        grid=(n_streams, 1 + n_blocks),
        in_specs=in_specs,
        out_specs=tuple(out_specs),
        out_shape=tuple(out_shape),
        scratch_shapes=[
            pltpu.VMEM((LANES, M_QK), F32), pltpu.VMEM((LANES, M_QK), F32),
            pltpu.VMEM((LANES, M_V), F32), pltpu.VMEM((LANES, LANES), F32),
        ],
        compiler_params=pltpu.CompilerParams(
            dimension_semantics=("arbitrary", "arbitrary"), vmem_limit_bytes=VMEM_LIMIT),
        name="mlstm_prompt" if n_blocks else "mlstm_cached",
    )(*args)


def _merge_kernel(xf_ref, xt_ref, attf_ref, attt_ref, hmf_ref, hmt_ref, zf_ref, zt_ref, mg_ref, wo_ref,
                  of_ref, ot_ref, *, n_frame_tiles):
    i = pl.program_id(0)

    def tile(x_ref, att_ref, hm_ref, og_ref, o_ref):
        parts = []
        for h in range(H_M):
            sl = slice(h * DV_M, (h + 1) * DV_M)
            parts.append(_rms(hm_ref[:, sl], mg_ref[:, sl]) * og_ref[:, sl])
        hmn = jnp.concatenate(parts, axis=1).astype(BF16)
        o_ref[...] = (x_ref[...] + _dot(att_ref[...].astype(BF16), wo_ref[0:ATT_V, :])
                      + _dot(hmn, wo_ref[ATT_V:, :]))

    @pl.when(i < n_frame_tiles)
    def _():
        tile(xf_ref, attf_ref, hmf_ref, zf_ref, of_ref)

    @pl.when(i == n_frame_tiles)
    def _():
        tile(xt_ref, attt_ref, hmt_ref, zt_ref, ot_ref)


def _merge(x_f, x_t, att_f, att_t, hm_f, hm_t, z_f, z_t, mg, w_out, layer):
    NF, D = x_f.shape
    NT = x_t.shape[0]
    bm = MERGE_ROWS
    nft = NF // bm
    assert NF % bm == 0
    ogcol = (N_MAIN - N_ATT - M_V) // M_V
    fr = lambda i: (jnp.minimum(i, nft - 1), 0)
    tl = lambda i: (0, 0)
    once = dict(pipeline_mode=pl.Buffered(1))
    return pl.pallas_call(
        functools.partial(_merge_kernel, n_frame_tiles=nft),
        grid=(nft + 1,),
        in_specs=[
            pl.BlockSpec((bm, D), fr), pl.BlockSpec((NT, D), tl, **once),
            pl.BlockSpec((bm, ATT_V), fr), pl.BlockSpec((NT, ATT_V), tl, **once),
            pl.BlockSpec((bm, M_V), fr), pl.BlockSpec((NT, M_V), tl, **once),
            pl.BlockSpec((bm, M_V), lambda i: (jnp.minimum(i, nft - 1), ogcol)),
            pl.BlockSpec((NT, M_V), lambda i: (0, ogcol), **once),
            pl.BlockSpec((1, M_V), tl),
            pl.BlockSpec((None, D, D), lambda i: (layer, 0, 0), **once),
        ],
        out_specs=(pl.BlockSpec((bm, D), fr), pl.BlockSpec((NT, D), tl)),
        out_shape=(jax.ShapeDtypeStruct((NF, D), F32), jax.ShapeDtypeStruct((NT, D), F32)),
        compiler_params=pltpu.CompilerParams(
            dimension_semantics=("arbitrary",), vmem_limit_bytes=VMEM_LIMIT),
        name="merge_out_proj",
    )(x_f, x_t, att_f, att_t, hm_f, hm_t, z_f, z_t, mg, w_out)


def _ffn_kernel(xf_ref, xt_ref, g2_ref, wu_ref, wd_ref, of_ref, ot_ref, xn_s, *, n_frame_tiles):
    i = pl.program_id(0)
    k = pl.program_id(1)

    def tile(x_ref, o_ref):
        nrows = x_ref.shape[0]

        @pl.when(k == 0)
        def _():
            x = x_ref[...]
            xn_s[0:nrows, :] = _rms(x, g2_ref[...]).astype(BF16)
            o_ref[...] = x

        u = jnp.maximum(_dot(xn_s[0:nrows, :], wu_ref[...]), 0.0)
        o_ref[...] += _dot((u * u).astype(BF16), wd_ref[...])

    @pl.when(i < n_frame_tiles)
    def _():
        tile(xf_ref, of_ref)

    @pl.when(i == n_frame_tiles)
    def _():
        tile(xt_ref, ot_ref)


def _ffn(x_f, x_t, g2, w_up, w_down, layer):
    NF, D = x_f.shape
    NT = x_t.shape[0]
    FF = w_up.shape[2]
    bm, fc = FF_ROWS, FF_COLS
    nft = NF // bm
    assert NF % bm == 0 and FF % fc == 0 and NT <= bm
    fr = lambda i, k: (jnp.minimum(i, nft - 1), 0)
    tl = lambda i, k: (0, 0)
    return pl.pallas_call(
        functools.partial(_ffn_kernel, n_frame_tiles=nft),
        grid=(nft + 1, FF // fc),
        in_specs=[
            pl.BlockSpec((bm, D), fr),
            pl.BlockSpec((NT, D), tl),
            pl.BlockSpec((1, D), tl),
            pl.BlockSpec((None, D, fc), lambda i, k: (layer, 0, k)),
            pl.BlockSpec((None, fc, D), lambda i, k: (layer, k, 0)),
        ],
        out_specs=(pl.BlockSpec((bm, D), fr), pl.BlockSpec((NT, D), tl)),
        out_shape=(jax.ShapeDtypeStruct((NF, D), F32), jax.ShapeDtypeStruct((NT, D), F32)),
        scratch_shapes=[pltpu.VMEM((bm, D), BF16)],
        compiler_params=pltpu.CompilerParams(
            dimension_semantics=("arbitrary", "arbitrary"), vmem_limit_bytes=VMEM_LIMIT),
        name="ffn",
    )(x_f, x_t, g2, w_up, w_down)


def kernel(x_prompt, x_sample, cache_k, cache_v, state_C, state_n, state_m, meta_tokens, rel_bias, norm1_g, w_in, b_i, b_f, q_norm_g, k_norm_g, lambda_q1, lambda_k1, lambda_q2, lambda_k2, subln_g, mnorm_g, w_out, norm2_g, w_up, w_down):
    B, S, D = x_prompt.shape
    Bd, Td, _ = x_sample.shape
    depth = w_in.shape[0]
    Pc = cache_k.shape[2]
    assert Td == N_META and S % ATT_TILE == 0 and S % M_TILE == 0 and Pc % LANES == 0
    assert ATT_TILE % CHUNK == 0 and CHUNK & (CHUNK - 1) == 0 and ATT_TILE >= MAX_DISTANCE

    n_frames = B * S
    n_meta = B * N_META
    n_samp = Bd * Td
    assert n_meta + n_samp <= TAIL_ROWS
    srow = n_meta // N_META

    meta = jnp.broadcast_to(meta_tokens.astype(F32)[None], (B, N_META, D)).reshape(n_meta, D)
    x_f = x_prompt.astype(F32).reshape(n_frames, D)
    x_t = jnp.concatenate([meta, x_sample.astype(F32).reshape(n_samp, D),
                           jnp.zeros((TAIL_ROWS - n_meta - n_samp, D), F32)], axis=0)

    bias = _bias_tiles(rel_bias.astype(F32), Pc, Td)
    ck = cache_k.astype(F32)
    cv = cache_v.astype(F32)
    zeros_pad_att = jnp.zeros((TAIL_ROWS - n_meta - n_samp, ATT_V), F32)

    w_in_t = jnp.swapaxes(w_in.astype(F32), 1, 2)
    w_out_b, w_up_b, w_down_b = (w.astype(BF16) for w in (w_out, w_up, w_down))
    outs = {name: [] for name in ("Cp", "np", "mp", "Cs", "ns", "ms")}
    caches = None

    for l in range(depth):
        lam_init = 0.8 - 0.6 * math.exp(-0.3 * l)
        gate_bias = jnp.pad(jnp.concatenate([b_i[l], b_f[l]]).astype(F32), (0, LANES - 2 * H_M))[None]
        lam_vecs = [v[l].astype(F32)[None] for v in (lambda_q1, lambda_k1, lambda_q2, lambda_k2)]
        sg = subln_g[l].astype(F32)[None]

        za_f, za_t, zm_f, zm_t, g_f, g_t, *caches = _inproj(
            x_f, x_t, norm1_g[l].astype(F32)[None], w_in_t, l, gate_bias,
            q_norm_g[l].astype(F32)[None], k_norm_g[l].astype(F32)[None], B, Bd, caches)

        att_f = _attn_prompt(za_f, za_t, bias, lam_vecs, sg, B, S, lam_init)
        att_m = _attn_small(za_t, 0, B, bias[3], lam_vecs, sg, lam_init)
        att_s = _attn_small(za_t, srow, Bd, bias[4], lam_vecs, sg, lam_init, cache=(ck, cv, l))
        att_t = jnp.concatenate([att_m, att_s, zeros_pad_att], axis=0)

        hm_m, hm_f, C_p, n_p, m_p = _mlstm(zm_t, g_t, 0, B, frames=(zm_f, g_f, S))
        init = (state_C[l].astype(F32), state_n[l].astype(F32)[:, :, None, :],
                jnp.broadcast_to(state_m[l].astype(F32)[:, :, None, None], (Bd, H_M, 8, LANES)))
        hm_s, C_s, n_s, m_s = _mlstm(zm_t, g_t, srow, Bd, init=init)
        hm_t = jnp.concatenate([hm_m, hm_s, zeros_pad_att], axis=0)

        x_f, x_t = _merge(x_f, x_t, att_f, att_t, hm_f, hm_t, zm_f, zm_t,
                          mnorm_g[l].astype(F32)[None], w_out_b, l)
        x_f, x_t = _ffn(x_f, x_t, norm2_g[l].astype(F32)[None], w_up_b, w_down_b, l)

        outs["Cp"].append(C_p)
        outs["np"].append(n_p[:, :, 0, :])
        outs["mp"].append(m_p[:, :, 0, 0])
        outs["Cs"].append(C_s)
        outs["ns"].append(n_s[:, :, 0, :])
        outs["ms"].append(m_s[:, :, 0, 0])

    y_prompt = x_f.reshape(B, S, D)
    y_sample = x_t[n_meta:n_meta + n_samp].reshape(Bd, Td, D)
    k_p, v_p, k_s, v_s = caches
    return (y_prompt, y_sample,
            k_p, v_p, jnp.stack(outs["Cp"]), jnp.stack(outs["np"]), jnp.stack(outs["mp"]),
            k_s.reshape(depth, Bd, Td, H_ATT, 2 * DQK_ATT), v_s.reshape(depth, Bd, Td, H_ATT, DV_ATT),
            jnp.stack(outs["Cs"]), jnp.stack(outs["ns"]), jnp.stack(outs["ms"]))
```

```python
import functools
import math

import jax
import jax.numpy as jnp
from jax import lax
from jax.experimental import pallas as pl
from jax.experimental.pallas import tpu as pltpu

F32 = jnp.float32
BF16 = jnp.bfloat16

CHUNK = 64
N_META = 16
H_ATT = 4
DQK_ATT = 128
DV_ATT = 256
H_M = 4
DQK_M = 128
DV_M = 256
N_BUCKETS = 32
MAX_DISTANCE = 128
EPS = 1e-6
NEG = -1e30
LOG2E = math.log2(math.e)

ATT_QK = H_ATT * 2 * DQK_ATT
ATT_V = H_ATT * DV_ATT
M_QK = H_M * DQK_M
M_V = H_M * DV_M
N_MAIN = 2 * ATT_QK + ATT_V + 2 * M_QK + M_V + M_V
N_ATT = 2 * ATT_QK + ATT_V

LANES = 128
ATT_TILE = 256
M_TILE = 256
TAIL_ROWS = 256
PROJ_ROWS = 1024
PROJ_COLS = 512
MERGE_ROWS = 512
CACHE_CHUNK = 1024
FF_ROWS = 512
FF_COLS = 1024
ZERO_FILL_ROWS = 256
V7X_VMEM_BYTES = 64 * 1024 * 1024
VMEM_LIMIT = V7X_VMEM_BYTES * 7 // 8

_NT = (((1,), (1,)), ((), ()))


def _dot(a, b):
    return jnp.dot(a, b, preferred_element_type=F32)


def _dot_nt(a, b):
    return lax.dot_general(a, b, _NT, preferred_element_type=F32)


def _log_sigmoid(x):
    return jnp.minimum(x, 0.0) - jnp.log1p(jnp.exp(-jnp.abs(x)))


def _col(a, idx):
    lane = lax.broadcasted_iota(jnp.int32, a.shape, 1)
    return jnp.sum(jnp.where(lane == idx, a, 0.0), axis=1, keepdims=True)


def _rms(x, gain):
    ms = jnp.mean(x * x, axis=-1, keepdims=True)
    return x * lax.rsqrt(ms + EPS) * gain


def _bias_from_rel(rel, rb_ref, h):
    nb = N_BUCKETS // 2
    max_exact = nb // 2
    ret = jnp.where(rel > 0, nb, 0)
    n = jnp.abs(rel)
    nf = jnp.maximum(n, 1).astype(F32)
    large = max_exact + (jnp.log(nf / max_exact) / math.log(MAX_DISTANCE / max_exact)
                         * (nb - max_exact)).astype(jnp.int32)
    large = jnp.minimum(large, nb - 1)
    bucket = ret + jnp.where(n < max_exact, n, large)
    out = jnp.zeros(rel.shape, F32)
    for b in range(N_BUCKETS):
        out = jnp.where(bucket == b, rb_ref[b, h], out)
    return out


def _bias_kernel(rb_ref, diag_ref, prev_ref, meta_ref, mm_ref, samp_ref, *, past_len, dec_seq):
    h = pl.program_id(0)
    shift = CHUNK.bit_length() - 1
    T = ATT_TILE
    far = _bias_from_rel(jnp.full((8, LANES), -(2 * MAX_DISTANCE + 1), jnp.int32), rb_ref, h)[0:1, 0:1]

    qi = lax.broadcasted_iota(jnp.int32, (T, T), 0)
    kj = lax.broadcasted_iota(jnp.int32, (T, T), 1)
    bias = (_bias_from_rel(kj - qi, rb_ref, h) - far) * LOG2E
    diag_ref[...] = jnp.where((kj >> shift) <= (qi >> shift), bias, NEG)
    prev_ref[...] = (_bias_from_rel(kj - T - qi, rb_ref, h) - far) * LOG2E

    qi = lax.broadcasted_iota(jnp.int32, (T, LANES), 0)
    kj = lax.broadcasted_iota(jnp.int32, (T, LANES), 1)
    for var in range(2):
        bias = (_bias_from_rel(kj - (N_META + qi + var * T), rb_ref, h) - far) * LOG2E
        meta_ref[var] = jnp.where(kj < N_META, bias, NEG)

    qi = lax.broadcasted_iota(jnp.int32, (N_META, LANES), 0)
    kj = lax.broadcasted_iota(jnp.int32, (N_META, LANES), 1)
    mm_ref[...] = jnp.where(kj < N_META, _bias_from_rel(kj - qi, rb_ref, h) * LOG2E, NEG)

    width = samp_ref.shape[-1]
    qi = lax.broadcasted_iota(jnp.int32, (dec_seq, width), 0) + past_len
    kj = lax.broadcasted_iota(jnp.int32, (dec_seq, width), 1)
    bias = _bias_from_rel(kj - qi, rb_ref, h) * LOG2E
    ok = ((kj >> shift) <= (qi >> shift)) & (kj < past_len + dec_seq)
    samp_ref[...] = jnp.where(ok, bias, NEG)


def _bias_tiles(rel_bias, past_len, dec_seq):
    T = ATT_TILE
    sw = past_len + LANES
    out_shape = (
        jax.ShapeDtypeStruct((H_ATT, T, T), F32),
        jax.ShapeDtypeStruct((H_ATT, T, T), F32),
        jax.ShapeDtypeStruct((H_ATT, 2, T, LANES), F32),
        jax.ShapeDtypeStruct((H_ATT, N_META, LANES), F32),
        jax.ShapeDtypeStruct((H_ATT, dec_seq, sw), F32),
    )
    out_specs = (
        pl.BlockSpec((None, T, T), lambda h: (h, 0, 0)),
        pl.BlockSpec((None, T, T), lambda h: (h, 0, 0)),
        pl.BlockSpec((None, 2, T, LANES), lambda h: (h, 0, 0, 0)),
        pl.BlockSpec((None, N_META, LANES), lambda h: (h, 0, 0)),
        pl.BlockSpec((None, dec_seq, sw), lambda h: (h, 0, 0)),
    )
    return pl.pallas_call(
        functools.partial(_bias_kernel, past_len=past_len, dec_seq=dec_seq),
        grid=(H_ATT,),
        in_specs=[pl.BlockSpec(memory_space=pltpu.SMEM)],
        out_specs=out_specs,
        out_shape=out_shape,
        name="rel_bias_tiles",
    )(rel_bias)


def _zero_fill_copies(zbuf, zsem, cache_refs, depth):
    rows = zbuf.shape[0]
    copies = []
    for dst in cache_refs:
        lead = dst.shape[1:-3]
        n_rows = dst.shape[-3]
        for l in range(1, depth):
            for b in range(lead[0] if lead else 1):
                view = dst.at[l, b] if lead else dst.at[l]
                for r0 in range(0, n_rows, rows):
                    n = min(rows, n_rows - r0)
                    copies.append(pltpu.make_async_copy(zbuf.at[pl.ds(0, n)], view.at[pl.ds(r0, n)],
                                                        zsem.at[len(copies)]))
    return copies


def _kv_copies(stage, sem, dst_prompt, dst_sample, i, *, layer, tail, n_batch, frames_per_batch, n_sample_rows):
    dst_prompt = dst_prompt.at[layer]
    dst_sample = dst_sample.at[layer]
    if not tail:
        rows = stage.shape[0]
        tiles_per_batch = frames_per_batch // rows
        b = i // tiles_per_batch
        t = i - b * tiles_per_batch
        return [pltpu.make_async_copy(stage, dst_prompt.at[b, pl.ds(N_META + t * rows, rows)], sem.at[0])]
    copies = [pltpu.make_async_copy(stage.at[pl.ds(b * N_META, N_META)], dst_prompt.at[b, pl.ds(0, N_META)],
                                    sem.at[b]) for b in range(n_batch)]
    copies.append(pltpu.make_async_copy(stage.at[pl.ds(n_batch * N_META, n_sample_rows)], dst_sample,
                                        sem.at[n_batch]))
    return copies


def _inproj_kernel(*refs, layer, depth, n_frame_tiles, n_batch, frames_per_batch, n_sample_rows):
    xf_ref, xt_ref, g1_ref, wa_ref, wg_ref, gb_ref, qg_ref, kg_ref = refs[:8]
    (zaf_ref, zat_ref, zmf_ref, zmt_ref, gf_ref, gt_ref, kp_ref, vp_ref, ks_ref, vs_ref,
     xn_s, y2d, stage, sem, zbuf, zsem) = refs[-16:]
    i = pl.program_id(0)
    j = pl.program_id(1)
    bn = wa_ref.shape[1]

    if layer == 0 and depth > 1:
        zero_copies = _zero_fill_copies(zbuf, zsem, (kp_ref, vp_ref, ks_ref, vs_ref), depth)

        @pl.when((i == 0) & (j == 0))
        def _():
            zbuf[...] = jnp.zeros(zbuf.shape, F32)
            for c in zero_copies:
                c.start()

        @pl.when((i == pl.num_programs(0) - 1) & (j == pl.num_programs(1) - 1))
        def _():
            for c in zero_copies:
                c.wait()

    assert ATT_QK % bn == 0 and ATT_V % bn == 0 and M_QK % bn == 0 and M_V % bn == 0 and bn % DV_ATT == 0
    j_k = ATT_QK // bn
    j_v = j_k + ATT_QK // bn
    j_mq = j_v + ATT_V // bn
    j_mk = j_mq + M_QK // bn
    j_mv = j_mk + M_QK // bn
    j_og = j_mv + M_V // bn

    def tile(x_ref, za_ref, zm_ref, g_ref, tail):
        nrows = x_ref.shape[0]
        st = stage.at[pl.ds(0, nrows)]
        copies = functools.partial(_kv_copies, st, sem, i=i, layer=layer, tail=tail, n_batch=n_batch,
                                   frames_per_batch=frames_per_batch, n_sample_rows=n_sample_rows)

        @pl.when(j == 0)
        def _():
            xb = _rms(x_ref[...], g1_ref[...]).astype(BF16)
            xn_s[0:nrows, :] = xb
            gates = _dot_nt(xb, wg_ref[0].astype(BF16)) + gb_ref[...]
            lane = lax.broadcasted_iota(jnp.int32, gates.shape, 1)
            g_ref[...] = jnp.where(lane < H_M, gates, _log_sigmoid(gates))

        acc = _dot_nt(xn_s[0:nrows, :], wa_ref[0].astype(BF16))

        def group_norm(gain_ref, scale):
            parts = []
            for gi in range(bn // DQK_ATT):
                y = _rms(acc[:, gi * DQK_ATT:(gi + 1) * DQK_ATT], gain_ref[...])
                parts.append(y if scale is None else y * scale)
            return jnp.concatenate(parts, axis=1)

        def stage_heads(y, jj, j0, dst_prompt, dst_sample):
            c0 = (jj - j0) * bn
            y2d[0:nrows, c0:c0 + bn] = y
            if jj == j0 + ATT_QK // bn - 1:
                if j0 != j_k:
                    for c in copies(kp_ref, ks_ref):
                        c.wait()
                st[...] = y2d[0:nrows, :].reshape(nrows, H_ATT, DV_ATT)
                for c in copies(dst_prompt, dst_sample):
                    c.start()

        @pl.when(j < j_k)
        def _():
            za_ref[...] = group_norm(qg_ref, DQK_ATT ** -0.5 * LOG2E).astype(BF16)

        for jj in range(j_k, j_v):
            @pl.when(j == jj)
            def _(jj=jj):
                y = group_norm(kg_ref, None)
                za_ref[...] = y.astype(BF16)
                stage_heads(y, jj, j_k, kp_ref, ks_ref)

        for jj in range(j_v, j_mq):
            @pl.when(j == jj)
            def _(jj=jj):
                za_ref[...] = acc.astype(BF16)
                stage_heads(acc, jj, j_v, vp_ref, vs_ref)

        @pl.when(j == j_mq)
        def _():
            for c in copies(vp_ref, vs_ref):
                c.wait()

        @pl.when(((j >= j_mq) & (j < j_mk)) | ((j >= j_mv) & (j < j_og)))
        def _():
            zm_ref[...] = acc

        @pl.when((j >= j_mk) & (j < j_mv))
        def _():
            zm_ref[...] = acc * (DQK_M ** -0.5)

        @pl.when(j >= j_og)
        def _():
            zm_ref[...] = jax.nn.sigmoid(acc)

    @pl.when(i < n_frame_tiles)
    def _():
        tile(xf_ref, zaf_ref, zmf_ref, gf_ref, False)

    @pl.when(i == n_frame_tiles)
    def _():
        tile(xt_ref, zat_ref, zmt_ref, gt_ref, True)


def _inproj(x_f, x_t, g1, w_all, layer, gate_bias, qg, kg, n_batch, n_sample_streams, caches):
    depth = w_all.shape[0]
    NF, D = x_f.shape
    NT = x_t.shape[0]
    bm, bn = PROJ_ROWS, PROJ_COLS
    nft = NF // bm
    nj = N_MAIN // bn
    nja = N_ATT // bn
    g_row = N_MAIN - M_V
    assert g_row % bn == 0 and g_row % 8 == 0 and w_all.shape[1] >= g_row + LANES

    def w_window(i, j):
        row = j * bn + jnp.where(j * bn >= g_row, 2 * H_M, 0)
        return (layer, pl.multiple_of(row, 8), 0)

    S = NF // n_batch
    L = N_META + S
    n_sample_rows = n_sample_streams * N_META
    assert NF % bm == 0 and S % bm == 0 and N_MAIN % bn == 0 and NT <= bm
    fr = lambda i: jnp.minimum(i, nft - 1)
    any_spec = pl.BlockSpec(memory_space=pl.ANY)
    assert (caches is None) == (layer == 0)
    prev = () if caches is None else tuple(caches)
    n_in = 8
    zero_rows = ZERO_FILL_ROWS
    n_zero = (depth - 1) * (2 * n_batch * pl.cdiv(L, zero_rows) + 2 * pl.cdiv(n_sample_rows, zero_rows))
    return pl.pallas_call(
        functools.partial(_inproj_kernel, layer=layer, depth=depth, n_frame_tiles=nft, n_batch=n_batch,
                          frames_per_batch=S, n_sample_rows=n_sample_rows),
        grid=(nft + 1, nj),
        input_output_aliases={n_in + c: 6 + c for c in range(len(prev))},
        in_specs=[
            pl.BlockSpec((bm, D), lambda i, j: (fr(i), 0)),
            pl.BlockSpec((NT, D), lambda i, j: (0, 0), pipeline_mode=pl.Buffered(1)),
            pl.BlockSpec((1, D), lambda i, j: (0, 0)),
            pl.BlockSpec((pl.Element(1), pl.Element(bn), pl.Element(D)), w_window),
            pl.BlockSpec((pl.Element(1), pl.Element(LANES), pl.Element(D)), lambda i, j: (layer, g_row, 0)),
            pl.BlockSpec((1, LANES), lambda i, j: (0, 0)),
            pl.BlockSpec((1, DQK_ATT), lambda i, j: (0, 0)),
            pl.BlockSpec((1, DQK_ATT), lambda i, j: (0, 0)),
        ] + [any_spec] * len(prev),
        out_specs=(
            pl.BlockSpec((bm, bn), lambda i, j: (fr(i), jnp.where(i < nft, jnp.minimum(j, nja - 1), nja - 1))),
            pl.BlockSpec((NT, bn), lambda i, j: (0, jnp.where(i < nft, 0, jnp.minimum(j, nja - 1)))),
            pl.BlockSpec((bm, bn), lambda i, j: (fr(i), jnp.where(i < nft, jnp.maximum(j - nja, 0), nj - nja - 1))),
            pl.BlockSpec((NT, bn), lambda i, j: (0, jnp.where(i < nft, 0, jnp.maximum(j - nja, 0)))),
            pl.BlockSpec((bm, LANES), lambda i, j: (fr(i), 0)),
            pl.BlockSpec((NT, LANES), lambda i, j: (0, 0)),
            any_spec, any_spec, any_spec, any_spec,
        ),
        out_shape=(
            jax.ShapeDtypeStruct((NF, N_ATT), BF16),
            jax.ShapeDtypeStruct((NT, N_ATT), BF16),
            jax.ShapeDtypeStruct((NF, N_MAIN - N_ATT), F32),
            jax.ShapeDtypeStruct((NT, N_MAIN - N_ATT), F32),
            jax.ShapeDtypeStruct((NF, LANES), F32),
            jax.ShapeDtypeStruct((NT, LANES), F32),
            jax.ShapeDtypeStruct((depth, n_batch, L, H_ATT, 2 * DQK_ATT), F32),
            jax.ShapeDtypeStruct((depth, n_batch, L, H_ATT, DV_ATT), F32),
            jax.ShapeDtypeStruct((depth, n_sample_rows, H_ATT, 2 * DQK_ATT), F32),
            jax.ShapeDtypeStruct((depth, n_sample_rows, H_ATT, DV_ATT), F32),
        ),
        scratch_shapes=[
            pltpu.VMEM((bm, D), BF16),
            pltpu.VMEM((bm, ATT_QK), F32),
            pltpu.VMEM((bm, H_ATT, DV_ATT), F32),
            pltpu.SemaphoreType.DMA((n_batch + 1,)),
            pltpu.VMEM((zero_rows, H_ATT, DV_ATT), F32),
            pltpu.SemaphoreType.DMA((max(n_zero, 1),)),
        ],
        compiler_params=pltpu.CompilerParams(
            dimension_semantics=("arbitrary", "arbitrary"), vmem_limit_bytes=VMEM_LIMIT),
        name="in_proj",
    )(x_f, x_t, g1, w_all, w_all, gate_bias, qg, kg, *prev)


def _diff_lambda(lq1_ref, lk1_ref, lq2_ref, lk2_ref, lam_init):
    a = jnp.sum(lq1_ref[...] * lk1_ref[...], axis=-1, keepdims=True)
    b = jnp.sum(lq2_ref[...] * lk2_ref[...], axis=-1, keepdims=True)
    return jnp.exp(a) - jnp.exp(b) + lam_init


def _subln(o, sg_ref, lam_init):
    return _rms(o, sg_ref[...]) * (1.0 - lam_init)


def _pad_rows(dst, src):
    dst[...] = jnp.zeros(dst.shape, dst.dtype)
    dst[0:src.shape[0], :] = src[...].astype(dst.dtype)


def _attn_prompt_kernel(q_ref, k_ref, v_ref, km_ref, vm_ref, bd_ref, bp_ref, bm_ref,
                        lq1_ref, lk1_ref, lq2_ref, lk2_ref, sg_ref, o_ref,
                        kmb_s, vmb_s, *, lam_init):
    T = ATT_TILE
    n_tiles = q_ref.shape[0] // T
    kb_s, vb_s = k_ref, v_ref
    _pad_rows(kmb_s, km_ref)
    _pad_rows(vmb_s, vm_ref)
    lam = _diff_lambda(lq1_ref, lk1_ref, lq2_ref, lk2_ref, lam_init)

    for j in range(n_tiles):
        rows = slice(j * T, (j + 1) * T)
        regions = [(kmb_s, vmb_s, slice(0, LANES), bm_ref[min(j, 1)])]
        if j >= 2:
            regions.append((kb_s, vb_s, slice(0, (j - 1) * T), None))
        if j >= 1:
            regions.append((kb_s, vb_s, slice((j - 1) * T, j * T), bp_ref[...]))
        regions.append((kb_s, vb_s, rows, bd_ref[...]))

        outs = []
        for c in range(2):
            cols = slice(c * DQK_ATT, (c + 1) * DQK_ATT)
            qc = q_ref[rows, cols]
            ss = [_dot_nt(qc, kr[rs, cols]) for kr, _, rs, _ in regions]
            ss = [s if br is None else s + br for s, (_, _, _, br) in zip(ss, regions)]
            m = functools.reduce(jnp.maximum, [jnp.max(s, axis=-1, keepdims=True) for s in ss])
            pc = [jnp.exp2(s - m) for s in ss]
            denom = functools.reduce(jnp.add, [jnp.sum(p, axis=-1, keepdims=True) for p in pc])
            pv = functools.reduce(jnp.add, [_dot(p.astype(BF16), vr[rs, :])
                                            for p, (_, vr, rs, _) in zip(pc, regions)])
            outs.append(pv / denom)
        o_ref[rows, :] = _subln(outs[0] - lam * outs[1], sg_ref, lam_init)


def _attn_prompt(z_f, z_t, bias, lam_vecs, sg, B, S, lam_init):
    T = ATT_TILE
    qkw = 2 * DQK_ATT
    kcol = ATT_QK // qkw
    vcol = 2 * ATT_QK // DV_ATT
    b_diag, b_prev, b_meta, _, _ = bias
    vec = pl.BlockSpec((1, DQK_ATT), lambda b, h: (0, 0))
    return pl.pallas_call(
        functools.partial(_attn_prompt_kernel, lam_init=lam_init),
        grid=(B, H_ATT),
        in_specs=[
            pl.BlockSpec((S, qkw), lambda b, h: (b, h)),
            pl.BlockSpec((S, qkw), lambda b, h: (b, kcol + h)),
            pl.BlockSpec((S, DV_ATT), lambda b, h: (b, vcol + h)),
            pl.BlockSpec((N_META, qkw), lambda b, h: (b, kcol + h)),
            pl.BlockSpec((N_META, DV_ATT), lambda b, h: (b, vcol + h)),
            pl.BlockSpec((None, T, T), lambda b, h: (h, 0, 0)),
            pl.BlockSpec((None, T, T), lambda b, h: (h, 0, 0)),
            pl.BlockSpec((None, 2, T, LANES), lambda b, h: (h, 0, 0, 0)),
            vec, vec, vec, vec,
            pl.BlockSpec((1, DV_ATT), lambda b, h: (0, 0)),
        ],
        out_specs=pl.BlockSpec((S, DV_ATT), lambda b, h: (b, h)),
        out_shape=jax.ShapeDtypeStruct((B * S, ATT_V), F32),
        scratch_shapes=[
            pltpu.VMEM((LANES, qkw), BF16),
            pltpu.VMEM((LANES, DV_ATT), BF16),
        ],
        compiler_params=pltpu.CompilerParams(
            dimension_semantics=("arbitrary", "arbitrary"), vmem_limit_bytes=VMEM_LIMIT),
        name="attn_prompt",
    )(z_f, z_f, z_f, z_t, z_t, b_diag, b_prev, b_meta, *lam_vecs, sg)


def _attn_small_kernel(*refs, lam_init, has_cache):
    if has_cache:
        (q_ref, kc_ref, vc_ref, kn_ref, vn_ref, b_ref,
         lq1_ref, lk1_ref, lq2_ref, lk2_ref, sg_ref, o_ref, kpad, vpad, kb_s, vb_s) = refs
        past_len = kb_s.shape[0]
        chunk = kc_ref.shape[0]
        c = pl.program_id(1)
        r0 = pl.multiple_of(c * chunk, chunk)
        kb_s[pl.ds(r0, chunk), :] = kc_ref[...].reshape(chunk, ATT_QK).astype(BF16)
        vb_s[pl.ds(r0, chunk), :] = vc_ref[...].reshape(chunk, ATT_V).astype(BF16)

        @pl.when(c == past_len // chunk - 1)
        def _():
            _attn_small_body(q_ref, kn_ref, vn_ref, b_ref, lq1_ref, lk1_ref, lq2_ref, lk2_ref, sg_ref, o_ref,
                             kpad, vpad, kb_s, vb_s, lam_init=lam_init, past_len=past_len)
    else:
        (q_ref, kn_ref, vn_ref, b_ref,
         lq1_ref, lk1_ref, lq2_ref, lk2_ref, sg_ref, o_ref, kpad, vpad) = refs
        _attn_small_body(q_ref, kn_ref, vn_ref, b_ref, lq1_ref, lk1_ref, lq2_ref, lk2_ref, sg_ref, o_ref,
                         kpad, vpad, None, None, lam_init=lam_init, past_len=0)


def _attn_small_body(q_ref, kn_ref, vn_ref, b_ref, lq1_ref, lk1_ref, lq2_ref, lk2_ref, sg_ref, o_ref,
                     kpad, vpad, kb_s, vb_s, *, lam_init, past_len):
    has_cache = past_len > 0
    lam = _diff_lambda(lq1_ref, lk1_ref, lq2_ref, lk2_ref, lam_init)
    _pad_rows(kpad, kn_ref)
    _pad_rows(vpad, vn_ref)
    for h in range(H_ATT):
        vsl = slice(h * DV_ATT, (h + 1) * DV_ATT)
        outs = []
        for c in range(2):
            sl = slice(h * 2 * DQK_ATT + c * DQK_ATT, h * 2 * DQK_ATT + (c + 1) * DQK_ATT)
            qc = q_ref[:, sl]
            s_new = _dot_nt(qc, kpad[:, sl]) + b_ref[h, :, past_len:]
            m = jnp.max(s_new, axis=-1, keepdims=True)
            if has_cache:
                s_old = _dot_nt(qc, kb_s[:, sl]) + b_ref[h, :, :past_len]
                m = jnp.maximum(m, jnp.max(s_old, axis=-1, keepdims=True))
            p_new = jnp.exp2(s_new - m)
            l = jnp.sum(p_new, axis=-1, keepdims=True)
            o = _dot(p_new.astype(BF16), vpad[:, vsl])
            if has_cache:
                p_old = jnp.exp2(s_old - m)
                l = l + jnp.sum(p_old, axis=-1, keepdims=True)
                o = o + _dot(p_old.astype(BF16), vb_s[:, vsl])
            outs.append(o / l)
        o_ref[:, vsl] = _subln(outs[0] - lam * outs[1], sg_ref, lam_init)


def _attn_small(z_t, row0, n_streams, bias, lam_vecs, sg, lam_init, cache=None):
    vec = pl.BlockSpec((1, DQK_ATT), lambda b, c: (0, 0))
    in_specs = [pl.BlockSpec((N_META, ATT_QK), lambda b, c: (row0 + b, 0))]
    args = [z_t]
    scratch = [pltpu.VMEM((LANES, ATT_QK), BF16), pltpu.VMEM((LANES, ATT_V), BF16)]
    n_chunks = 1
    if cache is not None:
        ck, cv, layer = cache
        past_len = ck.shape[2]
        chunk = min(past_len, CACHE_CHUNK)
        assert past_len % chunk == 0
        n_chunks = past_len // chunk
        in_specs += [
            pl.BlockSpec((None, None, chunk, H_ATT, 2 * DQK_ATT), lambda b, c: (layer, b, c, 0, 0)),
            pl.BlockSpec((None, None, chunk, H_ATT, DV_ATT), lambda b, c: (layer, b, c, 0, 0))]
        args += [ck, cv]
        scratch += [pltpu.VMEM((past_len, ATT_QK), BF16), pltpu.VMEM((past_len, ATT_V), BF16)]
    in_specs += [
        pl.BlockSpec((N_META, ATT_QK), lambda b, c: (row0 + b, 1)),
        pl.BlockSpec((N_META, ATT_V), lambda b, c: (row0 + b, 2 * ATT_QK // ATT_V)),
        pl.BlockSpec(bias.shape, lambda b, c: (0, 0, 0)),
        vec, vec, vec, vec,
        pl.BlockSpec((1, DV_ATT), lambda b, c: (0, 0)),
    ]
    args += [z_t, z_t, bias, *lam_vecs, sg]
    return pl.pallas_call(
        functools.partial(_attn_small_kernel, lam_init=lam_init, has_cache=cache is not None),
        grid=(n_streams, n_chunks),
        in_specs=in_specs,
        out_specs=pl.BlockSpec((N_META, ATT_V), lambda b, c: (b, 0)),
        out_shape=jax.ShapeDtypeStruct((n_streams * N_META, ATT_V), F32),
        scratch_shapes=scratch,
        compiler_params=pltpu.CompilerParams(
            dimension_semantics=("arbitrary", "arbitrary"), vmem_limit_bytes=VMEM_LIMIT),
        name="attn_cached" if cache is not None else "attn_meta",
    )(*args)


def _mlstm_kernel(*refs, n_blocks, has_init):
    refs = list(refs)
    qh_ref, kh_ref, vh_ref, gh_ref = refs[:4]
    pos = 4
    if n_blocks:
        qf_ref, kf_ref, vf_ref, gf_ref = refs[pos:pos + 4]
        pos += 4
    if has_init:
        c0_ref, n0_ref, m0_ref = refs[pos:pos + 3]
        pos += 3
    hh_ref = refs[pos]
    pos += 1
    if n_blocks:
        hf_ref = refs[pos]
        pos += 1
    c_ref, n_ref, m_ref = refs[pos:pos + 3]
    qs, ks, vs, gs = refs[pos + 3:pos + 7]

    t = pl.program_id(1)
    block = functools.partial(_mlstm_block, c_ref=c_ref, n_ref=n_ref, m_ref=m_ref)

    @pl.when(t == 0)
    def _():
        if has_init:
            c_ref[...] = c0_ref[...]
            n_ref[...] = n0_ref[...]
            m_ref[...] = m0_ref[...]
        else:
            c_ref[...] = jnp.zeros(c_ref.shape, F32)
            n_ref[...] = jnp.zeros(n_ref.shape, F32)
            m_ref[...] = jnp.zeros(m_ref.shape, F32)
        nh = qh_ref.shape[0]
        for dst, src in ((qs, qh_ref), (ks, kh_ref), (vs, vh_ref)):
            dst[...] = jnp.zeros(dst.shape, F32)
            dst[0:nh, :] = src[...]
        lane = lax.broadcasted_iota(jnp.int32, gs.shape, 1)
        gs[...] = jnp.where(lane < H_M, NEG, 0.0)
        gs[0:nh, :] = gh_ref[...]

        block(qs, ks, vs, gs, hh_ref, nh)

    if n_blocks:
        @pl.when(t > 0)
        def _():
            block(qf_ref, kf_ref, vf_ref, gf_ref, hf_ref, qf_ref.shape[0])


def _mlstm_block(q_src, k_src, v_src, g_src, h_dst, n_out, c_ref, n_ref, m_ref):
    T = q_src.shape[0]
    g = g_src[...]
    row = lax.broadcasted_iota(jnp.int32, (T, T), 0)
    colid = lax.broadcasted_iota(jnp.int32, (T, T), 1)
    causal = row >= colid
    tril = jnp.where(causal, 1.0, 0.0).astype(F32)
    cum = jnp.dot(tril, g, preferred_element_type=F32, precision=lax.Precision.HIGHEST)
    g_t = g.T
    cum_t = cum.T
    last = cum[T - 1:T, :]

    for h in range(H_M):
        q = q_src[:, h * DQK_M:(h + 1) * DQK_M]
        k = k_src[:, h * DQK_M:(h + 1) * DQK_M]
        v = v_src[:, h * DV_M:(h + 1) * DV_M]
        qb = q.astype(BF16)
        kb = k.astype(BF16)
        m_old = m_ref[h, 0:1, 0:1]
        c_old = c_ref[h]
        n_old = n_ref[h]
        bh_c = _col(cum, H_M + h)
        ih_c = _col(g, h)
        bh_r = cum_t[H_M + h:H_M + h + 1, :]
        ih_r = g_t[h:h + 1, :]
        d = jnp.where(causal, bh_c - bh_r + ih_r, NEG)
        inter = bh_c + m_old
        mt = jnp.maximum(inter, jnp.max(d, axis=-1, keepdims=True))
        a = jnp.exp(d - mt) * _dot_nt(qb, kb)
        iw = jnp.exp(inter - mt)
        num = _dot(a.astype(BF16), v.astype(BF16)) + iw * _dot_nt(qb, c_old.astype(BF16))
        den = jnp.sum(a, axis=-1, keepdims=True) + iw * jnp.sum(q * n_old, axis=-1, keepdims=True)
        den = jnp.maximum(jnp.abs(den), jnp.exp(-mt))
        hout = num / den
        h_dst[:, h * DV_M:(h + 1) * DV_M] = hout[0:n_out, :]

        bl = _col(last, H_M + h)
        g_c = bl - bh_c + ih_c
        m_new = jnp.maximum(bl + m_old, jnp.max(g_c, axis=0, keepdims=True))
        sw = jnp.exp(g_c - m_new)
        dec = jnp.exp(bl + m_old - m_new)
        c_ref[h] = dec * c_old + _dot(v.T.astype(BF16), (sw * k).astype(BF16))
        n_ref[h] = dec * n_old + jnp.sum(sw * k, axis=0, keepdims=True)
        m_ref[h] = jnp.broadcast_to(m_new, m_ref.shape[1:])


def _mlstm(z_t, g_t, row0, n_streams, frames=None, init=None):
    T = M_TILE
    qcol = 0
    vcol = 2 * M_QK // M_V
    nh = N_META
    in_specs = [
        pl.BlockSpec((nh, M_QK), lambda b, t: (row0 + b, qcol)),
        pl.BlockSpec((nh, M_QK), lambda b, t: (row0 + b, qcol + 1)),
        pl.BlockSpec((nh, M_V), lambda b, t: (row0 + b, vcol)),
        pl.BlockSpec((nh, LANES), lambda b, t: (row0 + b, 0)),
    ]
    args = [z_t, z_t, z_t, g_t]
    n_blocks = 0
    if frames is not None:
        z_f, g_f, fps = frames
        n_blocks = fps // T
        fidx = lambda b, t: b * n_blocks + jnp.maximum(t - 1, 0)
        in_specs += [
            pl.BlockSpec((T, M_QK), lambda b, t: (fidx(b, t), qcol)),
            pl.BlockSpec((T, M_QK), lambda b, t: (fidx(b, t), qcol + 1)),
            pl.BlockSpec((T, M_V), lambda b, t: (fidx(b, t), vcol)),
            pl.BlockSpec((T, LANES), lambda b, t: (fidx(b, t), 0)),
        ]
        args += [z_f, z_f, z_f, g_f]
    state_specs = (
        pl.BlockSpec((None, H_M, DV_M, DQK_M), lambda b, t: (b, 0, 0, 0)),
        pl.BlockSpec((None, H_M, 1, DQK_M), lambda b, t: (b, 0, 0, 0)),
        pl.BlockSpec((None, H_M, 8, LANES), lambda b, t: (b, 0, 0, 0)),
    )
    if init is not None:
        in_specs += list(state_specs)
        args += list(init)
    out_specs = [pl.BlockSpec((nh, M_V), lambda b, t: (b, 0))]
    out_shape = [jax.ShapeDtypeStruct((n_streams * nh, M_V), F32)]
    if n_blocks:
        out_specs.append(pl.BlockSpec((T, M_V), lambda b, t: (fidx(b, t), 0)))
        out_shape.append(jax.ShapeDtypeStruct((n_streams * fps, M_V), F32))
    out_specs += list(state_specs)
    out_shape += [
        jax.ShapeDtypeStruct((n_streams, H_M, DV_M, DQK_M), F32),
        jax.ShapeDtypeStruct((n_streams, H_M, 1, DQK_M), F32),
        jax.ShapeDtypeStruct((n_streams, H_M, 8, LANES), F32),
    ]
    return pl.pallas_call(
        functools.partial(_mlstm_kernel, n_blocks=n_blocks, has_init=init is not None),
        grid=(n_streams, 1 + n_blocks),
        in_specs=in_specs,
        out_specs=tuple(out_specs),
        out_shape=tuple(out_shape),
        scratch_shapes=[
            pltpu.VMEM((LANES, M_QK), F32), pltpu.VMEM((LANES, M_QK), F32),
            pltpu.VMEM((LANES, M_V), F32), pltpu.VMEM((LANES, LANES), F32),
        ],
        compiler_params=pltpu.CompilerParams(
            dimension_semantics=("arbitrary", "arbitrary"), vmem_limit_bytes=VMEM_LIMIT),
        name="mlstm_prompt" if n_blocks else "mlstm_cached",
    )(*args)


def _merge_kernel(xf_ref, xt_ref, attf_ref, attt_ref, hmf_ref, hmt_ref, zf_ref, zt_ref, mg_ref, wo_ref, g2_ref,
                  of_ref, ot_ref, nf_ref, nt_ref, *, n_frame_tiles):
    i = pl.program_id(0)

    def tile(x_ref, att_ref, hm_ref, og_ref, o_ref, n_ref):
        parts = []
        for h in range(H_M):
            sl = slice(h * DV_M, (h + 1) * DV_M)
            parts.append(_rms(hm_ref[:, sl], mg_ref[:, sl]) * og_ref[:, sl])
        hmn = jnp.concatenate(parts, axis=1).astype(BF16)
        y = (x_ref[...] + _dot(att_ref[...].astype(BF16), wo_ref[0:ATT_V, :])
             + _dot(hmn, wo_ref[ATT_V:, :]))
        o_ref[...] = y
        n_ref[...] = _rms(y, g2_ref[...]).astype(BF16)

    @pl.when(i < n_frame_tiles)
    def _():
        tile(xf_ref, attf_ref, hmf_ref, zf_ref, of_ref, nf_ref)

    @pl.when(i == n_frame_tiles)
    def _():
        tile(xt_ref, attt_ref, hmt_ref, zt_ref, ot_ref, nt_ref)


def _merge(x_f, x_t, att_f, att_t, hm_f, hm_t, z_f, z_t, mg, w_out, g2, layer):
    NF, D = x_f.shape
    NT = x_t.shape[0]
    bm = MERGE_ROWS
    nft = NF // bm
    assert NF % bm == 0
    ogcol = (N_MAIN - N_ATT - M_V) // M_V
    fr = lambda i: (jnp.minimum(i, nft - 1), 0)
    tl = lambda i: (0, 0)
    once = dict(pipeline_mode=pl.Buffered(1))
    return pl.pallas_call(
        functools.partial(_merge_kernel, n_frame_tiles=nft),
        grid=(nft + 1,),
        in_specs=[
            pl.BlockSpec((bm, D), fr), pl.BlockSpec((NT, D), tl, **once),
            pl.BlockSpec((bm, ATT_V), fr), pl.BlockSpec((NT, ATT_V), tl, **once),
            pl.BlockSpec((bm, M_V), fr), pl.BlockSpec((NT, M_V), tl, **once),
            pl.BlockSpec((bm, M_V), lambda i: (jnp.minimum(i, nft - 1), ogcol)),
            pl.BlockSpec((NT, M_V), lambda i: (0, ogcol), **once),
            pl.BlockSpec((1, M_V), tl),
            pl.BlockSpec((None, D, D), lambda i: (layer, 0, 0), **once),
            pl.BlockSpec((1, D), tl),
        ],
        out_specs=(pl.BlockSpec((bm, D), fr), pl.BlockSpec((NT, D), tl),
                   pl.BlockSpec((bm, D), fr), pl.BlockSpec((NT, D), tl)),
        out_shape=(jax.ShapeDtypeStruct((NF, D), F32), jax.ShapeDtypeStruct((NT, D), F32),
                   jax.ShapeDtypeStruct((NF, D), BF16), jax.ShapeDtypeStruct((NT, D), BF16)),
        compiler_params=pltpu.CompilerParams(
            dimension_semantics=("arbitrary",), vmem_limit_bytes=VMEM_LIMIT),
        name="merge_out_proj",
    )(x_f, x_t, att_f, att_t, hm_f, hm_t, z_f, z_t, mg, w_out, g2)


def _ffn_kernel(xf_ref, xt_ref, nf_ref, nt_ref, wu_ref, wd_ref, of_ref, ot_ref, *, n_frame_tiles):
    i = pl.program_id(0)
    k = pl.program_id(1)

    def tile(x_ref, n_ref, o_ref):
        @pl.when(k == 0)
        def _():
            o_ref[...] = x_ref[...]

        u = jnp.maximum(_dot(n_ref[...], wu_ref[...]), 0.0)
        o_ref[...] += _dot((u * u).astype(BF16), wd_ref[...])

    @pl.when(i < n_frame_tiles)
    def _():
        tile(xf_ref, nf_ref, of_ref)

    @pl.when(i == n_frame_tiles)
    def _():
        tile(xt_ref, nt_ref, ot_ref)


def _ffn(x_f, x_t, xn_f, xn_t, w_up, w_down, layer):
    NF, D = x_f.shape
    NT = x_t.shape[0]
    FF = w_up.shape[2]
    bm, fc = FF_ROWS, FF_COLS
    nft = NF // bm
    assert NF % bm == 0 and FF % fc == 0 and NT <= bm
    fr = lambda i, k: (jnp.minimum(i, nft - 1), 0)
    tl = lambda i, k: (0, 0)
    return pl.pallas_call(
        functools.partial(_ffn_kernel, n_frame_tiles=nft),
        grid=(nft + 1, FF // fc),
        in_specs=[
            pl.BlockSpec((bm, D), fr),
            pl.BlockSpec((NT, D), tl),
            pl.BlockSpec((bm, D), fr),
            pl.BlockSpec((NT, D), tl),
            pl.BlockSpec((None, D, fc), lambda i, k: (layer, 0, k)),
            pl.BlockSpec((None, fc, D), lambda i, k: (layer, k, 0)),
        ],
        out_specs=(pl.BlockSpec((bm, D), fr), pl.BlockSpec((NT, D), tl)),
        out_shape=(jax.ShapeDtypeStruct((NF, D), F32), jax.ShapeDtypeStruct((NT, D), F32)),
        compiler_params=pltpu.CompilerParams(
            dimension_semantics=("arbitrary", "arbitrary"), vmem_limit_bytes=VMEM_LIMIT),
        name="ffn",
    )(x_f, x_t, xn_f, xn_t, w_up, w_down)


def kernel(x_prompt, x_sample, cache_k, cache_v, state_C, state_n, state_m, meta_tokens, rel_bias, norm1_g, w_in, b_i, b_f, q_norm_g, k_norm_g, lambda_q1, lambda_k1, lambda_q2, lambda_k2, subln_g, mnorm_g, w_out, norm2_g, w_up, w_down):
    B, S, D = x_prompt.shape
    Bd, Td, _ = x_sample.shape
    depth = w_in.shape[0]
    Pc = cache_k.shape[2]
    assert Td == N_META and S % ATT_TILE == 0 and S % M_TILE == 0 and Pc % LANES == 0
    assert ATT_TILE % CHUNK == 0 and CHUNK & (CHUNK - 1) == 0 and ATT_TILE >= MAX_DISTANCE

    n_frames = B * S
    n_meta = B * N_META
    n_samp = Bd * Td
    assert n_meta + n_samp <= TAIL_ROWS
    srow = n_meta // N_META

    meta = jnp.broadcast_to(meta_tokens.astype(F32)[None], (B, N_META, D)).reshape(n_meta, D)
    x_f = x_prompt.astype(F32).reshape(n_frames, D)
    x_t = jnp.concatenate([meta, x_sample.astype(F32).reshape(n_samp, D),
                           jnp.zeros((TAIL_ROWS - n_meta - n_samp, D), F32)], axis=0)

    bias = _bias_tiles(rel_bias.astype(F32), Pc, Td)
    ck = cache_k.astype(F32)
    cv = cache_v.astype(F32)
    zeros_pad_att = jnp.zeros((TAIL_ROWS - n_meta - n_samp, ATT_V), F32)

    w_in_t = jnp.swapaxes(w_in.astype(F32), 1, 2)
    w_out_b, w_up_b, w_down_b = (w.astype(BF16) for w in (w_out, w_up, w_down))
    outs = {name: [] for name in ("Cp", "np", "mp", "Cs", "ns", "ms")}
    caches = None

    for l in range(depth):
        lam_init = 0.8 - 0.6 * math.exp(-0.3 * l)
        gate_bias = jnp.pad(jnp.concatenate([b_i[l], b_f[l]]).astype(F32), (0, LANES - 2 * H_M))[None]
        lam_vecs = [v[l].astype(F32)[None] for v in (lambda_q1, lambda_k1, lambda_q2, lambda_k2)]
        sg = subln_g[l].astype(F32)[None]

        za_f, za_t, zm_f, zm_t, g_f, g_t, *caches = _inproj(
            x_f, x_t, norm1_g[l].astype(F32)[None], w_in_t, l, gate_bias,
            q_norm_g[l].astype(F32)[None], k_norm_g[l].astype(F32)[None], B, Bd, caches)

        att_f = _attn_prompt(za_f, za_t, bias, lam_vecs, sg, B, S, lam_init)
        att_m = _attn_small(za_t, 0, B, bias[3], lam_vecs, sg, lam_init)
        att_s = _attn_small(za_t, srow, Bd, bias[4], lam_vecs, sg, lam_init, cache=(ck, cv, l))
        att_t = jnp.concatenate([att_m, att_s, zeros_pad_att], axis=0)

        hm_m, hm_f, C_p, n_p, m_p = _mlstm(zm_t, g_t, 0, B, frames=(zm_f, g_f, S))
        init = (state_C[l].astype(F32), state_n[l].astype(F32)[:, :, None, :],
                jnp.broadcast_to(state_m[l].astype(F32)[:, :, None, None], (Bd, H_M, 8, LANES)))
        hm_s, C_s, n_s, m_s = _mlstm(zm_t, g_t, srow, Bd, init=init)
        hm_t = jnp.concatenate([hm_m, hm_s, zeros_pad_att], axis=0)

        x_f, x_t, xn_f, xn_t = _merge(x_f, x_t, att_f, att_t, hm_f, hm_t, zm_f, zm_t,
                                      mnorm_g[l].astype(F32)[None], w_out_b, norm2_g[l].astype(F32)[None], l)
        x_f, x_t = _ffn(x_f, x_t, xn_f, xn_t, w_up_b, w_down_b, l)

        outs["Cp"].append(C_p)
        outs["np"].append(n_p[:, :, 0, :])
        outs["mp"].append(m_p[:, :, 0, 0])
        outs["Cs"].append(C_s)
        outs["ns"].append(n_s[:, :, 0, :])
        outs["ms"].append(m_s[:, :, 0, 0])

    y_prompt = x_f.reshape(B, S, D)
    y_sample = x_t[n_meta:n_meta + n_samp].reshape(Bd, Td, D)
    k_p, v_p, k_s, v_s = caches
    return (y_prompt, y_sample,
            k_p, v_p, jnp.stack(outs["Cp"]), jnp.stack(outs["np"]), jnp.stack(outs["mp"]),
            k_s.reshape(depth, Bd, Td, H_ATT, 2 * DQK_ATT), v_s.reshape(depth, Bd, Td, H_ATT, DV_ATT),
            jnp.stack(outs["Cs"]), jnp.stack(outs["ns"]), jnp.stack(outs["ms"]))
```
